```python
import math, functools
import jax, jax.numpy as jnp
from jax import lax
import numpy as np

D_MODEL = 1024
BATCH = 4
SEQ = 4096
DEPTH = 4
DEC_BATCH = 32
DEC_SEQ = 1
PAST_LEN = 8192
PAGE_SIZE = 128

GLA_HEADS = D_MODEL // 256
GLA_DV = 64
GLA_DK = 32
GLA_LR = 16
GLA_TAU = 16.0
GLA_CHUNK = 32
GLA_DQK = GLA_HEADS * GLA_DK
GLA_DVW = GLA_HEADS * GLA_DV
SB_HEADS = D_MODEL // 128
SB_HD = 64
SB_QBLOCK = 128
SB_W = SB_HEADS * SB_HD
SB_BIAS_INIT = -6.0
M_HEADS = D_MODEL // 256
M_HD = 64
M_GROUPS = 2
M_N = 128
M_CONV = 4
M_CHUNK = 64
M_DIN = M_HEADS * M_HD
M_CONV_DIM = M_DIN + 2 * M_GROUPS * M_N
D_MIX = GLA_DVW + SB_W + M_DIN
_IN_SPLITS = (GLA_DQK, GLA_DQK, GLA_DVW, GLA_DVW, GLA_LR, SB_W, SB_W, SB_W, M_DIN, M_CONV_DIM, M_HEADS)
D_IN = sum(_IN_SPLITS)
D_FF = 128 * ((8 * D_MODEL // 3 + 127) // 128)
FFN_CONV = 3
N_MOD = 6
EPS = 1e-6

kernel_name = 'hymba_gla_stickbreak_ssd_convffn_step'


def _rms(x, g=None):
    xf = x.astype(jnp.float32)
    y = xf * lax.rsqrt(jnp.mean(xf * xf, axis=-1, keepdims=True) + EPS)
    if g is not None:
        y = y * g.astype(jnp.float32)
    return y.astype(x.dtype)


def _causal_dwconv(u, buf, w, b):
    width, t = w.shape[0], u.shape[1]
    full = jnp.concatenate([buf, u], axis=1)
    y = b
    for i in range(width):
        y = y + full[:, i:i + t] * w[i]
    return y, full[:, full.shape[1] - (width - 1):]


def _mv(t):
    return jnp.moveaxis(t, 1, 0)


def _gla_chunked(q, k, v, log_a, s0):
    bsz, s, h, _ = q.shape
    n, c = s // GLA_CHUNK, GLA_CHUNK
    rs = lambda t: t.reshape(bsz, n, c, h, t.shape[-1])
    q, k, v, la = rs(q), rs(k), rs(v), rs(log_a.astype(jnp.float32))
    b = jnp.cumsum(la, axis=2)
    causal = jnp.tril(jnp.ones((c, c), bool))
    diff = b[:, :, :, None] - b[:, :, None, :]
    decay = jnp.exp(jnp.where(causal[:, :, None, None], diff, -jnp.inf))
    scores = jnp.einsum('bnthk,bnshk,bntshk->bnhts', q, k, decay)
    o_intra = jnp.einsum('bnhts,bnshv->bnthv', scores, v)
    b_last = b[:, :, -1]
    q_in = q * jnp.exp(b)
    k_out = k * jnp.exp(b_last[:, :, None] - b)

    def step(st, xs):
        qc, kc, vc, bl = xs
        o = jnp.einsum('bthk,bhkv->bthv', qc, st)
        st = st * jnp.exp(bl)[..., None] + jnp.einsum('bthk,bthv->bhkv', kc, vc)
        return st, o

    s_fin, o_inter = lax.scan(step, s0.astype(jnp.float32), (_mv(q_in), _mv(k_out), _mv(v), _mv(b_last)))
    o = o_intra + _mv(o_inter)
    return o.reshape(bsz, s, h, v.shape[-1]), s_fin


def _gla_steps(q, k, v, log_a, s0):
    def step(st, xs):
        qt, kt, vt, lat = xs
        st = st * jnp.exp(lat)[..., None] + kt[..., None] * vt[..., None, :]
        return st, jnp.einsum('bhk,bhkv->bhv', qt, st)

    s_fin, o = lax.scan(step, s0.astype(jnp.float32), (_mv(q), _mv(k), _mv(v), _mv(log_a.astype(jnp.float32))))
    return _mv(o), s_fin


def _stick_breaking(q, q_pos, k, v, k_pos, bias):
    z = jnp.einsum('bqhd,bkhd->bhqk', q, k).astype(jnp.float32) * (SB_HD ** -0.5)
    z = z + bias.astype(jnp.float32)[None, :, None, None]
    mask = k_pos[None, :] < q_pos[:, None]
    sp = jnp.where(mask, jax.nn.softplus(z), 0.0)
    later = lax.cumsum(sp, axis=3, reverse=True) - sp
    w = jnp.where(mask, jnp.exp(jax.nn.log_sigmoid(z) - later), 0.0)
    return jnp.einsum('bhqk,bkhd->bqhd', w.astype(v.dtype), v)


def _sb_prompt(q, k, v, bias):
    bsz, s, h, d = q.shape
    nb = s // SB_QBLOCK
    pos = jnp.arange(s)
    qb = _mv(q.reshape(bsz, nb, SB_QBLOCK, h, d))
    out = lax.map(lambda a: _stick_breaking(a[0], a[1], k, v, pos, bias), (qb, pos.reshape(nb, SB_QBLOCK)))
    return _mv(out).reshape(bsz, s, h, d)


def _ssd_chunked(x, dt, a, bm, cm, h0):
    bsz, s, h, p = x.shape
    n, c = s // M_CHUNK, M_CHUNK
    rs = lambda t: t.reshape((bsz, n, c) + t.shape[2:])
    x, dt, a, bm, cm = rs(x), rs(dt), rs(a), rs(bm), rs(cm)
    cs = jnp.cumsum(a, axis=2)
    causal = jnp.tril(jnp.ones((c, c), bool))
    seg = cs[:, :, :, None] - cs[:, :, None, :]
    decay = jnp.exp(jnp.where(causal[:, :, None], seg, -jnp.inf))
    xdt = x * dt[..., None]
    scores = jnp.einsum('bnthN,bnshN->bntsh', cm, bm) * decay
    y_intra = jnp.einsum('bntsh,bnshp->bnthp', scores, xdt)
    cs_last = cs[:, :, -1]
    c_in = cm * jnp.exp(cs)[..., None]
    b_out = bm * jnp.exp(cs_last[:, :, None] - cs)[..., None]

    def step(hs, xs):
        cc, bc, xc, cl = xs
        y = jnp.einsum('bthN,bhpN->bthp', cc, hs)
        hs = hs * jnp.exp(cl)[..., None, None] + jnp.einsum('bshN,bshp->bhpN', bc, xc)
        return hs, y

    h_fin, y_inter = lax.scan(step, h0.astype(jnp.float32), (_mv(c_in), _mv(b_out), _mv(xdt), _mv(cs_last)))
    y = y_intra + _mv(y_inter)
    return y.reshape(bsz, s, h, p), h_fin


def _ssd_steps(x, dt, a, bm, cm, h0):
    def step(hs, xs):
        xt, dtt, at, bt, ct = xs
        hs = hs * jnp.exp(at)[..., None, None] + (xt * dtt[..., None])[..., None] * bt[:, :, None, :]
        return hs, jnp.einsum('bhN,bhpN->bhp', ct, hs)

    h_fin, y = lax.scan(step, h0.astype(jnp.float32), (_mv(x), _mv(dt), _mv(a), _mv(bm), _mv(cm)))
    return _mv(y), h_fin


def _features(proj, lp):
    bsz, t, _ = proj.shape
    gq, gk, gv, gg, glr, sq, sk, sv, mz, mxbc, mdt = jnp.split(proj, np.cumsum(_IN_SPLITS)[:-1].tolist(), axis=-1)
    gate_logit = (glr @ lp['gla_w_gate2'] + lp['gla_b_gate']).astype(jnp.float32)
    return dict(
        gla_q=gq.reshape(bsz, t, GLA_HEADS, GLA_DK) * (GLA_DK ** -0.5),
        gla_k=gk.reshape(bsz, t, GLA_HEADS, GLA_DK),
        gla_v=gv.reshape(bsz, t, GLA_HEADS, GLA_DV),
        gla_la=(jax.nn.log_sigmoid(gate_logit) / GLA_TAU).reshape(bsz, t, GLA_HEADS, GLA_DK),
        gla_g=gg,
        sb_q=_rms(sq.reshape(bsz, t, SB_HEADS, SB_HD), lp['sb_q_norm_g']),
        sb_k=_rms(sk.reshape(bsz, t, SB_HEADS, SB_HD), lp['sb_k_norm_g']),
        sb_v=sv.reshape(bsz, t, SB_HEADS, SB_HD),
        m_z=mz, m_xbc=mxbc, m_dt=mdt)


def _mamba_inputs(xbc, dt_raw, lp):
    bsz, t, _ = xbc.shape
    xbc = jax.nn.silu(xbc)
    mx, mb, mc = jnp.split(xbc, [M_DIN, M_DIN + M_GROUPS * M_N], axis=-1)
    rep = M_HEADS // M_GROUPS
    mb = jnp.repeat(mb.reshape(bsz, t, M_GROUPS, M_N), rep, axis=2)
    mc = jnp.repeat(mc.reshape(bsz, t, M_GROUPS, M_N), rep, axis=2)
    dt = jax.nn.softplus((dt_raw + lp['m_dt_bias']).astype(jnp.float32))
    a = dt * -jnp.exp(lp['m_a_log'].astype(jnp.float32))
    return mx.reshape(bsz, t, M_HEADS, M_HD), dt, a, mb, mc


def _merge(f, gla_o, sb_o, m_x, m_y, lp, dtype):
    bsz, t = m_x.shape[:2]
    g = _rms(gla_o, lp['gla_norm_g']).reshape(bsz, t, GLA_DVW) * jax.nn.silu(f['gla_g'])
    s = _rms(sb_o, lp['sb_o_norm_g']).reshape(bsz, t, SB_W)
    y = (m_y + lp['m_d'][:, None] * m_x).reshape(bsz, t, M_DIN) * jax.nn.silu(f['m_z'])
    y = _rms(y.reshape(bsz, t, M_GROUPS, M_DIN // M_GROUPS)).reshape(bsz, t, M_DIN) * lp['m_norm_g']
    return jnp.concatenate([g.astype(dtype), s.astype(dtype), y.astype(dtype)], axis=-1)


def _prompt_core(proj, lp):
    f = _features(proj, lp)
    bsz = proj.shape[0]
    gla_o, gla_s = _gla_chunked(f['gla_q'], f['gla_k'], f['gla_v'], f['gla_la'],
                                jnp.zeros((bsz, GLA_HEADS, GLA_DK, GLA_DV), jnp.float32))
    sb_o = _sb_prompt(f['sb_q'], f['sb_k'], f['sb_v'], lp['sb_bias'])
    xbc, conv_new = _causal_dwconv(f['m_xbc'], jnp.zeros((bsz, M_CONV - 1, M_CONV_DIM), proj.dtype),
                                   lp['m_conv_w'], lp['m_conv_b'])
    mx, dt, a, mb, mc = _mamba_inputs(xbc, f['m_dt'], lp)
    m_y, m_h = _ssd_chunked(mx, dt, a, mb, mc, jnp.zeros((bsz, M_HEADS, M_HD, M_N), jnp.float32))
    mixed = _merge(f, gla_o, sb_o, mx, m_y, lp, proj.dtype)
    return mixed, (f['sb_k'], f['sb_v'], gla_s, conv_new, m_h)


def _sample_core(proj, lp, past_k, past_v, gla_s0, conv_buf, ssm_h0):
    f = _features(proj, lp)
    t = proj.shape[1]
    p = past_k.shape[1]
    gla_o, gla_s = _gla_steps(f['gla_q'], f['gla_k'], f['gla_v'], f['gla_la'], gla_s0)
    keys = jnp.concatenate([past_k, f['sb_k']], axis=1)
    vals = jnp.concatenate([past_v, f['sb_v']], axis=1)
    sb_o = _stick_breaking(f['sb_q'], p + jnp.arange(t), keys, vals, jnp.arange(p + t), lp['sb_bias'])
    xbc, conv_new = _causal_dwconv(f['m_xbc'], conv_buf, lp['m_conv_w'], lp['m_conv_b'])
    mx, dt, a, mb, mc = _mamba_inputs(xbc, f['m_dt'], lp)
    m_y, m_h = _ssd_steps(mx, dt, a, mb, mc, ssm_h0)
    mixed = _merge(f, gla_o, sb_o, mx, m_y, lp, proj.dtype)
    return mixed, (f['sb_k'], f['sb_v'], gla_s, conv_new, m_h)


def _layer(x, c, lp, core, ffn_buf):
    mod = jax.nn.silu(c) @ lp['w_ada'] + lp['b_ada']
    sh1, sc1, g1, sh2, sc2, g2 = [m[:, None] for m in jnp.split(mod, N_MOD, axis=-1)]
    h = _rms(x, lp['norm1_g']) * (1 + sc1) + sh1
    mixed, mix_state = core(h @ lp['w_in'], lp)
    x = x + g1 * (mixed @ lp['w_out'])
    h = _rms(x, lp['norm2_g']) * (1 + sc2) + sh2
    u, ffn_new = _causal_dwconv(h @ lp['ffn_w_up'], ffn_buf, lp['ffn_conv_w'], lp['ffn_conv_b'])
    gate, val = jnp.split(u, 2, axis=-1)
    x = x + g2 * ((jax.nn.silu(gate) * val) @ lp['ffn_w_down'])
    return x, mix_state, ffn_new


def setup_inputs(seed: int = 0) -> dict:
    key = jax.random.key(seed)
    ks = iter(jax.random.split(key, 48))

    def nrm(shape, scale):
        return scale * jax.random.normal(next(ks), shape, jnp.float32)

    def gain(shape):
        return 1.0 + nrm(shape, 0.02)

    n_pages = PAST_LEN // PAGE_SIZE
    n_used = DEC_BATCH * n_pages
    n_pool = (5 * n_used) // 4
    perm = jax.random.permutation(next(ks), n_pool)
    page_table = perm[:n_used].reshape(DEC_BATCH, n_pages).astype(jnp.int32)
    dt0 = jnp.exp(jax.random.uniform(next(ks), (DEPTH, M_HEADS), jnp.float32, math.log(1e-3), math.log(1e-1)))
    return dict(
        x_prompt=nrm((BATCH, SEQ, D_MODEL), 1.0),
        x_sample=nrm((DEC_BATCH, DEC_SEQ, D_MODEL), 1.0),
        cache_sb_k=nrm((DEPTH, n_pool, PAGE_SIZE, SB_HEADS, SB_HD), 1.0),
        cache_sb_v=nrm((DEPTH, n_pool, PAGE_SIZE, SB_HEADS, SB_HD), 1.0),
        state_gla=nrm((DEPTH, DEC_BATCH, GLA_HEADS, GLA_DK, GLA_DV), 0.3),
        state_mamba_conv=nrm((DEPTH, DEC_BATCH, M_CONV - 1, M_CONV_DIM), 1.0),
        state_mamba_ssm=nrm((DEPTH, DEC_BATCH, M_HEADS, M_HD, M_N), 0.1),
        state_ffn_conv=nrm((DEPTH, DEC_BATCH, FFN_CONV - 1, 2 * D_FF), 1.0),
        page_table=page_table,
        c_prompt=nrm((BATCH, D_MODEL), 1.0),
        c_sample=nrm((DEC_BATCH, D_MODEL), 1.0),
        norm1_g=gain((DEPTH, D_MODEL)),
        w_ada=nrm((DEPTH, D_MODEL, N_MOD * D_MODEL), 0.5 * D_MODEL ** -0.5),
        b_ada=nrm((DEPTH, N_MOD * D_MODEL), 0.02),
        w_in=nrm((DEPTH, D_MODEL, D_IN), D_MODEL ** -0.5),
        gla_w_gate2=nrm((DEPTH, GLA_LR, GLA_DQK), GLA_LR ** -0.5),
        gla_b_gate=nrm((DEPTH, GLA_DQK), 0.1),
        gla_norm_g=gain((DEPTH, GLA_DV)),
        sb_q_norm_g=gain((DEPTH, SB_HD)),
        sb_k_norm_g=gain((DEPTH, SB_HD)),
        sb_o_norm_g=gain((DEPTH, SB_HD)),
        sb_bias=SB_BIAS_INIT + jax.random.uniform(next(ks), (DEPTH, SB_HEADS), jnp.float32, -1.0, 1.0),
        m_conv_w=nrm((DEPTH, M_CONV, M_CONV_DIM), M_CONV ** -0.5),
        m_conv_b=nrm((DEPTH, M_CONV_DIM), 0.02),
        m_dt_bias=dt0 + jnp.log(-jnp.expm1(-dt0)),
        m_a_log=jnp.log(jax.random.uniform(next(ks), (DEPTH, M_HEADS), jnp.float32, 1.0, 16.0)),
        m_d=1.0 + nrm((DEPTH, M_HEADS), 0.1),
        m_norm_g=gain((DEPTH, M_DIN)),
        w_out=nrm((DEPTH, D_MIX, D_MODEL), D_MIX ** -0.5),
        norm2_g=gain((DEPTH, D_MODEL)),
        ffn_w_up=nrm((DEPTH, D_MODEL, 2 * D_FF), D_MODEL ** -0.5),
        ffn_conv_w=nrm((DEPTH, FFN_CONV, 2 * D_FF), FFN_CONV ** -0.5),
        ffn_conv_b=nrm((DEPTH, 2 * D_FF), 0.02),
        ffn_w_down=nrm((DEPTH, D_FF, D_MODEL), D_FF ** -0.5),
    )


def reference(x_prompt, x_sample, cache_sb_k, cache_sb_v, state_gla, state_mamba_conv, state_mamba_ssm,
              state_ffn_conv, page_table, c_prompt, c_sample, norm1_g, w_ada, b_ada, w_in, gla_w_gate2,
              gla_b_gate, gla_norm_g, sb_q_norm_g, sb_k_norm_g, sb_o_norm_g, sb_bias, m_conv_w, m_conv_b,
              m_dt_bias, m_a_log, m_d, m_norm_g, w_out, norm2_g, ffn_w_up, ffn_conv_w, ffn_conv_b, ffn_w_down):
    params = dict(norm1_g=norm1_g, w_ada=w_ada, b_ada=b_ada, w_in=w_in, gla_w_gate2=gla_w_gate2,
                  gla_b_gate=gla_b_gate, gla_norm_g=gla_norm_g, sb_q_norm_g=sb_q_norm_g,
                  sb_k_norm_g=sb_k_norm_g, sb_o_norm_g=sb_o_norm_g, sb_bias=sb_bias, m_conv_w=m_conv_w,
                  m_conv_b=m_conv_b, m_dt_bias=m_dt_bias, m_a_log=m_a_log, m_d=m_d, m_norm_g=m_norm_g,
                  w_out=w_out, norm2_g=norm2_g, ffn_w_up=ffn_w_up, ffn_conv_w=ffn_conv_w,
                  ffn_conv_b=ffn_conv_b, ffn_w_down=ffn_w_down)
    n_b, n_pages = page_table.shape
    past_len = n_pages * cache_sb_k.shape[2]
    xp, xs = x_prompt, x_sample
    p_states = [[] for _ in range(6)]
    s_states = [[] for _ in range(6)]
    for l in range(DEPTH):
        lp = {name: arr[l] for name, arr in params.items()}
        ffn0 = jnp.zeros((xp.shape[0], FFN_CONV - 1, 2 * D_FF), xp.dtype)
        xp, mix_p, ffn_p = _layer(xp, c_prompt, lp, _prompt_core, ffn0)
        past_k = cache_sb_k[l][page_table].reshape(n_b, past_len, SB_HEADS, SB_HD)
        past_v = cache_sb_v[l][page_table].reshape(n_b, past_len, SB_HEADS, SB_HD)
        core = functools.partial(_sample_core, past_k=past_k, past_v=past_v, gla_s0=state_gla[l],
                                 conv_buf=state_mamba_conv[l], ssm_h0=state_mamba_ssm[l])
        xs, mix_s, ffn_s = _layer(xs, c_sample, lp, core, state_ffn_conv[l])
        for lst, arr in zip(p_states, mix_p + (ffn_p,)):
            lst.append(arr)
        for lst, arr in zip(s_states, mix_s + (ffn_s,)):
            lst.append(arr)
    p_k, p_v, p_gla, p_mconv, p_mssm, p_fconv = [jnp.stack(t) for t in p_states]
    s_k, s_v, s_gla, s_mconv, s_mssm, s_fconv = [jnp.stack(t) for t in s_states]
    return (xp, xs, p_k, p_v, p_gla, p_mconv, p_mssm, p_fconv, s_k, s_v, s_gla, s_mconv, s_mssm, s_fconv)
```

```python
import functools

import jax
import jax.numpy as jnp
import numpy as np
from jax import lax
from jax.experimental import pallas as pl
from jax.experimental.pallas import tpu as pltpu

F32 = jnp.float32
BF16 = jnp.bfloat16

GLA_HEADS, GLA_DK, GLA_DV, GLA_LR, GLA_TAU = 4, 32, 64, 16, 16.0
GLA_DQK = GLA_HEADS * GLA_DK
GLA_DVW = GLA_HEADS * GLA_DV
SB_HEADS, SB_HD = 8, 64
SB_W = SB_HEADS * SB_HD
M_HEADS, M_HD, M_GROUPS, M_N, M_CONV = 4, 64, 2, 128, 4
M_DIN = M_HEADS * M_HD
M_CONV_DIM = M_DIN + 2 * M_GROUPS * M_N
FFN_CONV = 3
N_MOD = 6
EPS = 1e-6
NEG_BIG = -1e30

C_GLA = 0
C_SBQ = 768
C_SBK = 1280
C_SBV = 1792
C_MZ = 2304
C_MISC = 3328
D_INP = 3456
LANE = 128
SUBLANE = 8

VMEM_LIMIT = 56 * 1024 * 1024


def _cp(*sem):
    return pltpu.CompilerParams(dimension_semantics=sem, vmem_limit_bytes=VMEM_LIMIT)


def _dot(a, b):
    return jnp.dot(a.astype(BF16), b.astype(BF16), preferred_element_type=F32)


def _dot_nt(a, b):
    return lax.dot_general(a.astype(BF16), b.astype(BF16), (((1,), (1,)), ((), ())),
                           preferred_element_type=F32)


def _dot_tn(a, b):
    return lax.dot_general(a.astype(BF16), b.astype(BF16), (((0,), (0,)), ((), ())),
                           preferred_element_type=F32)


def _split3(x):
    hi = x.astype(BF16)
    r = x - hi.astype(F32)
    mid = r.astype(BF16)
    lo = (r - mid.astype(F32)).astype(BF16)
    return hi, mid, lo


def _dot_sel(x, e):
    hi, mid, lo = _split3(x)
    d = lambda p: jnp.dot(p, e, preferred_element_type=F32)
    return d(hi) + d(mid) + d(lo)


def _dot_sel_nt(e, x):
    hi, mid, lo = _split3(x)
    d = lambda p: lax.dot_general(e, p, (((1,), (1,)), ((), ())), preferred_element_type=F32)
    return d(hi) + d(mid) + d(lo)


def _sigmoid(x):
    return 1.0 / (1.0 + jnp.exp(-x))


def _silu(x):
    return x * _sigmoid(x)


def _softplus(x):
    return jnp.maximum(x, 0.0) + jnp.log1p(jnp.exp(-jnp.abs(x)))


def _log_sigmoid(x):
    return jnp.minimum(x, 0.0) - jnp.log1p(jnp.exp(-jnp.abs(x)))


def _rms_rows(x):
    return x * lax.rsqrt(jnp.mean(x * x, axis=-1, keepdims=True) + EPS)


def _seg_rms(x, eseg, width):
    ms = _dot_sel(x * x, eseg) * (1.0 / width)
    return x * lax.rsqrt(ms + EPS)


def _row_to_col(row):
    n = row.shape[1]
    eye = lax.broadcasted_iota(jnp.int32, (n, n), 0) == lax.broadcasted_iota(jnp.int32, (n, n), 1)
    return jnp.sum(jnp.where(eye, row, 0.0), axis=1, keepdims=True)


def _col_to_row(col):
    n = col.shape[0]
    eye = lax.broadcasted_iota(jnp.int32, (n, n), 0) == lax.broadcasted_iota(jnp.int32, (n, n), 1)
    return jnp.sum(jnp.where(eye, col, 0.0), axis=0, keepdims=True)


def _cumsum(x, axis, seg=None):
    n = x.shape[axis] if seg is None else seg
    idx = lax.broadcasted_iota(jnp.int32, x.shape, axis)
    if seg is not None:
        idx = idx % seg
    s = 1
    while s < n:
        x = x + jnp.where(idx >= s, pltpu.roll(x, s, axis), 0.0)
        s *= 2
    return x


def _mod_kernel(c_ref, w_ref, b_ref, o_ref):
    o_ref[...] = _dot(_silu(c_ref[...]), w_ref[...]) + b_ref[...]


def _modulation(c_all, w_ada, b_ada):
    depth, d, nd = w_ada.shape
    n = c_all.shape[0]
    tn = 1024
    return pl.pallas_call(
        _mod_kernel,
        grid=(depth, nd // tn),
        in_specs=[pl.BlockSpec((n, d), lambda l, j: (0, 0)),
                  pl.BlockSpec((None, d, tn), lambda l, j: (l, 0, j)),
                  pl.BlockSpec((None, 1, tn), lambda l, j: (l, 0, j))],
        out_specs=pl.BlockSpec((None, n, tn), lambda l, j: (l, 0, j)),
        out_shape=jax.ShapeDtypeStruct((depth, n, nd), F32),
        compiler_params=_cp("arbitrary", "arbitrary"),
        name="adaln_mod",
    )(c_all, w_ada, b_ada.reshape(depth, 1, nd))


def _norm_mod(x, g, sc, sh):
    return _rms_rows(x) * g * (1.0 + sc) + sh


def _inproj_kernel(x_ref, sc_ref, sh_ref, g_ref, w_ref, wg_ref, bg_ref, gq_ref, gk_ref, eseg_ref,
                   gla_ref, la_ref, q_ref, k_ref, kb_ref, v_ref, vb_ref, m_ref, misc_ref):
    h = _norm_mod(x_ref[...], g_ref[...], sc_ref[...], sh_ref[...])
    p = jnp.dot(h.astype(BF16), w_ref[...], preferred_element_type=F32)
    gla_ref[...] = p[:, C_GLA:C_SBQ]
    misc = p[:, C_MISC:D_INP]
    misc_ref[...] = misc
    la_ref[...] = _log_sigmoid(_dot(misc, wg_ref[...]) + bg_ref[...]) * (1.0 / GLA_TAU)
    eseg = eseg_ref[...]
    q = _seg_rms(p[:, C_SBQ:C_SBK], eseg, SB_HD) * gq_ref[...]
    q_ref[...] = (q * (SB_HD ** -0.5)).astype(BF16)
    k = _seg_rms(p[:, C_SBK:C_SBV], eseg, SB_HD) * gk_ref[...]
    k_ref[...] = k
    kb_ref[...] = k.astype(BF16)
    v = p[:, C_SBV:C_MZ]
    v_ref[...] = v
    vb_ref[...] = v.astype(BF16)
    m_ref[...] = p[:, C_MZ:C_MISC]


def _in_proj(x, sc, sh, g, w, wg, bg, gq, gk, eseg, *, tm, rows_per_mod):
    r, d = x.shape
    mrows = sc.shape[1]
    mod_spec = pl.BlockSpec((None, mrows, d), lambda i: (i // rows_per_mod, 0, 0))
    const = lambda shape: pl.BlockSpec(shape, lambda i: (0,) * len(shape))
    row = lambda n: pl.BlockSpec((tm, n), lambda i: (i, 0))
    outs = [(768, F32), (LANE, F32), (SB_W, BF16), (SB_W, F32), (SB_W, BF16), (SB_W, F32), (SB_W, BF16),
            (1024, F32), (LANE, F32)]
    return pl.pallas_call(
        _inproj_kernel,
        grid=(r // tm,),
        in_specs=[row(d), mod_spec, mod_spec, const((1, d)), const((d, D_INP)), const((LANE, LANE)),
                  const((1, LANE)), const((1, SB_W)), const((1, SB_W)), const((SB_W, SB_W))],
        out_specs=[row(n) for n, _ in outs],
        out_shape=[jax.ShapeDtypeStruct((r, n), dt) for n, dt in outs],
        compiler_params=_cp("arbitrary"),
        name="in_proj",
    )(x, sc, sh, g, w, wg, bg, gq, gk, eseg)


GLA_C = SUBLANE


def _gla_prompt_kernel(gla_ref, la_ref, eexp_ref, bd_ref, eseg_ref, gn_ref, o_ref, st_out_ref,
                       st_ref, b_ref, acc_ref):
    i = pl.program_id(1)
    tt = la_ref.shape[0]

    @pl.when(i == 0)
    def _():
        st_ref[...] = jnp.zeros_like(st_ref)

    b_ref[...] = _cumsum(la_ref[...], 0, seg=GLA_C)
    eexp = eexp_ref[...]
    bd = bd_ref[...]
    rowid = lax.broadcasted_iota(jnp.int32, (GLA_C, GLA_DQK), 0)
    rowid_v = lax.broadcasted_iota(jnp.int32, (GLA_C, GLA_DVW), 0)

    def chunk(c, carry):
        r0 = pl.multiple_of(c * GLA_C, GLA_C)
        bc = b_ref[pl.ds(r0, GLA_C), :]
        qc = gla_ref[pl.ds(r0, GLA_C), 0:GLA_DQK] * (GLA_DK ** -0.5)
        kc = gla_ref[pl.ds(r0, GLA_C), GLA_DQK:2 * GLA_DQK]
        vc = gla_ref[pl.ds(r0, GLA_C), 2 * GLA_DQK:2 * GLA_DQK + GLA_DVW]
        st = st_ref[...]
        o = _dot_nt(qc * jnp.exp(bc), st)
        for t in range(GLA_C):
            d = jnp.where(rowid <= t, bc[t:t + 1, :] - bc, NEG_BIG)
            p = jnp.exp(d) * (qc[t:t + 1, :] * kc)
            sc = _dot(p, eexp)
            ot = jnp.sum(sc * vc, axis=0, keepdims=True)
            o = o + jnp.where(rowid_v == t, ot, 0.0)
        bl = bc[GLA_C - 1:GLA_C, :]
        upd = _dot_tn(vc, kc * jnp.exp(bl - bc))
        st_ref[...] = st * jnp.exp(bl) + upd * bd
        acc_ref[pl.ds(r0, GLA_C), :] = o
        return carry

    lax.fori_loop(0, tt // GLA_C, chunk, 0)
    gg = gla_ref[:, 2 * GLA_DQK + GLA_DVW:2 * GLA_DQK + 2 * GLA_DVW]
    o = _seg_rms(acc_ref[...], eseg_ref[...], GLA_DV) * gn_ref[...] * _silu(gg)
    o_ref[...] = o.astype(BF16)
    st_out_ref[...] = st_ref[...]


def _gla_prompt(gla, la, eexp, bd, eseg, gn, *, bsz, seq, tt):
    nt = seq // tt
    const = lambda shape: pl.BlockSpec(shape, lambda b, i: (0,) * len(shape))
    return pl.pallas_call(
        _gla_prompt_kernel,
        grid=(bsz, nt),
        in_specs=[pl.BlockSpec((tt, 768), lambda b, i: (b * nt + i, 0)),
                  pl.BlockSpec((tt, LANE), lambda b, i: (b * nt + i, 0)),
                  const((GLA_DQK, GLA_DVW)), const((GLA_DVW, GLA_DQK)), const((GLA_DVW, GLA_DVW)),
                  const((1, GLA_DVW))],
        out_specs=[pl.BlockSpec((tt, GLA_DVW), lambda b, i: (b * nt + i, 0)),
                   pl.BlockSpec((None, GLA_DVW, GLA_DQK), lambda b, i: (b, 0, 0))],
        out_shape=[jax.ShapeDtypeStruct((bsz * seq, GLA_DVW), BF16),
                   jax.ShapeDtypeStruct((bsz, GLA_DVW, GLA_DQK), F32)],
        scratch_shapes=[pltpu.VMEM((GLA_DVW, GLA_DQK), F32), pltpu.VMEM((tt, GLA_DQK), F32),
                        pltpu.VMEM((tt, GLA_DVW), F32)],
        compiler_params=_cp("arbitrary", "arbitrary"),
        name="gla_prompt",
    )(gla, la, eexp, bd, eseg, gn)


SB_TK = LANE


def _sb_prompt_kernel(q_ref, k_ref, v_ref, bias_ref, uo_ref, hm_ref, eseg_ref, gn_ref, o_ref,
                      acc_ref, car_ref):
    i = pl.program_id(2)
    tq = q_ref.shape[0]
    ndiag = tq // SB_TK
    q = q_ref[...]
    bias = bias_ref[...]
    uo = uo_ref[...]
    m0 = hm_ref[0:1, :]
    m1 = hm_ref[1:2, :]
    acc_ref[...] = jnp.zeros_like(acc_ref)
    car_ref[...] = jnp.zeros_like(car_ref)
    qpos = i * tq + lax.broadcasted_iota(jnp.int32, (tq, 2 * SB_TK), 0)
    kloc = lax.broadcasted_iota(jnp.int32, (tq, 2 * SB_TK), 1) % SB_TK

    def block(j, masked):
        k0 = pl.multiple_of(j * SB_TK, SB_TK)
        kb = k_ref[pl.ds(k0, SB_TK), :]
        vb = v_ref[pl.ds(k0, SB_TK), :]
        kk = jnp.concatenate([kb * m0, kb * m1], axis=0)
        vv = jnp.concatenate([vb * m0, vb * m1], axis=0)
        z = lax.dot_general(q, kk, (((1,), (1,)), ((), ())), preferred_element_type=F32) + bias
        l1p = jnp.log1p(jnp.exp(-jnp.abs(z)))
        sp = jnp.maximum(z, 0.0) + l1p
        ls = jnp.minimum(z, 0.0) - l1p
        if masked:
            valid = (j * SB_TK + kloc) < qpos
            sp = jnp.where(valid, sp, 0.0)
        spb = sp.astype(BF16)
        c0 = jnp.dot(spb[:, :SB_TK], uo, preferred_element_type=F32)
        c1 = jnp.dot(spb[:, SB_TK:], uo, preferred_element_type=F32)
        car = car_ref[...]
        later = jnp.concatenate([c0[:, :SB_TK], c1[:, :SB_TK]], axis=1) + car
        car_ref[...] = car + jnp.concatenate([c0[:, SB_TK:], c1[:, SB_TK:]], axis=1)
        w = jnp.exp(ls - later)
        if masked:
            w = jnp.where(valid, w, 0.0)
        acc_ref[...] += jnp.dot(w.astype(BF16), vv, preferred_element_type=F32)

    nkv = (i + 1) * ndiag
    for dj in range(ndiag):
        block(nkv - 1 - dj, True)

    def body(jj, carry):
        block(nkv - ndiag - 1 - jj, False)
        return carry

    lax.fori_loop(0, nkv - ndiag, body, 0)
    o_ref[...] = (_seg_rms(acc_ref[...], eseg_ref[...], SB_HD) * gn_ref[...]).astype(BF16)


def _sb_prompt(q, k, v, bias2, uo, hm, eseg, gn, *, bsz, seq, tq):
    nq = seq // tq
    npair = SB_HEADS // 2
    const = lambda shape: pl.BlockSpec(shape, lambda b, p, i: (0,) * len(shape))
    return pl.pallas_call(
        _sb_prompt_kernel,
        grid=(bsz, npair, nq),
        in_specs=[pl.BlockSpec((tq, LANE), lambda b, p, i: (b * nq + i, p)),
                  pl.BlockSpec((seq, LANE), lambda b, p, i: (b, p)),
                  pl.BlockSpec((seq, LANE), lambda b, p, i: (b, p)),
                  pl.BlockSpec((None, 1, 2 * SB_TK), lambda b, p, i: (p, 0, 0)),
                  const((SB_TK, 2 * SB_TK)), const((SUBLANE, LANE)), const((LANE, LANE)), const((1, LANE))],
        out_specs=pl.BlockSpec((tq, LANE), lambda b, p, i: (b * nq + i, p)),
        out_shape=jax.ShapeDtypeStruct((bsz * seq, SB_W), BF16),
        scratch_shapes=[pltpu.VMEM((tq, LANE), F32), pltpu.VMEM((tq, 2 * SB_TK), F32)],
        compiler_params=_cp("arbitrary", "arbitrary", "arbitrary"),
        name="sb_prompt",
    )(q, k, v, bias2, uo, hm, eseg, gn)


def _ssm_params(dt_raw, dtb, nega):
    dt = _softplus(dt_raw + dtb)
    return dt, dt * nega


def _ssd_prompt_kernel(m_ref, prev_ref, misc_ref, dtt_ref, cw_ref, cb_ref, dtb_l_ref, nega_l_ref,
                       dtb_c_ref, nega_c_ref, e4_ref, e4t_ref, md_ref, gn_ref, o_ref, hs_out_ref,
                       hs_ref):
    i = pl.program_id(1)
    tt = m_ref.shape[0]

    @pl.when(i == 0)
    def _():
        hs_ref[...] = jnp.zeros_like(hs_ref)

    z = m_ref[:, 0:M_DIN]
    xbc = m_ref[:, M_DIN:M_DIN + M_CONV_DIM]
    prev = jnp.where(i > 0, prev_ref[:, M_DIN:M_DIN + M_CONV_DIM], 0.0)
    row = lax.broadcasted_iota(jnp.int32, (tt, M_CONV_DIM), 0)
    row8 = lax.broadcasted_iota(jnp.int32, (SUBLANE, M_CONV_DIM), 0)
    acc = cb_ref[...] + cw_ref[M_CONV - 1:M_CONV, :] * xbc
    for s in range(1, M_CONV):
        head = jnp.where(row8 < s, pltpu.roll(prev, s, 0), 0.0)
        head = jnp.concatenate([head, jnp.zeros((tt - SUBLANE, M_CONV_DIM), F32)], axis=0)
        shifted = jnp.where(row < s, head, pltpu.roll(xbc, s, 0))
        acc = acc + cw_ref[M_CONV - 1 - s:M_CONV - s, :] * shifted
    xc = _silu(acc)
    x = xc[:, 0:M_DIN]
    bm = xc[:, M_DIN:M_DIN + M_GROUPS * M_N]
    cm = xc[:, M_DIN + M_GROUPS * M_N:]

    e4 = e4_ref[...]
    dt_c, a_c = _ssm_params(misc_ref[...], dtb_l_ref[...], nega_l_ref[...])
    cs_c = _cumsum(a_c, 0)
    dt_r, a_r = _ssm_params(dtt_ref[...], dtb_c_ref[...], nega_c_ref[...])
    cs_r = _cumsum(a_r, 1)
    xdt = x * _dot_sel(dt_c, e4)
    ecs = _dot_sel(jnp.exp(cs_c), e4)

    g = [_dot_nt(cm[:, gi * M_N:(gi + 1) * M_N], bm[:, gi * M_N:(gi + 1) * M_N]) for gi in range(M_GROUPS)]
    causal = lax.broadcasted_iota(jnp.int32, (tt, tt), 0) >= lax.broadcasted_iota(jnp.int32, (tt, tt), 1)
    lane_head = lax.broadcasted_iota(jnp.int32, (tt, M_DIN), 1) // M_HD
    rep = M_HEADS // M_GROUPS
    y = jnp.zeros((tt, M_DIN), F32)
    for h in range(M_HEADS):
        decay = jnp.exp(jnp.where(causal, cs_c[:, h:h + 1] - cs_r[h:h + 1, :], NEG_BIG))
        yh = _dot(g[h // rep] * decay, xdt)
        y = jnp.where(lane_head == h, yh, y)

    hs = hs_ref[...]
    yi = [_dot_nt(cm[:, gi * M_N:(gi + 1) * M_N], hs) for gi in range(M_GROUPS)]
    y = y + jnp.where(lane_head < rep, yi[0], yi[1]) * ecs

    cs_last = cs_c[tt - 1:tt, :]
    xw = xdt * _dot_sel(jnp.exp(cs_last - cs_c), e4)
    zz = _dot_tn(xw, bm)
    half = M_DIN // M_GROUPS
    upd = jnp.concatenate([zz[:half, :M_N], zz[half:, M_N:]], axis=0)
    dec = _dot_sel_nt(e4t_ref[...], jnp.broadcast_to(jnp.exp(cs_last), (M_N, LANE)))
    hs_ref[...] = hs * dec + upd
    hs_out_ref[...] = hs_ref[...]

    y = (y + md_ref[...] * x) * _silu(z)
    y = jnp.concatenate([_rms_rows(y[:, gi * half:(gi + 1) * half]) for gi in range(M_GROUPS)], axis=1)
    o_ref[...] = (y * gn_ref[...]).astype(BF16)


def _ssd_prompt(m, misc, dtt, cw, cb, dtb_l, nega_l, dtb_c, nega_c, e4, e4t, md, gn, *, bsz, seq, tt):
    nt = seq // tt
    per8 = tt // SUBLANE
    const = lambda shape: pl.BlockSpec(shape, lambda b, i: (0,) * len(shape))
    return pl.pallas_call(
        _ssd_prompt_kernel,
        grid=(bsz, nt),
        in_specs=[pl.BlockSpec((tt, 1024), lambda b, i: (b * nt + i, 0)),
                  pl.BlockSpec((SUBLANE, 1024), lambda b, i: (jnp.maximum((b * nt + i) * per8 - 1, 0), 0)),
                  pl.BlockSpec((tt, LANE), lambda b, i: (b * nt + i, 0)),
                  pl.BlockSpec((None, SUBLANE, tt), lambda b, i: (b, 0, i)),
                  const((M_CONV, M_CONV_DIM)), const((1, M_CONV_DIM)), const((1, LANE)), const((1, LANE)),
                  const((SUBLANE, 1)), const((SUBLANE, 1)), const((LANE, M_DIN)), const((M_DIN, LANE)),
                  const((1, M_DIN)), const((1, M_DIN))],
        out_specs=[pl.BlockSpec((tt, M_DIN), lambda b, i: (b * nt + i, 0)),
                   pl.BlockSpec((None, M_DIN, M_N), lambda b, i: (b, 0, 0))],
        out_shape=[jax.ShapeDtypeStruct((bsz * seq, M_DIN), BF16),
                   jax.ShapeDtypeStruct((bsz, M_DIN, M_N), F32)],
        scratch_shapes=[pltpu.VMEM((M_DIN, M_N), F32)],
        compiler_params=_cp("arbitrary", "arbitrary"),
        name="ssd_prompt",
    )(m, m, misc, dtt, cw, cb, dtb_l, nega_l, dtb_c, nega_c, e4, e4t, md, gn)


def _step_kernel(gla_ref, la_ref, v4_ref, gg4_ref, s0_ref, m_ref, buf_ref, misc_ref, h0_ref,
                 cw_ref, cb_ref, dtb_ref, nega_ref, e4_ref, md_ref, gng_ref, gnm_ref,
                 g_ref, s_ref, y_ref, h_ref):
    q_col = _row_to_col(gla_ref[:, 0:GLA_DQK] * (GLA_DK ** -0.5))
    k_col = _row_to_col(gla_ref[:, GLA_DQK:2 * GLA_DQK])
    dec_col = _row_to_col(jnp.exp(la_ref[...]))
    v4 = v4_ref[...]
    v_exp = jnp.concatenate([jnp.broadcast_to(v4[h:h + 1, :], (GLA_DK, GLA_DV)) for h in range(GLA_HEADS)],
                            axis=0)
    s = s0_ref[...] * dec_col + k_col * v_exp
    s_ref[...] = s
    o4 = jnp.sum((q_col * s).reshape(GLA_HEADS, GLA_DK, GLA_DV), axis=1)
    g_ref[...] = _rms_rows(o4) * gng_ref[...] * _silu(gg4_ref[...])

    z = m_ref[:, 0:M_DIN]
    acc = cb_ref[...] + cw_ref[M_CONV - 1:M_CONV, :] * m_ref[:, M_DIN:M_DIN + M_CONV_DIM]
    for s_ in range(M_CONV - 1):
        acc = acc + cw_ref[s_:s_ + 1, :] * buf_ref[s_:s_ + 1, :]
    xc = _silu(acc)
    x = xc[:, 0:M_DIN]
    bm = xc[:, M_DIN:M_DIN + M_GROUPS * M_N]
    cm = xc[:, M_DIN + M_GROUPS * M_N:]
    dt, a = _ssm_params(misc_ref[...], dtb_ref[...], nega_ref[...])
    e4 = e4_ref[...]
    xdt_col = _row_to_col(x * _dot_sel(dt, e4))
    deca_col = _row_to_col(_dot_sel(jnp.exp(a), e4))
    half = M_DIN // M_GROUPS
    spread = lambda t: jnp.concatenate(
        [jnp.broadcast_to(t[:, gi * M_N:(gi + 1) * M_N], (half, M_N)) for gi in range(M_GROUPS)], axis=0)
    hs = h0_ref[...] * deca_col + xdt_col * spread(bm)
    h_ref[...] = hs
    y = _col_to_row(jnp.sum(hs * spread(cm), axis=1, keepdims=True))
    y = (y + md_ref[...] * x) * _silu(z)
    y = jnp.concatenate([_rms_rows(y[:, gi * half:(gi + 1) * half]) for gi in range(M_GROUPS)], axis=1)
    y_ref[...] = y * gnm_ref[...]


def _step_mixers(gla, la, v4, gg4, s0, m, buf, misc, h0, cw, cb, dtb, nega, e4, md, gng, gnm):
    nb = gla.shape[0]
    per = lambda *shape: pl.BlockSpec((None,) + shape, lambda b: (b,) + (0,) * len(shape))
    const = lambda shape: pl.BlockSpec(shape, lambda b: (0,) * len(shape))
    return pl.pallas_call(
        _step_kernel,
        grid=(nb,),
        in_specs=[per(1, 768), per(1, LANE), per(GLA_HEADS, GLA_DV), per(GLA_HEADS, GLA_DV),
                  per(GLA_DQK, GLA_DV), per(1, 1024), per(M_CONV - 1, M_CONV_DIM), per(1, LANE),
                  per(M_DIN, M_N),
                  const((M_CONV, M_CONV_DIM)), const((1, M_CONV_DIM)), const((1, LANE)), const((1, LANE)),
                  const((LANE, M_DIN)), const((1, M_DIN)), const((1, GLA_DV)), const((1, M_DIN))],
        out_specs=[per(GLA_HEADS, GLA_DV), per(GLA_DQK, GLA_DV), per(1, M_DIN), per(M_DIN, M_N)],
        out_shape=[jax.ShapeDtypeStruct((nb, GLA_HEADS, GLA_DV), F32),
                   jax.ShapeDtypeStruct((nb, GLA_DQK, GLA_DV), F32),
                   jax.ShapeDtypeStruct((nb, 1, M_DIN), F32),
                   jax.ShapeDtypeStruct((nb, M_DIN, M_N), F32)],
        compiler_params=_cp("arbitrary"),
        name="step_mixers",
    )(gla.reshape(nb, 1, 768), la.reshape(nb, 1, LANE), v4, gg4, s0, m.reshape(nb, 1, 1024), buf,
      misc.reshape(nb, 1, LANE), h0, cw, cb, dtb, nega, e4, md, gng, gnm)


SB_PP = 4


def _sb_decode_kernel(pt_ref, q_ref, *refs):
    ks = refs[0:SB_PP]
    vs = refs[SB_PP:2 * SB_PP]
    bias_ref, uo_ref, hm_ref, eseg_ref, gn_ref, o_ref, acc_ref, car_ref = refs[2 * SB_PP:]
    j = pl.program_id(1)

    @pl.when(j == 0)
    def _():
        acc_ref[...] = jnp.zeros_like(acc_ref)
        car_ref[...] = jnp.zeros_like(car_ref)

    hm = hm_ref[...]
    qbd = (q_ref[...].astype(F32) * hm).astype(BF16)
    uo = uo_ref[...]
    for r in range(SB_PP - 1, -1, -1):
        z = _dot_nt(qbd, ks[r][...]) + bias_ref[...]
        l1p = jnp.log1p(jnp.exp(-jnp.abs(z)))
        sp = jnp.maximum(z, 0.0) + l1p
        ls = jnp.minimum(z, 0.0) - l1p
        c = jnp.dot(sp.astype(BF16), uo, preferred_element_type=F32)
        car = car_ref[...]
        w = jnp.exp(ls - (c[:, :LANE] + car))
        car_ref[...] = car + c[:, LANE:]
        acc_ref[...] += _dot(w, vs[r][...])

    @pl.when(j == pl.num_programs(1) - 1)
    def _():
        o = jnp.sum(acc_ref[...] * hm, axis=0, keepdims=True)
        o_ref[...] = _seg_rms(o, eseg_ref[...], SB_HD) * gn_ref[...]


def _sb_decode(page_table, q, cache_k, cache_v, layer, bias_col, uo, hm, eseg, gn):
    nb, n_pages = page_table.shape
    page = cache_k.shape[2]
    ng = n_pages // SB_PP
    pt = page_table.reshape(-1)

    def kv_spec(r):
        return pl.BlockSpec((None, None, page, SB_W),
                            lambda b, j, pt_ref: (layer, pt_ref[b * n_pages + (ng - 1 - j) * SB_PP + r], 0, 0))

    const = lambda shape: pl.BlockSpec(shape, lambda b, j, pt_ref: (0,) * len(shape))
    grid_spec = pltpu.PrefetchScalarGridSpec(
        num_scalar_prefetch=1,
        grid=(nb, ng),
        in_specs=[pl.BlockSpec((None, 1, SB_W), lambda b, j, pt_ref: (b, 0, 0))]
        + [kv_spec(r) for r in range(SB_PP)] + [kv_spec(r) for r in range(SB_PP)]
        + [const((SB_HEADS, 1)), const((LANE, 2 * LANE)), const((SB_HEADS, SB_W)), const((SB_W, SB_W)),
           const((1, SB_W))],
        out_specs=pl.BlockSpec((None, 1, SB_W), lambda b, j, pt_ref: (b, 0, 0)),
        scratch_shapes=[pltpu.VMEM((SB_HEADS, SB_W), F32), pltpu.VMEM((SB_HEADS, LANE), F32)],
    )
    out = pl.pallas_call(
        _sb_decode_kernel,
        grid_spec=grid_spec,
        out_shape=jax.ShapeDtypeStruct((nb, 1, SB_W), F32),
        compiler_params=_cp("arbitrary", "arbitrary"),
        name="sb_decode",
    )(pt, q.reshape(nb, 1, SB_W), *([cache_k] * SB_PP), *([cache_v] * SB_PP), bias_col, uo, hm, eseg, gn)
    return out.reshape(nb, SB_W)


def _outproj_kernel(x_ref, gate_ref, g_ref, s_ref, y_ref, w_ref, o_ref):
    mix = (_dot(g_ref[...], w_ref[0:GLA_DVW, :]) + _dot(s_ref[...], w_ref[GLA_DVW:GLA_DVW + SB_W, :])
           + _dot(y_ref[...], w_ref[GLA_DVW + SB_W:, :]))
    o_ref[...] = x_ref[...] + gate_ref[...] * mix


def _out_proj(x, gate, g, s, y, w, *, tm, rows_per_mod):
    r, d = x.shape
    mrows = gate.shape[1]
    row = lambda n: pl.BlockSpec((tm, n), lambda i: (i, 0))
    return pl.pallas_call(
        _outproj_kernel,
        grid=(r // tm,),
        in_specs=[row(d), pl.BlockSpec((None, mrows, d), lambda i: (i // rows_per_mod, 0, 0)),
                  row(GLA_DVW), row(SB_W), row(M_DIN), pl.BlockSpec(w.shape, lambda i: (0, 0))],
        out_specs=row(d),
        out_shape=jax.ShapeDtypeStruct((r, d), F32),
        compiler_params=_cp("arbitrary"),
        name="out_proj",
    )(x, gate, g, s, y, w)


def _ffn_up_kernel(x_ref, sc_ref, sh_ref, g_ref, w_ref, u_ref):
    h = _norm_mod(x_ref[...], g_ref[...], sc_ref[...], sh_ref[...])
    u_ref[...] = jnp.dot(h.astype(BF16), w_ref[...], preferred_element_type=F32)


def _ffn_up(x, sc, sh, g, w, *, tm, rows_per_mod):
    r, d = x.shape
    n = w.shape[1]
    mrows = sc.shape[1]
    mod_spec = pl.BlockSpec((None, mrows, d), lambda i: (i // rows_per_mod, 0, 0))
    return pl.pallas_call(
        _ffn_up_kernel,
        grid=(r // tm,),
        in_specs=[pl.BlockSpec((tm, d), lambda i: (i, 0)), mod_spec, mod_spec,
                  pl.BlockSpec((1, d), lambda i: (0, 0)), pl.BlockSpec((d, n), lambda i: (0, 0))],
        out_specs=pl.BlockSpec((tm, n), lambda i: (i, 0)),
        out_shape=jax.ShapeDtypeStruct((r, n), F32),
        compiler_params=_cp("arbitrary"),
        name="ffn_up",
    )(x, sc, sh, g, w)


FFN_CK = 256


def _ffn_act_down(taps, cw_ref, cb_ref, w_ref):
    dff = w_ref.shape[0]

    def conv(c0):
        u, u1, u2 = taps(c0)
        cs = slice(c0, c0 + FFN_CK)
        return cb_ref[:, cs] + cw_ref[0:1, cs] * u2 + cw_ref[1:2, cs] * u1 + cw_ref[2:3, cs] * u

    out = None
    for c0 in range(0, dff, FFN_CK):
        part = _dot(_silu(conv(c0)) * conv(dff + c0), w_ref[c0:c0 + FFN_CK, :])
        out = part if out is None else out + part
    return out


def _ffn_down_prompt_kernel(x_ref, gate_ref, u_ref, prev_ref, cw_ref, cb_ref, w_ref, o_ref, *, tiles_per_seq):
    i = pl.program_id(0)
    tm = u_ref.shape[0]
    first = i % tiles_per_seq == 0
    row = lax.broadcasted_iota(jnp.int32, (tm, FFN_CK), 0)
    row8 = lax.broadcasted_iota(jnp.int32, (SUBLANE, FFN_CK), 0)

    def taps(c0):
        u = u_ref[:, c0:c0 + FFN_CK]
        prev = jnp.where(first, 0.0, prev_ref[:, c0:c0 + FFN_CK])
        shifted = []
        for s in (1, 2):
            head = jnp.where(row8 < s, pltpu.roll(prev, s, 0), 0.0)
            head = jnp.concatenate([head, jnp.zeros((tm - SUBLANE, FFN_CK), F32)], axis=0)
            shifted.append(jnp.where(row < s, head, pltpu.roll(u, s, 0)))
        return u, shifted[0], shifted[1]

    o_ref[...] = x_ref[...] + gate_ref[...] * _ffn_act_down(taps, cw_ref, cb_ref, w_ref)


def _ffn_down_prompt(x, gate, u, cw, cb, w, *, tm, seq):
    r, d = x.shape
    n = u.shape[1]
    tiles = seq // tm
    per8 = tm // SUBLANE
    return pl.pallas_call(
        functools.partial(_ffn_down_prompt_kernel, tiles_per_seq=tiles),
        grid=(r // tm,),
        in_specs=[pl.BlockSpec((tm, d), lambda i: (i, 0)),
                  pl.BlockSpec((None, 1, d), lambda i: (i // tiles, 0, 0)),
                  pl.BlockSpec((tm, n), lambda i: (i, 0)),
                  pl.BlockSpec((SUBLANE, n), lambda i: (jnp.maximum(i * per8 - 1, 0), 0)),
                  pl.BlockSpec((FFN_CONV, n), lambda i: (0, 0)), pl.BlockSpec((1, n), lambda i: (0, 0)),
                  pl.BlockSpec(w.shape, lambda i: (0, 0))],
        out_specs=pl.BlockSpec((tm, d), lambda i: (i, 0)),
        out_shape=jax.ShapeDtypeStruct((r, d), F32),
        compiler_params=_cp("arbitrary"),
        name="ffn_down_prompt",
    )(x, gate, u, u, cw, cb, w)


def _ffn_down_step_kernel(x_ref, gate_ref, u_ref, buf_ref, cw_ref, cb_ref, w_ref, o_ref):
    taps = lambda c0: (u_ref[:, c0:c0 + FFN_CK], buf_ref[1, :, c0:c0 + FFN_CK], buf_ref[0, :, c0:c0 + FFN_CK])
    o_ref[...] = x_ref[...] + gate_ref[...] * _ffn_act_down(taps, cw_ref, cb_ref, w_ref)


def _ffn_down_step(x, gate, u, buf_t, cw, cb, w):
    full = lambda a: pl.BlockSpec(a.shape, lambda i: (0,) * a.ndim)
    args = (x, gate, u, buf_t, cw, cb, w)
    return pl.pallas_call(
        _ffn_down_step_kernel,
        grid=(1,),
        in_specs=[full(a) for a in args],
        out_specs=full(x),
        out_shape=jax.ShapeDtypeStruct(x.shape, F32),
        compiler_params=_cp("arbitrary"),
        name="ffn_down_step",
    )(*args)


def _same_segment(n, width):
    i = np.arange(n)
    return jnp.asarray((i[:, None] // width) == (i[None, :] // width), BF16)


def _constants():
    k = np.arange(GLA_DQK)
    v = np.arange(GLA_DVW)
    eexp = (k[:, None] // GLA_DK) == (v[None, :] // GLA_DV)
    j = np.arange(SB_TK)
    uo = np.concatenate([j[:, None] > j[None, :], np.ones((SB_TK, SB_TK), bool)], axis=1)
    lane = np.arange(LANE)
    hm_pair = np.zeros((SUBLANE, LANE), np.float32)
    hm_pair[0] = lane < SB_HD
    hm_pair[1] = lane >= SB_HD
    w = np.arange(SB_W)
    hm_all = (w[None, :] // SB_HD) == np.arange(SB_HEADS)[:, None]
    p = np.arange(M_DIN)
    e4 = lane[:, None] == (p[None, :] // M_HD)
    return dict(
        eexp=jnp.asarray(eexp, BF16), bd=jnp.asarray(eexp.T, F32), eseg_gla=_same_segment(GLA_DVW, GLA_DV),
        uo=jnp.asarray(uo, BF16), hm_pair=jnp.asarray(hm_pair, BF16), hm_all=jnp.asarray(hm_all, F32),
        eseg_pair=_same_segment(LANE, SB_HD), eseg_sb=_same_segment(SB_W, SB_HD),
        e4=jnp.asarray(e4, BF16), e4t=jnp.asarray(e4.T, BF16))


def _pad_lanes(v, n=LANE):
    return jnp.zeros((1, n), F32).at[0, :v.shape[0]].set(v)


def _pad_col(v, n=SUBLANE):
    return jnp.zeros((n, 1), F32).at[:v.shape[0], 0].set(v)


def kernel(x_prompt, x_sample, cache_sb_k, cache_sb_v, state_gla, state_mamba_conv, state_mamba_ssm, state_ffn_conv, page_table, c_prompt, c_sample, norm1_g, w_ada, b_ada, w_in, gla_w_gate2, gla_b_gate, gla_norm_g, sb_q_norm_g, sb_k_norm_g, sb_o_norm_g, sb_bias, m_conv_w, m_conv_b, m_dt_bias, m_a_log, m_d, m_norm_g, w_out, norm2_g, ffn_w_up, ffn_conv_w, ffn_conv_b, ffn_w_down):
    bsz, seq, d = x_prompt.shape
    nb = x_sample.shape[0]
    depth = w_in.shape[0]
    dff = ffn_w_down.shape[1]
    n_pool, page = cache_sb_k.shape[1], cache_sb_k.shape[2]
    assert x_sample.shape[1] == 1 and d % LANE == 0 and seq % 256 == 0
    cst = _constants()

    tm = 256
    tq = 256
    tt_gla = 256
    tt_ssd = 128

    mod = _modulation(jnp.concatenate([c_prompt, c_sample], axis=0), w_ada, b_ada)
    mod = mod.reshape(depth, bsz + nb, N_MOD, d)
    mod_p = mod[:, :bsz].transpose(0, 2, 1, 3).reshape(depth, N_MOD, bsz, 1, d)
    mod_s = mod[:, bsz:].transpose(0, 2, 1, 3).reshape(depth, N_MOD, 1, nb, d)

    cache_k = cache_sb_k.reshape(depth, n_pool, page, SB_W)
    cache_v = cache_sb_v.reshape(depth, n_pool, page, SB_W)

    xp = x_prompt.reshape(bsz * seq, d)
    xs = x_sample.reshape(nb, d)
    outs = {name: [] for name in ("pk", "pv", "pg", "pc", "ph", "pf", "sk", "sv", "sg", "sc", "sh", "sf")}
    for l in range(depth):
        wi = w_in[l]
        o = np.cumsum([0, GLA_DQK, GLA_DQK, GLA_DVW, GLA_DVW, GLA_LR, SB_W, SB_W, SB_W, M_DIN, M_CONV_DIM, M_HEADS])
        sl = lambda a, b: wi[:, o[a]:o[b]]
        w_re = jnp.concatenate(
            [sl(0, 4), sl(5, 8), sl(8, 10), sl(10, 11), sl(4, 5),
             jnp.zeros((d, LANE - M_HEADS - GLA_LR), F32)], axis=1).astype(BF16)
        wg = jnp.zeros((LANE, LANE), F32).at[M_HEADS:M_HEADS + GLA_LR].set(gla_w_gate2[l]).astype(BF16)
        bg = gla_b_gate[l].reshape(1, GLA_DQK)
        gq = jnp.tile(sb_q_norm_g[l], SB_HEADS).reshape(1, SB_W)
        gk = jnp.tile(sb_k_norm_g[l], SB_HEADS).reshape(1, SB_W)
        go = jnp.tile(sb_o_norm_g[l], SB_HEADS).reshape(1, SB_W)
        gng = jnp.tile(gla_norm_g[l], GLA_HEADS).reshape(1, GLA_DVW)
        g1 = norm1_g[l].reshape(1, d)
        g2 = norm2_g[l].reshape(1, d)
        bias_pair = jnp.repeat(sb_bias[l], SB_TK).reshape(SB_HEADS // 2, 1, 2 * SB_TK)
        bias_col = sb_bias[l].reshape(SB_HEADS, 1)
        cw = m_conv_w[l]
        cb = m_conv_b[l].reshape(1, M_CONV_DIM)
        nega = -jnp.exp(m_a_log[l])
        dtb_l, nega_l = _pad_lanes(m_dt_bias[l]), _pad_lanes(nega)
        dtb_c, nega_c = _pad_col(m_dt_bias[l]), _pad_col(nega)
        md = jnp.repeat(m_d[l], M_HD).reshape(1, M_DIN)
        gnm = m_norm_g[l].reshape(1, M_DIN)
        wo = w_out[l].astype(BF16)
        wu = ffn_w_up[l].astype(BF16)
        wd = ffn_w_down[l].astype(BF16)
        fcw = ffn_conv_w[l]
        fcb = ffn_conv_b[l].reshape(1, 2 * dff)
        sh1, sc1, gt1, sh2, sc2, gt2 = range(N_MOD)

        mp = mod_p[l]
        gla, la, q, k, kb, v, vb, m, misc = _in_proj(
            xp, mp[sc1], mp[sh1], g1, w_re, wg, bg, gq, gk, cst["eseg_sb"], tm=tm, rows_per_mod=seq // tm)
        g_mix, g_state = _gla_prompt(gla, la, cst["eexp"], cst["bd"], cst["eseg_gla"], gng,
                                     bsz=bsz, seq=seq, tt=tt_gla)
        s_mix = _sb_prompt(q, kb, vb, bias_pair, cst["uo"], cst["hm_pair"], cst["eseg_pair"], go[:, :LANE],
                           bsz=bsz, seq=seq, tq=tq)
        dtt = jnp.pad(misc[:, :M_HEADS].reshape(bsz, seq, M_HEADS).transpose(0, 2, 1),
                      ((0, 0), (0, SUBLANE - M_HEADS), (0, 0)))
        y_mix, h_state = _ssd_prompt(m, misc, dtt, cw, cb, dtb_l, nega_l, dtb_c, nega_c, cst["e4"], cst["e4t"],
                                     md, gnm, bsz=bsz, seq=seq, tt=tt_ssd)
        xp = _out_proj(xp, mp[gt1], g_mix, s_mix, y_mix, wo, tm=tm, rows_per_mod=seq // tm)
        u = _ffn_up(xp, mp[sc2], mp[sh2], g2, wu, tm=tm, rows_per_mod=seq // tm)
        xp = _ffn_down_prompt(xp, mp[gt2], u, fcw, fcb, wd, tm=tm, seq=seq)

        outs["pk"].append(k.reshape(bsz, seq, SB_HEADS, SB_HD))
        outs["pv"].append(v.reshape(bsz, seq, SB_HEADS, SB_HD))
        gs = g_state.reshape(bsz, GLA_HEADS, GLA_DV, GLA_HEADS, GLA_DK)
        outs["pg"].append(jnp.stack([gs[:, h, :, h, :] for h in range(GLA_HEADS)], axis=1).transpose(0, 1, 3, 2))
        outs["pc"].append(m.reshape(bsz, seq, 1024)[:, seq - (M_CONV - 1):, M_DIN:])
        outs["ph"].append(h_state.reshape(bsz, M_HEADS, M_HD, M_N))
        outs["pf"].append(u.reshape(bsz, seq, 2 * dff)[:, seq - (FFN_CONV - 1):])

        ms = mod_s[l]
        gla, la, q, k, kb, v, vb, m, misc = _in_proj(
            xs, ms[sc1], ms[sh1], g1, w_re, wg, bg, gq, gk, cst["eseg_sb"], tm=nb, rows_per_mod=1)
        g4, g_state, y_mix, h_state = _step_mixers(
            gla, la, gla[:, 2 * GLA_DQK:2 * GLA_DQK + GLA_DVW].reshape(nb, GLA_HEADS, GLA_DV),
            gla[:, 2 * GLA_DQK + GLA_DVW:].reshape(nb, GLA_HEADS, GLA_DV),
            state_gla[l].reshape(nb, GLA_DQK, GLA_DV), m, state_mamba_conv[l], misc,
            state_mamba_ssm[l].reshape(nb, M_DIN, M_N), cw, cb, dtb_l, nega_l, cst["e4"], md,
            gla_norm_g[l].reshape(1, GLA_DV), gnm)
        s_mix = _sb_decode(page_table, q, cache_k, cache_v, l, bias_col, cst["uo"], cst["hm_all"],
                           cst["eseg_sb"], go)
        xs = _out_proj(xs, ms[gt1], g4.reshape(nb, GLA_DVW), s_mix, y_mix.reshape(nb, M_DIN), wo,
                       tm=nb, rows_per_mod=1)
        u = _ffn_up(xs, ms[sc2], ms[sh2], g2, wu, tm=nb, rows_per_mod=1)
        xs = _ffn_down_step(xs, ms[gt2][0], u, state_ffn_conv[l].transpose(1, 0, 2), fcw, fcb, wd)

        outs["sk"].append(k.reshape(nb, 1, SB_HEADS, SB_HD))
        outs["sv"].append(v.reshape(nb, 1, SB_HEADS, SB_HD))
        outs["sg"].append(g_state.reshape(nb, GLA_HEADS, GLA_DK, GLA_DV))
        outs["sc"].append(jnp.concatenate([state_mamba_conv[l][:, 1:], m[:, None, M_DIN:]], axis=1))
        outs["sh"].append(h_state.reshape(nb, M_HEADS, M_HD, M_N))
        outs["sf"].append(jnp.concatenate([state_ffn_conv[l][:, 1:], u[:, None, :]], axis=1))

    st = {name: jnp.stack(v) for name, v in outs.items()}
    return (xp.reshape(bsz, seq, d), xs.reshape(nb, 1, d), st["pk"], st["pv"], st["pg"], st["pc"], st["ph"],
            st["pf"], st["sk"], st["sv"], st["sg"], st["sc"], st["sh"], st["sf"])
```

```python
import functools

import jax
import jax.numpy as jnp
import numpy as np
from jax import lax
from jax.experimental import pallas as pl
from jax.experimental.pallas import tpu as pltpu

F32 = jnp.float32
BF16 = jnp.bfloat16

GLA_HEADS, GLA_DK, GLA_DV, GLA_LR, GLA_TAU = 4, 32, 64, 16, 16.0
GLA_DQK = GLA_HEADS * GLA_DK
GLA_DVW = GLA_HEADS * GLA_DV
SB_HEADS, SB_HD = 8, 64
SB_W = SB_HEADS * SB_HD
M_HEADS, M_HD, M_GROUPS, M_N, M_CONV = 4, 64, 2, 128, 4
M_DIN = M_HEADS * M_HD
M_CONV_DIM = M_DIN + 2 * M_GROUPS * M_N
FFN_CONV = 3
N_MOD = 6
EPS = 1e-6
NEG_BIG = -1e30
LOG2E = 1.4426950408889634

C_GLA = 0
C_SBQ = 768
C_SBK = 1280
C_SBV = 1792
C_MZ = 2304
C_MISC = 3328
D_INP = 3456
LANE = 128
SUBLANE = 8

VMEM_LIMIT = 56 * 1024 * 1024


def _cp(*sem):
    return pltpu.CompilerParams(dimension_semantics=sem, vmem_limit_bytes=VMEM_LIMIT)


def _dot(a, b):
    return jnp.dot(a.astype(BF16), b.astype(BF16), preferred_element_type=F32)


def _dot_nt(a, b):
    return lax.dot_general(a.astype(BF16), b.astype(BF16), (((1,), (1,)), ((), ())),
                           preferred_element_type=F32)


def _dot_tn(a, b):
    return lax.dot_general(a.astype(BF16), b.astype(BF16), (((0,), (0,)), ((), ())),
                           preferred_element_type=F32)


def _split3(x):
    hi = x.astype(BF16)
    r = x - hi.astype(F32)
    mid = r.astype(BF16)
    lo = (r - mid.astype(F32)).astype(BF16)
    return hi, mid, lo


def _dot_sel(x, e):
    hi, mid, lo = _split3(x)
    d = lambda p: jnp.dot(p, e, preferred_element_type=F32)
    return d(hi) + d(mid) + d(lo)


def _dot_sel_nt(e, x):
    hi, mid, lo = _split3(x)
    d = lambda p: lax.dot_general(e, p, (((1,), (1,)), ((), ())), preferred_element_type=F32)
    return d(hi) + d(mid) + d(lo)


def _sigmoid(x):
    return 1.0 / (1.0 + jnp.exp(-x))


def _silu(x):
    return x * _sigmoid(x)


def _softplus(x):
    return jnp.maximum(x, 0.0) + jnp.log1p(jnp.exp(-jnp.abs(x)))


def _log_sigmoid(x):
    return jnp.minimum(x, 0.0) - jnp.log1p(jnp.exp(-jnp.abs(x)))


def _rms_rows(x):
    return x * lax.rsqrt(jnp.mean(x * x, axis=-1, keepdims=True) + EPS)


def _seg_rms(x, eseg, width):
    ms = _dot_sel(x * x, eseg) * (1.0 / width)
    return x * lax.rsqrt(ms + EPS)


def _row_to_col(row):
    n = row.shape[1]
    eye = lax.broadcasted_iota(jnp.int32, (n, n), 0) == lax.broadcasted_iota(jnp.int32, (n, n), 1)
    return jnp.sum(jnp.where(eye, row, 0.0), axis=1, keepdims=True)


def _col_to_row(col):
    n = col.shape[0]
    eye = lax.broadcasted_iota(jnp.int32, (n, n), 0) == lax.broadcasted_iota(jnp.int32, (n, n), 1)
    return jnp.sum(jnp.where(eye, col, 0.0), axis=0, keepdims=True)


def _cumsum(x, axis, seg=None):
    n = x.shape[axis] if seg is None else seg
    idx = lax.broadcasted_iota(jnp.int32, x.shape, axis)
    if seg is not None:
        idx = idx % seg
    s = 1
    while s < n:
        x = x + jnp.where(idx >= s, pltpu.roll(x, s, axis), 0.0)
        s *= 2
    return x


def _mod_kernel(c_ref, w_ref, b_ref, o_ref):
    o_ref[...] = _dot(_silu(c_ref[...]), w_ref[...]) + b_ref[...]


def _modulation(c_all, w_ada, b_ada):
    depth, d, nd = w_ada.shape
    n = c_all.shape[0]
    tn = 1024
    return pl.pallas_call(
        _mod_kernel,
        grid=(depth, nd // tn),
        in_specs=[pl.BlockSpec((n, d), lambda l, j: (0, 0)),
                  pl.BlockSpec((None, d, tn), lambda l, j: (l, 0, j)),
                  pl.BlockSpec((None, 1, tn), lambda l, j: (l, 0, j))],
        out_specs=pl.BlockSpec((None, n, tn), lambda l, j: (l, 0, j)),
        out_shape=jax.ShapeDtypeStruct((depth, n, nd), F32),
        compiler_params=_cp("arbitrary", "arbitrary"),
        name="adaln_mod",
    )(c_all, w_ada, b_ada.reshape(depth, 1, nd))


def _norm_mod(x, g, sc, sh):
    return _rms_rows(x) * g * (1.0 + sc) + sh


def _inproj_kernel(x_ref, sc_ref, sh_ref, g_ref, w_ref, wg_ref, bg_ref, gq_ref, gk_ref, eseg_ref,
                   gla_ref, la_ref, q_ref, k_ref, kb_ref, v_ref, vb_ref, m_ref, misc_ref):
    h = _norm_mod(x_ref[...], g_ref[...], sc_ref[...], sh_ref[...])
    p = jnp.dot(h.astype(BF16), w_ref[...], preferred_element_type=F32)
    gla_ref[...] = p[:, C_GLA:C_SBQ]
    misc = p[:, C_MISC:D_INP]
    misc_ref[...] = misc
    la_ref[...] = _log_sigmoid(_dot(misc, wg_ref[...]) + bg_ref[...]) * (1.0 / GLA_TAU)
    eseg = eseg_ref[...]
    q = _seg_rms(p[:, C_SBQ:C_SBK], eseg, SB_HD) * gq_ref[...]
    q_ref[...] = (q * (SB_HD ** -0.5 * LOG2E)).astype(BF16)
    k = _seg_rms(p[:, C_SBK:C_SBV], eseg, SB_HD) * gk_ref[...]
    k_ref[...] = k
    kb_ref[...] = k.astype(BF16)
    v = p[:, C_SBV:C_MZ]
    v_ref[...] = v
    vb_ref[...] = v.astype(BF16)
    m_ref[...] = p[:, C_MZ:C_MISC]


def _in_proj(x, sc, sh, g, w, wg, bg, gq, gk, eseg, *, tm, rows_per_mod):
    r, d = x.shape
    mrows = sc.shape[1]
    mod_spec = pl.BlockSpec((None, mrows, d), lambda i: (i // rows_per_mod, 0, 0))
    const = lambda shape: pl.BlockSpec(shape, lambda i: (0,) * len(shape))
    row = lambda n: pl.BlockSpec((tm, n), lambda i: (i, 0))
    outs = [(768, F32), (LANE, F32), (SB_W, BF16), (SB_W, F32), (SB_W, BF16), (SB_W, F32), (SB_W, BF16),
            (1024, F32), (LANE, F32)]
    return pl.pallas_call(
        _inproj_kernel,
        grid=(r // tm,),
        in_specs=[row(d), mod_spec, mod_spec, const((1, d)), const((d, D_INP)), const((LANE, LANE)),
                  const((1, LANE)), const((1, SB_W)), const((1, SB_W)), const((SB_W, SB_W))],
        out_specs=[row(n) for n, _ in outs],
        out_shape=[jax.ShapeDtypeStruct((r, n), dt) for n, dt in outs],
        compiler_params=_cp("arbitrary"),
        name="in_proj",
    )(x, sc, sh, g, w, wg, bg, gq, gk, eseg)


GLA_C = SUBLANE


def _gla_prompt_kernel(gla_ref, la_ref, eexp_ref, bd_ref, eseg_ref, gn_ref, o_ref, st_out_ref,
                       st_ref, b_ref, acc_ref):
    i = pl.program_id(1)
    tt = la_ref.shape[0]

    @pl.when(i == 0)
    def _():
        st_ref[...] = jnp.zeros_like(st_ref)

    b_ref[...] = _cumsum(la_ref[...], 0, seg=GLA_C)
    eexp = eexp_ref[...]
    bd = bd_ref[...]
    rowid = lax.broadcasted_iota(jnp.int32, (GLA_C, GLA_DQK), 0)
    rowid_v = lax.broadcasted_iota(jnp.int32, (GLA_C, GLA_DVW), 0)

    def chunk(c, carry):
        r0 = pl.multiple_of(c * GLA_C, GLA_C)
        bc = b_ref[pl.ds(r0, GLA_C), :]
        qc = gla_ref[pl.ds(r0, GLA_C), 0:GLA_DQK] * (GLA_DK ** -0.5)
        kc = gla_ref[pl.ds(r0, GLA_C), GLA_DQK:2 * GLA_DQK]
        vc = gla_ref[pl.ds(r0, GLA_C), 2 * GLA_DQK:2 * GLA_DQK + GLA_DVW]
        st = st_ref[...]
        o = _dot_nt(qc * jnp.exp(bc), st)
        for t in range(GLA_C):
            d = jnp.where(rowid <= t, bc[t:t + 1, :] - bc, NEG_BIG)
            p = jnp.exp(d) * (qc[t:t + 1, :] * kc)
            sc = _dot(p, eexp)
            ot = jnp.sum(sc * vc, axis=0, keepdims=True)
            o = o + jnp.where(rowid_v == t, ot, 0.0)
        bl = bc[GLA_C - 1:GLA_C, :]
        upd = _dot_tn(vc, kc * jnp.exp(bl - bc))
        st_ref[...] = st * jnp.exp(bl) + upd * bd
        acc_ref[pl.ds(r0, GLA_C), :] = o
        return carry

    lax.fori_loop(0, tt // GLA_C, chunk, 0)
    gg = gla_ref[:, 2 * GLA_DQK + GLA_DVW:2 * GLA_DQK + 2 * GLA_DVW]
    o = _seg_rms(acc_ref[...], eseg_ref[...], GLA_DV) * gn_ref[...] * _silu(gg)
    o_ref[...] = o.astype(BF16)
    st_out_ref[...] = st_ref[...]


def _gla_prompt(gla, la, eexp, bd, eseg, gn, *, bsz, seq, tt):
    nt = seq // tt
    const = lambda shape: pl.BlockSpec(shape, lambda b, i: (0,) * len(shape))
    return pl.pallas_call(
        _gla_prompt_kernel,
        grid=(bsz, nt),
        in_specs=[pl.BlockSpec((tt, 768), lambda b, i: (b * nt + i, 0)),
                  pl.BlockSpec((tt, LANE), lambda b, i: (b * nt + i, 0)),
                  const((GLA_DQK, GLA_DVW)), const((GLA_DVW, GLA_DQK)), const((GLA_DVW, GLA_DVW)),
                  const((1, GLA_DVW))],
        out_specs=[pl.BlockSpec((tt, GLA_DVW), lambda b, i: (b * nt + i, 0)),
                   pl.BlockSpec((None, GLA_DVW, GLA_DQK), lambda b, i: (b, 0, 0))],
        out_shape=[jax.ShapeDtypeStruct((bsz * seq, GLA_DVW), BF16),
                   jax.ShapeDtypeStruct((bsz, GLA_DVW, GLA_DQK), F32)],
        scratch_shapes=[pltpu.VMEM((GLA_DVW, GLA_DQK), F32), pltpu.VMEM((tt, GLA_DQK), F32),
                        pltpu.VMEM((tt, GLA_DVW), F32)],
        compiler_params=_cp("arbitrary", "arbitrary"),
        name="gla_prompt",
    )(gla, la, eexp, bd, eseg, gn)


SB_TK = LANE
SB_UNROLL = 4


def _neg_abs(x):
    return lax.bitcast_convert_type(lax.bitcast_convert_type(x, jnp.int32) | jnp.int32(-2 ** 31), F32)


def _sb_prompt_kernel(q_ref, k_ref, v_ref, bias_ref, uo_ref, hm_ref, eseg_ref, gn_ref, o_ref,
                      acc_ref, car_ref, kk_ref, vv_ref):
    i = pl.program_id(2)
    tq = q_ref.shape[0]
    ndiag = tq // SB_TK
    nblk = kk_ref.shape[0]

    @pl.when(i == 0)
    def _():
        m0 = hm_ref[0:1, :]
        m1 = hm_ref[1:2, :]

        def fill(j, carry):
            k0 = pl.multiple_of(j * SB_TK, SB_TK)
            kb = k_ref[pl.ds(k0, SB_TK), :]
            vb = v_ref[pl.ds(k0, SB_TK), :]
            kk_ref[j] = jnp.concatenate([kb * m0, kb * m1], axis=0)
            vv_ref[j] = jnp.concatenate([vb * m0, vb * m1], axis=0)
            return carry

        lax.fori_loop(0, nblk, fill, 0)

    bias = bias_ref[...]
    uo = uo_ref[...]
    acc_ref[...] = jnp.zeros_like(acc_ref)
    car_ref[...] = jnp.zeros_like(car_ref)

    def block(j, r0, diag):
        rows = tq - r0
        z = lax.dot_general(q_ref[r0:, :], kk_ref[j], (((1,), (1,)), ((), ())),
                            preferred_element_type=F32) + bias
        sp = jnp.maximum(z, 0.0) + jnp.log(1.0 + jnp.exp2(_neg_abs(z))) * LOG2E
        ls = z - sp
        if diag:
            valid = (lax.broadcasted_iota(jnp.int32, (rows, 2 * SB_TK), 1) % SB_TK
                     < lax.broadcasted_iota(jnp.int32, (rows, 2 * SB_TK), 0))
            sp = jnp.where(valid, sp, 0.0)
        later = jnp.dot(sp.astype(BF16), uo, preferred_element_type=F32) + car_ref[r0:, :]
        w = jnp.exp2(ls - later)
        if diag:
            w = jnp.where(valid, w, 0.0)
        car_ref[r0:, :SB_TK] += jnp.sum(sp[:, :SB_TK], axis=1, keepdims=True)
        car_ref[r0:, SB_TK:] += jnp.sum(sp[:, SB_TK:], axis=1, keepdims=True)
        acc_ref[r0:, :] += jnp.dot(w.astype(BF16), vv_ref[j], preferred_element_type=F32)

    for c in range(ndiag - 1, -1, -1):
        block(i * ndiag + c, c * SB_TK, True)

    def body(jj, carry):
        for u in range(SB_UNROLL):
            block(i * ndiag - 1 - jj * SB_UNROLL - u, 0, False)
        return carry

    lax.fori_loop(0, (i * ndiag) // SB_UNROLL, body, 0)
    o_ref[...] = (_seg_rms(acc_ref[...], eseg_ref[...], SB_HD) * gn_ref[...]).astype(BF16)


def _sb_prompt(q, k, v, bias2, uo, hm, eseg, gn, *, bsz, seq, tq):
    nq = seq // tq
    npair = SB_HEADS // 2
    const = lambda shape: pl.BlockSpec(shape, lambda b, p, i: (0,) * len(shape))
    return pl.pallas_call(
        _sb_prompt_kernel,
        grid=(bsz, npair, nq),
        in_specs=[pl.BlockSpec((tq, LANE), lambda b, p, i: (b * nq + i, p)),
                  pl.BlockSpec((seq, LANE), lambda b, p, i: (b, p)),
                  pl.BlockSpec((seq, LANE), lambda b, p, i: (b, p)),
                  pl.BlockSpec((None, 1, 2 * SB_TK), lambda b, p, i: (p, 0, 0)),
                  const((2 * SB_TK, 2 * SB_TK)), const((SUBLANE, LANE)), const((LANE, LANE)), const((1, LANE))],
        out_specs=pl.BlockSpec((tq, LANE), lambda b, p, i: (b * nq + i, p)),
        out_shape=jax.ShapeDtypeStruct((bsz * seq, SB_W), BF16),
        scratch_shapes=[pltpu.VMEM((tq, LANE), F32), pltpu.VMEM((tq, 2 * SB_TK), F32),
                        pltpu.VMEM((seq // SB_TK, 2 * SB_TK, LANE), BF16),
                        pltpu.VMEM((seq // SB_TK, 2 * SB_TK, LANE), BF16)],
        compiler_params=_cp("arbitrary", "arbitrary", "arbitrary"),
        name="sb_prompt",
    )(q, k, v, bias2, uo, hm, eseg, gn)


def _ssm_params(dt_raw, dtb, nega):
    dt = _softplus(dt_raw + dtb)
    return dt, dt * nega


def _ssd_prompt_kernel(m_ref, prev_ref, misc_ref, dtt_ref, cw_ref, cb_ref, dtb_l_ref, nega_l_ref,
                       dtb_c_ref, nega_c_ref, e4_ref, e4t_ref, md_ref, gn_ref, o_ref, hs_out_ref,
                       hs_ref):
    i = pl.program_id(1)
    tt = m_ref.shape[0]

    @pl.when(i == 0)
    def _():
        hs_ref[...] = jnp.zeros_like(hs_ref)

    z = m_ref[:, 0:M_DIN]
    xbc = m_ref[:, M_DIN:M_DIN + M_CONV_DIM]
    prev = jnp.where(i > 0, prev_ref[:, M_DIN:M_DIN + M_CONV_DIM], 0.0)
    row = lax.broadcasted_iota(jnp.int32, (tt, M_CONV_DIM), 0)
    row8 = lax.broadcasted_iota(jnp.int32, (SUBLANE, M_CONV_DIM), 0)
    acc = cb_ref[...] + cw_ref[M_CONV - 1:M_CONV, :] * xbc
    for s in range(1, M_CONV):
        head = jnp.where(row8 < s, pltpu.roll(prev, s, 0), 0.0)
        head = jnp.concatenate([head, jnp.zeros((tt - SUBLANE, M_CONV_DIM), F32)], axis=0)
        shifted = jnp.where(row < s, head, pltpu.roll(xbc, s, 0))
        acc = acc + cw_ref[M_CONV - 1 - s:M_CONV - s, :] * shifted
    xc = _silu(acc)
    x = xc[:, 0:M_DIN]
    bm = xc[:, M_DIN:M_DIN + M_GROUPS * M_N]
    cm = xc[:, M_DIN + M_GROUPS * M_N:]

    e4 = e4_ref[...]
    dt_c, a_c = _ssm_params(misc_ref[...], dtb_l_ref[...], nega_l_ref[...])
    cs_c = _cumsum(a_c, 0)
    dt_r, a_r = _ssm_params(dtt_ref[...], dtb_c_ref[...], nega_c_ref[...])
    cs_r = _cumsum(a_r, 1)
    xdt = x * _dot_sel(dt_c, e4)
    ecs = _dot_sel(jnp.exp(cs_c), e4)

    g = [_dot_nt(cm[:, gi * M_N:(gi + 1) * M_N], bm[:, gi * M_N:(gi + 1) * M_N]) for gi in range(M_GROUPS)]
    causal = lax.broadcasted_iota(jnp.int32, (tt, tt), 0) >= lax.broadcasted_iota(jnp.int32, (tt, tt), 1)
    lane_head = lax.broadcasted_iota(jnp.int32, (tt, M_DIN), 1) // M_HD
    rep = M_HEADS // M_GROUPS
    y = jnp.zeros((tt, M_DIN), F32)
    for h in range(M_HEADS):
        decay = jnp.exp(jnp.where(causal, cs_c[:, h:h + 1] - cs_r[h:h + 1, :], NEG_BIG))
        yh = _dot(g[h // rep] * decay, xdt)
        y = jnp.where(lane_head == h, yh, y)

    hs = hs_ref[...]
    yi = [_dot_nt(cm[:, gi * M_N:(gi + 1) * M_N], hs) for gi in range(M_GROUPS)]
    y = y + jnp.where(lane_head < rep, yi[0], yi[1]) * ecs

    cs_last = cs_c[tt - 1:tt, :]
    xw = xdt * _dot_sel(jnp.exp(cs_last - cs_c), e4)
    zz = _dot_tn(xw, bm)
    half = M_DIN // M_GROUPS
    upd = jnp.concatenate([zz[:half, :M_N], zz[half:, M_N:]], axis=0)
    dec = _dot_sel_nt(e4t_ref[...], jnp.broadcast_to(jnp.exp(cs_last), (M_N, LANE)))
    hs_ref[...] = hs * dec + upd
    hs_out_ref[...] = hs_ref[...]

    y = (y + md_ref[...] * x) * _silu(z)
    y = jnp.concatenate([_rms_rows(y[:, gi * half:(gi + 1) * half]) for gi in range(M_GROUPS)], axis=1)
    o_ref[...] = (y * gn_ref[...]).astype(BF16)


def _ssd_prompt(m, misc, dtt, cw, cb, dtb_l, nega_l, dtb_c, nega_c, e4, e4t, md, gn, *, bsz, seq, tt):
    nt = seq // tt
    per8 = tt // SUBLANE
    const = lambda shape: pl.BlockSpec(shape, lambda b, i: (0,) * len(shape))
    return pl.pallas_call(
        _ssd_prompt_kernel,
        grid=(bsz, nt),
        in_specs=[pl.BlockSpec((tt, 1024), lambda b, i: (b * nt + i, 0)),
                  pl.BlockSpec((SUBLANE, 1024), lambda b, i: (jnp.maximum((b * nt + i) * per8 - 1, 0), 0)),
                  pl.BlockSpec((tt, LANE), lambda b, i: (b * nt + i, 0)),
                  pl.BlockSpec((None, SUBLANE, tt), lambda b, i: (b, 0, i)),
                  const((M_CONV, M_CONV_DIM)), const((1, M_CONV_DIM)), const((1, LANE)), const((1, LANE)),
                  const((SUBLANE, 1)), const((SUBLANE, 1)), const((LANE, M_DIN)), const((M_DIN, LANE)),
                  const((1, M_DIN)), const((1, M_DIN))],
        out_specs=[pl.BlockSpec((tt, M_DIN), lambda b, i: (b * nt + i, 0)),
                   pl.BlockSpec((None, M_DIN, M_N), lambda b, i: (b, 0, 0))],
        out_shape=[jax.ShapeDtypeStruct((bsz * seq, M_DIN), BF16),
                   jax.ShapeDtypeStruct((bsz, M_DIN, M_N), F32)],
        scratch_shapes=[pltpu.VMEM((M_DIN, M_N), F32)],
        compiler_params=_cp("arbitrary", "arbitrary"),
        name="ssd_prompt",
    )(m, m, misc, dtt, cw, cb, dtb_l, nega_l, dtb_c, nega_c, e4, e4t, md, gn)


def _step_kernel(gla_ref, la_ref, v4_ref, gg4_ref, s0_ref, m_ref, buf_ref, misc_ref, h0_ref,
                 cw_ref, cb_ref, dtb_ref, nega_ref, e4_ref, md_ref, gng_ref, gnm_ref,
                 g_ref, s_ref, y_ref, h_ref):
    q_col = _row_to_col(gla_ref[:, 0:GLA_DQK] * (GLA_DK ** -0.5))
    k_col = _row_to_col(gla_ref[:, GLA_DQK:2 * GLA_DQK])
    dec_col = _row_to_col(jnp.exp(la_ref[...]))
    v4 = v4_ref[...]
    v_exp = jnp.concatenate([jnp.broadcast_to(v4[h:h + 1, :], (GLA_DK, GLA_DV)) for h in range(GLA_HEADS)],
                            axis=0)
    s = s0_ref[...] * dec_col + k_col * v_exp
    s_ref[...] = s
    o4 = jnp.sum((q_col * s).reshape(GLA_HEADS, GLA_DK, GLA_DV), axis=1)
    g_ref[...] = _rms_rows(o4) * gng_ref[...] * _silu(gg4_ref[...])

    z = m_ref[:, 0:M_DIN]
    acc = cb_ref[...] + cw_ref[M_CONV - 1:M_CONV, :] * m_ref[:, M_DIN:M_DIN + M_CONV_DIM]
    for s_ in range(M_CONV - 1):
        acc = acc + cw_ref[s_:s_ + 1, :] * buf_ref[s_:s_ + 1, :]
    xc = _silu(acc)
    x = xc[:, 0:M_DIN]
    bm = xc[:, M_DIN:M_DIN + M_GROUPS * M_N]
    cm = xc[:, M_DIN + M_GROUPS * M_N:]
    dt, a = _ssm_params(misc_ref[...], dtb_ref[...], nega_ref[...])
    e4 = e4_ref[...]
    xdt_col = _row_to_col(x * _dot_sel(dt, e4))
    deca_col = _row_to_col(_dot_sel(jnp.exp(a), e4))
    half = M_DIN // M_GROUPS
    spread = lambda t: jnp.concatenate(
        [jnp.broadcast_to(t[:, gi * M_N:(gi + 1) * M_N], (half, M_N)) for gi in range(M_GROUPS)], axis=0)
    hs = h0_ref[...] * deca_col + xdt_col * spread(bm)
    h_ref[...] = hs
    y = _col_to_row(jnp.sum(hs * spread(cm), axis=1, keepdims=True))
    y = (y + md_ref[...] * x) * _silu(z)
    y = jnp.concatenate([_rms_rows(y[:, gi * half:(gi + 1) * half]) for gi in range(M_GROUPS)], axis=1)
    y_ref[...] = y * gnm_ref[...]


def _step_mixers(gla, la, v4, gg4, s0, m, buf, misc, h0, cw, cb, dtb, nega, e4, md, gng, gnm):
    nb = gla.shape[0]
    per = lambda *shape: pl.BlockSpec((None,) + shape, lambda b: (b,) + (0,) * len(shape))
    const = lambda shape: pl.BlockSpec(shape, lambda b: (0,) * len(shape))
    return pl.pallas_call(
        _step_kernel,
        grid=(nb,),
        in_specs=[per(1, 768), per(1, LANE), per(GLA_HEADS, GLA_DV), per(GLA_HEADS, GLA_DV),
                  per(GLA_DQK, GLA_DV), per(1, 1024), per(M_CONV - 1, M_CONV_DIM), per(1, LANE),
                  per(M_DIN, M_N),
                  const((M_CONV, M_CONV_DIM)), const((1, M_CONV_DIM)), const((1, LANE)), const((1, LANE)),
                  const((LANE, M_DIN)), const((1, M_DIN)), const((1, GLA_DV)), const((1, M_DIN))],
        out_specs=[per(GLA_HEADS, GLA_DV), per(GLA_DQK, GLA_DV), per(1, M_DIN), per(M_DIN, M_N)],
        out_shape=[jax.ShapeDtypeStruct((nb, GLA_HEADS, GLA_DV), F32),
                   jax.ShapeDtypeStruct((nb, GLA_DQK, GLA_DV), F32),
                   jax.ShapeDtypeStruct((nb, 1, M_DIN), F32),
                   jax.ShapeDtypeStruct((nb, M_DIN, M_N), F32)],
        compiler_params=_cp("arbitrary"),
        name="step_mixers",
    )(gla.reshape(nb, 1, 768), la.reshape(nb, 1, LANE), v4, gg4, s0, m.reshape(nb, 1, 1024), buf,
      misc.reshape(nb, 1, LANE), h0, cw, cb, dtb, nega, e4, md, gng, gnm)


SB_PP = 4


def _sb_decode_kernel(pt_ref, q_ref, *refs):
    ks = refs[0:SB_PP]
    vs = refs[SB_PP:2 * SB_PP]
    bias_ref, gn_ref, o_ref, acc_ref, car_ref = refs[2 * SB_PP:]
    j = pl.program_id(1)
    page = ks[0].shape[0]

    @pl.when(j == 0)
    def _():
        acc_ref[...] = jnp.zeros_like(acc_ref)
        car_ref[...] = jnp.zeros_like(car_ref)

    q = q_ref[...]
    bias = bias_ref[...]
    for r in range(SB_PP - 1, -1, -1):
        z = jnp.sum(ks[r][...] * q, axis=-1, keepdims=True) + bias
        sp = jnp.maximum(z, 0.0) + jnp.log(1.0 + jnp.exp2(-jnp.abs(z))) * LOG2E
        ls = z - sp
        incl = sp
        s = 1
        while s < page:
            incl = incl + jnp.concatenate([incl[s:], jnp.zeros((s,) + incl.shape[1:], F32)], axis=0)
            s *= 2
        car = car_ref[...]
        w = jnp.exp2(ls - (incl - sp + car))
        car_ref[...] = car + incl[0]
        acc_ref[...] += jnp.sum(w * vs[r][...], axis=0)

    @pl.when(j == pl.num_programs(1) - 1)
    def _():
        o_ref[...] = _rms_rows(acc_ref[...]) * gn_ref[...]


def _sb_decode(page_table, q, cache_k, cache_v, layer, bias_col, gn):
    nb, n_pages = page_table.shape
    page = cache_k.shape[2]
    ng = n_pages // SB_PP
    pt = page_table.reshape(-1)

    def kv_spec(r):
        return pl.BlockSpec((None, None, page, SB_HEADS, SB_HD),
                            lambda b, j, pt_ref: (layer, pt_ref[b * n_pages + (ng - 1 - j) * SB_PP + r], 0, 0, 0))

    const = lambda shape: pl.BlockSpec(shape, lambda b, j, pt_ref: (0,) * len(shape))
    per_seq = pl.BlockSpec((None, SB_HEADS, SB_HD), lambda b, j, pt_ref: (b, 0, 0))
    grid_spec = pltpu.PrefetchScalarGridSpec(
        num_scalar_prefetch=1,
        grid=(nb, ng),
        in_specs=[per_seq] + [kv_spec(r) for r in range(SB_PP)] + [kv_spec(r) for r in range(SB_PP)]
        + [const((SB_HEADS, 1)), const((1, SB_HD))],
        out_specs=per_seq,
        scratch_shapes=[pltpu.VMEM((SB_HEADS, SB_HD), F32), pltpu.VMEM((SB_HEADS, 1), F32)],
    )
    out = pl.pallas_call(
        _sb_decode_kernel,
        grid_spec=grid_spec,
        out_shape=jax.ShapeDtypeStruct((nb, SB_HEADS, SB_HD), F32),
        compiler_params=_cp("arbitrary", "arbitrary"),
        name="sb_decode",
    )(pt, q, *([cache_k] * SB_PP), *([cache_v] * SB_PP), bias_col, gn)
    return out.reshape(nb, SB_W)


def _outproj_kernel(x_ref, gate_ref, g_ref, s_ref, y_ref, w_ref, o_ref):
    mix = (_dot(g_ref[...], w_ref[0:GLA_DVW, :]) + _dot(s_ref[...], w_ref[GLA_DVW:GLA_DVW + SB_W, :])
           + _dot(y_ref[...], w_ref[GLA_DVW + SB_W:, :]))
    o_ref[...] = x_ref[...] + gate_ref[...] * mix


def _out_proj(x, gate, g, s, y, w, *, tm, rows_per_mod):
    r, d = x.shape
    mrows = gate.shape[1]
    row = lambda n: pl.BlockSpec((tm, n), lambda i: (i, 0))
    return pl.pallas_call(
        _outproj_kernel,
        grid=(r // tm,),
        in_specs=[row(d), pl.BlockSpec((None, mrows, d), lambda i: (i // rows_per_mod, 0, 0)),
                  row(GLA_DVW), row(SB_W), row(M_DIN), pl.BlockSpec(w.shape, lambda i: (0, 0))],
        out_specs=row(d),
        out_shape=jax.ShapeDtypeStruct((r, d), F32),
        compiler_params=_cp("arbitrary"),
        name="out_proj",
    )(x, gate, g, s, y, w)


def _ffn_up_kernel(x_ref, sc_ref, sh_ref, g_ref, w_ref, u_ref):
    h = _norm_mod(x_ref[...], g_ref[...], sc_ref[...], sh_ref[...])
    u_ref[...] = jnp.dot(h.astype(BF16), w_ref[...], preferred_element_type=F32)


def _ffn_up(x, sc, sh, g, w, *, tm, rows_per_mod):
    r, d = x.shape
    n = w.shape[1]
    mrows = sc.shape[1]
    mod_spec = pl.BlockSpec((None, mrows, d), lambda i: (i // rows_per_mod, 0, 0))
    return pl.pallas_call(
        _ffn_up_kernel,
        grid=(r // tm,),
        in_specs=[pl.BlockSpec((tm, d), lambda i: (i, 0)), mod_spec, mod_spec,
                  pl.BlockSpec((1, d), lambda i: (0, 0)), pl.BlockSpec((d, n), lambda i: (0, 0))],
        out_specs=pl.BlockSpec((tm, n), lambda i: (i, 0)),
        out_shape=jax.ShapeDtypeStruct((r, n), F32),
        compiler_params=_cp("arbitrary"),
        name="ffn_up",
    )(x, sc, sh, g, w)


FFN_CK = 256


def _ffn_act_down(taps, cw_ref, cb_ref, w_ref):
    dff = w_ref.shape[0]

    def conv(c0):
        u, u1, u2 = taps(c0)
        cs = slice(c0, c0 + FFN_CK)
        return cb_ref[:, cs] + cw_ref[0:1, cs] * u2 + cw_ref[1:2, cs] * u1 + cw_ref[2:3, cs] * u

    out = None
    for c0 in range(0, dff, FFN_CK):
        part = _dot(_silu(conv(c0)) * conv(dff + c0), w_ref[c0:c0 + FFN_CK, :])
        out = part if out is None else out + part
    return out


def _ffn_down_prompt_kernel(x_ref, gate_ref, u_ref, prev_ref, cw_ref, cb_ref, w_ref, o_ref, *, tiles_per_seq):
    i = pl.program_id(0)
    tm = u_ref.shape[0]
    first = i % tiles_per_seq == 0
    row = lax.broadcasted_iota(jnp.int32, (tm, FFN_CK), 0)
    row8 = lax.broadcasted_iota(jnp.int32, (SUBLANE, FFN_CK), 0)

    def taps(c0):
        u = u_ref[:, c0:c0 + FFN_CK]
        prev = jnp.where(first, 0.0, prev_ref[:, c0:c0 + FFN_CK])
        shifted = []
        for s in (1, 2):
            head = jnp.where(row8 < s, pltpu.roll(prev, s, 0), 0.0)
            head = jnp.concatenate([head, jnp.zeros((tm - SUBLANE, FFN_CK), F32)], axis=0)
            shifted.append(jnp.where(row < s, head, pltpu.roll(u, s, 0)))
        return u, shifted[0], shifted[1]

    o_ref[...] = x_ref[...] + gate_ref[...] * _ffn_act_down(taps, cw_ref, cb_ref, w_ref)


def _ffn_down_prompt(x, gate, u, cw, cb, w, *, tm, seq):
    r, d = x.shape
    n = u.shape[1]
    tiles = seq // tm
    per8 = tm // SUBLANE
    return pl.pallas_call(
        functools.partial(_ffn_down_prompt_kernel, tiles_per_seq=tiles),
        grid=(r // tm,),
        in_specs=[pl.BlockSpec((tm, d), lambda i: (i, 0)),
                  pl.BlockSpec((None, 1, d), lambda i: (i // tiles, 0, 0)),
                  pl.BlockSpec((tm, n), lambda i: (i, 0)),
                  pl.BlockSpec((SUBLANE, n), lambda i: (jnp.maximum(i * per8 - 1, 0), 0)),
                  pl.BlockSpec((FFN_CONV, n), lambda i: (0, 0)), pl.BlockSpec((1, n), lambda i: (0, 0)),
                  pl.BlockSpec(w.shape, lambda i: (0, 0))],
        out_specs=pl.BlockSpec((tm, d), lambda i: (i, 0)),
        out_shape=jax.ShapeDtypeStruct((r, d), F32),
        compiler_params=_cp("arbitrary"),
        name="ffn_down_prompt",
    )(x, gate, u, u, cw, cb, w)


def _ffn_down_step_kernel(x_ref, gate_ref, u_ref, buf_ref, cw_ref, cb_ref, w_ref, o_ref):
    taps = lambda c0: (u_ref[:, c0:c0 + FFN_CK], buf_ref[1, :, c0:c0 + FFN_CK], buf_ref[0, :, c0:c0 + FFN_CK])
    o_ref[...] = x_ref[...] + gate_ref[...] * _ffn_act_down(taps, cw_ref, cb_ref, w_ref)


def _ffn_down_step(x, gate, u, buf_t, cw, cb, w):
    full = lambda a: pl.BlockSpec(a.shape, lambda i: (0,) * a.ndim)
    args = (x, gate, u, buf_t, cw, cb, w)
    return pl.pallas_call(
        _ffn_down_step_kernel,
        grid=(1,),
        in_specs=[full(a) for a in args],
        out_specs=full(x),
        out_shape=jax.ShapeDtypeStruct(x.shape, F32),
        compiler_params=_cp("arbitrary"),
        name="ffn_down_step",
    )(*args)


def _same_segment(n, width):
    i = np.arange(n)
    return jnp.asarray((i[:, None] // width) == (i[None, :] // width), BF16)


def _constants():
    k = np.arange(GLA_DQK)
    v = np.arange(GLA_DVW)
    eexp = (k[:, None] // GLA_DK) == (v[None, :] // GLA_DV)
    j = np.arange(SB_TK)
    later = j[:, None] > j[None, :]
    uo = np.block([[later, np.zeros_like(later)], [np.zeros_like(later), later]])
    lane = np.arange(LANE)
    hm_pair = np.zeros((SUBLANE, LANE), np.float32)
    hm_pair[0] = lane < SB_HD
    hm_pair[1] = lane >= SB_HD
    p = np.arange(M_DIN)
    e4 = lane[:, None] == (p[None, :] // M_HD)
    return dict(
        eexp=jnp.asarray(eexp, BF16), bd=jnp.asarray(eexp.T, F32), eseg_gla=_same_segment(GLA_DVW, GLA_DV),
        uo=jnp.asarray(uo, BF16), hm_pair=jnp.asarray(hm_pair, BF16),
        eseg_pair=_same_segment(LANE, SB_HD), eseg_sb=_same_segment(SB_W, SB_HD),
        e4=jnp.asarray(e4, BF16), e4t=jnp.asarray(e4.T, BF16))


def _pad_lanes(v, n=LANE):
    return jnp.zeros((1, n), F32).at[0, :v.shape[0]].set(v)


def _pad_col(v, n=SUBLANE):
    return jnp.zeros((n, 1), F32).at[:v.shape[0], 0].set(v)


def kernel(x_prompt, x_sample, cache_sb_k, cache_sb_v, state_gla, state_mamba_conv, state_mamba_ssm, state_ffn_conv, page_table, c_prompt, c_sample, norm1_g, w_ada, b_ada, w_in, gla_w_gate2, gla_b_gate, gla_norm_g, sb_q_norm_g, sb_k_norm_g, sb_o_norm_g, sb_bias, m_conv_w, m_conv_b, m_dt_bias, m_a_log, m_d, m_norm_g, w_out, norm2_g, ffn_w_up, ffn_conv_w, ffn_conv_b, ffn_w_down):
    bsz, seq, d = x_prompt.shape
    nb = x_sample.shape[0]
    depth = w_in.shape[0]
    dff = ffn_w_down.shape[1]
    n_pool, page = cache_sb_k.shape[1], cache_sb_k.shape[2]
    assert x_sample.shape[1] == 1 and d % LANE == 0 and seq % 256 == 0
    cst = _constants()

    tm = 256
    tq = 512
    tt_gla = 256
    tt_ssd = 128
    assert (tq // SB_TK) % SB_UNROLL == 0 and page_table.shape[1] % SB_PP == 0
    assert seq % tq == 0 and seq % tm == 0 and seq % tt_gla == 0 and seq % tt_ssd == 0

    mod = _modulation(jnp.concatenate([c_prompt, c_sample], axis=0), w_ada, b_ada)
    mod = mod.reshape(depth, bsz + nb, N_MOD, d)
    mod_p = mod[:, :bsz].transpose(0, 2, 1, 3).reshape(depth, N_MOD, bsz, 1, d)
    mod_s = mod[:, bsz:].transpose(0, 2, 1, 3).reshape(depth, N_MOD, 1, nb, d)

    xp = x_prompt.reshape(bsz * seq, d)
    xs = x_sample.reshape(nb, d)
    outs = {name: [] for name in ("pk", "pv", "pg", "pc", "ph", "pf", "sk", "sv", "sg", "sc", "sh", "sf")}
    for l in range(depth):
        wi = w_in[l]
        o = np.cumsum([0, GLA_DQK, GLA_DQK, GLA_DVW, GLA_DVW, GLA_LR, SB_W, SB_W, SB_W, M_DIN, M_CONV_DIM, M_HEADS])
        sl = lambda a, b: wi[:, o[a]:o[b]]
        w_re = jnp.concatenate(
            [sl(0, 4), sl(5, 8), sl(8, 10), sl(10, 11), sl(4, 5),
             jnp.zeros((d, LANE - M_HEADS - GLA_LR), F32)], axis=1).astype(BF16)
        wg = jnp.zeros((LANE, LANE), F32).at[M_HEADS:M_HEADS + GLA_LR].set(gla_w_gate2[l]).astype(BF16)
        bg = gla_b_gate[l].reshape(1, GLA_DQK)
        gq = jnp.tile(sb_q_norm_g[l], SB_HEADS).reshape(1, SB_W)
        gk = jnp.tile(sb_k_norm_g[l], SB_HEADS).reshape(1, SB_W)
        go = jnp.tile(sb_o_norm_g[l], SB_HEADS).reshape(1, SB_W)
        gng = jnp.tile(gla_norm_g[l], GLA_HEADS).reshape(1, GLA_DVW)
        g1 = norm1_g[l].reshape(1, d)
        g2 = norm2_g[l].reshape(1, d)
        bias_pair = jnp.repeat(sb_bias[l] * LOG2E, SB_TK).reshape(SB_HEADS // 2, 1, 2 * SB_TK)
        bias_col = (sb_bias[l] * LOG2E).reshape(SB_HEADS, 1)
        cw = m_conv_w[l]
        cb = m_conv_b[l].reshape(1, M_CONV_DIM)
        nega = -jnp.exp(m_a_log[l])
        dtb_l, nega_l = _pad_lanes(m_dt_bias[l]), _pad_lanes(nega)
        dtb_c, nega_c = _pad_col(m_dt_bias[l]), _pad_col(nega)
        md = jnp.repeat(m_d[l], M_HD).reshape(1, M_DIN)
        gnm = m_norm_g[l].reshape(1, M_DIN)
        wo = w_out[l].astype(BF16)
        wu = ffn_w_up[l].astype(BF16)
        wd = ffn_w_down[l].astype(BF16)
        fcw = ffn_conv_w[l]
        fcb = ffn_conv_b[l].reshape(1, 2 * dff)
        sh1, sc1, gt1, sh2, sc2, gt2 = range(N_MOD)

        mp = mod_p[l]
        gla, la, q, k, kb, v, vb, m, misc = _in_proj(
            xp, mp[sc1], mp[sh1], g1, w_re, wg, bg, gq, gk, cst["eseg_sb"], tm=tm, rows_per_mod=seq // tm)
        g_mix, g_state = _gla_prompt(gla, la, cst["eexp"], cst["bd"], cst["eseg_gla"], gng,
                                     bsz=bsz, seq=seq, tt=tt_gla)
        s_mix = _sb_prompt(q, kb, vb, bias_pair, cst["uo"], cst["hm_pair"], cst["eseg_pair"], go[:, :LANE],
                           bsz=bsz, seq=seq, tq=tq)
        dtt = jnp.pad(misc[:, :M_HEADS].reshape(bsz, seq, M_HEADS).transpose(0, 2, 1),
                      ((0, 0), (0, SUBLANE - M_HEADS), (0, 0)))
        y_mix, h_state = _ssd_prompt(m, misc, dtt, cw, cb, dtb_l, nega_l, dtb_c, nega_c, cst["e4"], cst["e4t"],
                                     md, gnm, bsz=bsz, seq=seq, tt=tt_ssd)
        xp = _out_proj(xp, mp[gt1], g_mix, s_mix, y_mix, wo, tm=tm, rows_per_mod=seq // tm)
        u = _ffn_up(xp, mp[sc2], mp[sh2], g2, wu, tm=tm, rows_per_mod=seq // tm)
        xp = _ffn_down_prompt(xp, mp[gt2], u, fcw, fcb, wd, tm=tm, seq=seq)

        outs["pk"].append(k.reshape(bsz, seq, SB_HEADS, SB_HD))
        outs["pv"].append(v.reshape(bsz, seq, SB_HEADS, SB_HD))
        gs = g_state.reshape(bsz, GLA_HEADS, GLA_DV, GLA_HEADS, GLA_DK)
        outs["pg"].append(jnp.stack([gs[:, h, :, h, :] for h in range(GLA_HEADS)], axis=1).transpose(0, 1, 3, 2))
        outs["pc"].append(m.reshape(bsz, seq, 1024)[:, seq - (M_CONV - 1):, M_DIN:])
        outs["ph"].append(h_state.reshape(bsz, M_HEADS, M_HD, M_N))
        outs["pf"].append(u.reshape(bsz, seq, 2 * dff)[:, seq - (FFN_CONV - 1):])

        ms = mod_s[l]
        gla, la, q, k, kb, v, vb, m, misc = _in_proj(
            xs, ms[sc1], ms[sh1], g1, w_re, wg, bg, gq, gk, cst["eseg_sb"], tm=nb, rows_per_mod=1)
        g4, g_state, y_mix, h_state = _step_mixers(
            gla, la, gla[:, 2 * GLA_DQK:2 * GLA_DQK + GLA_DVW].reshape(nb, GLA_HEADS, GLA_DV),
            gla[:, 2 * GLA_DQK + GLA_DVW:].reshape(nb, GLA_HEADS, GLA_DV),
            state_gla[l].reshape(nb, GLA_DQK, GLA_DV), m, state_mamba_conv[l], misc,
            state_mamba_ssm[l].reshape(nb, M_DIN, M_N), cw, cb, dtb_l, nega_l, cst["e4"], md,
            gla_norm_g[l].reshape(1, GLA_DV), gnm)
        s_mix = _sb_decode(page_table, q.astype(F32).reshape(nb, SB_HEADS, SB_HD), cache_sb_k, cache_sb_v, l,
                           bias_col, sb_o_norm_g[l].reshape(1, SB_HD))
        xs = _out_proj(xs, ms[gt1], g4.reshape(nb, GLA_DVW), s_mix, y_mix.reshape(nb, M_DIN), wo,
                       tm=nb, rows_per_mod=1)
        u = _ffn_up(xs, ms[sc2], ms[sh2], g2, wu, tm=nb, rows_per_mod=1)
        xs = _ffn_down_step(xs, ms[gt2][0], u, state_ffn_conv[l].transpose(1, 0, 2), fcw, fcb, wd)

        outs["sk"].append(k.reshape(nb, 1, SB_HEADS, SB_HD))
        outs["sv"].append(v.reshape(nb, 1, SB_HEADS, SB_HD))
        outs["sg"].append(g_state.reshape(nb, GLA_HEADS, GLA_DK, GLA_DV))
        outs["sc"].append(jnp.concatenate([state_mamba_conv[l][:, 1:], m[:, None, M_DIN:]], axis=1))
        outs["sh"].append(h_state.reshape(nb, M_HEADS, M_HD, M_N))
        outs["sf"].append(jnp.concatenate([state_ffn_conv[l][:, 1:], u[:, None, :]], axis=1))

    st = {name: jnp.stack(v) for name, v in outs.items()}
    return (xp.reshape(bsz, seq, d), xs.reshape(nb, 1, d), st["pk"], st["pv"], st["pg"], st["pc"], st["ph"],
            st["pf"], st["sk"], st["sv"], st["sg"], st["sc"], st["sh"], st["sf"])
```

```python
import functools

import jax
import jax.numpy as jnp
import numpy as np
from jax import lax
from jax.experimental import pallas as pl
from jax.experimental.pallas import tpu as pltpu

F32 = jnp.float32
BF16 = jnp.bfloat16

GLA_HEADS, GLA_DK, GLA_DV, GLA_LR, GLA_TAU = 4, 32, 64, 16, 16.0
GLA_DQK = GLA_HEADS * GLA_DK
GLA_DVW = GLA_HEADS * GLA_DV
SB_HEADS, SB_HD = 8, 64
SB_W = SB_HEADS * SB_HD
M_HEADS, M_HD, M_GROUPS, M_N, M_CONV = 4, 64, 2, 128, 4
M_DIN = M_HEADS * M_HD
M_CONV_DIM = M_DIN + 2 * M_GROUPS * M_N
FFN_CONV = 3
N_MOD = 6
EPS = 1e-6
NEG_BIG = -1e30
LOG2E = 1.4426950408889634

C_GLA = 0
C_SBQ = 768
C_SBK = 1280
C_SBV = 1792
C_MZ = 2304
C_MISC = 3328
D_INP = 3456
LANE = 128
SUBLANE = 8

VMEM_LIMIT = 56 * 1024 * 1024


def _cp(*sem):
    return pltpu.CompilerParams(dimension_semantics=sem, vmem_limit_bytes=VMEM_LIMIT)


def _dot(a, b):
    return jnp.dot(a.astype(BF16), b.astype(BF16), preferred_element_type=F32)


def _dot_nt(a, b):
    return lax.dot_general(a.astype(BF16), b.astype(BF16), (((1,), (1,)), ((), ())),
                           preferred_element_type=F32)


def _dot_tn(a, b):
    return lax.dot_general(a.astype(BF16), b.astype(BF16), (((0,), (0,)), ((), ())),
                           preferred_element_type=F32)


def _split3(x):
    hi = x.astype(BF16)
    r = x - hi.astype(F32)
    mid = r.astype(BF16)
    lo = (r - mid.astype(F32)).astype(BF16)
    return hi, mid, lo


def _dot_sel(x, e):
    hi, mid, lo = _split3(x)
    d = lambda p: jnp.dot(p, e, preferred_element_type=F32)
    return d(hi) + d(mid) + d(lo)


def _dot_sel_nt(e, x):
    hi, mid, lo = _split3(x)
    d = lambda p: lax.dot_general(e, p, (((1,), (1,)), ((), ())), preferred_element_type=F32)
    return d(hi) + d(mid) + d(lo)


def _sigmoid(x):
    return 1.0 / (1.0 + jnp.exp(-x))


def _silu(x):
    return x * _sigmoid(x)


def _softplus(x):
    return jnp.maximum(x, 0.0) + jnp.log1p(jnp.exp(-jnp.abs(x)))


def _log_sigmoid(x):
    return jnp.minimum(x, 0.0) - jnp.log1p(jnp.exp(-jnp.abs(x)))


def _rms_rows(x):
    return x * lax.rsqrt(jnp.mean(x * x, axis=-1, keepdims=True) + EPS)


def _seg_rms(x, eseg, width):
    xx = x * x
    hi = xx.astype(BF16)
    lo = (xx - hi.astype(F32)).astype(BF16)
    blocks = []
    for c in range(0, x.shape[1], LANE):
        d = lambda p: jnp.dot(p[:, c:c + LANE], eseg, preferred_element_type=F32)
        blocks.append(d(hi) + d(lo))
    ms = jnp.concatenate(blocks, axis=1) * (1.0 / width)
    return x * lax.rsqrt(ms + EPS)


def _row_to_col(row):
    n = row.shape[1]
    eye = lax.broadcasted_iota(jnp.int32, (n, n), 0) == lax.broadcasted_iota(jnp.int32, (n, n), 1)
    return jnp.sum(jnp.where(eye, row, 0.0), axis=1, keepdims=True)


def _col_to_row(col):
    n = col.shape[0]
    eye = lax.broadcasted_iota(jnp.int32, (n, n), 0) == lax.broadcasted_iota(jnp.int32, (n, n), 1)
    return jnp.sum(jnp.where(eye, col, 0.0), axis=0, keepdims=True)


def _cumsum(x, axis, seg=None):
    n = x.shape[axis] if seg is None else seg
    idx = lax.broadcasted_iota(jnp.int32, x.shape, axis)
    if seg is not None:
        idx = idx % seg
    s = 1
    while s < n:
        x = x + jnp.where(idx >= s, pltpu.roll(x, s, axis), 0.0)
        s *= 2
    return x


def _mod_kernel(c_ref, w_ref, b_ref, o_ref):
    o_ref[...] = _dot(_silu(c_ref[...]), w_ref[...]) + b_ref[...]


def _modulation(c_all, w_ada, b_ada):
    depth, d, nd = w_ada.shape
    n = c_all.shape[0]
    tn = 1024
    return pl.pallas_call(
        _mod_kernel,
        grid=(depth, nd // tn),
        in_specs=[pl.BlockSpec((n, d), lambda l, j: (0, 0)),
                  pl.BlockSpec((None, d, tn), lambda l, j: (l, 0, j)),
                  pl.BlockSpec((None, 1, tn), lambda l, j: (l, 0, j))],
        out_specs=pl.BlockSpec((None, n, tn), lambda l, j: (l, 0, j)),
        out_shape=jax.ShapeDtypeStruct((depth, n, nd), F32),
        compiler_params=_cp("arbitrary", "arbitrary"),
        name="adaln_mod",
    )(c_all, w_ada, b_ada.reshape(depth, 1, nd))


def _norm_mod(x, g, sc, sh):
    return _rms_rows(x) * g * (1.0 + sc) + sh


def _inproj_kernel(x_ref, sc_ref, sh_ref, g_ref, w_ref, wg_ref, bg_ref, gq_ref, gk_ref, eseg_ref,
                   gla_ref, la_ref, q_ref, k_ref, kb_ref, v_ref, vb_ref, m_ref, misc_ref, *, kv_transposed):
    h = _norm_mod(x_ref[...], g_ref[...], sc_ref[...], sh_ref[...])
    p = jnp.dot(h.astype(BF16), w_ref[...], preferred_element_type=F32)
    gla_ref[...] = p[:, C_GLA:C_SBQ]
    misc = p[:, C_MISC:D_INP]
    misc_ref[...] = misc
    la_ref[...] = _log_sigmoid(_dot(misc, wg_ref[...]) + bg_ref[...]) * (1.0 / GLA_TAU)
    eseg = eseg_ref[...]
    q = _seg_rms(p[:, C_SBQ:C_SBK], eseg, SB_HD) * gq_ref[...]
    q_ref[...] = (q * (SB_HD ** -0.5 * LOG2E)).astype(BF16)
    k = _seg_rms(p[:, C_SBK:C_SBV], eseg, SB_HD) * gk_ref[...]
    k_ref[...] = k.T if kv_transposed else k
    kb_ref[...] = k.astype(BF16)
    v = p[:, C_SBV:C_MZ]
    v_ref[...] = v.T if kv_transposed else v
    vb_ref[...] = v.astype(BF16)
    m_ref[...] = p[:, C_MZ:C_MISC]


def _in_proj(x, sc, sh, g, w, wg, bg, gq, gk, eseg, *, tm, rows_per_mod, kv_transposed):
    r, d = x.shape
    mrows = sc.shape[1]
    mod_spec = pl.BlockSpec((None, mrows, d), lambda i: (i // rows_per_mod, 0, 0))
    const = lambda shape: pl.BlockSpec(shape, lambda i: (0,) * len(shape))
    row = lambda n: pl.BlockSpec((tm, n), lambda i: (i, 0))
    outs = [(768, F32), (LANE, F32), (SB_W, BF16), (SB_W, F32), (SB_W, BF16), (SB_W, F32), (SB_W, BF16),
            (1024, F32), (LANE, F32)]
    out_specs = [row(n) for n, _ in outs]
    out_shape = [jax.ShapeDtypeStruct((r, n), dt) for n, dt in outs]
    if kv_transposed:
        seq = rows_per_mod * tm
        for idx in (3, 5):
            out_specs[idx] = pl.BlockSpec((None, SB_W, tm), lambda i: (i // rows_per_mod, 0, i % rows_per_mod))
            out_shape[idx] = jax.ShapeDtypeStruct((r // seq, SB_W, seq), F32)
    return pl.pallas_call(
        functools.partial(_inproj_kernel, kv_transposed=kv_transposed),
        grid=(r // tm,),
        in_specs=[row(d), mod_spec, mod_spec, const((1, d)), const((d, D_INP)), const((LANE, LANE)),
                  const((1, LANE)), const((1, SB_W)), const((1, SB_W)), const((LANE, LANE))],
        out_specs=out_specs,
        out_shape=out_shape,
        compiler_params=_cp("arbitrary"),
        name="in_proj",
    )(x, sc, sh, g, w, wg, bg, gq, gk, eseg)


GLA_C = SUBLANE
GLA_UNROLL = 8


def _gla_prompt_kernel(gla_ref, la_ref, eexp_ref, bd_ref, eseg_ref, gn_ref, o_ref, st_out_ref,
                       st_ref, b_ref, acc_ref):
    i = pl.program_id(1)
    tt = la_ref.shape[0]

    @pl.when(i == 0)
    def _():
        st_ref[...] = jnp.zeros_like(st_ref)

    b_ref[...] = _cumsum(la_ref[...], 0, seg=GLA_C)
    eexp = eexp_ref[...]
    bd = bd_ref[...]
    rowid = lax.broadcasted_iota(jnp.int32, (GLA_C, GLA_DQK), 0)
    rowid_v = lax.broadcasted_iota(jnp.int32, (GLA_C, GLA_DVW), 0)

    def group(gi, carry):
        steps = []
        for u in range(GLA_UNROLL):
            r0 = pl.multiple_of((gi * GLA_UNROLL + u) * GLA_C, GLA_C)
            bc = b_ref[pl.ds(r0, GLA_C), :]
            qc = gla_ref[pl.ds(r0, GLA_C), 0:GLA_DQK] * (GLA_DK ** -0.5)
            kc = gla_ref[pl.ds(r0, GLA_C), GLA_DQK:2 * GLA_DQK]
            vc = gla_ref[pl.ds(r0, GLA_C), 2 * GLA_DQK:2 * GLA_DQK + GLA_DVW]
            steps.append((r0, bc, qc, kc, vc, bc[GLA_C - 1:GLA_C, :]))
        ps = []
        for r0, bc, qc, kc, vc, bl in steps:
            for t in range(GLA_C):
                d = jnp.where(rowid <= t, bc[t:t + 1, :] - bc, NEG_BIG)
                ps.append(jnp.exp(d) * (qc[t:t + 1, :] * kc))
        sc = _dot(jnp.concatenate(ps, axis=0), eexp)
        upds = [_dot_tn(vc, kc * jnp.exp(bl - bc)) * bd for r0, bc, qc, kc, vc, bl in steps]
        st = st_ref[...]
        for u, (r0, bc, qc, kc, vc, bl) in enumerate(steps):
            o = _dot_nt(qc * jnp.exp(bc), st)
            st = st * jnp.exp(bl) + upds[u]
            for t in range(GLA_C):
                row0 = (u * GLA_C + t) * GLA_C
                ot = jnp.sum(sc[row0:row0 + GLA_C, :] * vc, axis=0, keepdims=True)
                o = o + jnp.where(rowid_v == t, ot, 0.0)
            acc_ref[pl.ds(r0, GLA_C), :] = o
        st_ref[...] = st
        return carry

    lax.fori_loop(0, tt // (GLA_C * GLA_UNROLL), group, 0)
    gg = gla_ref[:, 2 * GLA_DQK + GLA_DVW:2 * GLA_DQK + 2 * GLA_DVW]
    o = _seg_rms(acc_ref[...], eseg_ref[...], GLA_DV) * gn_ref[...] * _silu(gg)
    o_ref[...] = o.astype(BF16)
    st_out_ref[...] = st_ref[...]


def _gla_prompt(gla, la, eexp, bd, eseg, gn, *, bsz, seq, tt):
    nt = seq // tt
    const = lambda shape: pl.BlockSpec(shape, lambda b, i: (0,) * len(shape))
    return pl.pallas_call(
        _gla_prompt_kernel,
        grid=(bsz, nt),
        in_specs=[pl.BlockSpec((tt, 768), lambda b, i: (b * nt + i, 0)),
                  pl.BlockSpec((tt, LANE), lambda b, i: (b * nt + i, 0)),
                  const((GLA_DQK, GLA_DVW)), const((GLA_DVW, GLA_DQK)), const((LANE, LANE)),
                  const((1, GLA_DVW))],
        out_specs=[pl.BlockSpec((tt, GLA_DVW), lambda b, i: (b * nt + i, 0)),
                   pl.BlockSpec((None, GLA_DVW, GLA_DQK), lambda b, i: (b, 0, 0))],
        out_shape=[jax.ShapeDtypeStruct((bsz * seq, GLA_DVW), BF16),
                   jax.ShapeDtypeStruct((bsz, GLA_DVW, GLA_DQK), F32)],
        scratch_shapes=[pltpu.VMEM((GLA_DVW, GLA_DQK), F32), pltpu.VMEM((tt, GLA_DQK), F32),
                        pltpu.VMEM((tt, GLA_DVW), F32)],
        compiler_params=_cp("arbitrary", "arbitrary"),
        name="gla_prompt",
    )(gla, la, eexp, bd, eseg, gn)


SB_TK = LANE
SB_UNROLL = 4


def _neg_abs(x):
    return lax.bitcast_convert_type(lax.bitcast_convert_type(x, jnp.int32) | jnp.int32(-2 ** 31), F32)


def _sb_prompt_kernel(q_ref, k_ref, v_ref, bias_ref, uo_ref, hm_ref, eseg_ref, gn_ref, o_ref,
                      acc_ref, car_ref, kk_ref, vv_ref):
    i = pl.program_id(2)
    tq = q_ref.shape[0]
    ndiag = tq // SB_TK
    nblk = kk_ref.shape[0]

    @pl.when(i == 0)
    def _():
        m0 = hm_ref[0:1, :]
        m1 = hm_ref[1:2, :]

        def fill(j, carry):
            k0 = pl.multiple_of(j * SB_TK, SB_TK)
            kb = k_ref[pl.ds(k0, SB_TK), :]
            vb = v_ref[pl.ds(k0, SB_TK), :]
            kk_ref[j] = jnp.concatenate([kb * m0, kb * m1], axis=0)
            vv_ref[j] = jnp.concatenate([vb * m0, vb * m1], axis=0)
            return carry

        lax.fori_loop(0, nblk, fill, 0)

    bias = bias_ref[...]
    uo = uo_ref[...]
    acc_ref[...] = jnp.zeros_like(acc_ref)
    car_ref[...] = jnp.zeros_like(car_ref)

    def block(j, r0, diag):
        rows = tq - r0
        z = lax.dot_general(q_ref[r0:, :], kk_ref[j], (((1,), (1,)), ((), ())),
                            preferred_element_type=F32) + bias
        sp = jnp.maximum(z, 0.0) + jnp.log(1.0 + jnp.exp2(_neg_abs(z))) * LOG2E
        ls = z - sp
        if diag:
            valid = (lax.broadcasted_iota(jnp.int32, (rows, 2 * SB_TK), 1) % SB_TK
                     < lax.broadcasted_iota(jnp.int32, (rows, 2 * SB_TK), 0))
            sp = jnp.where(valid, sp, 0.0)
        later = jnp.dot(sp.astype(BF16), uo, preferred_element_type=F32) + car_ref[r0:, :]
        w = jnp.exp2(ls - later)
        if diag:
            w = jnp.where(valid, w, 0.0)
        car_ref[r0:, :SB_TK] += jnp.sum(sp[:, :SB_TK], axis=1, keepdims=True)
        car_ref[r0:, SB_TK:] += jnp.sum(sp[:, SB_TK:], axis=1, keepdims=True)
        acc_ref[r0:, :] += jnp.dot(w.astype(BF16), vv_ref[j], preferred_element_type=F32)

    for c in range(ndiag - 1, -1, -1):
        block(i * ndiag + c, c * SB_TK, True)

    def body(jj, carry):
        for u in range(SB_UNROLL):
            block(i * ndiag - 1 - jj * SB_UNROLL - u, 0, False)
        return carry

    lax.fori_loop(0, (i * ndiag) // SB_UNROLL, body, 0)
    o_ref[...] = (_seg_rms(acc_ref[...], eseg_ref[...], SB_HD) * gn_ref[...]).astype(BF16)


def _sb_prompt(q, k, v, bias2, uo, hm, eseg, gn, *, bsz, seq, tq):
    nq = seq // tq
    npair = SB_HEADS // 2
    const = lambda shape: pl.BlockSpec(shape, lambda b, p, i: (0,) * len(shape))
    return pl.pallas_call(
        _sb_prompt_kernel,
        grid=(bsz, npair, nq),
        in_specs=[pl.BlockSpec((tq, LANE), lambda b, p, i: (b * nq + i, p)),
                  pl.BlockSpec((seq, LANE), lambda b, p, i: (b, p)),
                  pl.BlockSpec((seq, LANE), lambda b, p, i: (b, p)),
                  pl.BlockSpec((None, 1, 2 * SB_TK), lambda b, p, i: (p, 0, 0)),
                  const((2 * SB_TK, 2 * SB_TK)), const((SUBLANE, LANE)), const((LANE, LANE)), const((1, LANE))],
        out_specs=pl.BlockSpec((tq, LANE), lambda b, p, i: (b * nq + i, p)),
        out_shape=jax.ShapeDtypeStruct((bsz * seq, SB_W), BF16),
        scratch_shapes=[pltpu.VMEM((tq, LANE), F32), pltpu.VMEM((tq, 2 * SB_TK), F32),
                        pltpu.VMEM((seq // SB_TK, 2 * SB_TK, LANE), BF16),
                        pltpu.VMEM((seq // SB_TK, 2 * SB_TK, LANE), BF16)],
        compiler_params=_cp("arbitrary", "arbitrary", "arbitrary"),
        name="sb_prompt",
    )(q, k, v, bias2, uo, hm, eseg, gn)


def _ssm_params(dt_raw, dtb, nega):
    dt = _softplus(dt_raw + dtb)
    return dt, dt * nega


def _ssd_prompt_kernel(m_ref, prev_ref, misc_ref, dtt_ref, cw_ref, cb_ref, dtb_l_ref, nega_l_ref,
                       dtb_c_ref, nega_c_ref, e4_ref, e4t_ref, md_ref, gn_ref, o_ref, hs_out_ref,
                       hs_ref):
    i = pl.program_id(1)
    tt = m_ref.shape[0]

    @pl.when(i == 0)
    def _():
        hs_ref[...] = jnp.zeros_like(hs_ref)

    z = m_ref[:, 0:M_DIN]
    xbc = m_ref[:, M_DIN:M_DIN + M_CONV_DIM]
    prev = jnp.where(i > 0, prev_ref[:, M_DIN:M_DIN + M_CONV_DIM], 0.0)
    row = lax.broadcasted_iota(jnp.int32, (tt, M_CONV_DIM), 0)
    row8 = lax.broadcasted_iota(jnp.int32, (SUBLANE, M_CONV_DIM), 0)
    acc = cb_ref[...] + cw_ref[M_CONV - 1:M_CONV, :] * xbc
    for s in range(1, M_CONV):
        head = jnp.where(row8 < s, pltpu.roll(prev, s, 0), 0.0)
        head = jnp.concatenate([head, jnp.zeros((tt - SUBLANE, M_CONV_DIM), F32)], axis=0)
        shifted = jnp.where(row < s, head, pltpu.roll(xbc, s, 0))
        acc = acc + cw_ref[M_CONV - 1 - s:M_CONV - s, :] * shifted
    xc = _silu(acc)
    x = xc[:, 0:M_DIN]
    bm = xc[:, M_DIN:M_DIN + M_GROUPS * M_N]
    cm = xc[:, M_DIN + M_GROUPS * M_N:]

    e4 = e4_ref[...]
    dt_c, a_c = _ssm_params(misc_ref[...], dtb_l_ref[...], nega_l_ref[...])
    cs_c = _cumsum(a_c, 0)
    dt_r, a_r = _ssm_params(dtt_ref[...], dtb_c_ref[...], nega_c_ref[...])
    cs_r = _cumsum(a_r, 1)
    xdt = x * _dot_sel(dt_c, e4)
    ecs = _dot_sel(jnp.exp(cs_c), e4)

    g = [_dot_nt(cm[:, gi * M_N:(gi + 1) * M_N], bm[:, gi * M_N:(gi + 1) * M_N]) for gi in range(M_GROUPS)]
    causal = lax.broadcasted_iota(jnp.int32, (tt, tt), 0) >= lax.broadcasted_iota(jnp.int32, (tt, tt), 1)
    lane_head = lax.broadcasted_iota(jnp.int32, (tt, M_DIN), 1) // M_HD
    rep = M_HEADS // M_GROUPS
    y = jnp.zeros((tt, M_DIN), F32)
    for h in range(M_HEADS):
        decay = jnp.exp(jnp.where(causal, cs_c[:, h:h + 1] - cs_r[h:h + 1, :], NEG_BIG))
        yh = _dot(g[h // rep] * decay, xdt)
        y = jnp.where(lane_head == h, yh, y)

    hs = hs_ref[...]
    yi = [_dot_nt(cm[:, gi * M_N:(gi + 1) * M_N], hs) for gi in range(M_GROUPS)]
    y = y + jnp.where(lane_head < rep, yi[0], yi[1]) * ecs

    cs_last = cs_c[tt - 1:tt, :]
    xw = xdt * _dot_sel(jnp.exp(cs_last - cs_c), e4)
    zz = _dot_tn(xw, bm)
    half = M_DIN // M_GROUPS
    upd = jnp.concatenate([zz[:half, :M_N], zz[half:, M_N:]], axis=0)
    dec = _dot_sel_nt(e4t_ref[...], jnp.broadcast_to(jnp.exp(cs_last), (M_N, LANE)))
    hs_ref[...] = hs * dec + upd
    hs_out_ref[...] = hs_ref[...]

    y = (y + md_ref[...] * x) * _silu(z)
    y = jnp.concatenate([_rms_rows(y[:, gi * half:(gi + 1) * half]) for gi in range(M_GROUPS)], axis=1)
    o_ref[...] = (y * gn_ref[...]).astype(BF16)


def _ssd_prompt(m, misc, dtt, cw, cb, dtb_l, nega_l, dtb_c, nega_c, e4, e4t, md, gn, *, bsz, seq, tt):
    nt = seq // tt
    per8 = tt // SUBLANE
    const = lambda shape: pl.BlockSpec(shape, lambda b, i: (0,) * len(shape))
    return pl.pallas_call(
        _ssd_prompt_kernel,
        grid=(bsz, nt),
        in_specs=[pl.BlockSpec((tt, 1024), lambda b, i: (b * nt + i, 0)),
                  pl.BlockSpec((SUBLANE, 1024), lambda b, i: (jnp.maximum((b * nt + i) * per8 - 1, 0), 0)),
                  pl.BlockSpec((tt, LANE), lambda b, i: (b * nt + i, 0)),
                  pl.BlockSpec((None, SUBLANE, tt), lambda b, i: (b, 0, i)),
                  const((M_CONV, M_CONV_DIM)), const((1, M_CONV_DIM)), const((1, LANE)), const((1, LANE)),
                  const((SUBLANE, 1)), const((SUBLANE, 1)), const((LANE, M_DIN)), const((M_DIN, LANE)),
                  const((1, M_DIN)), const((1, M_DIN))],
        out_specs=[pl.BlockSpec((tt, M_DIN), lambda b, i: (b * nt + i, 0)),
                   pl.BlockSpec((None, M_DIN, M_N), lambda b, i: (b, 0, 0))],
        out_shape=[jax.ShapeDtypeStruct((bsz * seq, M_DIN), BF16),
                   jax.ShapeDtypeStruct((bsz, M_DIN, M_N), F32)],
        scratch_shapes=[pltpu.VMEM((M_DIN, M_N), F32)],
        compiler_params=_cp("arbitrary", "arbitrary"),
        name="ssd_prompt",
    )(m, m, misc, dtt, cw, cb, dtb_l, nega_l, dtb_c, nega_c, e4, e4t, md, gn)


def _step_kernel(gla_ref, la_ref, v4_ref, gg4_ref, s0_ref, m_ref, buf_ref, misc_ref, h0_ref,
                 cw_ref, cb_ref, dtb_ref, nega_ref, e4_ref, md_ref, gng_ref, gnm_ref,
                 g_ref, s_ref, y_ref, h_ref):
    q_col = _row_to_col(gla_ref[:, 0:GLA_DQK] * (GLA_DK ** -0.5))
    k_col = _row_to_col(gla_ref[:, GLA_DQK:2 * GLA_DQK])
    dec_col = _row_to_col(jnp.exp(la_ref[...]))
    v4 = v4_ref[...]
    v_exp = jnp.concatenate([jnp.broadcast_to(v4[h:h + 1, :], (GLA_DK, GLA_DV)) for h in range(GLA_HEADS)],
                            axis=0)
    s = s0_ref[...] * dec_col + k_col * v_exp
    s_ref[...] = s
    o4 = jnp.sum((q_col * s).reshape(GLA_HEADS, GLA_DK, GLA_DV), axis=1)
    g_ref[...] = _rms_rows(o4) * gng_ref[...] * _silu(gg4_ref[...])

    z = m_ref[:, 0:M_DIN]
    acc = cb_ref[...] + cw_ref[M_CONV - 1:M_CONV, :] * m_ref[:, M_DIN:M_DIN + M_CONV_DIM]
    for s_ in range(M_CONV - 1):
        acc = acc + cw_ref[s_:s_ + 1, :] * buf_ref[s_:s_ + 1, :]
    xc = _silu(acc)
    x = xc[:, 0:M_DIN]
    bm = xc[:, M_DIN:M_DIN + M_GROUPS * M_N]
    cm = xc[:, M_DIN + M_GROUPS * M_N:]
    dt, a = _ssm_params(misc_ref[...], dtb_ref[...], nega_ref[...])
    e4 = e4_ref[...]
    xdt_col = _row_to_col(x * _dot_sel(dt, e4))
    deca_col = _row_to_col(_dot_sel(jnp.exp(a), e4))
    half = M_DIN // M_GROUPS
    spread = lambda t: jnp.concatenate(
        [jnp.broadcast_to(t[:, gi * M_N:(gi + 1) * M_N], (half, M_N)) for gi in range(M_GROUPS)], axis=0)
    hs = h0_ref[...] * deca_col + xdt_col * spread(bm)
    h_ref[...] = hs
    y = _col_to_row(jnp.sum(hs * spread(cm), axis=1, keepdims=True))
    y = (y + md_ref[...] * x) * _silu(z)
    y = jnp.concatenate([_rms_rows(y[:, gi * half:(gi + 1) * half]) for gi in range(M_GROUPS)], axis=1)
    y_ref[...] = y * gnm_ref[...]


def _step_mixers(gla, la, v4, gg4, s0, m, buf, misc, h0, cw, cb, dtb, nega, e4, md, gng, gnm):
    nb = gla.shape[0]
    per = lambda *shape: pl.BlockSpec((None,) + shape, lambda b: (b,) + (0,) * len(shape))
    const = lambda shape: pl.BlockSpec(shape, lambda b: (0,) * len(shape))
    return pl.pallas_call(
        _step_kernel,
        grid=(nb,),
        in_specs=[per(1, 768), per(1, LANE), per(GLA_HEADS, GLA_DV), per(GLA_HEADS, GLA_DV),
                  per(GLA_DQK, GLA_DV), per(1, 1024), per(M_CONV - 1, M_CONV_DIM), per(1, LANE),
                  per(M_DIN, M_N),
                  const((M_CONV, M_CONV_DIM)), const((1, M_CONV_DIM)), const((1, LANE)), const((1, LANE)),
                  const((LANE, M_DIN)), const((1, M_DIN)), const((1, GLA_DV)), const((1, M_DIN))],
        out_specs=[per(GLA_HEADS, GLA_DV), per(GLA_DQK, GLA_DV), per(1, M_DIN), per(M_DIN, M_N)],
        out_shape=[jax.ShapeDtypeStruct((nb, GLA_HEADS, GLA_DV), F32),
                   jax.ShapeDtypeStruct((nb, GLA_DQK, GLA_DV), F32),
                   jax.ShapeDtypeStruct((nb, 1, M_DIN), F32),
                   jax.ShapeDtypeStruct((nb, M_DIN, M_N), F32)],
        compiler_params=_cp("arbitrary"),
        name="step_mixers",
    )(gla.reshape(nb, 1, 768), la.reshape(nb, 1, LANE), v4, gg4, s0, m.reshape(nb, 1, 1024), buf,
      misc.reshape(nb, 1, LANE), h0, cw, cb, dtb, nega, e4, md, gng, gnm)


SB_PP = 8


def _sb_decode_kernel(pt_ref, q_ref, *refs):
    ks = refs[0:SB_PP]
    vs = refs[SB_PP:2 * SB_PP]
    bias_ref, gn_ref, o_ref, acc_ref, car_ref, qb_ref = refs[2 * SB_PP:]
    j = pl.program_id(1)
    page = ks[0].shape[2]

    @pl.when(j == 0)
    def _():
        acc_ref[...] = jnp.zeros_like(acc_ref)
        car_ref[...] = jnp.zeros_like(car_ref)
        qb_ref[...] = jnp.broadcast_to(q_ref[...], qb_ref.shape)

    bias = bias_ref[...]
    lane = lax.broadcasted_iota(jnp.int32, (SB_HEADS, page), 1)
    for r in range(SB_PP - 1, -1, -1):
        z = jnp.sum(ks[r][...] * qb_ref[...], axis=1) + bias
        sp = jnp.maximum(z, 0.0) + jnp.log(1.0 + jnp.exp2(_neg_abs(z))) * LOG2E
        ls = z - sp
        incl = sp
        s = 1
        while s < page:
            incl = incl + jnp.where(lane < page - s, pltpu.roll(incl, page - s, 1), 0.0)
            s *= 2
        car = car_ref[...]
        w = jnp.exp2(ls - (incl - sp + car))
        car_ref[...] = car + incl[:, 0:1]
        for h in range(SB_HEADS):
            acc_ref[h] += vs[r][h] * w[h:h + 1, :]

    @pl.when(j == pl.num_programs(1) - 1)
    def _():
        o_ref[...] = _rms_rows(jnp.sum(acc_ref[...], axis=-1)) * gn_ref[...]


def _sb_decode(page_table, q, cache_kt, cache_vt, layer, bias_col, gn):
    nb, n_pages = page_table.shape
    page = cache_kt.shape[4]
    ng = n_pages // SB_PP
    pt = page_table.reshape(-1)

    def kv_spec(r):
        return pl.BlockSpec((None, None, SB_HEADS, SB_HD, page),
                            lambda b, j, pt_ref: (layer, pt_ref[b * n_pages + (ng - 1 - j) * SB_PP + r], 0, 0, 0))

    const = lambda shape: pl.BlockSpec(shape, lambda b, j, pt_ref: (0,) * len(shape))
    grid_spec = pltpu.PrefetchScalarGridSpec(
        num_scalar_prefetch=1,
        grid=(nb, ng),
        in_specs=[pl.BlockSpec((None, SB_HEADS, SB_HD, 1), lambda b, j, pt_ref: (b, 0, 0, 0))]
        + [kv_spec(r) for r in range(SB_PP)] + [kv_spec(r) for r in range(SB_PP)]
        + [const((SB_HEADS, 1)), const((1, SB_HD))],
        out_specs=pl.BlockSpec((None, SB_HEADS, SB_HD), lambda b, j, pt_ref: (b, 0, 0)),
        scratch_shapes=[pltpu.VMEM((SB_HEADS, SB_HD, page), F32), pltpu.VMEM((SB_HEADS, 1), F32),
                        pltpu.VMEM((SB_HEADS, SB_HD, page), F32)],
    )
    out = pl.pallas_call(
        _sb_decode_kernel,
        grid_spec=grid_spec,
        out_shape=jax.ShapeDtypeStruct((nb, SB_HEADS, SB_HD), F32),
        compiler_params=_cp("arbitrary", "arbitrary"),
        name="sb_decode",
    )(pt, q, *([cache_kt] * SB_PP), *([cache_vt] * SB_PP), bias_col, gn)
    return out.reshape(nb, SB_W)


def _outproj_kernel(x_ref, gate_ref, g_ref, s_ref, y_ref, w_ref, o_ref):
    mix = (_dot(g_ref[...], w_ref[0:GLA_DVW, :]) + _dot(s_ref[...], w_ref[GLA_DVW:GLA_DVW + SB_W, :])
           + _dot(y_ref[...], w_ref[GLA_DVW + SB_W:, :]))
    o_ref[...] = x_ref[...] + gate_ref[...] * mix


def _out_proj(x, gate, g, s, y, w, *, tm, rows_per_mod):
    r, d = x.shape
    mrows = gate.shape[1]
    row = lambda n: pl.BlockSpec((tm, n), lambda i: (i, 0))
    return pl.pallas_call(
        _outproj_kernel,
        grid=(r // tm,),
        in_specs=[row(d), pl.BlockSpec((None, mrows, d), lambda i: (i // rows_per_mod, 0, 0)),
                  row(GLA_DVW), row(SB_W), row(M_DIN), pl.BlockSpec(w.shape, lambda i: (0, 0))],
        out_specs=row(d),
        out_shape=jax.ShapeDtypeStruct((r, d), F32),
        compiler_params=_cp("arbitrary"),
        name="out_proj",
    )(x, gate, g, s, y, w)


def _ffn_up_kernel(x_ref, sc_ref, sh_ref, g_ref, w_ref, u_ref):
    h = _norm_mod(x_ref[...], g_ref[...], sc_ref[...], sh_ref[...])
    u_ref[...] = jnp.dot(h.astype(BF16), w_ref[...], preferred_element_type=F32)


def _ffn_up(x, sc, sh, g, w, *, tm, rows_per_mod):
    r, d = x.shape
    n = w.shape[1]
    mrows = sc.shape[1]
    mod_spec = pl.BlockSpec((None, mrows, d), lambda i: (i // rows_per_mod, 0, 0))
    return pl.pallas_call(
        _ffn_up_kernel,
        grid=(r // tm,),
        in_specs=[pl.BlockSpec((tm, d), lambda i: (i, 0)), mod_spec, mod_spec,
                  pl.BlockSpec((1, d), lambda i: (0, 0)), pl.BlockSpec((d, n), lambda i: (0, 0))],
        out_specs=pl.BlockSpec((tm, n), lambda i: (i, 0)),
        out_shape=jax.ShapeDtypeStruct((r, n), F32),
        compiler_params=_cp("arbitrary"),
        name="ffn_up",
    )(x, sc, sh, g, w)


FFN_CK = 256


def _ffn_act_down(taps, cw_ref, cb_ref, w_ref):
    dff = w_ref.shape[0]

    def conv(c0):
        u, u1, u2 = taps(c0)
        cs = slice(c0, c0 + FFN_CK)
        return cb_ref[:, cs] + cw_ref[0:1, cs] * u2 + cw_ref[1:2, cs] * u1 + cw_ref[2:3, cs] * u

    out = None
    for c0 in range(0, dff, FFN_CK):
        part = _dot(_silu(conv(c0)) * conv(dff + c0), w_ref[c0:c0 + FFN_CK, :])
        out = part if out is None else out + part
    return out


def _ffn_down_prompt_kernel(x_ref, gate_ref, u_ref, prev_ref, cw_ref, cb_ref, w_ref, o_ref, *, tiles_per_seq):
    i = pl.program_id(0)
    tm = u_ref.shape[0]
    first = i % tiles_per_seq == 0
    row = lax.broadcasted_iota(jnp.int32, (tm, FFN_CK), 0)
    row8 = lax.broadcasted_iota(jnp.int32, (SUBLANE, FFN_CK), 0)

    def taps(c0):
        u = u_ref[:, c0:c0 + FFN_CK]
        prev = jnp.where(first, 0.0, prev_ref[:, c0:c0 + FFN_CK])
        shifted = []
        for s in (1, 2):
            head = jnp.where(row8 < s, pltpu.roll(prev, s, 0), 0.0)
            head = jnp.concatenate([head, jnp.zeros((tm - SUBLANE, FFN_CK), F32)], axis=0)
            shifted.append(jnp.where(row < s, head, pltpu.roll(u, s, 0)))
        return u, shifted[0], shifted[1]

    o_ref[...] = x_ref[...] + gate_ref[...] * _ffn_act_down(taps, cw_ref, cb_ref, w_ref)


def _ffn_down_prompt(x, gate, u, cw, cb, w, *, tm, seq):
    r, d = x.shape
    n = u.shape[1]
    tiles = seq // tm
    per8 = tm // SUBLANE
    return pl.pallas_call(
        functools.partial(_ffn_down_prompt_kernel, tiles_per_seq=tiles),
        grid=(r // tm,),
        in_specs=[pl.BlockSpec((tm, d), lambda i: (i, 0)),
                  pl.BlockSpec((None, 1, d), lambda i: (i // tiles, 0, 0)),
                  pl.BlockSpec((tm, n), lambda i: (i, 0)),
                  pl.BlockSpec((SUBLANE, n), lambda i: (jnp.maximum(i * per8 - 1, 0), 0)),
                  pl.BlockSpec((FFN_CONV, n), lambda i: (0, 0)), pl.BlockSpec((1, n), lambda i: (0, 0)),
                  pl.BlockSpec(w.shape, lambda i: (0, 0))],
        out_specs=pl.BlockSpec((tm, d), lambda i: (i, 0)),
        out_shape=jax.ShapeDtypeStruct((r, d), F32),
        compiler_params=_cp("arbitrary"),
        name="ffn_down_prompt",
    )(x, gate, u, u, cw, cb, w)


def _ffn_down_step_kernel(x_ref, gate_ref, u_ref, buf_ref, cw_ref, cb_ref, w_ref, o_ref):
    taps = lambda c0: (u_ref[:, c0:c0 + FFN_CK], buf_ref[1, :, c0:c0 + FFN_CK], buf_ref[0, :, c0:c0 + FFN_CK])
    o_ref[...] = x_ref[...] + gate_ref[...] * _ffn_act_down(taps, cw_ref, cb_ref, w_ref)


def _ffn_down_step(x, gate, u, buf_t, cw, cb, w):
    full = lambda a: pl.BlockSpec(a.shape, lambda i: (0,) * a.ndim)
    args = (x, gate, u, buf_t, cw, cb, w)
    return pl.pallas_call(
        _ffn_down_step_kernel,
        grid=(1,),
        in_specs=[full(a) for a in args],
        out_specs=full(x),
        out_shape=jax.ShapeDtypeStruct(x.shape, F32),
        compiler_params=_cp("arbitrary"),
        name="ffn_down_step",
    )(*args)


def _same_segment(n, width):
    i = np.arange(n)
    return jnp.asarray((i[:, None] // width) == (i[None, :] // width), BF16)


def _constants():
    k = np.arange(GLA_DQK)
    v = np.arange(GLA_DVW)
    eexp = (k[:, None] // GLA_DK) == (v[None, :] // GLA_DV)
    j = np.arange(SB_TK)
    later = j[:, None] > j[None, :]
    uo = np.block([[later, np.zeros_like(later)], [np.zeros_like(later), later]])
    lane = np.arange(LANE)
    hm_pair = np.zeros((SUBLANE, LANE), np.float32)
    hm_pair[0] = lane < SB_HD
    hm_pair[1] = lane >= SB_HD
    p = np.arange(M_DIN)
    e4 = lane[:, None] == (p[None, :] // M_HD)
    return dict(
        eexp=jnp.asarray(eexp, BF16), bd=jnp.asarray(eexp.T, F32),
        uo=jnp.asarray(uo, BF16), hm_pair=jnp.asarray(hm_pair, BF16), eseg64=_same_segment(LANE, 64),
        e4=jnp.asarray(e4, BF16), e4t=jnp.asarray(e4.T, BF16))


def _pad_lanes(v, n=LANE):
    return jnp.zeros((1, n), F32).at[0, :v.shape[0]].set(v)


def _pad_col(v, n=SUBLANE):
    return jnp.zeros((n, 1), F32).at[:v.shape[0], 0].set(v)


def kernel(x_prompt, x_sample, cache_sb_k, cache_sb_v, state_gla, state_mamba_conv, state_mamba_ssm, state_ffn_conv, page_table, c_prompt, c_sample, norm1_g, w_ada, b_ada, w_in, gla_w_gate2, gla_b_gate, gla_norm_g, sb_q_norm_g, sb_k_norm_g, sb_o_norm_g, sb_bias, m_conv_w, m_conv_b, m_dt_bias, m_a_log, m_d, m_norm_g, w_out, norm2_g, ffn_w_up, ffn_conv_w, ffn_conv_b, ffn_w_down):
    bsz, seq, d = x_prompt.shape
    nb = x_sample.shape[0]
    depth = w_in.shape[0]
    dff = ffn_w_down.shape[1]
    n_pool, page = cache_sb_k.shape[1], cache_sb_k.shape[2]
    assert x_sample.shape[1] == 1 and d % LANE == 0 and seq % 256 == 0
    cst = _constants()

    tm = 256
    tq = 512
    tt_gla = 256
    tt_ssd = 128
    assert (tq // SB_TK) % SB_UNROLL == 0 and page_table.shape[1] % SB_PP == 0
    assert seq % tq == 0 and seq % tm == 0 and seq % tt_gla == 0 and seq % tt_ssd == 0

    mod = _modulation(jnp.concatenate([c_prompt, c_sample], axis=0), w_ada, b_ada)
    mod = mod.reshape(depth, bsz + nb, N_MOD, d)
    mod_p = mod[:, :bsz].transpose(0, 2, 1, 3).reshape(depth, N_MOD, bsz, 1, d)
    mod_s = mod[:, bsz:].transpose(0, 2, 1, 3).reshape(depth, N_MOD, 1, nb, d)

    cache_kt = cache_sb_k.transpose(0, 1, 3, 4, 2)
    cache_vt = cache_sb_v.transpose(0, 1, 3, 4, 2)

    xp = x_prompt.reshape(bsz * seq, d)
    xs = x_sample.reshape(nb, d)
    outs = {name: [] for name in ("pk", "pv", "pg", "pc", "ph", "pf", "sk", "sv", "sg", "sc", "sh", "sf")}
    for l in range(depth):
        wi = w_in[l]
        o = np.cumsum([0, GLA_DQK, GLA_DQK, GLA_DVW, GLA_DVW, GLA_LR, SB_W, SB_W, SB_W, M_DIN, M_CONV_DIM, M_HEADS])
        sl = lambda a, b: wi[:, o[a]:o[b]]
        w_re = jnp.concatenate(
            [sl(0, 4), sl(5, 8), sl(8, 10), sl(10, 11), sl(4, 5),
             jnp.zeros((d, LANE - M_HEADS - GLA_LR), F32)], axis=1).astype(BF16)
        wg = jnp.zeros((LANE, LANE), F32).at[M_HEADS:M_HEADS + GLA_LR].set(gla_w_gate2[l]).astype(BF16)
        bg = gla_b_gate[l].reshape(1, GLA_DQK)
        gq = jnp.tile(sb_q_norm_g[l], SB_HEADS).reshape(1, SB_W)
        gk = jnp.tile(sb_k_norm_g[l], SB_HEADS).reshape(1, SB_W)
        go = jnp.tile(sb_o_norm_g[l], SB_HEADS).reshape(1, SB_W)
        gng = jnp.tile(gla_norm_g[l], GLA_HEADS).reshape(1, GLA_DVW)
        g1 = norm1_g[l].reshape(1, d)
        g2 = norm2_g[l].reshape(1, d)
        bias_pair = jnp.repeat(sb_bias[l] * LOG2E, SB_TK).reshape(SB_HEADS // 2, 1, 2 * SB_TK)
        bias_col = (sb_bias[l] * LOG2E).reshape(SB_HEADS, 1)
        cw = m_conv_w[l]
        cb = m_conv_b[l].reshape(1, M_CONV_DIM)
        nega = -jnp.exp(m_a_log[l])
        dtb_l, nega_l = _pad_lanes(m_dt_bias[l]), _pad_lanes(nega)
        dtb_c, nega_c = _pad_col(m_dt_bias[l]), _pad_col(nega)
        md = jnp.repeat(m_d[l], M_HD).reshape(1, M_DIN)
        gnm = m_norm_g[l].reshape(1, M_DIN)
        wo = w_out[l].astype(BF16)
        wu = ffn_w_up[l].astype(BF16)
        wd = ffn_w_down[l].astype(BF16)
        fcw = ffn_conv_w[l]
        fcb = ffn_conv_b[l].reshape(1, 2 * dff)
        sh1, sc1, gt1, sh2, sc2, gt2 = range(N_MOD)

        mp = mod_p[l]
        gla, la, q, k, kb, v, vb, m, misc = _in_proj(
            xp, mp[sc1], mp[sh1], g1, w_re, wg, bg, gq, gk, cst["eseg64"], tm=tm, rows_per_mod=seq // tm,
            kv_transposed=True)
        g_mix, g_state = _gla_prompt(gla, la, cst["eexp"], cst["bd"], cst["eseg64"], gng,
                                     bsz=bsz, seq=seq, tt=tt_gla)
        s_mix = _sb_prompt(q, kb, vb, bias_pair, cst["uo"], cst["hm_pair"], cst["eseg64"], go[:, :LANE],
                           bsz=bsz, seq=seq, tq=tq)
        dtt = jnp.pad(misc[:, :M_HEADS].reshape(bsz, seq, M_HEADS).transpose(0, 2, 1),
                      ((0, 0), (0, SUBLANE - M_HEADS), (0, 0)))
        y_mix, h_state = _ssd_prompt(m, misc, dtt, cw, cb, dtb_l, nega_l, dtb_c, nega_c, cst["e4"], cst["e4t"],
                                     md, gnm, bsz=bsz, seq=seq, tt=tt_ssd)
        xp = _out_proj(xp, mp[gt1], g_mix, s_mix, y_mix, wo, tm=tm, rows_per_mod=seq // tm)
        u = _ffn_up(xp, mp[sc2], mp[sh2], g2, wu, tm=tm, rows_per_mod=seq // tm)
        xp = _ffn_down_prompt(xp, mp[gt2], u, fcw, fcb, wd, tm=tm, seq=seq)

        outs["pk"].append(k.reshape(bsz, SB_HEADS, SB_HD, seq).transpose(0, 3, 1, 2))
        outs["pv"].append(v.reshape(bsz, SB_HEADS, SB_HD, seq).transpose(0, 3, 1, 2))
        gs = g_state.reshape(bsz, GLA_HEADS, GLA_DV, GLA_HEADS, GLA_DK)
        outs["pg"].append(jnp.stack([gs[:, h, :, h, :] for h in range(GLA_HEADS)], axis=1).transpose(0, 1, 3, 2))
        outs["pc"].append(m.reshape(bsz, seq, 1024)[:, seq - (M_CONV - 1):, M_DIN:])
        outs["ph"].append(h_state.reshape(bsz, M_HEADS, M_HD, M_N))
        outs["pf"].append(u.reshape(bsz, seq, 2 * dff)[:, seq - (FFN_CONV - 1):])

        ms = mod_s[l]
        gla, la, q, k, kb, v, vb, m, misc = _in_proj(
            xs, ms[sc1], ms[sh1], g1, w_re, wg, bg, gq, gk, cst["eseg64"], tm=nb, rows_per_mod=1,
            kv_transposed=False)
        g4, g_state, y_mix, h_state = _step_mixers(
            gla, la, gla[:, 2 * GLA_DQK:2 * GLA_DQK + GLA_DVW].reshape(nb, GLA_HEADS, GLA_DV),
            gla[:, 2 * GLA_DQK + GLA_DVW:].reshape(nb, GLA_HEADS, GLA_DV),
            state_gla[l].reshape(nb, GLA_DQK, GLA_DV), m, state_mamba_conv[l], misc,
            state_mamba_ssm[l].reshape(nb, M_DIN, M_N), cw, cb, dtb_l, nega_l, cst["e4"], md,
            gla_norm_g[l].reshape(1, GLA_DV), gnm)
        s_mix = _sb_decode(page_table, q.astype(F32).reshape(nb, SB_HEADS, SB_HD, 1), cache_kt, cache_vt, l,
                           bias_col, sb_o_norm_g[l].reshape(1, SB_HD))
        xs = _out_proj(xs, ms[gt1], g4.reshape(nb, GLA_DVW), s_mix, y_mix.reshape(nb, M_DIN), wo,
                       tm=nb, rows_per_mod=1)
        u = _ffn_up(xs, ms[sc2], ms[sh2], g2, wu, tm=nb, rows_per_mod=1)
        xs = _ffn_down_step(xs, ms[gt2][0], u, state_ffn_conv[l].transpose(1, 0, 2), fcw, fcb, wd)

        outs["sk"].append(k.reshape(nb, 1, SB_HEADS, SB_HD))
        outs["sv"].append(v.reshape(nb, 1, SB_HEADS, SB_HD))
        outs["sg"].append(g_state.reshape(nb, GLA_HEADS, GLA_DK, GLA_DV))
        outs["sc"].append(jnp.concatenate([state_mamba_conv[l][:, 1:], m[:, None, M_DIN:]], axis=1))
        outs["sh"].append(h_state.reshape(nb, M_HEADS, M_HD, M_N))
        outs["sf"].append(jnp.concatenate([state_ffn_conv[l][:, 1:], u[:, None, :]], axis=1))

    st = {name: jnp.stack(v) for name, v in outs.items()}
    return (xp.reshape(bsz, seq, d), xs.reshape(nb, 1, d), st["pk"], st["pv"], st["pg"], st["pc"], st["ph"],
            st["pf"], st["sk"], st["sv"], st["sg"], st["sc"], st["sh"], st["sf"])
```

```python
import functools

import jax
import jax.numpy as jnp
import numpy as np
from jax import lax
from jax.experimental import pallas as pl
from jax.experimental.pallas import tpu as pltpu

F32 = jnp.float32
BF16 = jnp.bfloat16

GLA_HEADS, GLA_DK, GLA_DV, GLA_LR, GLA_TAU = 4, 32, 64, 16, 16.0
GLA_DQK = GLA_HEADS * GLA_DK
GLA_DVW = GLA_HEADS * GLA_DV
SB_HEADS, SB_HD = 8, 64
SB_W = SB_HEADS * SB_HD
M_HEADS, M_HD, M_GROUPS, M_N, M_CONV = 4, 64, 2, 128, 4
M_DIN = M_HEADS * M_HD
M_CONV_DIM = M_DIN + 2 * M_GROUPS * M_N
FFN_CONV = 3
N_MOD = 6
EPS = 1e-6
NEG_BIG = -1e30
LOG2E = 1.4426950408889634

C_GLA = 0
C_SBQ = 768
C_SBK = 1280
C_SBV = 1792
C_MZ = 2304
C_MISC = 3328
D_INP = 3456
LANE = 128
SUBLANE = 8

VMEM_LIMIT = 56 * 1024 * 1024


def _cp(*sem):
    return pltpu.CompilerParams(dimension_semantics=sem, vmem_limit_bytes=VMEM_LIMIT)


def _dot(a, b):
    return jnp.dot(a.astype(BF16), b.astype(BF16), preferred_element_type=F32)


def _dot_nt(a, b):
    return lax.dot_general(a.astype(BF16), b.astype(BF16), (((1,), (1,)), ((), ())),
                           preferred_element_type=F32)


def _dot_tn(a, b):
    return lax.dot_general(a.astype(BF16), b.astype(BF16), (((0,), (0,)), ((), ())),
                           preferred_element_type=F32)


def _split3(x):
    hi = x.astype(BF16)
    r = x - hi.astype(F32)
    mid = r.astype(BF16)
    lo = (r - mid.astype(F32)).astype(BF16)
    return hi, mid, lo


def _dot_sel(x, e):
    hi, mid, lo = _split3(x)
    d = lambda p: jnp.dot(p, e, preferred_element_type=F32)
    return d(hi) + d(mid) + d(lo)


def _dot_sel_nt(e, x):
    hi, mid, lo = _split3(x)
    d = lambda p: lax.dot_general(e, p, (((1,), (1,)), ((), ())), preferred_element_type=F32)
    return d(hi) + d(mid) + d(lo)


def _sigmoid(x):
    return 1.0 / (1.0 + jnp.exp(-x))


def _silu(x):
    return x * _sigmoid(x)


def _softplus(x):
    return jnp.maximum(x, 0.0) + jnp.log1p(jnp.exp(-jnp.abs(x)))


def _log_sigmoid(x):
    return jnp.minimum(x, 0.0) - jnp.log1p(jnp.exp(-jnp.abs(x)))


def _rms_rows(x):
    return x * lax.rsqrt(jnp.mean(x * x, axis=-1, keepdims=True) + EPS)


def _seg_rms(x, eseg, width):
    xx = x * x
    hi = xx.astype(BF16)
    lo = (xx - hi.astype(F32)).astype(BF16)
    blocks = []
    for c in range(0, x.shape[1], LANE):
        d = lambda p: jnp.dot(p[:, c:c + LANE], eseg, preferred_element_type=F32)
        blocks.append(d(hi) + d(lo))
    ms = jnp.concatenate(blocks, axis=1) * (1.0 / width)
    return x * lax.rsqrt(ms + EPS)


def _row_to_col(row):
    n = row.shape[1]
    eye = lax.broadcasted_iota(jnp.int32, (n, n), 0) == lax.broadcasted_iota(jnp.int32, (n, n), 1)
    return jnp.sum(jnp.where(eye, row, 0.0), axis=1, keepdims=True)


def _col_to_row(col):
    n = col.shape[0]
    eye = lax.broadcasted_iota(jnp.int32, (n, n), 0) == lax.broadcasted_iota(jnp.int32, (n, n), 1)
    return jnp.sum(jnp.where(eye, col, 0.0), axis=0, keepdims=True)


def _cumsum(x, axis, seg=None):
    n = x.shape[axis] if seg is None else seg
    idx = lax.broadcasted_iota(jnp.int32, x.shape, axis)
    if seg is not None:
        idx = idx % seg
    s = 1
    while s < n:
        x = x + jnp.where(idx >= s, pltpu.roll(x, s, axis), 0.0)
        s *= 2
    return x


def _mod_kernel(c_ref, w_ref, b_ref, o_ref):
    o_ref[...] = _dot(_silu(c_ref[...]), w_ref[...]) + b_ref[...]


def _modulation(c_all, w_ada, b_ada):
    depth, d, nd = w_ada.shape
    n = c_all.shape[0]
    tn = 1024
    return pl.pallas_call(
        _mod_kernel,
        grid=(depth, nd // tn),
        in_specs=[pl.BlockSpec((n, d), lambda l, j: (0, 0)),
                  pl.BlockSpec((None, d, tn), lambda l, j: (l, 0, j)),
                  pl.BlockSpec((None, 1, tn), lambda l, j: (l, 0, j))],
        out_specs=pl.BlockSpec((None, n, tn), lambda l, j: (l, 0, j)),
        out_shape=jax.ShapeDtypeStruct((depth, n, nd), F32),
        compiler_params=_cp("arbitrary", "arbitrary"),
        name="adaln_mod",
    )(c_all, w_ada, b_ada.reshape(depth, 1, nd))


def _norm_mod(x, g, sc, sh):
    return _rms_rows(x) * g * (1.0 + sc) + sh


def _inproj_kernel(x_ref, sc_ref, sh_ref, g_ref, w_ref, wg_ref, bg_ref, gq_ref, gk_ref, eseg_ref,
                   gla_ref, la_ref, q_ref, k_ref, kb_ref, v_ref, vb_ref, m_ref, misc_ref, *, kv_transposed):
    h = _norm_mod(x_ref[...], g_ref[...], sc_ref[...], sh_ref[...])
    p = lax.dot_general(h.astype(BF16), w_ref[...], (((1,), (1,)), ((), ())), preferred_element_type=F32)
    gla_ref[...] = p[:, C_GLA:C_SBQ]
    misc = p[:, C_MISC:D_INP]
    misc_ref[...] = misc
    la_ref[...] = _log_sigmoid(_dot(misc, wg_ref[...]) + bg_ref[...]) * (1.0 / GLA_TAU)
    eseg = eseg_ref[...]
    q = _seg_rms(p[:, C_SBQ:C_SBK], eseg, SB_HD) * gq_ref[...]
    q_ref[...] = (q * (SB_HD ** -0.5 * LOG2E)).astype(BF16)
    k = _seg_rms(p[:, C_SBK:C_SBV], eseg, SB_HD) * gk_ref[...]
    k_ref[...] = k.T if kv_transposed else k
    kb_ref[...] = k.astype(BF16)
    v = p[:, C_SBV:C_MZ]
    v_ref[...] = v.T if kv_transposed else v
    vb_ref[...] = v.astype(BF16)
    m_ref[...] = p[:, C_MZ:C_MISC]


def _in_proj(x, sc, sh, g, w, wg, bg, gq, gk, eseg, *, tm, rows_per_mod, kv_transposed):
    r, d = x.shape
    mrows = sc.shape[1]
    mod_spec = pl.BlockSpec((None, mrows, d), lambda i: (i // rows_per_mod, 0, 0))
    const = lambda shape: pl.BlockSpec(shape, lambda i: (0,) * len(shape))
    row = lambda n: pl.BlockSpec((tm, n), lambda i: (i, 0))
    outs = [(768, F32), (LANE, F32), (SB_W, BF16), (SB_W, F32), (SB_W, BF16), (SB_W, F32), (SB_W, BF16),
            (1024, F32), (LANE, F32)]
    out_specs = [row(n) for n, _ in outs]
    out_shape = [jax.ShapeDtypeStruct((r, n), dt) for n, dt in outs]
    if kv_transposed:
        seq = rows_per_mod * tm
        for idx in (3, 5):
            out_specs[idx] = pl.BlockSpec((None, SB_W, tm), lambda i: (i // rows_per_mod, 0, i % rows_per_mod))
            out_shape[idx] = jax.ShapeDtypeStruct((r // seq, SB_W, seq), F32)
    return pl.pallas_call(
        functools.partial(_inproj_kernel, kv_transposed=kv_transposed),
        grid=(r // tm,),
        in_specs=[row(d), mod_spec, mod_spec, const((1, d)), const((D_INP, d)), const((LANE, LANE)),
                  const((1, LANE)), const((1, SB_W)), const((1, SB_W)), const((LANE, LANE))],
        out_specs=out_specs,
        out_shape=out_shape,
        compiler_params=_cp("arbitrary"),
        name="in_proj",
    )(x, sc, sh, g, w, wg, bg, gq, gk, eseg)


GLA_C = SUBLANE
GLA_UNROLL = 8


def _gla_prompt_kernel(gla_ref, la_ref, eexp_ref, bd_ref, eseg_ref, gn_ref, o_ref, st_out_ref,
                       st_ref, b_ref, acc_ref):
    i = pl.program_id(1)
    tt = la_ref.shape[0]

    @pl.when(i == 0)
    def _():
        st_ref[...] = jnp.zeros_like(st_ref)

    b_ref[...] = _cumsum(la_ref[...], 0, seg=GLA_C)
    eexp = eexp_ref[...]
    bd = bd_ref[...]
    rowid = lax.broadcasted_iota(jnp.int32, (GLA_C, GLA_DQK), 0)
    rowid_v = lax.broadcasted_iota(jnp.int32, (GLA_C, GLA_DVW), 0)

    def group(gi, carry):
        steps = []
        for u in range(GLA_UNROLL):
            r0 = pl.multiple_of((gi * GLA_UNROLL + u) * GLA_C, GLA_C)
            bc = b_ref[pl.ds(r0, GLA_C), :]
            qc = gla_ref[pl.ds(r0, GLA_C), 0:GLA_DQK] * (GLA_DK ** -0.5)
            kc = gla_ref[pl.ds(r0, GLA_C), GLA_DQK:2 * GLA_DQK]
            vc = gla_ref[pl.ds(r0, GLA_C), 2 * GLA_DQK:2 * GLA_DQK + GLA_DVW]
            steps.append((r0, bc, qc, kc, vc, bc[GLA_C - 1:GLA_C, :]))
        ps = []
        for r0, bc, qc, kc, vc, bl in steps:
            for t in range(GLA_C):
                d = jnp.where(rowid <= t, bc[t:t + 1, :] - bc, NEG_BIG)
                ps.append(jnp.exp(d) * (qc[t:t + 1, :] * kc))
        sc = _dot(jnp.concatenate(ps, axis=0), eexp)
        upds = [_dot_tn(vc, kc * jnp.exp(bl - bc)) * bd for r0, bc, qc, kc, vc, bl in steps]
        st = st_ref[...]
        for u, (r0, bc, qc, kc, vc, bl) in enumerate(steps):
            o = _dot_nt(qc * jnp.exp(bc), st)
            st = st * jnp.exp(bl) + upds[u]
            for t in range(GLA_C):
                row0 = (u * GLA_C + t) * GLA_C
                ot = jnp.sum(sc[row0:row0 + GLA_C, :] * vc, axis=0, keepdims=True)
                o = o + jnp.where(rowid_v == t, ot, 0.0)
            acc_ref[pl.ds(r0, GLA_C), :] = o
        st_ref[...] = st
        return carry

    lax.fori_loop(0, tt // (GLA_C * GLA_UNROLL), group, 0)
    gg = gla_ref[:, 2 * GLA_DQK + GLA_DVW:2 * GLA_DQK + 2 * GLA_DVW]
    o = _seg_rms(acc_ref[...], eseg_ref[...], GLA_DV) * gn_ref[...] * _silu(gg)
    o_ref[...] = o.astype(BF16)
    st_out_ref[...] = st_ref[...]


def _gla_prompt(gla, la, eexp, bd, eseg, gn, *, bsz, seq, tt):
    nt = seq // tt
    const = lambda shape: pl.BlockSpec(shape, lambda b, i: (0,) * len(shape))
    return pl.pallas_call(
        _gla_prompt_kernel,
        grid=(bsz, nt),
        in_specs=[pl.BlockSpec((tt, 768), lambda b, i: (b * nt + i, 0)),
                  pl.BlockSpec((tt, LANE), lambda b, i: (b * nt + i, 0)),
                  const((GLA_DQK, GLA_DVW)), const((GLA_DVW, GLA_DQK)), const((LANE, LANE)),
                  const((1, GLA_DVW))],
        out_specs=[pl.BlockSpec((tt, GLA_DVW), lambda b, i: (b * nt + i, 0)),
                   pl.BlockSpec((None, GLA_DVW, GLA_DQK), lambda b, i: (b, 0, 0))],
        out_shape=[jax.ShapeDtypeStruct((bsz * seq, GLA_DVW), BF16),
                   jax.ShapeDtypeStruct((bsz, GLA_DVW, GLA_DQK), F32)],
        scratch_shapes=[pltpu.VMEM((GLA_DVW, GLA_DQK), F32), pltpu.VMEM((tt, GLA_DQK), F32),
                        pltpu.VMEM((tt, GLA_DVW), F32)],
        compiler_params=_cp("arbitrary", "arbitrary"),
        name="gla_prompt",
    )(gla, la, eexp, bd, eseg, gn)


SB_TK = LANE
SB_UNROLL = 8


def _neg_abs(x):
    return lax.bitcast_convert_type(lax.bitcast_convert_type(x, jnp.int32) | jnp.int32(-2 ** 31), F32)


def _sb_prompt_kernel(q_ref, k_ref, v_ref, bias_ref, uo_ref, hm_ref, eseg_ref, gn_ref, o_ref,
                      acc_ref, car_ref, kk_ref, vv_ref):
    i = pl.program_id(2)
    tq = q_ref.shape[0]
    ndiag = tq // SB_TK
    nblk = kk_ref.shape[0]

    @pl.when(i == 0)
    def _():
        m0 = hm_ref[0:1, :]
        m1 = hm_ref[1:2, :]

        def fill(j, carry):
            k0 = pl.multiple_of(j * SB_TK, SB_TK)
            kb = k_ref[pl.ds(k0, SB_TK), :]
            vb = v_ref[pl.ds(k0, SB_TK), :]
            kk_ref[j] = jnp.concatenate([kb * m0, kb * m1], axis=0)
            vv_ref[j] = jnp.concatenate([vb * m0, vb * m1], axis=0)
            return carry

        lax.fori_loop(0, nblk, fill, 0)

    bias = bias_ref[...]
    uo = uo_ref[...]
    acc_ref[...] = jnp.zeros_like(acc_ref)
    car_ref[...] = jnp.zeros_like(car_ref)

    def block(j, r0, diag):
        rows = tq - r0
        z = lax.dot_general(q_ref[r0:, :], kk_ref[j], (((1,), (1,)), ((), ())),
                            preferred_element_type=F32) + bias
        sp = jnp.maximum(z, 0.0) + jnp.log(1.0 + jnp.exp2(_neg_abs(z))) * LOG2E
        ls = z - sp
        if diag:
            valid = (lax.broadcasted_iota(jnp.int32, (rows, 2 * SB_TK), 1) % SB_TK
                     < lax.broadcasted_iota(jnp.int32, (rows, 2 * SB_TK), 0))
            sp = jnp.where(valid, sp, 0.0)
        later = jnp.dot(sp.astype(BF16), uo, preferred_element_type=F32) + car_ref[r0:, :]
        w = jnp.exp2(ls - later)
        if diag:
            w = jnp.where(valid, w, 0.0)
        car_ref[r0:, :SB_TK] += jnp.sum(sp[:, :SB_TK], axis=1, keepdims=True)
        car_ref[r0:, SB_TK:] += jnp.sum(sp[:, SB_TK:], axis=1, keepdims=True)
        acc_ref[r0:, :] += jnp.dot(w.astype(BF16), vv_ref[j], preferred_element_type=F32)

    for c in range(ndiag - 1, -1, -1):
        block(i * ndiag + c, c * SB_TK, True)

    n_off = i * ndiag
    n_main = n_off // SB_UNROLL

    def body(jj, carry):
        for u in range(SB_UNROLL):
            block(n_off - 1 - jj * SB_UNROLL - u, 0, False)
        return carry

    lax.fori_loop(0, n_main, body, 0)
    if SB_UNROLL > ndiag:
        @pl.when(n_off - n_main * SB_UNROLL > 0)
        def _():
            for u in range(ndiag):
                block(ndiag - 1 - u, 0, False)

    o_ref[...] = (_seg_rms(acc_ref[...], eseg_ref[...], SB_HD) * gn_ref[...]).astype(BF16)


def _sb_prompt(q, k, v, bias2, uo, hm, eseg, gn, *, bsz, seq, tq):
    nq = seq // tq
    npair = SB_HEADS // 2
    const = lambda shape: pl.BlockSpec(shape, lambda b, p, i: (0,) * len(shape))
    return pl.pallas_call(
        _sb_prompt_kernel,
        grid=(bsz, npair, nq),
        in_specs=[pl.BlockSpec((tq, LANE), lambda b, p, i: (b * nq + i, p)),
                  pl.BlockSpec((seq, LANE), lambda b, p, i: (b, p)),
                  pl.BlockSpec((seq, LANE), lambda b, p, i: (b, p)),
                  pl.BlockSpec((None, 1, 2 * SB_TK), lambda b, p, i: (p, 0, 0)),
                  const((2 * SB_TK, 2 * SB_TK)), const((SUBLANE, LANE)), const((LANE, LANE)), const((1, LANE))],
        out_specs=pl.BlockSpec((tq, LANE), lambda b, p, i: (b * nq + i, p)),
        out_shape=jax.ShapeDtypeStruct((bsz * seq, SB_W), BF16),
        scratch_shapes=[pltpu.VMEM((tq, LANE), F32), pltpu.VMEM((tq, 2 * SB_TK), F32),
                        pltpu.VMEM((seq // SB_TK, 2 * SB_TK, LANE), BF16),
                        pltpu.VMEM((seq // SB_TK, 2 * SB_TK, LANE), BF16)],
        compiler_params=_cp("arbitrary", "arbitrary", "arbitrary"),
        name="sb_prompt",
    )(q, k, v, bias2, uo, hm, eseg, gn)


def _ssm_params(dt_raw, dtb, nega):
    dt = _softplus(dt_raw + dtb)
    return dt, dt * nega


def _ssd_prompt_kernel(m_ref, prev_ref, misc_ref, dtt_ref, cw_ref, cb_ref, dtb_l_ref, nega_l_ref,
                       dtb_c_ref, nega_c_ref, e4_ref, e4t_ref, md_ref, gn_ref, o_ref, hs_out_ref,
                       hs_ref):
    i = pl.program_id(1)
    tt = m_ref.shape[0]

    @pl.when(i == 0)
    def _():
        hs_ref[...] = jnp.zeros_like(hs_ref)

    z = m_ref[:, 0:M_DIN]
    xbc = m_ref[:, M_DIN:M_DIN + M_CONV_DIM]
    prev = jnp.where(i > 0, prev_ref[:, M_DIN:M_DIN + M_CONV_DIM], 0.0)
    row = lax.broadcasted_iota(jnp.int32, (tt, M_CONV_DIM), 0)
    row8 = lax.broadcasted_iota(jnp.int32, (SUBLANE, M_CONV_DIM), 0)
    acc = cb_ref[...] + cw_ref[M_CONV - 1:M_CONV, :] * xbc
    for s in range(1, M_CONV):
        head = jnp.where(row8 < s, pltpu.roll(prev, s, 0), 0.0)
        head = jnp.concatenate([head, jnp.zeros((tt - SUBLANE, M_CONV_DIM), F32)], axis=0)
        shifted = jnp.where(row < s, head, pltpu.roll(xbc, s, 0))
        acc = acc + cw_ref[M_CONV - 1 - s:M_CONV - s, :] * shifted
    xc = _silu(acc)
    x = xc[:, 0:M_DIN]
    bm = xc[:, M_DIN:M_DIN + M_GROUPS * M_N]
    cm = xc[:, M_DIN + M_GROUPS * M_N:]

    e4 = e4_ref[...]
    dt_c, a_c = _ssm_params(misc_ref[...], dtb_l_ref[...], nega_l_ref[...])
    cs_c = _cumsum(a_c, 0)
    dt_r, a_r = _ssm_params(dtt_ref[...], dtb_c_ref[...], nega_c_ref[...])
    cs_r = _cumsum(a_r, 1)
    xdt = x * _dot_sel(dt_c, e4)
    ecs = _dot_sel(jnp.exp(cs_c), e4)

    g = [_dot_nt(cm[:, gi * M_N:(gi + 1) * M_N], bm[:, gi * M_N:(gi + 1) * M_N]) for gi in range(M_GROUPS)]
    causal = lax.broadcasted_iota(jnp.int32, (tt, tt), 0) >= lax.broadcasted_iota(jnp.int32, (tt, tt), 1)
    lane_head = lax.broadcasted_iota(jnp.int32, (tt, M_DIN), 1) // M_HD
    rep = M_HEADS // M_GROUPS
    y = jnp.zeros((tt, M_DIN), F32)
    for h in range(M_HEADS):
        decay = jnp.exp(jnp.where(causal, cs_c[:, h:h + 1] - cs_r[h:h + 1, :], NEG_BIG))
        yh = _dot(g[h // rep] * decay, xdt)
        y = jnp.where(lane_head == h, yh, y)

    hs = hs_ref[...]
    yi = [_dot_nt(cm[:, gi * M_N:(gi + 1) * M_N], hs) for gi in range(M_GROUPS)]
    y = y + jnp.where(lane_head < rep, yi[0], yi[1]) * ecs

    cs_last = cs_c[tt - 1:tt, :]
    xw = xdt * _dot_sel(jnp.exp(cs_last - cs_c), e4)
    zz = _dot_tn(xw, bm)
    half = M_DIN // M_GROUPS
    upd = jnp.concatenate([zz[:half, :M_N], zz[half:, M_N:]], axis=0)
    dec = _dot_sel_nt(e4t_ref[...], jnp.broadcast_to(jnp.exp(cs_last), (M_N, LANE)))
    hs_ref[...] = hs * dec + upd
    hs_out_ref[...] = hs_ref[...]

    y = (y + md_ref[...] * x) * _silu(z)
    y = jnp.concatenate([_rms_rows(y[:, gi * half:(gi + 1) * half]) for gi in range(M_GROUPS)], axis=1)
    o_ref[...] = (y * gn_ref[...]).astype(BF16)


def _ssd_prompt(m, misc, dtt, cw, cb, dtb_l, nega_l, dtb_c, nega_c, e4, e4t, md, gn, *, bsz, seq, tt):
    nt = seq // tt
    per8 = tt // SUBLANE
    const = lambda shape: pl.BlockSpec(shape, lambda b, i: (0,) * len(shape))
    return pl.pallas_call(
        _ssd_prompt_kernel,
        grid=(bsz, nt),
        in_specs=[pl.BlockSpec((tt, 1024), lambda b, i: (b * nt + i, 0)),
                  pl.BlockSpec((SUBLANE, 1024), lambda b, i: (jnp.maximum((b * nt + i) * per8 - 1, 0), 0)),
                  pl.BlockSpec((tt, LANE), lambda b, i: (b * nt + i, 0)),
                  pl.BlockSpec((None, SUBLANE, tt), lambda b, i: (b, 0, i)),
                  const((M_CONV, M_CONV_DIM)), const((1, M_CONV_DIM)), const((1, LANE)), const((1, LANE)),
                  const((SUBLANE, 1)), const((SUBLANE, 1)), const((LANE, M_DIN)), const((M_DIN, LANE)),
                  const((1, M_DIN)), const((1, M_DIN))],
        out_specs=[pl.BlockSpec((tt, M_DIN), lambda b, i: (b * nt + i, 0)),
                   pl.BlockSpec((None, M_DIN, M_N), lambda b, i: (b, 0, 0))],
        out_shape=[jax.ShapeDtypeStruct((bsz * seq, M_DIN), BF16),
                   jax.ShapeDtypeStruct((bsz, M_DIN, M_N), F32)],
        scratch_shapes=[pltpu.VMEM((M_DIN, M_N), F32)],
        compiler_params=_cp("arbitrary", "arbitrary"),
        name="ssd_prompt",
    )(m, m, misc, dtt, cw, cb, dtb_l, nega_l, dtb_c, nega_c, e4, e4t, md, gn)


def _step_kernel(gla_ref, la_ref, v4_ref, gg4_ref, s0_ref, m_ref, buf_ref, misc_ref, h0_ref,
                 cw_ref, cb_ref, dtb_ref, nega_ref, e4_ref, md_ref, gng_ref, gnm_ref,
                 g_ref, s_ref, y_ref, h_ref):
    q_col = _row_to_col(gla_ref[:, 0:GLA_DQK] * (GLA_DK ** -0.5))
    k_col = _row_to_col(gla_ref[:, GLA_DQK:2 * GLA_DQK])
    dec_col = _row_to_col(jnp.exp(la_ref[...]))
    v4 = v4_ref[...]
    v_exp = jnp.concatenate([jnp.broadcast_to(v4[h:h + 1, :], (GLA_DK, GLA_DV)) for h in range(GLA_HEADS)],
                            axis=0)
    s = s0_ref[...] * dec_col + k_col * v_exp
    s_ref[...] = s
    o4 = jnp.sum((q_col * s).reshape(GLA_HEADS, GLA_DK, GLA_DV), axis=1)
    g_ref[...] = _rms_rows(o4) * gng_ref[...] * _silu(gg4_ref[...])

    z = m_ref[:, 0:M_DIN]
    acc = cb_ref[...] + cw_ref[M_CONV - 1:M_CONV, :] * m_ref[:, M_DIN:M_DIN + M_CONV_DIM]
    for s_ in range(M_CONV - 1):
        acc = acc + cw_ref[s_:s_ + 1, :] * buf_ref[s_:s_ + 1, :]
    xc = _silu(acc)
    x = xc[:, 0:M_DIN]
    bm = xc[:, M_DIN:M_DIN + M_GROUPS * M_N]
    cm = xc[:, M_DIN + M_GROUPS * M_N:]
    dt, a = _ssm_params(misc_ref[...], dtb_ref[...], nega_ref[...])
    e4 = e4_ref[...]
    xdt_col = _row_to_col(x * _dot_sel(dt, e4))
    deca_col = _row_to_col(_dot_sel(jnp.exp(a), e4))
    half = M_DIN // M_GROUPS
    spread = lambda t: jnp.concatenate(
        [jnp.broadcast_to(t[:, gi * M_N:(gi + 1) * M_N], (half, M_N)) for gi in range(M_GROUPS)], axis=0)
    hs = h0_ref[...] * deca_col + xdt_col * spread(bm)
    h_ref[...] = hs
    y = _col_to_row(jnp.sum(hs * spread(cm), axis=1, keepdims=True))
    y = (y + md_ref[...] * x) * _silu(z)
    y = jnp.concatenate([_rms_rows(y[:, gi * half:(gi + 1) * half]) for gi in range(M_GROUPS)], axis=1)
    y_ref[...] = y * gnm_ref[...]


def _step_mixers(gla, la, v4, gg4, s0, m, buf, misc, h0, cw, cb, dtb, nega, e4, md, gng, gnm):
    nb = gla.shape[0]
    per = lambda *shape: pl.BlockSpec((None,) + shape, lambda b: (b,) + (0,) * len(shape))
    const = lambda shape: pl.BlockSpec(shape, lambda b: (0,) * len(shape))
    return pl.pallas_call(
        _step_kernel,
        grid=(nb,),
        in_specs=[per(1, 768), per(1, LANE), per(GLA_HEADS, GLA_DV), per(GLA_HEADS, GLA_DV),
                  per(GLA_DQK, GLA_DV), per(1, 1024), per(M_CONV - 1, M_CONV_DIM), per(1, LANE),
                  per(M_DIN, M_N),
                  const((M_CONV, M_CONV_DIM)), const((1, M_CONV_DIM)), const((1, LANE)), const((1, LANE)),
                  const((LANE, M_DIN)), const((1, M_DIN)), const((1, GLA_DV)), const((1, M_DIN))],
        out_specs=[per(GLA_HEADS, GLA_DV), per(GLA_DQK, GLA_DV), per(1, M_DIN), per(M_DIN, M_N)],
        out_shape=[jax.ShapeDtypeStruct((nb, GLA_HEADS, GLA_DV), F32),
                   jax.ShapeDtypeStruct((nb, GLA_DQK, GLA_DV), F32),
                   jax.ShapeDtypeStruct((nb, 1, M_DIN), F32),
                   jax.ShapeDtypeStruct((nb, M_DIN, M_N), F32)],
        compiler_params=_cp("arbitrary"),
        name="step_mixers",
    )(gla.reshape(nb, 1, 768), la.reshape(nb, 1, LANE), v4, gg4, s0, m.reshape(nb, 1, 1024), buf,
      misc.reshape(nb, 1, LANE), h0, cw, cb, dtb, nega, e4, md, gng, gnm)


SB_PP = 8


def _sb_decode_kernel(pt_ref, q_ref, *refs):
    ks = refs[0:SB_PP]
    vs = refs[SB_PP:2 * SB_PP]
    bias_ref, gn_ref, uinc_ref, o_ref, acc_ref, car_ref, qb_ref = refs[2 * SB_PP:]
    j = pl.program_id(1)
    page = ks[0].shape[2]
    uinc = uinc_ref[...]

    @pl.when(j == 0)
    def _():
        acc_ref[...] = jnp.zeros_like(acc_ref)
        car_ref[...] = jnp.zeros_like(car_ref)
        qb_ref[...] = jnp.broadcast_to(q_ref[...], qb_ref.shape)

    bias = bias_ref[...]
    for r in range(SB_PP - 1, -1, -1):
        prod = (ks[r][...] * qb_ref[...]).reshape(SB_HEADS, SB_HD // SUBLANE, SUBLANE, page)
        z = jnp.sum(jnp.sum(prod, axis=1), axis=1) + bias
        sp = jnp.maximum(z, 0.0) + jnp.log(1.0 + jnp.exp2(_neg_abs(z))) * LOG2E
        ls = z - sp
        hi = sp.astype(BF16)
        lo = (sp - hi.astype(F32)).astype(BF16)
        incl = (jnp.dot(hi, uinc, preferred_element_type=F32) + jnp.dot(lo, uinc, preferred_element_type=F32))
        car = car_ref[...]
        w = jnp.exp2(ls - (incl - sp + car))
        car_ref[...] = car + incl[:, 0:1]
        for h in range(SB_HEADS):
            acc_ref[h] += vs[r][h] * w[h:h + 1, :]

    @pl.when(j == pl.num_programs(1) - 1)
    def _():
        o_ref[...] = _rms_rows(jnp.sum(acc_ref[...], axis=-1)) * gn_ref[...]


def _sb_decode(page_table, q, cache_kt, cache_vt, layer, bias_col, gn, uinc):
    nb, n_pages = page_table.shape
    page = cache_kt.shape[4]
    ng = n_pages // SB_PP
    pt = page_table.reshape(-1)

    def kv_spec(r):
        return pl.BlockSpec((None, None, SB_HEADS, SB_HD, page),
                            lambda b, j, pt_ref: (layer, pt_ref[b * n_pages + (ng - 1 - j) * SB_PP + r], 0, 0, 0))

    const = lambda shape: pl.BlockSpec(shape, lambda b, j, pt_ref: (0,) * len(shape))
    grid_spec = pltpu.PrefetchScalarGridSpec(
        num_scalar_prefetch=1,
        grid=(nb, ng),
        in_specs=[pl.BlockSpec((None, SB_HEADS, SB_HD, 1), lambda b, j, pt_ref: (b, 0, 0, 0))]
        + [kv_spec(r) for r in range(SB_PP)] + [kv_spec(r) for r in range(SB_PP)]
        + [const((SB_HEADS, 1)), const((1, SB_HD)), const((page, page))],
        out_specs=pl.BlockSpec((None, SB_HEADS, SB_HD), lambda b, j, pt_ref: (b, 0, 0)),
        scratch_shapes=[pltpu.VMEM((SB_HEADS, SB_HD, page), F32), pltpu.VMEM((SB_HEADS, 1), F32),
                        pltpu.VMEM((SB_HEADS, SB_HD, page), F32)],
    )
    out = pl.pallas_call(
        _sb_decode_kernel,
        grid_spec=grid_spec,
        out_shape=jax.ShapeDtypeStruct((nb, SB_HEADS, SB_HD), F32),
        compiler_params=_cp("arbitrary", "arbitrary"),
        name="sb_decode",
    )(pt, q, *([cache_kt] * SB_PP), *([cache_vt] * SB_PP), bias_col, gn, uinc)
    return out.reshape(nb, SB_W)


def _outproj_kernel(x_ref, gate_ref, g_ref, s_ref, y_ref, w_ref, o_ref):
    mix = (_dot(g_ref[...], w_ref[0:GLA_DVW, :]) + _dot(s_ref[...], w_ref[GLA_DVW:GLA_DVW + SB_W, :])
           + _dot(y_ref[...], w_ref[GLA_DVW + SB_W:, :]))
    o_ref[...] = x_ref[...] + gate_ref[...] * mix


def _out_proj(x, gate, g, s, y, w, *, tm, rows_per_mod):
    r, d = x.shape
    mrows = gate.shape[1]
    row = lambda n: pl.BlockSpec((tm, n), lambda i: (i, 0))
    return pl.pallas_call(
        _outproj_kernel,
        grid=(r // tm,),
        in_specs=[row(d), pl.BlockSpec((None, mrows, d), lambda i: (i // rows_per_mod, 0, 0)),
                  row(GLA_DVW), row(SB_W), row(M_DIN), pl.BlockSpec(w.shape, lambda i: (0, 0))],
        out_specs=row(d),
        out_shape=jax.ShapeDtypeStruct((r, d), F32),
        compiler_params=_cp("arbitrary"),
        name="out_proj",
    )(x, gate, g, s, y, w)


def _ffn_up_kernel(x_ref, sc_ref, sh_ref, g_ref, w_ref, u_ref):
    h = _norm_mod(x_ref[...], g_ref[...], sc_ref[...], sh_ref[...])
    u_ref[...] = jnp.dot(h.astype(BF16), w_ref[...], preferred_element_type=F32)


def _ffn_up(x, sc, sh, g, w, *, tm, rows_per_mod):
    r, d = x.shape
    n = w.shape[1]
    mrows = sc.shape[1]
    mod_spec = pl.BlockSpec((None, mrows, d), lambda i: (i // rows_per_mod, 0, 0))
    return pl.pallas_call(
        _ffn_up_kernel,
        grid=(r // tm,),
        in_specs=[pl.BlockSpec((tm, d), lambda i: (i, 0)), mod_spec, mod_spec,
                  pl.BlockSpec((1, d), lambda i: (0, 0)), pl.BlockSpec((d, n), lambda i: (0, 0))],
        out_specs=pl.BlockSpec((tm, n), lambda i: (i, 0)),
        out_shape=jax.ShapeDtypeStruct((r, n), F32),
        compiler_params=_cp("arbitrary"),
        name="ffn_up",
    )(x, sc, sh, g, w)


FFN_CK = 256


def _ffn_act_down(taps, cw_ref, cb_ref, w_ref):
    dff = w_ref.shape[0]

    def conv(c0):
        u, u1, u2 = taps(c0)
        cs = slice(c0, c0 + FFN_CK)
        return cb_ref[:, cs] + cw_ref[0:1, cs] * u2 + cw_ref[1:2, cs] * u1 + cw_ref[2:3, cs] * u

    out = None
    for c0 in range(0, dff, FFN_CK):
        part = _dot(_silu(conv(c0)) * conv(dff + c0), w_ref[c0:c0 + FFN_CK, :])
        out = part if out is None else out + part
    return out


def _ffn_up_act_kernel(x_ref, sc_ref, sh_ref, g_ref, w_ref, cw_ref, cb_ref, a_ref, tail_ref, carry_ref,
                       *, tiles_per_seq):
    i = pl.program_id(0)
    tm = x_ref.shape[0]
    dff = a_ref.shape[1]
    first = i % tiles_per_seq == 0
    h = _norm_mod(x_ref[...], g_ref[...], sc_ref[...], sh_ref[...]).astype(BF16)
    row8 = lax.broadcasted_iota(jnp.int32, (SUBLANE, FFN_CK), 0)

    def conv(base):
        cs = slice(base, base + FFN_CK)
        u = jnp.dot(h, w_ref[:, cs], preferred_element_type=F32)
        prev = jnp.where(first, 0.0, carry_ref[:, cs])
        last = u[tm - SUBLANE:, :]
        carry_ref[:, cs] = last
        tail_ref[:, cs] = last
        acc = cb_ref[:, cs] + cw_ref[FFN_CONV - 1:FFN_CONV, cs] * u
        for s in range(1, FFN_CONV):
            rolled = pltpu.roll(u, s, 0)
            head = jnp.where(row8 < s, pltpu.roll(prev, s, 0), rolled[:SUBLANE, :])
            shifted = jnp.concatenate([head, rolled[SUBLANE:, :]], axis=0)
            acc = acc + cw_ref[FFN_CONV - 1 - s:FFN_CONV - s, cs] * shifted
        return acc

    for c0 in range(0, dff, FFN_CK):
        a_ref[:, c0:c0 + FFN_CK] = (_silu(conv(c0)) * conv(dff + c0)).astype(BF16)


def _ffn_up_act(x, sc, sh, g, w, cw, cb, *, tm, seq):
    r, d = x.shape
    n = w.shape[1]
    dff = n // 2
    tiles = seq // tm
    mod_spec = pl.BlockSpec((None, 1, d), lambda i: (i // tiles, 0, 0))
    const = lambda shape: pl.BlockSpec(shape, lambda i: (0,) * len(shape))
    return pl.pallas_call(
        functools.partial(_ffn_up_act_kernel, tiles_per_seq=tiles),
        grid=(r // tm,),
        in_specs=[pl.BlockSpec((tm, d), lambda i: (i, 0)), mod_spec, mod_spec, const((1, d)), const((d, n)),
                  const((FFN_CONV, n)), const((1, n))],
        out_specs=[pl.BlockSpec((tm, dff), lambda i: (i, 0)),
                   pl.BlockSpec((None, SUBLANE, n), lambda i: (i // tiles, 0, 0))],
        out_shape=[jax.ShapeDtypeStruct((r, dff), BF16), jax.ShapeDtypeStruct((r // seq, SUBLANE, n), F32)],
        scratch_shapes=[pltpu.VMEM((SUBLANE, n), F32)],
        compiler_params=_cp("arbitrary"),
        name="ffn_up_act",
    )(x, sc, sh, g, w, cw, cb)


def _ffn_down_kernel(x_ref, gate_ref, a_ref, w_ref, o_ref):
    o_ref[...] = x_ref[...] + gate_ref[...] * jnp.dot(a_ref[...], w_ref[...], preferred_element_type=F32)


def _ffn_down(x, gate, a, w, *, tm, seq):
    r, d = x.shape
    tiles = seq // tm
    return pl.pallas_call(
        _ffn_down_kernel,
        grid=(r // tm,),
        in_specs=[pl.BlockSpec((tm, d), lambda i: (i, 0)),
                  pl.BlockSpec((None, 1, d), lambda i: (i // tiles, 0, 0)),
                  pl.BlockSpec((tm, a.shape[1]), lambda i: (i, 0)),
                  pl.BlockSpec(w.shape, lambda i: (0, 0))],
        out_specs=pl.BlockSpec((tm, d), lambda i: (i, 0)),
        out_shape=jax.ShapeDtypeStruct((r, d), F32),
        compiler_params=_cp("arbitrary"),
        name="ffn_down",
    )(x, gate, a, w)


def _ffn_down_step_kernel(x_ref, gate_ref, u_ref, buf_ref, cw_ref, cb_ref, w_ref, o_ref):
    taps = lambda c0: (u_ref[:, c0:c0 + FFN_CK], buf_ref[1, :, c0:c0 + FFN_CK], buf_ref[0, :, c0:c0 + FFN_CK])
    o_ref[...] = x_ref[...] + gate_ref[...] * _ffn_act_down(taps, cw_ref, cb_ref, w_ref)


def _ffn_down_step(x, gate, u, buf_t, cw, cb, w):
    full = lambda a: pl.BlockSpec(a.shape, lambda i: (0,) * a.ndim)
    args = (x, gate, u, buf_t, cw, cb, w)
    return pl.pallas_call(
        _ffn_down_step_kernel,
        grid=(1,),
        in_specs=[full(a) for a in args],
        out_specs=full(x),
        out_shape=jax.ShapeDtypeStruct(x.shape, F32),
        compiler_params=_cp("arbitrary"),
        name="ffn_down_step",
    )(*args)


def _same_segment(n, width):
    i = np.arange(n)
    return jnp.asarray((i[:, None] // width) == (i[None, :] // width), BF16)


def _constants():
    k = np.arange(GLA_DQK)
    v = np.arange(GLA_DVW)
    eexp = (k[:, None] // GLA_DK) == (v[None, :] // GLA_DV)
    j = np.arange(SB_TK)
    later = j[:, None] > j[None, :]
    uo = np.block([[later, np.zeros_like(later)], [np.zeros_like(later), later]])
    lane = np.arange(LANE)
    hm_pair = np.zeros((SUBLANE, LANE), np.float32)
    hm_pair[0] = lane < SB_HD
    hm_pair[1] = lane >= SB_HD
    p = np.arange(M_DIN)
    e4 = lane[:, None] == (p[None, :] // M_HD)
    return dict(
        eexp=jnp.asarray(eexp, BF16), bd=jnp.asarray(eexp.T, F32),
        uo=jnp.asarray(uo, BF16), hm_pair=jnp.asarray(hm_pair, BF16), eseg64=_same_segment(LANE, 64),
        e4=jnp.asarray(e4, BF16), e4t=jnp.asarray(e4.T, BF16))


def _pad_lanes(v, n=LANE):
    return jnp.zeros((1, n), F32).at[0, :v.shape[0]].set(v)


def _pad_col(v, n=SUBLANE):
    return jnp.zeros((n, 1), F32).at[:v.shape[0], 0].set(v)


def kernel(x_prompt, x_sample, cache_sb_k, cache_sb_v, state_gla, state_mamba_conv, state_mamba_ssm, state_ffn_conv, page_table, c_prompt, c_sample, norm1_g, w_ada, b_ada, w_in, gla_w_gate2, gla_b_gate, gla_norm_g, sb_q_norm_g, sb_k_norm_g, sb_o_norm_g, sb_bias, m_conv_w, m_conv_b, m_dt_bias, m_a_log, m_d, m_norm_g, w_out, norm2_g, ffn_w_up, ffn_conv_w, ffn_conv_b, ffn_w_down):
    bsz, seq, d = x_prompt.shape
    nb = x_sample.shape[0]
    depth = w_in.shape[0]
    dff = ffn_w_down.shape[1]
    n_pool, page = cache_sb_k.shape[1], cache_sb_k.shape[2]
    assert x_sample.shape[1] == 1 and d % LANE == 0 and seq % 256 == 0
    cst = _constants()

    tm = 256
    tq = 512
    tt_gla = 256
    tt_ssd = 128
    assert SB_UNROLL == 2 * (tq // SB_TK) or (tq // SB_TK) % SB_UNROLL == 0
    assert page_table.shape[1] % SB_PP == 0
    assert seq % tq == 0 and seq % tm == 0 and seq % tt_gla == 0 and seq % tt_ssd == 0

    mod = _modulation(jnp.concatenate([c_prompt, c_sample], axis=0), w_ada, b_ada)
    mod = mod.reshape(depth, bsz + nb, N_MOD, d)
    mod_p = mod[:, :bsz].transpose(0, 2, 1, 3).reshape(depth, N_MOD, bsz, 1, d)
    mod_s = mod[:, bsz:].transpose(0, 2, 1, 3).reshape(depth, N_MOD, 1, nb, d)

    cache_kt = cache_sb_k.transpose(0, 1, 3, 4, 2)
    cache_vt = cache_sb_v.transpose(0, 1, 3, 4, 2)

    xp = x_prompt.reshape(bsz * seq, d)
    xs = x_sample.reshape(nb, d)
    w_in_t = w_in.transpose(2, 0, 1)
    outs = {name: [] for name in ("pk", "pv", "pg", "pc", "ph", "pf", "sk", "sv", "sg", "sc", "sh", "sf")}
    for l in range(depth):
        wi = w_in_t[:, l, :]
        o = np.cumsum([0, GLA_DQK, GLA_DQK, GLA_DVW, GLA_DVW, GLA_LR, SB_W, SB_W, SB_W, M_DIN, M_CONV_DIM, M_HEADS])
        sl = lambda a, b: wi[o[a]:o[b]]
        w_re = jnp.concatenate(
            [sl(0, 4), sl(5, 8), sl(8, 10), sl(10, 11), sl(4, 5),
             jnp.zeros((LANE - M_HEADS - GLA_LR, d), F32)], axis=0).astype(BF16)
        wg = jnp.zeros((LANE, LANE), F32).at[M_HEADS:M_HEADS + GLA_LR].set(gla_w_gate2[l]).astype(BF16)
        bg = gla_b_gate[l].reshape(1, GLA_DQK)
        gq = jnp.tile(sb_q_norm_g[l], SB_HEADS).reshape(1, SB_W)
        gk = jnp.tile(sb_k_norm_g[l], SB_HEADS).reshape(1, SB_W)
        go = jnp.tile(sb_o_norm_g[l], SB_HEADS).reshape(1, SB_W)
        gng = jnp.tile(gla_norm_g[l], GLA_HEADS).reshape(1, GLA_DVW)
        g1 = norm1_g[l].reshape(1, d)
        g2 = norm2_g[l].reshape(1, d)
        bias_pair = jnp.repeat(sb_bias[l] * LOG2E, SB_TK).reshape(SB_HEADS // 2, 1, 2 * SB_TK)
        bias_col = (sb_bias[l] * LOG2E).reshape(SB_HEADS, 1)
        cw = m_conv_w[l]
        cb = m_conv_b[l].reshape(1, M_CONV_DIM)
        nega = -jnp.exp(m_a_log[l])
        dtb_l, nega_l = _pad_lanes(m_dt_bias[l]), _pad_lanes(nega)
        dtb_c, nega_c = _pad_col(m_dt_bias[l]), _pad_col(nega)
        md = jnp.repeat(m_d[l], M_HD).reshape(1, M_DIN)
        gnm = m_norm_g[l].reshape(1, M_DIN)
        wo = w_out[l].astype(BF16)
        wu = ffn_w_up[l].astype(BF16)
        wd = ffn_w_down[l].astype(BF16)
        fcw = ffn_conv_w[l]
        fcb = ffn_conv_b[l].reshape(1, 2 * dff)
        sh1, sc1, gt1, sh2, sc2, gt2 = range(N_MOD)

        mp = mod_p[l]
        gla, la, q, k, kb, v, vb, m, misc = _in_proj(
            xp, mp[sc1], mp[sh1], g1, w_re, wg, bg, gq, gk, cst["eseg64"], tm=tm, rows_per_mod=seq // tm,
            kv_transposed=True)
        g_mix, g_state = _gla_prompt(gla, la, cst["eexp"], cst["bd"], cst["eseg64"], gng,
                                     bsz=bsz, seq=seq, tt=tt_gla)
        s_mix = _sb_prompt(q, kb, vb, bias_pair, cst["uo"], cst["hm_pair"], cst["eseg64"], go[:, :LANE],
                           bsz=bsz, seq=seq, tq=tq)
        dtt = jnp.pad(misc[:, :M_HEADS].reshape(bsz, seq, M_HEADS).transpose(0, 2, 1),
                      ((0, 0), (0, SUBLANE - M_HEADS), (0, 0)))
        y_mix, h_state = _ssd_prompt(m, misc, dtt, cw, cb, dtb_l, nega_l, dtb_c, nega_c, cst["e4"], cst["e4t"],
                                     md, gnm, bsz=bsz, seq=seq, tt=tt_ssd)
        xp = _out_proj(xp, mp[gt1], g_mix, s_mix, y_mix, wo, tm=tm, rows_per_mod=seq // tm)
        act, u_tail = _ffn_up_act(xp, mp[sc2], mp[sh2], g2, wu, fcw, fcb, tm=tm, seq=seq)
        xp = _ffn_down(xp, mp[gt2], act, wd, tm=tm, seq=seq)

        outs["pk"].append(k.reshape(bsz, SB_HEADS, SB_HD, seq).transpose(0, 3, 1, 2))
        outs["pv"].append(v.reshape(bsz, SB_HEADS, SB_HD, seq).transpose(0, 3, 1, 2))
        gs = g_state.reshape(bsz, GLA_HEADS, GLA_DV, GLA_HEADS, GLA_DK)
        outs["pg"].append(jnp.stack([gs[:, h, :, h, :] for h in range(GLA_HEADS)], axis=1).transpose(0, 1, 3, 2))
        outs["pc"].append(m.reshape(bsz, seq, 1024)[:, seq - (M_CONV - 1):, M_DIN:])
        outs["ph"].append(h_state.reshape(bsz, M_HEADS, M_HD, M_N))
        outs["pf"].append(u_tail[:, SUBLANE - (FFN_CONV - 1):])

        ms = mod_s[l]
        gla, la, q, k, kb, v, vb, m, misc = _in_proj(
            xs, ms[sc1], ms[sh1], g1, w_re, wg, bg, gq, gk, cst["eseg64"], tm=nb, rows_per_mod=1,
            kv_transposed=False)
        g4, g_state, y_mix, h_state = _step_mixers(
            gla, la, gla[:, 2 * GLA_DQK:2 * GLA_DQK + GLA_DVW].reshape(nb, GLA_HEADS, GLA_DV),
            gla[:, 2 * GLA_DQK + GLA_DVW:].reshape(nb, GLA_HEADS, GLA_DV),
            state_gla[l].reshape(nb, GLA_DQK, GLA_DV), m, state_mamba_conv[l], misc,
            state_mamba_ssm[l].reshape(nb, M_DIN, M_N), cw, cb, dtb_l, nega_l, cst["e4"], md,
            gla_norm_g[l].reshape(1, GLA_DV), gnm)
        s_mix = _sb_decode(page_table, q.astype(F32).reshape(nb, SB_HEADS, SB_HD, 1), cache_kt, cache_vt, l,
                           bias_col, sb_o_norm_g[l].reshape(1, SB_HD),
                           jnp.asarray(np.tril(np.ones((page, page), np.float32)), BF16))
        xs = _out_proj(xs, ms[gt1], g4.reshape(nb, GLA_DVW), s_mix, y_mix.reshape(nb, M_DIN), wo,
                       tm=nb, rows_per_mod=1)
        u = _ffn_up(xs, ms[sc2], ms[sh2], g2, wu, tm=nb, rows_per_mod=1)
        xs = _ffn_down_step(xs, ms[gt2][0], u, state_ffn_conv[l].transpose(1, 0, 2), fcw, fcb, wd)

        outs["sk"].append(k.reshape(nb, 1, SB_HEADS, SB_HD))
        outs["sv"].append(v.reshape(nb, 1, SB_HEADS, SB_HD))
        outs["sg"].append(g_state.reshape(nb, GLA_HEADS, GLA_DK, GLA_DV))
        outs["sc"].append(jnp.concatenate([state_mamba_conv[l][:, 1:], m[:, None, M_DIN:]], axis=1))
        outs["sh"].append(h_state.reshape(nb, M_HEADS, M_HD, M_N))
        outs["sf"].append(jnp.concatenate([state_ffn_conv[l][:, 1:], u[:, None, :]], axis=1))

    st = {name: jnp.stack(v) for name, v in outs.items()}
    return (xp.reshape(bsz, seq, d), xs.reshape(nb, 1, d), st["pk"], st["pv"], st["pg"], st["pc"], st["ph"],
            st["pf"], st["sk"], st["sv"], st["sg"], st["sc"], st["sh"], st["sf"])
```

```python
import functools

import jax
import jax.numpy as jnp
import numpy as np
from jax import lax
from jax.experimental import pallas as pl
from jax.experimental.pallas import tpu as pltpu

F32 = jnp.float32
BF16 = jnp.bfloat16

GLA_HEADS, GLA_DK, GLA_DV, GLA_LR, GLA_TAU = 4, 32, 64, 16, 16.0
GLA_DQK = GLA_HEADS * GLA_DK
GLA_DVW = GLA_HEADS * GLA_DV
SB_HEADS, SB_HD = 8, 64
SB_W = SB_HEADS * SB_HD
M_HEADS, M_HD, M_GROUPS, M_N, M_CONV = 4, 64, 2, 128, 4
M_DIN = M_HEADS * M_HD
M_CONV_DIM = M_DIN + 2 * M_GROUPS * M_N
FFN_CONV = 3
N_MOD = 6
EPS = 1e-6
NEG_BIG = -1e30
LOG2E = 1.4426950408889634

C_GLA = 0
C_SBQ = 768
C_SBK = 1280
C_SBV = 1792
C_MZ = 2304
C_MISC = 3328
D_INP = 3456
LANE = 128
SUBLANE = 8

VMEM_LIMIT = 56 * 1024 * 1024


def _cp(*sem, flags=None):
    return pltpu.CompilerParams(dimension_semantics=sem, vmem_limit_bytes=VMEM_LIMIT, flags=flags)


def _dot(a, b):
    return jnp.dot(a.astype(BF16), b.astype(BF16), preferred_element_type=F32)


def _dot_nt(a, b):
    return lax.dot_general(a.astype(BF16), b.astype(BF16), (((1,), (1,)), ((), ())),
                           preferred_element_type=F32)


def _dot_tn(a, b):
    return lax.dot_general(a.astype(BF16), b.astype(BF16), (((0,), (0,)), ((), ())),
                           preferred_element_type=F32)


def _split3(x):
    hi = x.astype(BF16)
    r = x - hi.astype(F32)
    mid = r.astype(BF16)
    lo = (r - mid.astype(F32)).astype(BF16)
    return hi, mid, lo


def _dot_sel(x, e):
    hi, mid, lo = _split3(x)
    d = lambda p: jnp.dot(p, e, preferred_element_type=F32)
    return d(hi) + d(mid) + d(lo)


def _dot_sel_nt(e, x):
    hi, mid, lo = _split3(x)
    d = lambda p: lax.dot_general(e, p, (((1,), (1,)), ((), ())), preferred_element_type=F32)
    return d(hi) + d(mid) + d(lo)


def _sigmoid(x):
    return 1.0 / (1.0 + jnp.exp(-x))


def _silu(x):
    return x * _sigmoid(x)


def _softplus(x):
    return jnp.maximum(x, 0.0) + jnp.log1p(jnp.exp(-jnp.abs(x)))


def _log_sigmoid(x):
    return jnp.minimum(x, 0.0) - jnp.log1p(jnp.exp(-jnp.abs(x)))


def _rms_rows(x):
    return x * lax.rsqrt(jnp.mean(x * x, axis=-1, keepdims=True) + EPS)


def _seg_rms(x, eseg, width):
    xx = x * x
    hi = xx.astype(BF16)
    lo = (xx - hi.astype(F32)).astype(BF16)
    blocks = []
    for c in range(0, x.shape[1], LANE):
        d = lambda p: jnp.dot(p[:, c:c + LANE], eseg, preferred_element_type=F32)
        blocks.append(d(hi) + d(lo))
    ms = jnp.concatenate(blocks, axis=1) * (1.0 / width)
    return x * lax.rsqrt(ms + EPS)


def _row_to_col(row):
    n = row.shape[1]
    eye = lax.broadcasted_iota(jnp.int32, (n, n), 0) == lax.broadcasted_iota(jnp.int32, (n, n), 1)
    return jnp.sum(jnp.where(eye, row, 0.0), axis=1, keepdims=True)


def _col_to_row(col):
    n = col.shape[0]
    eye = lax.broadcasted_iota(jnp.int32, (n, n), 0) == lax.broadcasted_iota(jnp.int32, (n, n), 1)
    return jnp.sum(jnp.where(eye, col, 0.0), axis=0, keepdims=True)


def _cumsum(x, axis, seg=None):
    n = x.shape[axis] if seg is None else seg
    idx = lax.broadcasted_iota(jnp.int32, x.shape, axis)
    if seg is not None:
        idx = idx % seg
    s = 1
    while s < n:
        x = x + jnp.where(idx >= s, pltpu.roll(x, s, axis), 0.0)
        s *= 2
    return x


def _mod_kernel(c_ref, w_ref, b_ref, o_ref):
    o_ref[...] = _dot(_silu(c_ref[...]), w_ref[...]) + b_ref[...]


def _modulation(c_all, w_ada, b_ada):
    depth, d, nd = w_ada.shape
    n = c_all.shape[0]
    tn = 1024
    return pl.pallas_call(
        _mod_kernel,
        grid=(depth, nd // tn),
        in_specs=[pl.BlockSpec((n, d), lambda l, j: (0, 0)),
                  pl.BlockSpec((None, d, tn), lambda l, j: (l, 0, j)),
                  pl.BlockSpec((None, 1, tn), lambda l, j: (l, 0, j))],
        out_specs=pl.BlockSpec((None, n, tn), lambda l, j: (l, 0, j)),
        out_shape=jax.ShapeDtypeStruct((depth, n, nd), F32),
        compiler_params=_cp("arbitrary", "arbitrary"),
        name="adaln_mod",
    )(c_all, w_ada, b_ada.reshape(depth, 1, nd))


def _norm_mod(x, g, sc, sh):
    return _rms_rows(x) * g * (1.0 + sc) + sh


def _inproj_kernel(x_ref, sc_ref, sh_ref, g_ref, w_ref, wg_ref, bg_ref, gq_ref, gk_ref, eseg_ref,
                   gla_ref, la_ref, q_ref, k_ref, kb_ref, v_ref, vb_ref, m_ref, misc_ref, *, kv_transposed):
    h = _norm_mod(x_ref[...], g_ref[...], sc_ref[...], sh_ref[...])
    p = lax.dot_general(h.astype(BF16), w_ref[...], (((1,), (1,)), ((), ())), preferred_element_type=F32)
    gla_ref[...] = p[:, C_GLA:C_SBQ]
    misc = p[:, C_MISC:D_INP]
    misc_ref[...] = misc
    la_ref[...] = _log_sigmoid(_dot(misc, wg_ref[...]) + bg_ref[...]) * (1.0 / GLA_TAU)
    eseg = eseg_ref[...]
    q = _seg_rms(p[:, C_SBQ:C_SBK], eseg, SB_HD) * gq_ref[...]
    q_ref[...] = (q * (SB_HD ** -0.5 * LOG2E)).astype(BF16)
    k = _seg_rms(p[:, C_SBK:C_SBV], eseg, SB_HD) * gk_ref[...]
    k_ref[...] = k.T if kv_transposed else k
    kb_ref[...] = k.astype(BF16)
    v = p[:, C_SBV:C_MZ]
    v_ref[...] = v.T if kv_transposed else v
    vb_ref[...] = v.astype(BF16)
    m_ref[...] = p[:, C_MZ:C_MISC]


def _in_proj(x, sc, sh, g, w, wg, bg, gq, gk, eseg, *, tm, rows_per_mod, kv_transposed):
    r, d = x.shape
    mrows = sc.shape[1]
    mod_spec = pl.BlockSpec((None, mrows, d), lambda i: (i // rows_per_mod, 0, 0))
    const = lambda shape: pl.BlockSpec(shape, lambda i: (0,) * len(shape))
    row = lambda n: pl.BlockSpec((tm, n), lambda i: (i, 0))
    outs = [(768, F32), (LANE, F32), (SB_W, BF16), (SB_W, F32), (SB_W, BF16), (SB_W, F32), (SB_W, BF16),
            (1024, F32), (LANE, F32)]
    out_specs = [row(n) for n, _ in outs]
    out_shape = [jax.ShapeDtypeStruct((r, n), dt) for n, dt in outs]
    if kv_transposed:
        seq = rows_per_mod * tm
        for idx in (3, 5):
            out_specs[idx] = pl.BlockSpec((None, SB_W, tm), lambda i: (i // rows_per_mod, 0, i % rows_per_mod))
            out_shape[idx] = jax.ShapeDtypeStruct((r // seq, SB_W, seq), F32)
    return pl.pallas_call(
        functools.partial(_inproj_kernel, kv_transposed=kv_transposed),
        grid=(r // tm,),
        in_specs=[row(d), mod_spec, mod_spec, const((1, d)), const((D_INP, d)), const((LANE, LANE)),
                  const((1, LANE)), const((1, SB_W)), const((1, SB_W)), const((LANE, LANE))],
        out_specs=out_specs,
        out_shape=out_shape,
        compiler_params=_cp("arbitrary"),
        name="in_proj",
    )(x, sc, sh, g, w, wg, bg, gq, gk, eseg)


GLA_C = SUBLANE
GLA_UNROLL = 8


def _gla_prompt_kernel(gla_ref, la_ref, eexp_ref, bd_ref, eseg_ref, gn_ref, o_ref, st_out_ref,
                       st_ref, b_ref, acc_ref):
    i = pl.program_id(1)
    tt = la_ref.shape[0]

    @pl.when(i == 0)
    def _():
        st_ref[...] = jnp.zeros_like(st_ref)

    b_ref[...] = _cumsum(la_ref[...], 0, seg=GLA_C)
    eexp = eexp_ref[...]
    bd = bd_ref[...]
    rowid = lax.broadcasted_iota(jnp.int32, (GLA_C, GLA_DQK), 0)
    rowid_v = lax.broadcasted_iota(jnp.int32, (GLA_C, GLA_DVW), 0)

    def group(gi, carry):
        steps = []
        for u in range(GLA_UNROLL):
            r0 = pl.multiple_of((gi * GLA_UNROLL + u) * GLA_C, GLA_C)
            bc = b_ref[pl.ds(r0, GLA_C), :]
            qc = gla_ref[pl.ds(r0, GLA_C), 0:GLA_DQK] * (GLA_DK ** -0.5)
            kc = gla_ref[pl.ds(r0, GLA_C), GLA_DQK:2 * GLA_DQK]
            vc = gla_ref[pl.ds(r0, GLA_C), 2 * GLA_DQK:2 * GLA_DQK + GLA_DVW]
            steps.append((r0, bc, qc, kc, vc, bc[GLA_C - 1:GLA_C, :]))
        ps = []
        for r0, bc, qc, kc, vc, bl in steps:
            for t in range(GLA_C):
                d = jnp.where(rowid <= t, bc[t:t + 1, :] - bc, NEG_BIG)
                ps.append(jnp.exp(d) * (qc[t:t + 1, :] * kc))
        sc = _dot(jnp.concatenate(ps, axis=0), eexp)
        upds = [_dot_tn(vc, kc * jnp.exp(bl - bc)) * bd for r0, bc, qc, kc, vc, bl in steps]
        st = st_ref[...]
        for u, (r0, bc, qc, kc, vc, bl) in enumerate(steps):
            o = _dot_nt(qc * jnp.exp(bc), st)
            st = st * jnp.exp(bl) + upds[u]
            for t in range(GLA_C):
                row0 = (u * GLA_C + t) * GLA_C
                ot = jnp.sum(sc[row0:row0 + GLA_C, :] * vc, axis=0, keepdims=True)
                o = o + jnp.where(rowid_v == t, ot, 0.0)
            acc_ref[pl.ds(r0, GLA_C), :] = o
        st_ref[...] = st
        return carry

    lax.fori_loop(0, tt // (GLA_C * GLA_UNROLL), group, 0)
    gg = gla_ref[:, 2 * GLA_DQK + GLA_DVW:2 * GLA_DQK + 2 * GLA_DVW]
    o = _seg_rms(acc_ref[...], eseg_ref[...], GLA_DV) * gn_ref[...] * _silu(gg)
    o_ref[...] = o.astype(BF16)
    st_out_ref[...] = st_ref[...]


def _gla_prompt(gla, la, eexp, bd, eseg, gn, *, bsz, seq, tt):
    nt = seq // tt
    const = lambda shape: pl.BlockSpec(shape, lambda b, i: (0,) * len(shape))
    return pl.pallas_call(
        _gla_prompt_kernel,
        grid=(bsz, nt),
        in_specs=[pl.BlockSpec((tt, 768), lambda b, i: (b * nt + i, 0)),
                  pl.BlockSpec((tt, LANE), lambda b, i: (b * nt + i, 0)),
                  const((GLA_DQK, GLA_DVW)), const((GLA_DVW, GLA_DQK)), const((LANE, LANE)),
                  const((1, GLA_DVW))],
        out_specs=[pl.BlockSpec((tt, GLA_DVW), lambda b, i: (b * nt + i, 0)),
                   pl.BlockSpec((None, GLA_DVW, GLA_DQK), lambda b, i: (b, 0, 0))],
        out_shape=[jax.ShapeDtypeStruct((bsz * seq, GLA_DVW), BF16),
                   jax.ShapeDtypeStruct((bsz, GLA_DVW, GLA_DQK), F32)],
        scratch_shapes=[pltpu.VMEM((GLA_DVW, GLA_DQK), F32), pltpu.VMEM((tt, GLA_DQK), F32),
                        pltpu.VMEM((tt, GLA_DVW), F32)],
        compiler_params=_cp("arbitrary", "arbitrary"),
        name="gla_prompt",
    )(gla, la, eexp, bd, eseg, gn)


SB_TK = LANE
SB_UNROLL = 8


def _neg_abs(x):
    return lax.bitcast_convert_type(lax.bitcast_convert_type(x, jnp.int32) | jnp.int32(-2 ** 31), F32)


def _sb_prompt_kernel(q_ref, k_ref, v_ref, bias_ref, uo_ref, hm_ref, eseg_ref, gn_ref, o_ref,
                      acc_ref, car_ref, kk_ref, vv_ref):
    i = pl.program_id(2)
    tq = q_ref.shape[0]
    ndiag = tq // SB_TK
    nblk = kk_ref.shape[0]

    @pl.when(i == 0)
    def _():
        m0 = hm_ref[0:1, :]
        m1 = hm_ref[1:2, :]

        def fill(j, carry):
            k0 = pl.multiple_of(j * SB_TK, SB_TK)
            kb = k_ref[pl.ds(k0, SB_TK), :]
            vb = v_ref[pl.ds(k0, SB_TK), :]
            kk_ref[j] = jnp.concatenate([kb * m0, kb * m1], axis=0)
            vv_ref[j] = jnp.concatenate([vb * m0, vb * m1], axis=0)
            return carry

        lax.fori_loop(0, nblk, fill, 0)

    bias = bias_ref[...]
    uo = uo_ref[...]
    acc_ref[...] = jnp.zeros_like(acc_ref)
    car_ref[...] = jnp.zeros_like(car_ref)

    def block(j, r0, diag):
        rows = tq - r0
        z = lax.dot_general(q_ref[r0:, :], kk_ref[j], (((1,), (1,)), ((), ())),
                            preferred_element_type=F32) + bias
        sp = jnp.maximum(z, 0.0) + jnp.log(1.0 + jnp.exp2(_neg_abs(z))) * LOG2E
        ls = z - sp
        if diag:
            valid = (lax.broadcasted_iota(jnp.int32, (rows, 2 * SB_TK), 1) % SB_TK
                     < lax.broadcasted_iota(jnp.int32, (rows, 2 * SB_TK), 0))
            sp = jnp.where(valid, sp, 0.0)
        later = jnp.dot(sp.astype(BF16), uo, preferred_element_type=F32) + car_ref[r0:, :]
        w = jnp.exp2(ls - later)
        if diag:
            w = jnp.where(valid, w, 0.0)
        car_ref[r0:, :SB_TK] += jnp.sum(sp[:, :SB_TK], axis=1, keepdims=True)
        car_ref[r0:, SB_TK:] += jnp.sum(sp[:, SB_TK:], axis=1, keepdims=True)
        acc_ref[r0:, :] += jnp.dot(w.astype(BF16), vv_ref[j], preferred_element_type=F32)

    for c in range(ndiag - 1, -1, -1):
        block(i * ndiag + c, c * SB_TK, True)

    n_off = i * ndiag
    n_main = n_off // SB_UNROLL

    def body(jj, carry):
        for u in range(SB_UNROLL):
            block(n_off - 1 - jj * SB_UNROLL - u, 0, False)
        return carry

    lax.fori_loop(0, n_main, body, 0)
    if SB_UNROLL > ndiag:
        @pl.when(n_off - n_main * SB_UNROLL > 0)
        def _():
            for u in range(ndiag):
                block(ndiag - 1 - u, 0, False)

    o_ref[...] = (_seg_rms(acc_ref[...], eseg_ref[...], SB_HD) * gn_ref[...]).astype(BF16)


def _sb_prompt(q, k, v, bias2, uo, hm, eseg, gn, *, bsz, seq, tq):
    nq = seq // tq
    npair = SB_HEADS // 2
    const = lambda shape: pl.BlockSpec(shape, lambda b, p, i: (0,) * len(shape))
    return pl.pallas_call(
        _sb_prompt_kernel,
        grid=(bsz, npair, nq),
        in_specs=[pl.BlockSpec((tq, LANE), lambda b, p, i: (b * nq + i, p)),
                  pl.BlockSpec((seq, LANE), lambda b, p, i: (b, p)),
                  pl.BlockSpec((seq, LANE), lambda b, p, i: (b, p)),
                  pl.BlockSpec((None, 1, 2 * SB_TK), lambda b, p, i: (p, 0, 0)),
                  const((2 * SB_TK, 2 * SB_TK)), const((SUBLANE, LANE)), const((LANE, LANE)), const((1, LANE))],
        out_specs=pl.BlockSpec((tq, LANE), lambda b, p, i: (b * nq + i, p)),
        out_shape=jax.ShapeDtypeStruct((bsz * seq, SB_W), BF16),
        scratch_shapes=[pltpu.VMEM((tq, LANE), F32), pltpu.VMEM((tq, 2 * SB_TK), F32),
                        pltpu.VMEM((seq // SB_TK, 2 * SB_TK, LANE), BF16),
                        pltpu.VMEM((seq // SB_TK, 2 * SB_TK, LANE), BF16)],
        compiler_params=_cp("arbitrary", "arbitrary", "arbitrary"),
        name="sb_prompt",
    )(q, k, v, bias2, uo, hm, eseg, gn)


def _ssm_params(dt_raw, dtb, nega):
    dt = _softplus(dt_raw + dtb)
    return dt, dt * nega


SSD_C = 128


def _ssd_prompt_kernel(m_ref, prev_ref, misc_ref, dtt_ref, cw_ref, cb_ref, dtb_l_ref, nega_l_ref,
                       dtb_c_ref, nega_c_ref, e4_ref, e4t_ref, md_ref, gn_ref, o_ref, hs_out_ref,
                       hs_ref):
    i = pl.program_id(1)
    tt = m_ref.shape[0]
    cl = SSD_C

    @pl.when(i == 0)
    def _():
        hs_ref[...] = jnp.zeros_like(hs_ref)

    e4 = e4_ref[...]
    row = lax.broadcasted_iota(jnp.int32, (cl, M_CONV_DIM), 0)
    row8 = lax.broadcasted_iota(jnp.int32, (SUBLANE, M_CONV_DIM), 0)
    causal = lax.broadcasted_iota(jnp.int32, (cl, cl), 0) >= lax.broadcasted_iota(jnp.int32, (cl, cl), 1)
    lane_head = lax.broadcasted_iota(jnp.int32, (cl, M_DIN), 1) // M_HD
    rep = M_HEADS // M_GROUPS
    half = M_DIN // M_GROUPS

    for r0 in range(0, tt, cl):
        z = m_ref[r0:r0 + cl, 0:M_DIN]
        xbc = m_ref[r0:r0 + cl, M_DIN:M_DIN + M_CONV_DIM]
        if r0 == 0:
            prev = jnp.where(i > 0, prev_ref[:, M_DIN:M_DIN + M_CONV_DIM], 0.0)
        else:
            prev = m_ref[r0 - SUBLANE:r0, M_DIN:M_DIN + M_CONV_DIM]
        acc = cb_ref[...] + cw_ref[M_CONV - 1:M_CONV, :] * xbc
        for s in range(1, M_CONV):
            head = jnp.where(row8 < s, pltpu.roll(prev, s, 0), 0.0)
            head = jnp.concatenate([head, jnp.zeros((cl - SUBLANE, M_CONV_DIM), F32)], axis=0)
            shifted = jnp.where(row < s, head, pltpu.roll(xbc, s, 0))
            acc = acc + cw_ref[M_CONV - 1 - s:M_CONV - s, :] * shifted
        xc = _silu(acc)
        x = xc[:, 0:M_DIN]
        bm = xc[:, M_DIN:M_DIN + M_GROUPS * M_N]
        cm = xc[:, M_DIN + M_GROUPS * M_N:]

        dt_c, a_c = _ssm_params(misc_ref[r0:r0 + cl, :], dtb_l_ref[...], nega_l_ref[...])
        cs_c = _cumsum(a_c, 0)
        dt_r, a_r = _ssm_params(dtt_ref[:, r0:r0 + cl], dtb_c_ref[...], nega_c_ref[...])
        cs_r = _cumsum(a_r, 1)
        xdt = x * _dot_sel(dt_c, e4)
        ecs = _dot_sel(jnp.exp(cs_c), e4)

        g = [_dot_nt(cm[:, gi * M_N:(gi + 1) * M_N], bm[:, gi * M_N:(gi + 1) * M_N]) for gi in range(M_GROUPS)]
        y = jnp.zeros((cl, M_DIN), F32)
        for h in range(M_HEADS):
            decay = jnp.exp(jnp.where(causal, cs_c[:, h:h + 1] - cs_r[h:h + 1, :], NEG_BIG))
            yh = _dot(g[h // rep] * decay, xdt)
            y = jnp.where(lane_head == h, yh, y)

        cs_last = cs_c[cl - 1:cl, :]
        xw = xdt * _dot_sel(jnp.exp(cs_last - cs_c), e4)
        zz = _dot_tn(xw, bm)
        upd = jnp.concatenate([zz[:half, :M_N], zz[half:, M_N:]], axis=0)
        dec = _dot_sel_nt(e4t_ref[...], jnp.broadcast_to(jnp.exp(cs_last), (M_N, LANE)))

        hs = hs_ref[...]
        yi = [_dot_nt(cm[:, gi * M_N:(gi + 1) * M_N], hs) for gi in range(M_GROUPS)]
        hs_ref[...] = hs * dec + upd
        y = y + jnp.where(lane_head < rep, yi[0], yi[1]) * ecs

        y = (y + md_ref[...] * x) * _silu(z)
        y = jnp.concatenate([_rms_rows(y[:, gi * half:(gi + 1) * half]) for gi in range(M_GROUPS)], axis=1)
        o_ref[r0:r0 + cl, :] = (y * gn_ref[...]).astype(BF16)

    hs_out_ref[...] = hs_ref[...]


def _ssd_prompt(m, misc, dtt, cw, cb, dtb_l, nega_l, dtb_c, nega_c, e4, e4t, md, gn, *, bsz, seq, tt):
    nt = seq // tt
    per8 = tt // SUBLANE
    const = lambda shape: pl.BlockSpec(shape, lambda b, i: (0,) * len(shape))
    return pl.pallas_call(
        _ssd_prompt_kernel,
        grid=(bsz, nt),
        in_specs=[pl.BlockSpec((tt, 1024), lambda b, i: (b * nt + i, 0)),
                  pl.BlockSpec((SUBLANE, 1024), lambda b, i: (jnp.maximum((b * nt + i) * per8 - 1, 0), 0)),
                  pl.BlockSpec((tt, LANE), lambda b, i: (b * nt + i, 0)),
                  pl.BlockSpec((None, SUBLANE, tt), lambda b, i: (b, 0, i)),
                  const((M_CONV, M_CONV_DIM)), const((1, M_CONV_DIM)), const((1, LANE)), const((1, LANE)),
                  const((SUBLANE, 1)), const((SUBLANE, 1)), const((LANE, M_DIN)), const((M_DIN, LANE)),
                  const((1, M_DIN)), const((1, M_DIN))],
        out_specs=[pl.BlockSpec((tt, M_DIN), lambda b, i: (b * nt + i, 0)),
                   pl.BlockSpec((None, M_DIN, M_N), lambda b, i: (b, 0, 0))],
        out_shape=[jax.ShapeDtypeStruct((bsz * seq, M_DIN), BF16),
                   jax.ShapeDtypeStruct((bsz, M_DIN, M_N), F32)],
        scratch_shapes=[pltpu.VMEM((M_DIN, M_N), F32)],
        compiler_params=_cp("arbitrary", "arbitrary"),
        name="ssd_prompt",
    )(m, m, misc, dtt, cw, cb, dtb_l, nega_l, dtb_c, nega_c, e4, e4t, md, gn)


def _step_kernel(gla_ref, la_ref, v4_ref, gg4_ref, s0_ref, m_ref, buf_ref, misc_ref, h0_ref,
                 cw_ref, cb_ref, dtb_ref, nega_ref, e4_ref, md_ref, gng_ref, gnm_ref,
                 g_ref, s_ref, y_ref, h_ref):
    q_col = _row_to_col(gla_ref[:, 0:GLA_DQK] * (GLA_DK ** -0.5))
    k_col = _row_to_col(gla_ref[:, GLA_DQK:2 * GLA_DQK])
    dec_col = _row_to_col(jnp.exp(la_ref[...]))
    v4 = v4_ref[...]
    v_exp = jnp.concatenate([jnp.broadcast_to(v4[h:h + 1, :], (GLA_DK, GLA_DV)) for h in range(GLA_HEADS)],
                            axis=0)
    s = s0_ref[...] * dec_col + k_col * v_exp
    s_ref[...] = s
    o4 = jnp.sum((q_col * s).reshape(GLA_HEADS, GLA_DK, GLA_DV), axis=1)
    g_ref[...] = _rms_rows(o4) * gng_ref[...] * _silu(gg4_ref[...])

    z = m_ref[:, 0:M_DIN]
    acc = cb_ref[...] + cw_ref[M_CONV - 1:M_CONV, :] * m_ref[:, M_DIN:M_DIN + M_CONV_DIM]
    for s_ in range(M_CONV - 1):
        acc = acc + cw_ref[s_:s_ + 1, :] * buf_ref[s_:s_ + 1, :]
    xc = _silu(acc)
    x = xc[:, 0:M_DIN]
    bm = xc[:, M_DIN:M_DIN + M_GROUPS * M_N]
    cm = xc[:, M_DIN + M_GROUPS * M_N:]
    dt, a = _ssm_params(misc_ref[...], dtb_ref[...], nega_ref[...])
    e4 = e4_ref[...]
    xdt_col = _row_to_col(x * _dot_sel(dt, e4))
    deca_col = _row_to_col(_dot_sel(jnp.exp(a), e4))
    half = M_DIN // M_GROUPS
    spread = lambda t: jnp.concatenate(
        [jnp.broadcast_to(t[:, gi * M_N:(gi + 1) * M_N], (half, M_N)) for gi in range(M_GROUPS)], axis=0)
    hs = h0_ref[...] * deca_col + xdt_col * spread(bm)
    h_ref[...] = hs
    y = _col_to_row(jnp.sum(hs * spread(cm), axis=1, keepdims=True))
    y = (y + md_ref[...] * x) * _silu(z)
    y = jnp.concatenate([_rms_rows(y[:, gi * half:(gi + 1) * half]) for gi in range(M_GROUPS)], axis=1)
    y_ref[...] = y * gnm_ref[...]


def _step_mixers(gla, la, v4, gg4, s0, m, buf, misc, h0, cw, cb, dtb, nega, e4, md, gng, gnm):
    nb = gla.shape[0]
    per = lambda *shape: pl.BlockSpec((None,) + shape, lambda b: (b,) + (0,) * len(shape))
    const = lambda shape: pl.BlockSpec(shape, lambda b: (0,) * len(shape))
    return pl.pallas_call(
        _step_kernel,
        grid=(nb,),
        in_specs=[per(1, 768), per(1, LANE), per(GLA_HEADS, GLA_DV), per(GLA_HEADS, GLA_DV),
                  per(GLA_DQK, GLA_DV), per(1, 1024), per(M_CONV - 1, M_CONV_DIM), per(1, LANE),
                  per(M_DIN, M_N),
                  const((M_CONV, M_CONV_DIM)), const((1, M_CONV_DIM)), const((1, LANE)), const((1, LANE)),
                  const((LANE, M_DIN)), const((1, M_DIN)), const((1, GLA_DV)), const((1, M_DIN))],
        out_specs=[per(GLA_HEADS, GLA_DV), per(GLA_DQK, GLA_DV), per(1, M_DIN), per(M_DIN, M_N)],
        out_shape=[jax.ShapeDtypeStruct((nb, GLA_HEADS, GLA_DV), F32),
                   jax.ShapeDtypeStruct((nb, GLA_DQK, GLA_DV), F32),
                   jax.ShapeDtypeStruct((nb, 1, M_DIN), F32),
                   jax.ShapeDtypeStruct((nb, M_DIN, M_N), F32)],
        compiler_params=_cp("arbitrary"),
        name="step_mixers",
    )(gla.reshape(nb, 1, 768), la.reshape(nb, 1, LANE), v4, gg4, s0, m.reshape(nb, 1, 1024), buf,
      misc.reshape(nb, 1, LANE), h0, cw, cb, dtb, nega, e4, md, gng, gnm)


SB_PP = 16


def _sb_decode_kernel(pt_ref, q_ref, *refs):
    ks = refs[0:SB_PP]
    vs = refs[SB_PP:2 * SB_PP]
    bias_ref, gn_ref, uinc_ref, o_ref, acc_ref, car_ref, qb_ref = refs[2 * SB_PP:]
    j = pl.program_id(1)
    page = ks[0].shape[2]
    uinc = uinc_ref[...]

    @pl.when(j == 0)
    def _():
        acc_ref[...] = jnp.zeros_like(acc_ref)
        car_ref[...] = jnp.zeros_like(car_ref)
        qb_ref[...] = jnp.broadcast_to(q_ref[...], qb_ref.shape)

    bias = bias_ref[...]
    for r in range(SB_PP - 1, -1, -1):
        prod = (ks[r][...] * qb_ref[...]).reshape(SB_HEADS, SB_HD // SUBLANE, SUBLANE, page)
        z = jnp.sum(jnp.sum(prod, axis=1), axis=1) + bias
        sp = jnp.maximum(z, 0.0) + jnp.log(1.0 + jnp.exp2(_neg_abs(z))) * LOG2E
        ls = z - sp
        hi = sp.astype(BF16)
        lo = (sp - hi.astype(F32)).astype(BF16)
        incl = (jnp.dot(hi, uinc, preferred_element_type=F32) + jnp.dot(lo, uinc, preferred_element_type=F32))
        car = car_ref[...]
        w = jnp.exp2(ls - (incl - sp + car))
        car_ref[...] = car + incl[:, 0:1]
        for h in range(SB_HEADS):
            acc_ref[h] += vs[r][h] * w[h:h + 1, :]

    @pl.when(j == pl.num_programs(1) - 1)
    def _():
        o_ref[...] = _rms_rows(jnp.sum(acc_ref[...], axis=-1)) * gn_ref[...]


def _sb_decode(page_table, q, cache_kt, cache_vt, layer, bias_col, gn, uinc):
    nb, n_pages = page_table.shape
    page = cache_kt.shape[4]
    ng = n_pages // SB_PP
    pt = page_table.reshape(-1)

    def kv_spec(r):
        return pl.BlockSpec((None, None, SB_HEADS, SB_HD, page),
                            lambda b, j, pt_ref: (layer, pt_ref[b * n_pages + (ng - 1 - j) * SB_PP + r], 0, 0, 0))

    const = lambda shape: pl.BlockSpec(shape, lambda b, j, pt_ref: (0,) * len(shape))
    grid_spec = pltpu.PrefetchScalarGridSpec(
        num_scalar_prefetch=1,
        grid=(nb, ng),
        in_specs=[pl.BlockSpec((None, SB_HEADS, SB_HD, 1), lambda b, j, pt_ref: (b, 0, 0, 0))]
        + [kv_spec(r) for r in range(SB_PP)] + [kv_spec(r) for r in range(SB_PP)]
        + [const((SB_HEADS, 1)), const((1, SB_HD)), const((page, page))],
        out_specs=pl.BlockSpec((None, SB_HEADS, SB_HD), lambda b, j, pt_ref: (b, 0, 0)),
        scratch_shapes=[pltpu.VMEM((SB_HEADS, SB_HD, page), F32), pltpu.VMEM((SB_HEADS, 1), F32),
                        pltpu.VMEM((SB_HEADS, SB_HD, page), F32)],
    )
    out = pl.pallas_call(
        _sb_decode_kernel,
        grid_spec=grid_spec,
        out_shape=jax.ShapeDtypeStruct((nb, SB_HEADS, SB_HD), F32),
        compiler_params=_cp("arbitrary", "arbitrary"),
        name="sb_decode",
    )(pt, q, *([cache_kt] * SB_PP), *([cache_vt] * SB_PP), bias_col, gn, uinc)
    return out.reshape(nb, SB_W)


def _outproj_kernel(x_ref, gate_ref, g_ref, s_ref, y_ref, w_ref, o_ref):
    mix = (_dot(g_ref[...], w_ref[0:GLA_DVW, :]) + _dot(s_ref[...], w_ref[GLA_DVW:GLA_DVW + SB_W, :])
           + _dot(y_ref[...], w_ref[GLA_DVW + SB_W:, :]))
    o_ref[...] = x_ref[...] + gate_ref[...] * mix


def _out_proj(x, gate, g, s, y, w, *, tm, rows_per_mod):
    r, d = x.shape
    mrows = gate.shape[1]
    row = lambda n: pl.BlockSpec((tm, n), lambda i: (i, 0))
    return pl.pallas_call(
        _outproj_kernel,
        grid=(r // tm,),
        in_specs=[row(d), pl.BlockSpec((None, mrows, d), lambda i: (i // rows_per_mod, 0, 0)),
                  row(GLA_DVW), row(SB_W), row(M_DIN), pl.BlockSpec(w.shape, lambda i: (0, 0))],
        out_specs=row(d),
        out_shape=jax.ShapeDtypeStruct((r, d), F32),
        compiler_params=_cp("arbitrary"),
        name="out_proj",
    )(x, gate, g, s, y, w)


def _ffn_up_kernel(x_ref, sc_ref, sh_ref, g_ref, w_ref, u_ref):
    h = _norm_mod(x_ref[...], g_ref[...], sc_ref[...], sh_ref[...])
    u_ref[...] = jnp.dot(h.astype(BF16), w_ref[...], preferred_element_type=F32)


def _ffn_up(x, sc, sh, g, w, *, tm, rows_per_mod):
    r, d = x.shape
    n = w.shape[1]
    mrows = sc.shape[1]
    mod_spec = pl.BlockSpec((None, mrows, d), lambda i: (i // rows_per_mod, 0, 0))
    return pl.pallas_call(
        _ffn_up_kernel,
        grid=(r // tm,),
        in_specs=[pl.BlockSpec((tm, d), lambda i: (i, 0)), mod_spec, mod_spec,
                  pl.BlockSpec((1, d), lambda i: (0, 0)), pl.BlockSpec((d, n), lambda i: (0, 0))],
        out_specs=pl.BlockSpec((tm, n), lambda i: (i, 0)),
        out_shape=jax.ShapeDtypeStruct((r, n), F32),
        compiler_params=_cp("arbitrary"),
        name="ffn_up",
    )(x, sc, sh, g, w)


FFN_CK = 256
FFN_RB = 64


def _ffn_act_down(taps, cw_ref, cb_ref, w_ref):
    dff = w_ref.shape[0]

    def conv(c0):
        u, u1, u2 = taps(c0)
        cs = slice(c0, c0 + FFN_CK)
        return cb_ref[:, cs] + cw_ref[0:1, cs] * u2 + cw_ref[1:2, cs] * u1 + cw_ref[2:3, cs] * u

    out = None
    for c0 in range(0, dff, FFN_CK):
        part = _dot(_silu(conv(c0)) * conv(dff + c0), w_ref[c0:c0 + FFN_CK, :])
        out = part if out is None else out + part
    return out


def _ffn_up_act_kernel(x_ref, sc_ref, sh_ref, g_ref, w_ref, cw_ref, cb_ref, a_ref, tail_ref, carry_ref,
                       u_ref, *, tiles_per_seq):
    i = pl.program_id(0)
    tm = x_ref.shape[0]
    dff = a_ref.shape[1]
    first = i % tiles_per_seq == 0
    h = _norm_mod(x_ref[...], g_ref[...], sc_ref[...], sh_ref[...]).astype(BF16)
    row8 = lax.broadcasted_iota(jnp.int32, (SUBLANE, FFN_CK), 0)

    def up(base, slot, half):
        cs = slice(base, base + FFN_CK)
        u = jnp.dot(h, w_ref[:, cs], preferred_element_type=F32)
        u_ref[slot, half, 0:SUBLANE, :] = jnp.where(first, 0.0, carry_ref[:, cs])
        u_ref[slot, half, SUBLANE:, :] = u
        last = u[tm - SUBLANE:, :]
        carry_ref[:, cs] = last
        tail_ref[:, cs] = last

    def conv(slot, half, base, rb):
        cs = slice(base, base + FFN_CK)
        acc = cb_ref[:, cs]
        for s in range(FFN_CONV):
            r0 = SUBLANE + rb - s
            acc = acc + cw_ref[FFN_CONV - 1 - s:FFN_CONV - s, cs] * u_ref[slot, half, r0:r0 + FFN_RB, :]
        return acc

    for n, c0 in enumerate(range(0, dff, FFN_CK)):
        slot = n % 2
        up(c0, slot, 0)
        up(dff + c0, slot, 1)
        for rb in range(0, tm, FFN_RB):
            a_ref[rb:rb + FFN_RB, c0:c0 + FFN_CK] = (
                _silu(conv(slot, 0, c0, rb)) * conv(slot, 1, dff + c0, rb)).astype(BF16)


def _ffn_up_act(x, sc, sh, g, w, cw, cb, *, tm, seq):
    r, d = x.shape
    n = w.shape[1]
    dff = n // 2
    tiles = seq // tm
    mod_spec = pl.BlockSpec((None, 1, d), lambda i: (i // tiles, 0, 0))
    const = lambda shape: pl.BlockSpec(shape, lambda i: (0,) * len(shape))
    return pl.pallas_call(
        functools.partial(_ffn_up_act_kernel, tiles_per_seq=tiles),
        grid=(r // tm,),
        in_specs=[pl.BlockSpec((tm, d), lambda i: (i, 0)), mod_spec, mod_spec, const((1, d)), const((d, n)),
                  const((FFN_CONV, n)), const((1, n))],
        out_specs=[pl.BlockSpec((tm, dff), lambda i: (i, 0)),
                   pl.BlockSpec((None, SUBLANE, n), lambda i: (i // tiles, 0, 0))],
        out_shape=[jax.ShapeDtypeStruct((r, dff), BF16), jax.ShapeDtypeStruct((r // seq, SUBLANE, n), F32)],
        scratch_shapes=[pltpu.VMEM((SUBLANE, n), F32), pltpu.VMEM((2, 2, SUBLANE + tm, FFN_CK), F32)],
        compiler_params=_cp("arbitrary"),
        name="ffn_up_act",
    )(x, sc, sh, g, w, cw, cb)


def _ffn_down_kernel(x_ref, gate_ref, a_ref, w_ref, o_ref):
    o_ref[...] = x_ref[...] + gate_ref[...] * jnp.dot(a_ref[...], w_ref[...], preferred_element_type=F32)


def _ffn_down(x, gate, a, w, *, tm, seq):
    r, d = x.shape
    tiles = seq // tm
    return pl.pallas_call(
        _ffn_down_kernel,
        grid=(r // tm,),
        in_specs=[pl.BlockSpec((tm, d), lambda i: (i, 0)),
                  pl.BlockSpec((None, 1, d), lambda i: (i // tiles, 0, 0)),
                  pl.BlockSpec((tm, a.shape[1]), lambda i: (i, 0)),
                  pl.BlockSpec(w.shape, lambda i: (0, 0))],
        out_specs=pl.BlockSpec((tm, d), lambda i: (i, 0)),
        out_shape=jax.ShapeDtypeStruct((r, d), F32),
        compiler_params=_cp("arbitrary"),
        name="ffn_down",
    )(x, gate, a, w)


def _ffn_down_step_kernel(x_ref, gate_ref, u_ref, buf_ref, cw_ref, cb_ref, w_ref, o_ref):
    taps = lambda c0: (u_ref[:, c0:c0 + FFN_CK], buf_ref[1, :, c0:c0 + FFN_CK], buf_ref[0, :, c0:c0 + FFN_CK])
    o_ref[...] = x_ref[...] + gate_ref[...] * _ffn_act_down(taps, cw_ref, cb_ref, w_ref)


def _ffn_down_step(x, gate, u, buf_t, cw, cb, w):
    full = lambda a: pl.BlockSpec(a.shape, lambda i: (0,) * a.ndim)
    args = (x, gate, u, buf_t, cw, cb, w)
    return pl.pallas_call(
        _ffn_down_step_kernel,
        grid=(1,),
        in_specs=[full(a) for a in args],
        out_specs=full(x),
        out_shape=jax.ShapeDtypeStruct(x.shape, F32),
        compiler_params=_cp("arbitrary"),
        name="ffn_down_step",
    )(*args)


def _same_segment(n, width):
    i = np.arange(n)
    return jnp.asarray((i[:, None] // width) == (i[None, :] // width), BF16)


def _constants():
    k = np.arange(GLA_DQK)
    v = np.arange(GLA_DVW)
    eexp = (k[:, None] // GLA_DK) == (v[None, :] // GLA_DV)
    j = np.arange(SB_TK)
    later = j[:, None] > j[None, :]
    uo = np.block([[later, np.zeros_like(later)], [np.zeros_like(later), later]])
    lane = np.arange(LANE)
    hm_pair = np.zeros((SUBLANE, LANE), np.float32)
    hm_pair[0] = lane < SB_HD
    hm_pair[1] = lane >= SB_HD
    p = np.arange(M_DIN)
    e4 = lane[:, None] == (p[None, :] // M_HD)
    return dict(
        eexp=jnp.asarray(eexp, BF16), bd=jnp.asarray(eexp.T, F32),
        uo=jnp.asarray(uo, BF16), hm_pair=jnp.asarray(hm_pair, BF16), eseg64=_same_segment(LANE, 64),
        e4=jnp.asarray(e4, BF16), e4t=jnp.asarray(e4.T, BF16))


def _pad_lanes(v, n=LANE):
    return jnp.zeros((1, n), F32).at[0, :v.shape[0]].set(v)


def _pad_col(v, n=SUBLANE):
    return jnp.zeros((n, 1), F32).at[:v.shape[0], 0].set(v)


def kernel(x_prompt, x_sample, cache_sb_k, cache_sb_v, state_gla, state_mamba_conv, state_mamba_ssm, state_ffn_conv, page_table, c_prompt, c_sample, norm1_g, w_ada, b_ada, w_in, gla_w_gate2, gla_b_gate, gla_norm_g, sb_q_norm_g, sb_k_norm_g, sb_o_norm_g, sb_bias, m_conv_w, m_conv_b, m_dt_bias, m_a_log, m_d, m_norm_g, w_out, norm2_g, ffn_w_up, ffn_conv_w, ffn_conv_b, ffn_w_down):
    bsz, seq, d = x_prompt.shape
    nb = x_sample.shape[0]
    depth = w_in.shape[0]
    dff = ffn_w_down.shape[1]
    n_pool, page = cache_sb_k.shape[1], cache_sb_k.shape[2]
    assert x_sample.shape[1] == 1 and d % LANE == 0 and seq % 256 == 0
    cst = _constants()

    tm = 256
    tm_mm = 512
    tq = 512
    tt_gla = 256
    tt_ssd = 2 * SSD_C
    assert SB_UNROLL == 2 * (tq // SB_TK) or (tq // SB_TK) % SB_UNROLL == 0
    assert page_table.shape[1] % SB_PP == 0
    assert seq % tq == 0 and seq % tm == 0 and seq % tm_mm == 0 and seq % tt_gla == 0 and seq % tt_ssd == 0

    mod = _modulation(jnp.concatenate([c_prompt, c_sample], axis=0), w_ada, b_ada)
    mod = mod.reshape(depth, bsz + nb, N_MOD, d)
    mod_p = mod[:, :bsz].transpose(0, 2, 1, 3).reshape(depth, N_MOD, bsz, 1, d)
    mod_s = mod[:, bsz:].transpose(0, 2, 1, 3).reshape(depth, N_MOD, 1, nb, d)

    cache_kt = cache_sb_k.transpose(0, 1, 3, 4, 2)
    cache_vt = cache_sb_v.transpose(0, 1, 3, 4, 2)

    xp = x_prompt.reshape(bsz * seq, d)
    xs = x_sample.reshape(nb, d)
    w_in_t = w_in.transpose(2, 0, 1)
    outs = {name: [] for name in ("pk", "pv", "pg", "pc", "ph", "pf", "sk", "sv", "sg", "sc", "sh", "sf")}
    for l in range(depth):
        wi = w_in_t[:, l, :]
        o = np.cumsum([0, GLA_DQK, GLA_DQK, GLA_DVW, GLA_DVW, GLA_LR, SB_W, SB_W, SB_W, M_DIN, M_CONV_DIM, M_HEADS])
        sl = lambda a, b: wi[o[a]:o[b]]
        w_re = jnp.concatenate(
            [sl(0, 4), sl(5, 8), sl(8, 10), sl(10, 11), sl(4, 5),
             jnp.zeros((LANE - M_HEADS - GLA_LR, d), F32)], axis=0).astype(BF16)
        wg = jnp.zeros((LANE, LANE), F32).at[M_HEADS:M_HEADS + GLA_LR].set(gla_w_gate2[l]).astype(BF16)
        bg = gla_b_gate[l].reshape(1, GLA_DQK)
        gq = jnp.tile(sb_q_norm_g[l], SB_HEADS).reshape(1, SB_W)
        gk = jnp.tile(sb_k_norm_g[l], SB_HEADS).reshape(1, SB_W)
        go = jnp.tile(sb_o_norm_g[l], SB_HEADS).reshape(1, SB_W)
        gng = jnp.tile(gla_norm_g[l], GLA_HEADS).reshape(1, GLA_DVW)
        g1 = norm1_g[l].reshape(1, d)
        g2 = norm2_g[l].reshape(1, d)
        bias_pair = jnp.repeat(sb_bias[l] * LOG2E, SB_TK).reshape(SB_HEADS // 2, 1, 2 * SB_TK)
        bias_col = (sb_bias[l] * LOG2E).reshape(SB_HEADS, 1)
        cw = m_conv_w[l]
        cb = m_conv_b[l].reshape(1, M_CONV_DIM)
        nega = -jnp.exp(m_a_log[l])
        dtb_l, nega_l = _pad_lanes(m_dt_bias[l]), _pad_lanes(nega)
        dtb_c, nega_c = _pad_col(m_dt_bias[l]), _pad_col(nega)
        md = jnp.repeat(m_d[l], M_HD).reshape(1, M_DIN)
        gnm = m_norm_g[l].reshape(1, M_DIN)
        wo = w_out[l].astype(BF16)
        wu = ffn_w_up[l].astype(BF16)
        wd = ffn_w_down[l].astype(BF16)
        fcw = ffn_conv_w[l]
        fcb = ffn_conv_b[l].reshape(1, 2 * dff)
        sh1, sc1, gt1, sh2, sc2, gt2 = range(N_MOD)

        mp = mod_p[l]
        gla, la, q, k, kb, v, vb, m, misc = _in_proj(
            xp, mp[sc1], mp[sh1], g1, w_re, wg, bg, gq, gk, cst["eseg64"], tm=tm, rows_per_mod=seq // tm,
            kv_transposed=True)
        g_mix, g_state = _gla_prompt(gla, la, cst["eexp"], cst["bd"], cst["eseg64"], gng,
                                     bsz=bsz, seq=seq, tt=tt_gla)
        s_mix = _sb_prompt(q, kb, vb, bias_pair, cst["uo"], cst["hm_pair"], cst["eseg64"], go[:, :LANE],
                           bsz=bsz, seq=seq, tq=tq)
        dtt = jnp.pad(misc[:, :M_HEADS].reshape(bsz, seq, M_HEADS).transpose(0, 2, 1),
                      ((0, 0), (0, SUBLANE - M_HEADS), (0, 0)))
        y_mix, h_state = _ssd_prompt(m, misc, dtt, cw, cb, dtb_l, nega_l, dtb_c, nega_c, cst["e4"], cst["e4t"],
                                     md, gnm, bsz=bsz, seq=seq, tt=tt_ssd)
        xp = _out_proj(xp, mp[gt1], g_mix, s_mix, y_mix, wo, tm=tm_mm, rows_per_mod=seq // tm_mm)
        act, u_tail = _ffn_up_act(xp, mp[sc2], mp[sh2], g2, wu, fcw, fcb, tm=tm, seq=seq)
        xp = _ffn_down(xp, mp[gt2], act, wd, tm=tm_mm, seq=seq)

        outs["pk"].append(k.reshape(bsz, SB_HEADS, SB_HD, seq).transpose(0, 3, 1, 2))
        outs["pv"].append(v.reshape(bsz, SB_HEADS, SB_HD, seq).transpose(0, 3, 1, 2))
        gs = g_state.reshape(bsz, GLA_HEADS, GLA_DV, GLA_HEADS, GLA_DK)
        outs["pg"].append(jnp.stack([gs[:, h, :, h, :] for h in range(GLA_HEADS)], axis=1).transpose(0, 1, 3, 2))
        outs["pc"].append(m.reshape(bsz, seq, 1024)[:, seq - (M_CONV - 1):, M_DIN:])
        outs["ph"].append(h_state.reshape(bsz, M_HEADS, M_HD, M_N))
        outs["pf"].append(u_tail[:, SUBLANE - (FFN_CONV - 1):])

        ms = mod_s[l]
        gla, la, q, k, kb, v, vb, m, misc = _in_proj(
            xs, ms[sc1], ms[sh1], g1, w_re, wg, bg, gq, gk, cst["eseg64"], tm=nb, rows_per_mod=1,
            kv_transposed=False)
        g4, g_state, y_mix, h_state = _step_mixers(
            gla, la, gla[:, 2 * GLA_DQK:2 * GLA_DQK + GLA_DVW].reshape(nb, GLA_HEADS, GLA_DV),
            gla[:, 2 * GLA_DQK + GLA_DVW:].reshape(nb, GLA_HEADS, GLA_DV),
            state_gla[l].reshape(nb, GLA_DQK, GLA_DV), m, state_mamba_conv[l], misc,
            state_mamba_ssm[l].reshape(nb, M_DIN, M_N), cw, cb, dtb_l, nega_l, cst["e4"], md,
            gla_norm_g[l].reshape(1, GLA_DV), gnm)
        s_mix = _sb_decode(page_table, q.astype(F32).reshape(nb, SB_HEADS, SB_HD, 1), cache_kt, cache_vt, l,
                           bias_col, sb_o_norm_g[l].reshape(1, SB_HD),
                           jnp.asarray(np.tril(np.ones((page, page), np.float32)), BF16))
        xs = _out_proj(xs, ms[gt1], g4.reshape(nb, GLA_DVW), s_mix, y_mix.reshape(nb, M_DIN), wo,
                       tm=nb, rows_per_mod=1)
        u = _ffn_up(xs, ms[sc2], ms[sh2], g2, wu, tm=nb, rows_per_mod=1)
        xs = _ffn_down_step(xs, ms[gt2][0], u, state_ffn_conv[l].transpose(1, 0, 2), fcw, fcb, wd)

        outs["sk"].append(k.reshape(nb, 1, SB_HEADS, SB_HD))
        outs["sv"].append(v.reshape(nb, 1, SB_HEADS, SB_HD))
        outs["sg"].append(g_state.reshape(nb, GLA_HEADS, GLA_DK, GLA_DV))
        outs["sc"].append(jnp.concatenate([state_mamba_conv[l][:, 1:], m[:, None, M_DIN:]], axis=1))
        outs["sh"].append(h_state.reshape(nb, M_HEADS, M_HD, M_N))
        outs["sf"].append(jnp.concatenate([state_ffn_conv[l][:, 1:], u[:, None, :]], axis=1))

    st = {name: jnp.stack(v) for name, v in outs.items()}
    return (xp.reshape(bsz, seq, d), xs.reshape(nb, 1, d), st["pk"], st["pv"], st["pg"], st["pc"], st["ph"],
            st["pf"], st["sk"], st["sv"], st["sg"], st["sc"], st["sh"], st["sf"])
```

```python
import functools

import jax
import jax.numpy as jnp
import numpy as np
from jax import lax
from jax.experimental import pallas as pl
from jax.experimental.pallas import tpu as pltpu

F32 = jnp.float32
BF16 = jnp.bfloat16

GLA_HEADS, GLA_DK, GLA_DV, GLA_LR, GLA_TAU = 4, 32, 64, 16, 16.0
GLA_DQK = GLA_HEADS * GLA_DK
GLA_DVW = GLA_HEADS * GLA_DV
SB_HEADS, SB_HD = 8, 64
SB_W = SB_HEADS * SB_HD
M_HEADS, M_HD, M_GROUPS, M_N, M_CONV = 4, 64, 2, 128, 4
M_DIN = M_HEADS * M_HD
M_CONV_DIM = M_DIN + 2 * M_GROUPS * M_N
FFN_CONV = 3
N_MOD = 6
EPS = 1e-6
NEG_BIG = -1e30
LOG2E = 1.4426950408889634

C_GLA = 0
C_SBQ = 768
C_SBK = 1280
C_SBV = 1792
C_MZ = 2304
C_MISC = 3328
D_INP = 3456
LANE = 128
SUBLANE = 8

VMEM_LIMIT = 56 * 1024 * 1024


def _cp(*sem, flags=None):
    return pltpu.CompilerParams(dimension_semantics=sem, vmem_limit_bytes=VMEM_LIMIT, flags=flags)


def _dot(a, b):
    return jnp.dot(a.astype(BF16), b.astype(BF16), preferred_element_type=F32)


def _dot_nt(a, b):
    return lax.dot_general(a.astype(BF16), b.astype(BF16), (((1,), (1,)), ((), ())),
                           preferred_element_type=F32)


def _dot_tn(a, b):
    return lax.dot_general(a.astype(BF16), b.astype(BF16), (((0,), (0,)), ((), ())),
                           preferred_element_type=F32)


def _split3(x):
    hi = x.astype(BF16)
    r = x - hi.astype(F32)
    mid = r.astype(BF16)
    lo = (r - mid.astype(F32)).astype(BF16)
    return hi, mid, lo


def _dot_sel(x, e):
    hi, mid, lo = _split3(x)
    d = lambda p: jnp.dot(p, e, preferred_element_type=F32)
    return d(hi) + d(mid) + d(lo)


def _dot_sel_nt(e, x):
    hi, mid, lo = _split3(x)
    d = lambda p: lax.dot_general(e, p, (((1,), (1,)), ((), ())), preferred_element_type=F32)
    return d(hi) + d(mid) + d(lo)


def _sigmoid(x):
    return 1.0 / (1.0 + jnp.exp(-x))


def _silu(x):
    return x * _sigmoid(x)


def _softplus(x):
    return jnp.maximum(x, 0.0) + jnp.log1p(jnp.exp(-jnp.abs(x)))


def _log_sigmoid(x):
    return jnp.minimum(x, 0.0) - jnp.log1p(jnp.exp(-jnp.abs(x)))


def _rms_rows(x):
    return x * lax.rsqrt(jnp.mean(x * x, axis=-1, keepdims=True) + EPS)


def _seg_rms(x, eseg, width):
    xx = x * x
    hi = xx.astype(BF16)
    lo = (xx - hi.astype(F32)).astype(BF16)
    blocks = []
    for c in range(0, x.shape[1], LANE):
        d = lambda p: jnp.dot(p[:, c:c + LANE], eseg, preferred_element_type=F32)
        blocks.append(d(hi) + d(lo))
    ms = jnp.concatenate(blocks, axis=1) * (1.0 / width)
    return x * lax.rsqrt(ms + EPS)


def _row_to_col(row):
    n = row.shape[1]
    eye = lax.broadcasted_iota(jnp.int32, (n, n), 0) == lax.broadcasted_iota(jnp.int32, (n, n), 1)
    return jnp.sum(jnp.where(eye, row, 0.0), axis=1, keepdims=True)


def _col_to_row(col):
    n = col.shape[0]
    eye = lax.broadcasted_iota(jnp.int32, (n, n), 0) == lax.broadcasted_iota(jnp.int32, (n, n), 1)
    return jnp.sum(jnp.where(eye, col, 0.0), axis=0, keepdims=True)


def _cumsum(x, axis, seg=None):
    n = x.shape[axis] if seg is None else seg
    idx = lax.broadcasted_iota(jnp.int32, x.shape, axis)
    if seg is not None:
        idx = idx % seg
    s = 1
    while s < n:
        x = x + jnp.where(idx >= s, pltpu.roll(x, s, axis), 0.0)
        s *= 2
    return x


def _mod_kernel(c_ref, w_ref, b_ref, o_ref):
    o_ref[...] = _dot(_silu(c_ref[...]), w_ref[...]) + b_ref[...]


def _modulation(c_all, w_ada, b_ada):
    depth, d, nd = w_ada.shape
    n = c_all.shape[0]
    tn = 1024
    return pl.pallas_call(
        _mod_kernel,
        grid=(depth, nd // tn),
        in_specs=[pl.BlockSpec((n, d), lambda l, j: (0, 0)),
                  pl.BlockSpec((None, d, tn), lambda l, j: (l, 0, j)),
                  pl.BlockSpec((None, 1, tn), lambda l, j: (l, 0, j))],
        out_specs=pl.BlockSpec((None, n, tn), lambda l, j: (l, 0, j)),
        out_shape=jax.ShapeDtypeStruct((depth, n, nd), F32),
        compiler_params=_cp("arbitrary", "arbitrary"),
        name="adaln_mod",
    )(c_all, w_ada, b_ada.reshape(depth, 1, nd))


def _norm_mod(x, g, sc, sh):
    return _rms_rows(x) * g * (1.0 + sc) + sh


def _inproj_kernel(x_ref, sc_ref, sh_ref, g_ref, w_ref, wg_ref, bg_ref, gq_ref, gk_ref, eseg_ref,
                   gla_ref, la_ref, q_ref, k_ref, kb_ref, v_ref, vb_ref, m_ref, misc_ref, *, kv_transposed):
    h = _norm_mod(x_ref[...], g_ref[...], sc_ref[...], sh_ref[...])
    p = lax.dot_general(h.astype(BF16), w_ref[...], (((1,), (1,)), ((), ())), preferred_element_type=F32)
    gla_ref[...] = p[:, C_GLA:C_SBQ]
    misc = p[:, C_MISC:D_INP]
    misc_ref[...] = misc
    la_ref[...] = _log_sigmoid(_dot(misc, wg_ref[...]) + bg_ref[...]) * (1.0 / GLA_TAU)
    eseg = eseg_ref[...]
    q = _seg_rms(p[:, C_SBQ:C_SBK], eseg, SB_HD) * gq_ref[...]
    q_ref[...] = (q * (SB_HD ** -0.5 * LOG2E)).astype(BF16)
    k = _seg_rms(p[:, C_SBK:C_SBV], eseg, SB_HD) * gk_ref[...]
    k_ref[...] = k.T if kv_transposed else k
    kb_ref[...] = k.astype(BF16)
    v = p[:, C_SBV:C_MZ]
    v_ref[...] = v.T if kv_transposed else v
    vb_ref[...] = v.astype(BF16)
    m_ref[...] = p[:, C_MZ:C_MISC]


def _in_proj(x, sc, sh, g, w, wg, bg, gq, gk, eseg, *, tm, rows_per_mod, kv_transposed):
    r, d = x.shape
    mrows = sc.shape[1]
    mod_spec = pl.BlockSpec((None, mrows, d), lambda i: (i // rows_per_mod, 0, 0))
    const = lambda shape: pl.BlockSpec(shape, lambda i: (0,) * len(shape))
    row = lambda n: pl.BlockSpec((tm, n), lambda i: (i, 0))
    outs = [(768, F32), (LANE, F32), (SB_W, BF16), (SB_W, F32), (SB_W, BF16), (SB_W, F32), (SB_W, BF16),
            (1024, F32), (LANE, F32)]
    out_specs = [row(n) for n, _ in outs]
    out_shape = [jax.ShapeDtypeStruct((r, n), dt) for n, dt in outs]
    if kv_transposed:
        seq = rows_per_mod * tm
        for idx in (3, 5):
            out_specs[idx] = pl.BlockSpec((None, SB_W, tm), lambda i: (i // rows_per_mod, 0, i % rows_per_mod))
            out_shape[idx] = jax.ShapeDtypeStruct((r // seq, SB_W, seq), F32)
    return pl.pallas_call(
        functools.partial(_inproj_kernel, kv_transposed=kv_transposed),
        grid=(r // tm,),
        in_specs=[row(d), mod_spec, mod_spec, const((1, d)), const((D_INP, d)), const((LANE, LANE)),
                  const((1, LANE)), const((1, SB_W)), const((1, SB_W)), const((LANE, LANE))],
        out_specs=out_specs,
        out_shape=out_shape,
        compiler_params=_cp("arbitrary"),
        name="in_proj",
    )(x, sc, sh, g, w, wg, bg, gq, gk, eseg)


GLA_C = SUBLANE
GLA_UNROLL = 8


def _gla_prompt_kernel(gla_ref, la_ref, eexp_ref, bd_ref, eseg_ref, gn_ref, o_ref, st_out_ref,
                       st_ref, b_ref, acc_ref):
    i = pl.program_id(1)
    tt = la_ref.shape[0]

    @pl.when(i == 0)
    def _():
        st_ref[...] = jnp.zeros_like(st_ref)

    b_ref[...] = _cumsum(la_ref[...], 0, seg=GLA_C)
    eexp = eexp_ref[...]
    bd = bd_ref[...]
    rowid = lax.broadcasted_iota(jnp.int32, (GLA_C, GLA_DQK), 0)
    rowid_v = lax.broadcasted_iota(jnp.int32, (GLA_C, GLA_DVW), 0)

    def group(gi, carry):
        steps = []
        for u in range(GLA_UNROLL):
            r0 = pl.multiple_of((gi * GLA_UNROLL + u) * GLA_C, GLA_C)
            bc = b_ref[pl.ds(r0, GLA_C), :]
            qc = gla_ref[pl.ds(r0, GLA_C), 0:GLA_DQK] * (GLA_DK ** -0.5)
            kc = gla_ref[pl.ds(r0, GLA_C), GLA_DQK:2 * GLA_DQK]
            vc = gla_ref[pl.ds(r0, GLA_C), 2 * GLA_DQK:2 * GLA_DQK + GLA_DVW]
            steps.append((r0, bc, qc, kc, vc, bc[GLA_C - 1:GLA_C, :]))
        ps = []
        for r0, bc, qc, kc, vc, bl in steps:
            for t in range(GLA_C):
                d = jnp.where(rowid <= t, bc[t:t + 1, :] - bc, NEG_BIG)
                ps.append(jnp.exp(d) * (qc[t:t + 1, :] * kc))
        sc = _dot(jnp.concatenate(ps, axis=0), eexp)
        upds = [_dot_tn(vc, kc * jnp.exp(bl - bc)) * bd for r0, bc, qc, kc, vc, bl in steps]
        st = st_ref[...]
        for u, (r0, bc, qc, kc, vc, bl) in enumerate(steps):
            o = _dot_nt(qc * jnp.exp(bc), st)
            st = st * jnp.exp(bl) + upds[u]
            for t in range(GLA_C):
                row0 = (u * GLA_C + t) * GLA_C
                ot = jnp.sum(sc[row0:row0 + GLA_C, :] * vc, axis=0, keepdims=True)
                o = o + jnp.where(rowid_v == t, ot, 0.0)
            acc_ref[pl.ds(r0, GLA_C), :] = o
        st_ref[...] = st
        return carry

    lax.fori_loop(0, tt // (GLA_C * GLA_UNROLL), group, 0)
    gg = gla_ref[:, 2 * GLA_DQK + GLA_DVW:2 * GLA_DQK + 2 * GLA_DVW]
    o = _seg_rms(acc_ref[...], eseg_ref[...], GLA_DV) * gn_ref[...] * _silu(gg)
    o_ref[...] = o.astype(BF16)
    st_out_ref[...] = st_ref[...]


def _gla_prompt(gla, la, eexp, bd, eseg, gn, *, bsz, seq, tt):
    nt = seq // tt
    const = lambda shape: pl.BlockSpec(shape, lambda b, i: (0,) * len(shape))
    return pl.pallas_call(
        _gla_prompt_kernel,
        grid=(bsz, nt),
        in_specs=[pl.BlockSpec((tt, 768), lambda b, i: (b * nt + i, 0)),
                  pl.BlockSpec((tt, LANE), lambda b, i: (b * nt + i, 0)),
                  const((GLA_DQK, GLA_DVW)), const((GLA_DVW, GLA_DQK)), const((LANE, LANE)),
                  const((1, GLA_DVW))],
        out_specs=[pl.BlockSpec((tt, GLA_DVW), lambda b, i: (b * nt + i, 0)),
                   pl.BlockSpec((None, GLA_DVW, GLA_DQK), lambda b, i: (b, 0, 0))],
        out_shape=[jax.ShapeDtypeStruct((bsz * seq, GLA_DVW), BF16),
                   jax.ShapeDtypeStruct((bsz, GLA_DVW, GLA_DQK), F32)],
        scratch_shapes=[pltpu.VMEM((GLA_DVW, GLA_DQK), F32), pltpu.VMEM((tt, GLA_DQK), F32),
                        pltpu.VMEM((tt, GLA_DVW), F32)],
        compiler_params=_cp("arbitrary", "arbitrary"),
        name="gla_prompt",
    )(gla, la, eexp, bd, eseg, gn)


SB_TK = LANE
SB_NB = 3
SB_UNROLL = 8


def _neg_abs(x):
    return lax.bitcast_convert_type(lax.bitcast_convert_type(x, jnp.int32) | jnp.int32(-2 ** 31), F32)


def _sb_prompt_kernel(q_ref, k_ref, v_ref, bcol_ref, uo_ref, hm_ref, eseg_ref, gn_ref, o_ref,
                      acc_ref, car_ref, kk_ref, vv_ref, qx_ref):
    i = pl.program_id(2)
    tq = q_ref.shape[0]
    ndiag = tq // SB_TK
    nblk = kk_ref.shape[0]

    @pl.when(i == 0)
    def _():
        m0 = hm_ref[0:1, :]
        m1 = hm_ref[1:2, :]
        bcol = bcol_ref[...]

        def fill(j, carry):
            k0 = pl.multiple_of(j * SB_TK, SB_TK)
            kb = k_ref[pl.ds(k0, SB_TK), :]
            vb = v_ref[pl.ds(k0, SB_TK), :]
            kk_ref[j] = jnp.concatenate([jnp.concatenate([kb * m0, kb * m1], axis=0), bcol], axis=1)
            vv_ref[j] = jnp.concatenate([vb * m0, vb * m1], axis=0)
            return carry

        lax.fori_loop(0, nblk, fill, 0)

    ones = jnp.where(lax.broadcasted_iota(jnp.int32, (tq, LANE), 1) < SB_NB, 1.0, 0.0).astype(BF16)
    qx_ref[...] = jnp.concatenate([q_ref[...], ones], axis=1)
    uo = uo_ref[...]
    acc_ref[...] = jnp.zeros_like(acc_ref)
    car_ref[...] = jnp.zeros_like(car_ref)

    def block(j, r0, diag):
        rows = tq - r0
        z = lax.dot_general(qx_ref[r0:, :], kk_ref[j], (((1,), (1,)), ((), ())),
                            preferred_element_type=F32)
        sp = jnp.maximum(z, 0.0) + jnp.log(1.0 + jnp.exp2(_neg_abs(z))) * LOG2E
        if diag:
            valid = (lax.broadcasted_iota(jnp.int32, (rows, 2 * SB_TK), 1) % SB_TK
                     < lax.broadcasted_iota(jnp.int32, (rows, 2 * SB_TK), 0))
            sp = jnp.where(valid, sp, 0.0)
        incl = jnp.dot(sp.astype(BF16), uo, preferred_element_type=F32) + car_ref[r0:, :]
        w = jnp.exp2(z - incl)
        if diag:
            w = jnp.where(valid, w, 0.0)
        car_ref[r0:, :SB_TK] += jnp.sum(sp[:, :SB_TK], axis=1, keepdims=True)
        car_ref[r0:, SB_TK:] += jnp.sum(sp[:, SB_TK:], axis=1, keepdims=True)
        acc_ref[r0:, :] += jnp.dot(w.astype(BF16), vv_ref[j], preferred_element_type=F32)

    for c in range(ndiag - 1, -1, -1):
        block(i * ndiag + c, c * SB_TK, True)

    n_off = i * ndiag
    n_main = n_off // SB_UNROLL

    def body(jj, carry):
        for u in range(SB_UNROLL):
            block(n_off - 1 - jj * SB_UNROLL - u, 0, False)
        return carry

    lax.fori_loop(0, n_main, body, 0)
    if SB_UNROLL > ndiag:
        @pl.when(n_off - n_main * SB_UNROLL > 0)
        def _():
            for u in range(ndiag):
                block(ndiag - 1 - u, 0, False)

    o_ref[...] = (_seg_rms(acc_ref[...], eseg_ref[...], SB_HD) * gn_ref[...]).astype(BF16)


def _sb_prompt(q, k, v, bias2, uo, hm, eseg, gn, *, bsz, seq, tq):
    nq = seq // tq
    npair = SB_HEADS // 2
    const = lambda shape: pl.BlockSpec(shape, lambda b, p, i: (0,) * len(shape))
    return pl.pallas_call(
        _sb_prompt_kernel,
        grid=(bsz, npair, nq),
        in_specs=[pl.BlockSpec((tq, LANE), lambda b, p, i: (b * nq + i, p)),
                  pl.BlockSpec((seq, LANE), lambda b, p, i: (b, p)),
                  pl.BlockSpec((seq, LANE), lambda b, p, i: (b, p)),
                  pl.BlockSpec((None, 2 * SB_TK, LANE), lambda b, p, i: (p, 0, 0)),
                  const((2 * SB_TK, 2 * SB_TK)), const((SUBLANE, LANE)), const((LANE, LANE)), const((1, LANE))],
        out_specs=pl.BlockSpec((tq, LANE), lambda b, p, i: (b * nq + i, p)),
        out_shape=jax.ShapeDtypeStruct((bsz * seq, SB_W), BF16),
        scratch_shapes=[pltpu.VMEM((tq, LANE), F32), pltpu.VMEM((tq, 2 * SB_TK), F32),
                        pltpu.VMEM((seq // SB_TK, 2 * SB_TK, 2 * LANE), BF16),
                        pltpu.VMEM((seq // SB_TK, 2 * SB_TK, LANE), BF16),
                        pltpu.VMEM((tq, 2 * LANE), BF16)],
        compiler_params=_cp("arbitrary", "arbitrary", "arbitrary"),
        name="sb_prompt",
    )(q, k, v, bias2, uo, hm, eseg, gn)


def _ssm_params(dt_raw, dtb, nega):
    dt = _softplus(dt_raw + dtb)
    return dt, dt * nega


SSD_C = 128


def _ssd_prompt_kernel(m_ref, prev_ref, misc_ref, dtt_ref, cw_ref, cb_ref, dtb_l_ref, nega_l_ref,
                       dtb_c_ref, nega_c_ref, e4_ref, e4t_ref, md_ref, gn_ref, o_ref, hs_out_ref,
                       hs_ref):
    i = pl.program_id(1)
    tt = m_ref.shape[0]
    cl = SSD_C

    @pl.when(i == 0)
    def _():
        hs_ref[...] = jnp.zeros_like(hs_ref)

    e4 = e4_ref[...]
    row = lax.broadcasted_iota(jnp.int32, (cl, M_CONV_DIM), 0)
    row8 = lax.broadcasted_iota(jnp.int32, (SUBLANE, M_CONV_DIM), 0)
    causal = lax.broadcasted_iota(jnp.int32, (cl, cl), 0) >= lax.broadcasted_iota(jnp.int32, (cl, cl), 1)
    lane_head = lax.broadcasted_iota(jnp.int32, (cl, M_DIN), 1) // M_HD
    rep = M_HEADS // M_GROUPS
    half = M_DIN // M_GROUPS

    for r0 in range(0, tt, cl):
        z = m_ref[r0:r0 + cl, 0:M_DIN]
        xbc = m_ref[r0:r0 + cl, M_DIN:M_DIN + M_CONV_DIM]
        if r0 == 0:
            prev = jnp.where(i > 0, prev_ref[:, M_DIN:M_DIN + M_CONV_DIM], 0.0)
        else:
            prev = m_ref[r0 - SUBLANE:r0, M_DIN:M_DIN + M_CONV_DIM]
        acc = cb_ref[...] + cw_ref[M_CONV - 1:M_CONV, :] * xbc
        for s in range(1, M_CONV):
            head = jnp.where(row8 < s, pltpu.roll(prev, s, 0), 0.0)
            head = jnp.concatenate([head, jnp.zeros((cl - SUBLANE, M_CONV_DIM), F32)], axis=0)
            shifted = jnp.where(row < s, head, pltpu.roll(xbc, s, 0))
            acc = acc + cw_ref[M_CONV - 1 - s:M_CONV - s, :] * shifted
        xc = _silu(acc)
        x = xc[:, 0:M_DIN]
        bm = xc[:, M_DIN:M_DIN + M_GROUPS * M_N]
        cm = xc[:, M_DIN + M_GROUPS * M_N:]

        dt_c, a_c = _ssm_params(misc_ref[r0:r0 + cl, :], dtb_l_ref[...], nega_l_ref[...])
        cs_c = _cumsum(a_c, 0)
        dt_r, a_r = _ssm_params(dtt_ref[:, r0:r0 + cl], dtb_c_ref[...], nega_c_ref[...])
        cs_r = _cumsum(a_r, 1)
        xdt = x * _dot_sel(dt_c, e4)
        ecs = _dot_sel(jnp.exp(cs_c), e4)

        g = [_dot_nt(cm[:, gi * M_N:(gi + 1) * M_N], bm[:, gi * M_N:(gi + 1) * M_N]) for gi in range(M_GROUPS)]
        y = jnp.zeros((cl, M_DIN), F32)
        for h in range(M_HEADS):
            decay = jnp.exp(jnp.where(causal, cs_c[:, h:h + 1] - cs_r[h:h + 1, :], NEG_BIG))
            yh = _dot(g[h // rep] * decay, xdt)
            y = jnp.where(lane_head == h, yh, y)

        cs_last = cs_c[cl - 1:cl, :]
        xw = xdt * _dot_sel(jnp.exp(cs_last - cs_c), e4)
        zz = _dot_tn(xw, bm)
        upd = jnp.concatenate([zz[:half, :M_N], zz[half:, M_N:]], axis=0)
        dec = _dot_sel_nt(e4t_ref[...], jnp.broadcast_to(jnp.exp(cs_last), (M_N, LANE)))

        hs = hs_ref[...]
        yi = [_dot_nt(cm[:, gi * M_N:(gi + 1) * M_N], hs) for gi in range(M_GROUPS)]
        hs_ref[...] = hs * dec + upd
        y = y + jnp.where(lane_head < rep, yi[0], yi[1]) * ecs

        y = (y + md_ref[...] * x) * _silu(z)
        y = jnp.concatenate([_rms_rows(y[:, gi * half:(gi + 1) * half]) for gi in range(M_GROUPS)], axis=1)
        o_ref[r0:r0 + cl, :] = (y * gn_ref[...]).astype(BF16)

    hs_out_ref[...] = hs_ref[...]


def _ssd_prompt(m, misc, dtt, cw, cb, dtb_l, nega_l, dtb_c, nega_c, e4, e4t, md, gn, *, bsz, seq, tt):
    nt = seq // tt
    per8 = tt // SUBLANE
    const = lambda shape: pl.BlockSpec(shape, lambda b, i: (0,) * len(shape))
    return pl.pallas_call(
        _ssd_prompt_kernel,
        grid=(bsz, nt),
        in_specs=[pl.BlockSpec((tt, 1024), lambda b, i: (b * nt + i, 0)),
                  pl.BlockSpec((SUBLANE, 1024), lambda b, i: (jnp.maximum((b * nt + i) * per8 - 1, 0), 0)),
                  pl.BlockSpec((tt, LANE), lambda b, i: (b * nt + i, 0)),
                  pl.BlockSpec((None, SUBLANE, tt), lambda b, i: (b, 0, i)),
                  const((M_CONV, M_CONV_DIM)), const((1, M_CONV_DIM)), const((1, LANE)), const((1, LANE)),
                  const((SUBLANE, 1)), const((SUBLANE, 1)), const((LANE, M_DIN)), const((M_DIN, LANE)),
                  const((1, M_DIN)), const((1, M_DIN))],
        out_specs=[pl.BlockSpec((tt, M_DIN), lambda b, i: (b * nt + i, 0)),
                   pl.BlockSpec((None, M_DIN, M_N), lambda b, i: (b, 0, 0))],
        out_shape=[jax.ShapeDtypeStruct((bsz * seq, M_DIN), BF16),
                   jax.ShapeDtypeStruct((bsz, M_DIN, M_N), F32)],
        scratch_shapes=[pltpu.VMEM((M_DIN, M_N), F32)],
        compiler_params=_cp("arbitrary", "arbitrary"),
        name="ssd_prompt",
    )(m, m, misc, dtt, cw, cb, dtb_l, nega_l, dtb_c, nega_c, e4, e4t, md, gn)


def _step_kernel(gla_ref, la_ref, v4_ref, gg4_ref, s0_ref, m_ref, buf_ref, misc_ref, h0_ref,
                 cw_ref, cb_ref, dtb_ref, nega_ref, e4_ref, md_ref, gng_ref, gnm_ref,
                 g_ref, s_ref, y_ref, h_ref):
    q_col = _row_to_col(gla_ref[:, 0:GLA_DQK] * (GLA_DK ** -0.5))
    k_col = _row_to_col(gla_ref[:, GLA_DQK:2 * GLA_DQK])
    dec_col = _row_to_col(jnp.exp(la_ref[...]))
    v4 = v4_ref[...]
    v_exp = jnp.concatenate([jnp.broadcast_to(v4[h:h + 1, :], (GLA_DK, GLA_DV)) for h in range(GLA_HEADS)],
                            axis=0)
    s = s0_ref[...] * dec_col + k_col * v_exp
    s_ref[...] = s
    o4 = jnp.sum((q_col * s).reshape(GLA_HEADS, GLA_DK, GLA_DV), axis=1)
    g_ref[...] = _rms_rows(o4) * gng_ref[...] * _silu(gg4_ref[...])

    z = m_ref[:, 0:M_DIN]
    acc = cb_ref[...] + cw_ref[M_CONV - 1:M_CONV, :] * m_ref[:, M_DIN:M_DIN + M_CONV_DIM]
    for s_ in range(M_CONV - 1):
        acc = acc + cw_ref[s_:s_ + 1, :] * buf_ref[s_:s_ + 1, :]
    xc = _silu(acc)
    x = xc[:, 0:M_DIN]
    bm = xc[:, M_DIN:M_DIN + M_GROUPS * M_N]
    cm = xc[:, M_DIN + M_GROUPS * M_N:]
    dt, a = _ssm_params(misc_ref[...], dtb_ref[...], nega_ref[...])
    e4 = e4_ref[...]
    xdt_col = _row_to_col(x * _dot_sel(dt, e4))
    deca_col = _row_to_col(_dot_sel(jnp.exp(a), e4))
    half = M_DIN // M_GROUPS
    spread = lambda t: jnp.concatenate(
        [jnp.broadcast_to(t[:, gi * M_N:(gi + 1) * M_N], (half, M_N)) for gi in range(M_GROUPS)], axis=0)
    hs = h0_ref[...] * deca_col + xdt_col * spread(bm)
    h_ref[...] = hs
    y = _col_to_row(jnp.sum(hs * spread(cm), axis=1, keepdims=True))
    y = (y + md_ref[...] * x) * _silu(z)
    y = jnp.concatenate([_rms_rows(y[:, gi * half:(gi + 1) * half]) for gi in range(M_GROUPS)], axis=1)
    y_ref[...] = y * gnm_ref[...]


def _step_mixers(gla, la, v4, gg4, s0, m, buf, misc, h0, cw, cb, dtb, nega, e4, md, gng, gnm):
    nb = gla.shape[0]
    per = lambda *shape: pl.BlockSpec((None,) + shape, lambda b: (b,) + (0,) * len(shape))
    const = lambda shape: pl.BlockSpec(shape, lambda b: (0,) * len(shape))
    return pl.pallas_call(
        _step_kernel,
        grid=(nb,),
        in_specs=[per(1, 768), per(1, LANE), per(GLA_HEADS, GLA_DV), per(GLA_HEADS, GLA_DV),
                  per(GLA_DQK, GLA_DV), per(1, 1024), per(M_CONV - 1, M_CONV_DIM), per(1, LANE),
                  per(M_DIN, M_N),
                  const((M_CONV, M_CONV_DIM)), const((1, M_CONV_DIM)), const((1, LANE)), const((1, LANE)),
                  const((LANE, M_DIN)), const((1, M_DIN)), const((1, GLA_DV)), const((1, M_DIN))],
        out_specs=[per(GLA_HEADS, GLA_DV), per(GLA_DQK, GLA_DV), per(1, M_DIN), per(M_DIN, M_N)],
        out_shape=[jax.ShapeDtypeStruct((nb, GLA_HEADS, GLA_DV), F32),
                   jax.ShapeDtypeStruct((nb, GLA_DQK, GLA_DV), F32),
                   jax.ShapeDtypeStruct((nb, 1, M_DIN), F32),
                   jax.ShapeDtypeStruct((nb, M_DIN, M_N), F32)],
        compiler_params=_cp("arbitrary"),
        name="step_mixers",
    )(gla.reshape(nb, 1, 768), la.reshape(nb, 1, LANE), v4, gg4, s0, m.reshape(nb, 1, 1024), buf,
      misc.reshape(nb, 1, LANE), h0, cw, cb, dtb, nega, e4, md, gng, gnm)


SB_PP = 16


def _sb_decode_kernel(pt_ref, q_ref, *refs):
    ks = refs[0:SB_PP]
    vs = refs[SB_PP:2 * SB_PP]
    bias_ref, gn_ref, uinc_ref, o_ref, acc_ref, car_ref, qb_ref = refs[2 * SB_PP:]
    j = pl.program_id(1)
    page = ks[0].shape[2]
    uinc = uinc_ref[...]

    @pl.when(j == 0)
    def _():
        acc_ref[...] = jnp.zeros_like(acc_ref)
        car_ref[...] = jnp.zeros_like(car_ref)
        qb_ref[...] = jnp.broadcast_to(q_ref[...], qb_ref.shape)

    bias = bias_ref[...]
    for r in range(SB_PP - 1, -1, -1):
        prod = (ks[r][...] * qb_ref[...]).reshape(SB_HEADS, SB_HD // SUBLANE, SUBLANE, page)
        z = jnp.sum(jnp.sum(prod, axis=1), axis=1) + bias
        sp = jnp.maximum(z, 0.0) + jnp.log(1.0 + jnp.exp2(_neg_abs(z))) * LOG2E
        ls = z - sp
        hi = sp.astype(BF16)
        lo = (sp - hi.astype(F32)).astype(BF16)
        incl = (jnp.dot(hi, uinc, preferred_element_type=F32) + jnp.dot(lo, uinc, preferred_element_type=F32))
        car = car_ref[...]
        w = jnp.exp2(ls - (incl - sp + car))
        car_ref[...] = car + incl[:, 0:1]
        for h in range(SB_HEADS):
            acc_ref[h] += vs[r][h] * w[h:h + 1, :]

    @pl.when(j == pl.num_programs(1) - 1)
    def _():
        o_ref[...] = _rms_rows(jnp.sum(acc_ref[...], axis=-1)) * gn_ref[...]


def _sb_decode(page_table, q, cache_kt, cache_vt, layer, bias_col, gn, uinc):
    nb, n_pages = page_table.shape
    page = cache_kt.shape[4]
    ng = n_pages // SB_PP
    pt = page_table.reshape(-1)

    def kv_spec(r):
        return pl.BlockSpec((None, None, SB_HEADS, SB_HD, page),
                            lambda b, j, pt_ref: (layer, pt_ref[b * n_pages + (ng - 1 - j) * SB_PP + r], 0, 0, 0))

    const = lambda shape: pl.BlockSpec(shape, lambda b, j, pt_ref: (0,) * len(shape))
    grid_spec = pltpu.PrefetchScalarGridSpec(
        num_scalar_prefetch=1,
        grid=(nb, ng),
        in_specs=[pl.BlockSpec((None, SB_HEADS, SB_HD, 1), lambda b, j, pt_ref: (b, 0, 0, 0))]
        + [kv_spec(r) for r in range(SB_PP)] + [kv_spec(r) for r in range(SB_PP)]
        + [const((SB_HEADS, 1)), const((1, SB_HD)), const((page, page))],
        out_specs=pl.BlockSpec((None, SB_HEADS, SB_HD), lambda b, j, pt_ref: (b, 0, 0)),
        scratch_shapes=[pltpu.VMEM((SB_HEADS, SB_HD, page), F32), pltpu.VMEM((SB_HEADS, 1), F32),
                        pltpu.VMEM((SB_HEADS, SB_HD, page), F32)],
    )
    out = pl.pallas_call(
        _sb_decode_kernel,
        grid_spec=grid_spec,
        out_shape=jax.ShapeDtypeStruct((nb, SB_HEADS, SB_HD), F32),
        compiler_params=_cp("arbitrary", "arbitrary"),
        name="sb_decode",
    )(pt, q, *([cache_kt] * SB_PP), *([cache_vt] * SB_PP), bias_col, gn, uinc)
    return out.reshape(nb, SB_W)


def _outproj_kernel(x_ref, gate_ref, g_ref, s_ref, y_ref, w_ref, o_ref):
    mix = (_dot(g_ref[...], w_ref[0:GLA_DVW, :]) + _dot(s_ref[...], w_ref[GLA_DVW:GLA_DVW + SB_W, :])
           + _dot(y_ref[...], w_ref[GLA_DVW + SB_W:, :]))
    o_ref[...] = x_ref[...] + gate_ref[...] * mix


def _out_proj(x, gate, g, s, y, w, *, tm, rows_per_mod):
    r, d = x.shape
    mrows = gate.shape[1]
    row = lambda n: pl.BlockSpec((tm, n), lambda i: (i, 0))
    return pl.pallas_call(
        _outproj_kernel,
        grid=(r // tm,),
        in_specs=[row(d), pl.BlockSpec((None, mrows, d), lambda i: (i // rows_per_mod, 0, 0)),
                  row(GLA_DVW), row(SB_W), row(M_DIN), pl.BlockSpec(w.shape, lambda i: (0, 0))],
        out_specs=row(d),
        out_shape=jax.ShapeDtypeStruct((r, d), F32),
        compiler_params=_cp("arbitrary"),
        name="out_proj",
    )(x, gate, g, s, y, w)


def _ffn_up_kernel(x_ref, sc_ref, sh_ref, g_ref, w_ref, u_ref):
    h = _norm_mod(x_ref[...], g_ref[...], sc_ref[...], sh_ref[...])
    u_ref[...] = jnp.dot(h.astype(BF16), w_ref[...], preferred_element_type=F32)


def _ffn_up(x, sc, sh, g, w, *, tm, rows_per_mod):
    r, d = x.shape
    n = w.shape[1]
    mrows = sc.shape[1]
    mod_spec = pl.BlockSpec((None, mrows, d), lambda i: (i // rows_per_mod, 0, 0))
    return pl.pallas_call(
        _ffn_up_kernel,
        grid=(r // tm,),
        in_specs=[pl.BlockSpec((tm, d), lambda i: (i, 0)), mod_spec, mod_spec,
                  pl.BlockSpec((1, d), lambda i: (0, 0)), pl.BlockSpec((d, n), lambda i: (0, 0))],
        out_specs=pl.BlockSpec((tm, n), lambda i: (i, 0)),
        out_shape=jax.ShapeDtypeStruct((r, n), F32),
        compiler_params=_cp("arbitrary"),
        name="ffn_up",
    )(x, sc, sh, g, w)


FFN_CK = 256
FFN_RB = 64


def _ffn_act_down(taps, cw_ref, cb_ref, w_ref):
    dff = w_ref.shape[0]

    def conv(c0):
        u, u1, u2 = taps(c0)
        cs = slice(c0, c0 + FFN_CK)
        return cb_ref[:, cs] + cw_ref[0:1, cs] * u2 + cw_ref[1:2, cs] * u1 + cw_ref[2:3, cs] * u

    out = None
    for c0 in range(0, dff, FFN_CK):
        part = _dot(_silu(conv(c0)) * conv(dff + c0), w_ref[c0:c0 + FFN_CK, :])
        out = part if out is None else out + part
    return out


def _ffn_up_act_kernel(x_ref, sc_ref, sh_ref, g_ref, w_ref, cw_ref, cb_ref, a_ref, tail_ref, carry_ref,
                       u_ref, *, tiles_per_seq):
    i = pl.program_id(0)
    tm = x_ref.shape[0]
    dff = a_ref.shape[1]
    first = i % tiles_per_seq == 0
    h = _norm_mod(x_ref[...], g_ref[...], sc_ref[...], sh_ref[...]).astype(BF16)
    row8 = lax.broadcasted_iota(jnp.int32, (SUBLANE, FFN_CK), 0)

    def up(base, slot, half):
        cs = slice(base, base + FFN_CK)
        u = jnp.dot(h, w_ref[:, cs], preferred_element_type=F32)
        u_ref[slot, half, 0:SUBLANE, :] = jnp.where(first, 0.0, carry_ref[:, cs])
        u_ref[slot, half, SUBLANE:, :] = u
        last = u[tm - SUBLANE:, :]
        carry_ref[:, cs] = last
        tail_ref[:, cs] = last

    def conv(slot, half, base, rb):
        cs = slice(base, base + FFN_CK)
        acc = cb_ref[:, cs]
        for s in range(FFN_CONV):
            r0 = SUBLANE + rb - s
            acc = acc + cw_ref[FFN_CONV - 1 - s:FFN_CONV - s, cs] * u_ref[slot, half, r0:r0 + FFN_RB, :]
        return acc

    for n, c0 in enumerate(range(0, dff, FFN_CK)):
        slot = n % 2
        up(c0, slot, 0)
        up(dff + c0, slot, 1)
        for rb in range(0, tm, FFN_RB):
            a_ref[rb:rb + FFN_RB, c0:c0 + FFN_CK] = (
                _silu(conv(slot, 0, c0, rb)) * conv(slot, 1, dff + c0, rb)).astype(BF16)


def _ffn_up_act(x, sc, sh, g, w, cw, cb, *, tm, seq):
    r, d = x.shape
    n = w.shape[1]
    dff = n // 2
    tiles = seq // tm
    mod_spec = pl.BlockSpec((None, 1, d), lambda i: (i // tiles, 0, 0))
    const = lambda shape: pl.BlockSpec(shape, lambda i: (0,) * len(shape))
    return pl.pallas_call(
        functools.partial(_ffn_up_act_kernel, tiles_per_seq=tiles),
        grid=(r // tm,),
        in_specs=[pl.BlockSpec((tm, d), lambda i: (i, 0)), mod_spec, mod_spec, const((1, d)), const((d, n)),
                  const((FFN_CONV, n)), const((1, n))],
        out_specs=[pl.BlockSpec((tm, dff), lambda i: (i, 0)),
                   pl.BlockSpec((None, SUBLANE, n), lambda i: (i // tiles, 0, 0))],
        out_shape=[jax.ShapeDtypeStruct((r, dff), BF16), jax.ShapeDtypeStruct((r // seq, SUBLANE, n), F32)],
        scratch_shapes=[pltpu.VMEM((SUBLANE, n), F32), pltpu.VMEM((2, 2, SUBLANE + tm, FFN_CK), F32)],
        compiler_params=_cp("arbitrary"),
        name="ffn_up_act",
    )(x, sc, sh, g, w, cw, cb)


def _ffn_down_kernel(x_ref, gate_ref, a_ref, w_ref, o_ref):
    o_ref[...] = x_ref[...] + gate_ref[...] * jnp.dot(a_ref[...], w_ref[...], preferred_element_type=F32)


def _ffn_down(x, gate, a, w, *, tm, seq):
    r, d = x.shape
    tiles = seq // tm
    return pl.pallas_call(
        _ffn_down_kernel,
        grid=(r // tm,),
        in_specs=[pl.BlockSpec((tm, d), lambda i: (i, 0)),
                  pl.BlockSpec((None, 1, d), lambda i: (i // tiles, 0, 0)),
                  pl.BlockSpec((tm, a.shape[1]), lambda i: (i, 0)),
                  pl.BlockSpec(w.shape, lambda i: (0, 0))],
        out_specs=pl.BlockSpec((tm, d), lambda i: (i, 0)),
        out_shape=jax.ShapeDtypeStruct((r, d), F32),
        compiler_params=_cp("arbitrary"),
        name="ffn_down",
    )(x, gate, a, w)


def _ffn_down_step_kernel(x_ref, gate_ref, u_ref, buf_ref, cw_ref, cb_ref, w_ref, o_ref):
    taps = lambda c0: (u_ref[:, c0:c0 + FFN_CK], buf_ref[1, :, c0:c0 + FFN_CK], buf_ref[0, :, c0:c0 + FFN_CK])
    o_ref[...] = x_ref[...] + gate_ref[...] * _ffn_act_down(taps, cw_ref, cb_ref, w_ref)


def _ffn_down_step(x, gate, u, buf_t, cw, cb, w):
    full = lambda a: pl.BlockSpec(a.shape, lambda i: (0,) * a.ndim)
    args = (x, gate, u, buf_t, cw, cb, w)
    return pl.pallas_call(
        _ffn_down_step_kernel,
        grid=(1,),
        in_specs=[full(a) for a in args],
        out_specs=full(x),
        out_shape=jax.ShapeDtypeStruct(x.shape, F32),
        compiler_params=_cp("arbitrary"),
        name="ffn_down_step",
    )(*args)


def _same_segment(n, width):
    i = np.arange(n)
    return jnp.asarray((i[:, None] // width) == (i[None, :] // width), BF16)


def _constants():
    k = np.arange(GLA_DQK)
    v = np.arange(GLA_DVW)
    eexp = (k[:, None] // GLA_DK) == (v[None, :] // GLA_DV)
    j = np.arange(SB_TK)
    later = j[:, None] >= j[None, :]
    uo = np.block([[later, np.zeros_like(later)], [np.zeros_like(later), later]])
    lane = np.arange(LANE)
    hm_pair = np.zeros((SUBLANE, LANE), np.float32)
    hm_pair[0] = lane < SB_HD
    hm_pair[1] = lane >= SB_HD
    p = np.arange(M_DIN)
    e4 = lane[:, None] == (p[None, :] // M_HD)
    return dict(
        eexp=jnp.asarray(eexp, BF16), bd=jnp.asarray(eexp.T, F32),
        uo=jnp.asarray(uo, BF16), hm_pair=jnp.asarray(hm_pair, BF16), eseg64=_same_segment(LANE, 64),
        e4=jnp.asarray(e4, BF16), e4t=jnp.asarray(e4.T, BF16))


def _pad_lanes(v, n=LANE):
    return jnp.zeros((1, n), F32).at[0, :v.shape[0]].set(v)


def _pad_col(v, n=SUBLANE):
    return jnp.zeros((n, 1), F32).at[:v.shape[0], 0].set(v)


def kernel(x_prompt, x_sample, cache_sb_k, cache_sb_v, state_gla, state_mamba_conv, state_mamba_ssm, state_ffn_conv, page_table, c_prompt, c_sample, norm1_g, w_ada, b_ada, w_in, gla_w_gate2, gla_b_gate, gla_norm_g, sb_q_norm_g, sb_k_norm_g, sb_o_norm_g, sb_bias, m_conv_w, m_conv_b, m_dt_bias, m_a_log, m_d, m_norm_g, w_out, norm2_g, ffn_w_up, ffn_conv_w, ffn_conv_b, ffn_w_down):
    bsz, seq, d = x_prompt.shape
    nb = x_sample.shape[0]
    depth = w_in.shape[0]
    dff = ffn_w_down.shape[1]
    n_pool, page = cache_sb_k.shape[1], cache_sb_k.shape[2]
    assert x_sample.shape[1] == 1 and d % LANE == 0 and seq % 256 == 0
    cst = _constants()

    tm = 256
    tm_mm = 512
    tq = 512
    tt_gla = 256
    tt_ssd = 2 * SSD_C
    assert SB_UNROLL == 2 * (tq // SB_TK) or (tq // SB_TK) % SB_UNROLL == 0
    assert page_table.shape[1] % SB_PP == 0
    assert seq % tq == 0 and seq % tm == 0 and seq % tm_mm == 0 and seq % tt_gla == 0 and seq % tt_ssd == 0

    mod = _modulation(jnp.concatenate([c_prompt, c_sample], axis=0), w_ada, b_ada)
    mod = mod.reshape(depth, bsz + nb, N_MOD, d)
    mod_p = mod[:, :bsz].transpose(0, 2, 1, 3).reshape(depth, N_MOD, bsz, 1, d)
    mod_s = mod[:, bsz:].transpose(0, 2, 1, 3).reshape(depth, N_MOD, 1, nb, d)

    cache_kt = cache_sb_k.transpose(0, 1, 3, 4, 2)
    cache_vt = cache_sb_v.transpose(0, 1, 3, 4, 2)

    xp = x_prompt.reshape(bsz * seq, d)
    xs = x_sample.reshape(nb, d)
    w_in_t = w_in.transpose(2, 0, 1)
    outs = {name: [] for name in ("pk", "pv", "pg", "pc", "ph", "pf", "sk", "sv", "sg", "sc", "sh", "sf")}
    for l in range(depth):
        wi = w_in_t[:, l, :]
        o = np.cumsum([0, GLA_DQK, GLA_DQK, GLA_DVW, GLA_DVW, GLA_LR, SB_W, SB_W, SB_W, M_DIN, M_CONV_DIM, M_HEADS])
        sl = lambda a, b: wi[o[a]:o[b]]
        w_re = jnp.concatenate(
            [sl(0, 4), sl(5, 8), sl(8, 10), sl(10, 11), sl(4, 5),
             jnp.zeros((LANE - M_HEADS - GLA_LR, d), F32)], axis=0).astype(BF16)
        wg = jnp.zeros((LANE, LANE), F32).at[M_HEADS:M_HEADS + GLA_LR].set(gla_w_gate2[l]).astype(BF16)
        bg = gla_b_gate[l].reshape(1, GLA_DQK)
        gq = jnp.tile(sb_q_norm_g[l], SB_HEADS).reshape(1, SB_W)
        gk = jnp.tile(sb_k_norm_g[l], SB_HEADS).reshape(1, SB_W)
        go = jnp.tile(sb_o_norm_g[l], SB_HEADS).reshape(1, SB_W)
        gng = jnp.tile(gla_norm_g[l], GLA_HEADS).reshape(1, GLA_DVW)
        g1 = norm1_g[l].reshape(1, d)
        g2 = norm2_g[l].reshape(1, d)
        bias2 = sb_bias[l] * LOG2E
        b_hi = bias2.astype(BF16)
        b_mid = (bias2 - b_hi.astype(F32)).astype(BF16)
        b_lo = (bias2 - b_hi.astype(F32) - b_mid.astype(F32)).astype(BF16)
        pieces = jnp.stack([b_hi, b_mid, b_lo], axis=1)
        bias_cols = jnp.pad(jnp.repeat(pieces, SB_TK, axis=0).reshape(SB_HEADS // 2, 2 * SB_TK, SB_NB),
                            ((0, 0), (0, 0), (0, LANE - SB_NB)))
        bias_col = bias2.reshape(SB_HEADS, 1)
        cw = m_conv_w[l]
        cb = m_conv_b[l].reshape(1, M_CONV_DIM)
        nega = -jnp.exp(m_a_log[l])
        dtb_l, nega_l = _pad_lanes(m_dt_bias[l]), _pad_lanes(nega)
        dtb_c, nega_c = _pad_col(m_dt_bias[l]), _pad_col(nega)
        md = jnp.repeat(m_d[l], M_HD).reshape(1, M_DIN)
        gnm = m_norm_g[l].reshape(1, M_DIN)
        wo = w_out[l].astype(BF16)
        wu = ffn_w_up[l].astype(BF16)
        wd = ffn_w_down[l].astype(BF16)
        fcw = ffn_conv_w[l]
        fcb = ffn_conv_b[l].reshape(1, 2 * dff)
        sh1, sc1, gt1, sh2, sc2, gt2 = range(N_MOD)

        mp = mod_p[l]
        gla, la, q, k, kb, v, vb, m, misc = _in_proj(
            xp, mp[sc1], mp[sh1], g1, w_re, wg, bg, gq, gk, cst["eseg64"], tm=tm_mm, rows_per_mod=seq // tm_mm,
            kv_transposed=True)
        g_mix, g_state = _gla_prompt(gla, la, cst["eexp"], cst["bd"], cst["eseg64"], gng,
                                     bsz=bsz, seq=seq, tt=tt_gla)
        s_mix = _sb_prompt(q, kb, vb, bias_cols, cst["uo"], cst["hm_pair"], cst["eseg64"], go[:, :LANE],
                           bsz=bsz, seq=seq, tq=tq)
        dtt = jnp.pad(misc[:, :M_HEADS].reshape(bsz, seq, M_HEADS).transpose(0, 2, 1),
                      ((0, 0), (0, SUBLANE - M_HEADS), (0, 0)))
        y_mix, h_state = _ssd_prompt(m, misc, dtt, cw, cb, dtb_l, nega_l, dtb_c, nega_c, cst["e4"], cst["e4t"],
                                     md, gnm, bsz=bsz, seq=seq, tt=tt_ssd)
        xp = _out_proj(xp, mp[gt1], g_mix, s_mix, y_mix, wo, tm=tm_mm, rows_per_mod=seq // tm_mm)
        act, u_tail = _ffn_up_act(xp, mp[sc2], mp[sh2], g2, wu, fcw, fcb, tm=tm, seq=seq)
        xp = _ffn_down(xp, mp[gt2], act, wd, tm=tm_mm, seq=seq)

        outs["pk"].append(k.reshape(bsz, SB_HEADS, SB_HD, seq).transpose(0, 3, 1, 2))
        outs["pv"].append(v.reshape(bsz, SB_HEADS, SB_HD, seq).transpose(0, 3, 1, 2))
        gs = g_state.reshape(bsz, GLA_HEADS, GLA_DV, GLA_HEADS, GLA_DK)
        outs["pg"].append(jnp.stack([gs[:, h, :, h, :] for h in range(GLA_HEADS)], axis=1).transpose(0, 1, 3, 2))
        outs["pc"].append(m.reshape(bsz, seq, 1024)[:, seq - (M_CONV - 1):, M_DIN:])
        outs["ph"].append(h_state.reshape(bsz, M_HEADS, M_HD, M_N))
        outs["pf"].append(u_tail[:, SUBLANE - (FFN_CONV - 1):])

        ms = mod_s[l]
        gla, la, q, k, kb, v, vb, m, misc = _in_proj(
            xs, ms[sc1], ms[sh1], g1, w_re, wg, bg, gq, gk, cst["eseg64"], tm=nb, rows_per_mod=1,
            kv_transposed=False)
        g4, g_state, y_mix, h_state = _step_mixers(
            gla, la, gla[:, 2 * GLA_DQK:2 * GLA_DQK + GLA_DVW].reshape(nb, GLA_HEADS, GLA_DV),
            gla[:, 2 * GLA_DQK + GLA_DVW:].reshape(nb, GLA_HEADS, GLA_DV),
            state_gla[l].reshape(nb, GLA_DQK, GLA_DV), m, state_mamba_conv[l], misc,
            state_mamba_ssm[l].reshape(nb, M_DIN, M_N), cw, cb, dtb_l, nega_l, cst["e4"], md,
            gla_norm_g[l].reshape(1, GLA_DV), gnm)
        s_mix = _sb_decode(page_table, q.astype(F32).reshape(nb, SB_HEADS, SB_HD, 1), cache_kt, cache_vt, l,
                           bias_col, sb_o_norm_g[l].reshape(1, SB_HD),
                           jnp.asarray(np.tril(np.ones((page, page), np.float32)), BF16))
        xs = _out_proj(xs, ms[gt1], g4.reshape(nb, GLA_DVW), s_mix, y_mix.reshape(nb, M_DIN), wo,
                       tm=nb, rows_per_mod=1)
        u = _ffn_up(xs, ms[sc2], ms[sh2], g2, wu, tm=nb, rows_per_mod=1)
        xs = _ffn_down_step(xs, ms[gt2][0], u, state_ffn_conv[l].transpose(1, 0, 2), fcw, fcb, wd)

        outs["sk"].append(k.reshape(nb, 1, SB_HEADS, SB_HD))
        outs["sv"].append(v.reshape(nb, 1, SB_HEADS, SB_HD))
        outs["sg"].append(g_state.reshape(nb, GLA_HEADS, GLA_DK, GLA_DV))
        outs["sc"].append(jnp.concatenate([state_mamba_conv[l][:, 1:], m[:, None, M_DIN:]], axis=1))
        outs["sh"].append(h_state.reshape(nb, M_HEADS, M_HD, M_N))
        outs["sf"].append(jnp.concatenate([state_ffn_conv[l][:, 1:], u[:, None, :]], axis=1))

    st = {name: jnp.stack(v) for name, v in outs.items()}
    return (xp.reshape(bsz, seq, d), xs.reshape(nb, 1, d), st["pk"], st["pv"], st["pg"], st["pc"], st["ph"],
            st["pf"], st["sk"], st["sv"], st["sg"], st["sc"], st["sh"], st["sf"])
```

```python
import functools

import jax
import jax.numpy as jnp
import numpy as np
from jax import lax
from jax.experimental import pallas as pl
from jax.experimental.pallas import tpu as pltpu

F32 = jnp.float32
BF16 = jnp.bfloat16

GLA_HEADS, GLA_DK, GLA_DV, GLA_LR, GLA_TAU = 4, 32, 64, 16, 16.0
GLA_DQK = GLA_HEADS * GLA_DK
GLA_DVW = GLA_HEADS * GLA_DV
SB_HEADS, SB_HD = 8, 64
SB_W = SB_HEADS * SB_HD
M_HEADS, M_HD, M_GROUPS, M_N, M_CONV = 4, 64, 2, 128, 4
M_DIN = M_HEADS * M_HD
M_CONV_DIM = M_DIN + 2 * M_GROUPS * M_N
FFN_CONV = 3
N_MOD = 6
EPS = 1e-6
NEG_BIG = -1e30
LOG2E = 1.4426950408889634

C_GLA = 0
C_SBQ = 768
C_SBK = 1280
C_SBV = 1792
C_MZ = 2304
C_MISC = 3328
D_INP = 3456
LANE = 128
SUBLANE = 8

VMEM_LIMIT = 56 * 1024 * 1024


def _cp(*sem, flags=None):
    return pltpu.CompilerParams(dimension_semantics=sem, vmem_limit_bytes=VMEM_LIMIT, flags=flags)


def _dot(a, b):
    return jnp.dot(a.astype(BF16), b.astype(BF16), preferred_element_type=F32)


def _dot_nt(a, b):
    return lax.dot_general(a.astype(BF16), b.astype(BF16), (((1,), (1,)), ((), ())),
                           preferred_element_type=F32)


def _dot_tn(a, b):
    return lax.dot_general(a.astype(BF16), b.astype(BF16), (((0,), (0,)), ((), ())),
                           preferred_element_type=F32)


def _split3(x):
    hi = x.astype(BF16)
    r = x - hi.astype(F32)
    mid = r.astype(BF16)
    lo = (r - mid.astype(F32)).astype(BF16)
    return hi, mid, lo


def _dot_sel(x, e):
    hi, mid, lo = _split3(x)
    d = lambda p: jnp.dot(p, e, preferred_element_type=F32)
    return d(hi) + d(mid) + d(lo)


def _dot_sel_nt(e, x):
    hi, mid, lo = _split3(x)
    d = lambda p: lax.dot_general(e, p, (((1,), (1,)), ((), ())), preferred_element_type=F32)
    return d(hi) + d(mid) + d(lo)


def _sigmoid(x):
    return 1.0 / (1.0 + jnp.exp(-x))


def _silu(x):
    return x * _sigmoid(x)


def _softplus(x):
    return jnp.maximum(x, 0.0) + jnp.log1p(jnp.exp(-jnp.abs(x)))


def _log_sigmoid(x):
    return jnp.minimum(x, 0.0) - jnp.log1p(jnp.exp(-jnp.abs(x)))


def _rms_rows(x):
    return x * lax.rsqrt(jnp.mean(x * x, axis=-1, keepdims=True) + EPS)


def _seg_rms(x, eseg, width):
    xx = x * x
    hi = xx.astype(BF16)
    lo = (xx - hi.astype(F32)).astype(BF16)
    blocks = []
    for c in range(0, x.shape[1], LANE):
        d = lambda p: jnp.dot(p[:, c:c + LANE], eseg, preferred_element_type=F32)
        blocks.append(d(hi) + d(lo))
    ms = jnp.concatenate(blocks, axis=1) * (1.0 / width)
    return x * lax.rsqrt(ms + EPS)


def _row_to_col(row):
    n = row.shape[1]
    eye = lax.broadcasted_iota(jnp.int32, (n, n), 0) == lax.broadcasted_iota(jnp.int32, (n, n), 1)
    return jnp.sum(jnp.where(eye, row, 0.0), axis=1, keepdims=True)


def _col_to_row(col):
    n = col.shape[0]
    eye = lax.broadcasted_iota(jnp.int32, (n, n), 0) == lax.broadcasted_iota(jnp.int32, (n, n), 1)
    return jnp.sum(jnp.where(eye, col, 0.0), axis=0, keepdims=True)


def _cumsum(x, axis, seg=None):
    n = x.shape[axis] if seg is None else seg
    idx = lax.broadcasted_iota(jnp.int32, x.shape, axis)
    if seg is not None:
        idx = idx % seg
    s = 1
    while s < n:
        x = x + jnp.where(idx >= s, pltpu.roll(x, s, axis), 0.0)
        s *= 2
    return x


def _mod_kernel(c_ref, w_ref, b_ref, o_ref):
    o_ref[...] = _dot(_silu(c_ref[...]), w_ref[...]) + b_ref[...]


def _modulation(c_all, w_ada, b_ada):
    depth, d, nd = w_ada.shape
    n = c_all.shape[0]
    tn = 1024
    return pl.pallas_call(
        _mod_kernel,
        grid=(depth, nd // tn),
        in_specs=[pl.BlockSpec((n, d), lambda l, j: (0, 0)),
                  pl.BlockSpec((None, d, tn), lambda l, j: (l, 0, j)),
                  pl.BlockSpec((None, 1, tn), lambda l, j: (l, 0, j))],
        out_specs=pl.BlockSpec((None, n, tn), lambda l, j: (l, 0, j)),
        out_shape=jax.ShapeDtypeStruct((depth, n, nd), F32),
        compiler_params=_cp("arbitrary", "arbitrary"),
        name="adaln_mod",
    )(c_all, w_ada, b_ada.reshape(depth, 1, nd))


def _norm_mod(x, g, sc, sh):
    return _rms_rows(x) * g * (1.0 + sc) + sh


def _inproj_kernel(x_ref, sc_ref, sh_ref, g_ref, w_ref, wg_ref, bg_ref, gq_ref, gk_ref, eseg_ref,
                   gla_ref, la_ref, q_ref, k_ref, kb_ref, v_ref, vb_ref, m_ref, misc_ref, *, kv_transposed):
    h = _norm_mod(x_ref[...], g_ref[...], sc_ref[...], sh_ref[...])
    p = lax.dot_general(h.astype(BF16), w_ref[...], (((1,), (1,)), ((), ())), preferred_element_type=F32)
    gla_ref[...] = p[:, C_GLA:C_SBQ]
    misc = p[:, C_MISC:D_INP]
    misc_ref[...] = misc
    la_ref[...] = _log_sigmoid(_dot(misc, wg_ref[...]) + bg_ref[...]) * (1.0 / GLA_TAU)
    eseg = eseg_ref[...]
    q = _seg_rms(p[:, C_SBQ:C_SBK], eseg, SB_HD) * gq_ref[...]
    q_ref[...] = (q * (SB_HD ** -0.5 * LOG2E)).astype(BF16)
    k = _seg_rms(p[:, C_SBK:C_SBV], eseg, SB_HD) * gk_ref[...]
    k_ref[...] = k.T if kv_transposed else k
    kb_ref[...] = k.astype(BF16)
    v = p[:, C_SBV:C_MZ]
    v_ref[...] = v.T if kv_transposed else v
    vb_ref[...] = v.astype(BF16)
    m_ref[...] = p[:, C_MZ:C_MISC]


def _in_proj(x, sc, sh, g, w, wg, bg, gq, gk, eseg, *, tm, rows_per_mod, kv_transposed):
    r, d = x.shape
    mrows = sc.shape[1]
    mod_spec = pl.BlockSpec((None, mrows, d), lambda i: (i // rows_per_mod, 0, 0))
    const = lambda shape: pl.BlockSpec(shape, lambda i: (0,) * len(shape))
    row = lambda n: pl.BlockSpec((tm, n), lambda i: (i, 0))
    outs = [(768, F32), (LANE, F32), (SB_W, BF16), (SB_W, F32), (SB_W, BF16), (SB_W, F32), (SB_W, BF16),
            (1024, F32), (LANE, F32)]
    out_specs = [row(n) for n, _ in outs]
    out_shape = [jax.ShapeDtypeStruct((r, n), dt) for n, dt in outs]
    if kv_transposed:
        seq = rows_per_mod * tm
        for idx in (3, 5):
            out_specs[idx] = pl.BlockSpec((None, SB_W, tm), lambda i: (i // rows_per_mod, 0, i % rows_per_mod))
            out_shape[idx] = jax.ShapeDtypeStruct((r // seq, SB_W, seq), F32)
    return pl.pallas_call(
        functools.partial(_inproj_kernel, kv_transposed=kv_transposed),
        grid=(r // tm,),
        in_specs=[row(d), mod_spec, mod_spec, const((1, d)), const((D_INP, d)), const((LANE, LANE)),
                  const((1, LANE)), const((1, SB_W)), const((1, SB_W)), const((LANE, LANE))],
        out_specs=out_specs,
        out_shape=out_shape,
        compiler_params=_cp("arbitrary"),
        name="in_proj",
    )(x, sc, sh, g, w, wg, bg, gq, gk, eseg)


GLA_C = SUBLANE
GLA_UNROLL = 16


def _gla_prompt_kernel(gla_ref, la_ref, eexp_ref, bd_ref, eseg_ref, gn_ref, o_ref, st_out_ref,
                       st_ref, b_ref, acc_ref):
    i = pl.program_id(1)
    tt = la_ref.shape[0]

    @pl.when(i == 0)
    def _():
        st_ref[...] = jnp.zeros_like(st_ref)

    b_ref[...] = _cumsum(la_ref[...], 0, seg=GLA_C)
    eexp = eexp_ref[...]
    bd = bd_ref[...]
    rowid = lax.broadcasted_iota(jnp.int32, (GLA_C, GLA_DQK), 0)
    rowid_v = lax.broadcasted_iota(jnp.int32, (GLA_C, GLA_DVW), 0)

    def group(gi, carry):
        steps = []
        for u in range(GLA_UNROLL):
            r0 = pl.multiple_of((gi * GLA_UNROLL + u) * GLA_C, GLA_C)
            bc = b_ref[pl.ds(r0, GLA_C), :]
            qc = gla_ref[pl.ds(r0, GLA_C), 0:GLA_DQK] * (GLA_DK ** -0.5)
            kc = gla_ref[pl.ds(r0, GLA_C), GLA_DQK:2 * GLA_DQK]
            vc = gla_ref[pl.ds(r0, GLA_C), 2 * GLA_DQK:2 * GLA_DQK + GLA_DVW]
            steps.append((r0, bc, qc, kc, vc, bc[GLA_C - 1:GLA_C, :]))
        ps = []
        for r0, bc, qc, kc, vc, bl in steps:
            for t in range(GLA_C):
                d = jnp.where(rowid <= t, bc[t:t + 1, :] - bc, NEG_BIG)
                ps.append(jnp.exp(d) * (qc[t:t + 1, :] * kc))
        sc = _dot(jnp.concatenate(ps, axis=0), eexp)
        upds = [_dot_tn(vc, kc * jnp.exp(bl - bc)) * bd for r0, bc, qc, kc, vc, bl in steps]
        st = st_ref[...]
        for u, (r0, bc, qc, kc, vc, bl) in enumerate(steps):
            o = _dot_nt(qc * jnp.exp(bc), st)
            st = st * jnp.exp(bl) + upds[u]
            for t in range(GLA_C):
                row0 = (u * GLA_C + t) * GLA_C
                ot = jnp.sum(sc[row0:row0 + GLA_C, :] * vc, axis=0, keepdims=True)
                o = o + jnp.where(rowid_v == t, ot, 0.0)
            acc_ref[pl.ds(r0, GLA_C), :] = o
        st_ref[...] = st
        return carry

    lax.fori_loop(0, tt // (GLA_C * GLA_UNROLL), group, 0)
    gg = gla_ref[:, 2 * GLA_DQK + GLA_DVW:2 * GLA_DQK + 2 * GLA_DVW]
    o = _seg_rms(acc_ref[...], eseg_ref[...], GLA_DV) * gn_ref[...] * _silu(gg)
    o_ref[...] = o.astype(BF16)
    st_out_ref[...] = st_ref[...]


def _gla_prompt(gla, la, eexp, bd, eseg, gn, *, bsz, seq, tt):
    nt = seq // tt
    const = lambda shape: pl.BlockSpec(shape, lambda b, i: (0,) * len(shape))
    return pl.pallas_call(
        _gla_prompt_kernel,
        grid=(bsz, nt),
        in_specs=[pl.BlockSpec((tt, 768), lambda b, i: (b * nt + i, 0)),
                  pl.BlockSpec((tt, LANE), lambda b, i: (b * nt + i, 0)),
                  const((GLA_DQK, GLA_DVW)), const((GLA_DVW, GLA_DQK)), const((LANE, LANE)),
                  const((1, GLA_DVW))],
        out_specs=[pl.BlockSpec((tt, GLA_DVW), lambda b, i: (b * nt + i, 0)),
                   pl.BlockSpec((None, GLA_DVW, GLA_DQK), lambda b, i: (b, 0, 0))],
        out_shape=[jax.ShapeDtypeStruct((bsz * seq, GLA_DVW), BF16),
                   jax.ShapeDtypeStruct((bsz, GLA_DVW, GLA_DQK), F32)],
        scratch_shapes=[pltpu.VMEM((GLA_DVW, GLA_DQK), F32), pltpu.VMEM((tt, GLA_DQK), F32),
                        pltpu.VMEM((tt, GLA_DVW), F32)],
        compiler_params=_cp("arbitrary", "arbitrary"),
        name="gla_prompt",
    )(gla, la, eexp, bd, eseg, gn)


SB_TK = LANE
SB_NB = 3
SB_UNROLL = 8


def _neg_abs(x):
    return lax.bitcast_convert_type(lax.bitcast_convert_type(x, jnp.int32) | jnp.int32(-2 ** 31), F32)


def _sb_prompt_kernel(q_ref, k_ref, v_ref, bcol_ref, uo_ref, hm_ref, eseg_ref, gn_ref, o_ref,
                      acc_ref, car_ref, kk_ref, vv_ref, qx_ref):
    i = pl.program_id(2)
    tq = q_ref.shape[0]
    ndiag = tq // SB_TK
    nblk = kk_ref.shape[0]

    @pl.when(i == 0)
    def _():
        m0 = hm_ref[0:1, :]
        m1 = hm_ref[1:2, :]
        bcol = bcol_ref[...]

        def fill(j, carry):
            k0 = pl.multiple_of(j * SB_TK, SB_TK)
            kb = k_ref[pl.ds(k0, SB_TK), :]
            vb = v_ref[pl.ds(k0, SB_TK), :]
            kk_ref[j] = jnp.concatenate([jnp.concatenate([kb * m0, kb * m1], axis=0), bcol], axis=1)
            vv_ref[j] = jnp.concatenate([vb * m0, vb * m1], axis=0)
            return carry

        lax.fori_loop(0, nblk, fill, 0)

    ones = jnp.where(lax.broadcasted_iota(jnp.int32, (tq, LANE), 1) < SB_NB, 1.0, 0.0).astype(BF16)
    qx_ref[...] = jnp.concatenate([q_ref[...], ones], axis=1)
    uo = uo_ref[...]
    acc_ref[...] = jnp.zeros_like(acc_ref)
    car_ref[...] = jnp.zeros_like(car_ref)

    def block(j, r0, diag):
        rows = tq - r0
        z = lax.dot_general(qx_ref[r0:, :], kk_ref[j], (((1,), (1,)), ((), ())),
                            preferred_element_type=F32)
        sp = jnp.maximum(z, 0.0) + jnp.log(1.0 + jnp.exp2(_neg_abs(z))) * LOG2E
        if diag:
            valid = (lax.broadcasted_iota(jnp.int32, (SB_TK, 2 * SB_TK), 1) % SB_TK
                     < lax.broadcasted_iota(jnp.int32, (SB_TK, 2 * SB_TK), 0))
            def mask_head(t):
                head = jnp.where(valid, t[:SB_TK], 0.0)
                return head if rows == SB_TK else jnp.concatenate([head, t[SB_TK:]], axis=0)

            sp = mask_head(sp)
        incl = jnp.dot(sp.astype(BF16), uo, preferred_element_type=F32) + car_ref[r0:, :]
        w = jnp.exp2(z - incl)
        if diag:
            w = mask_head(w)
        car_ref[r0:, :SB_TK] += jnp.sum(sp[:, :SB_TK], axis=1, keepdims=True)
        car_ref[r0:, SB_TK:] += jnp.sum(sp[:, SB_TK:], axis=1, keepdims=True)
        acc_ref[r0:, :] += jnp.dot(w.astype(BF16), vv_ref[j], preferred_element_type=F32)

    for c in range(ndiag - 1, -1, -1):
        block(i * ndiag + c, c * SB_TK, True)

    n_off = i * ndiag
    n_main = n_off // SB_UNROLL

    def body(jj, carry):
        for u in range(SB_UNROLL):
            block(n_off - 1 - jj * SB_UNROLL - u, 0, False)
        return carry

    lax.fori_loop(0, n_main, body, 0)
    if SB_UNROLL > ndiag:
        @pl.when(n_off - n_main * SB_UNROLL > 0)
        def _():
            for u in range(ndiag):
                block(ndiag - 1 - u, 0, False)

    o_ref[...] = (_seg_rms(acc_ref[...], eseg_ref[...], SB_HD) * gn_ref[...]).astype(BF16)


def _sb_prompt(q, k, v, bias2, uo, hm, eseg, gn, *, bsz, seq, tq):
    nq = seq // tq
    npair = SB_HEADS // 2
    const = lambda shape: pl.BlockSpec(shape, lambda b, p, i: (0,) * len(shape))
    return pl.pallas_call(
        _sb_prompt_kernel,
        grid=(bsz, npair, nq),
        in_specs=[pl.BlockSpec((tq, LANE), lambda b, p, i: (b * nq + i, p)),
                  pl.BlockSpec((seq, LANE), lambda b, p, i: (b, p)),
                  pl.BlockSpec((seq, LANE), lambda b, p, i: (b, p)),
                  pl.BlockSpec((None, 2 * SB_TK, LANE), lambda b, p, i: (p, 0, 0)),
                  const((2 * SB_TK, 2 * SB_TK)), const((SUBLANE, LANE)), const((LANE, LANE)), const((1, LANE))],
        out_specs=pl.BlockSpec((tq, LANE), lambda b, p, i: (b * nq + i, p)),
        out_shape=jax.ShapeDtypeStruct((bsz * seq, SB_W), BF16),
        scratch_shapes=[pltpu.VMEM((tq, LANE), F32), pltpu.VMEM((tq, 2 * SB_TK), F32),
                        pltpu.VMEM((seq // SB_TK, 2 * SB_TK, 2 * LANE), BF16),
                        pltpu.VMEM((seq // SB_TK, 2 * SB_TK, LANE), BF16),
                        pltpu.VMEM((tq, 2 * LANE), BF16)],
        compiler_params=_cp("arbitrary", "arbitrary", "arbitrary"),
        name="sb_prompt",
    )(q, k, v, bias2, uo, hm, eseg, gn)


def _ssm_params(dt_raw, dtb, nega):
    dt = _softplus(dt_raw + dtb)
    return dt, dt * nega


SSD_C = 128


def _ssd_prompt_kernel(m_ref, prev_ref, misc_ref, dtt_ref, cw_ref, cb_ref, dtb_l_ref, nega_l_ref,
                       dtb_c_ref, nega_c_ref, e4_ref, e4t_ref, md_ref, gn_ref, o_ref, hs_out_ref,
                       hs_ref):
    i = pl.program_id(1)
    tt = m_ref.shape[0]
    cl = SSD_C

    @pl.when(i == 0)
    def _():
        hs_ref[...] = jnp.zeros_like(hs_ref)

    e4 = e4_ref[...]
    row = lax.broadcasted_iota(jnp.int32, (cl, M_CONV_DIM), 0)
    row8 = lax.broadcasted_iota(jnp.int32, (SUBLANE, M_CONV_DIM), 0)
    causal = lax.broadcasted_iota(jnp.int32, (cl, cl), 0) >= lax.broadcasted_iota(jnp.int32, (cl, cl), 1)
    lane_head = lax.broadcasted_iota(jnp.int32, (cl, M_DIN), 1) // M_HD
    rep = M_HEADS // M_GROUPS
    half = M_DIN // M_GROUPS

    for r0 in range(0, tt, cl):
        z = m_ref[r0:r0 + cl, 0:M_DIN]
        xbc = m_ref[r0:r0 + cl, M_DIN:M_DIN + M_CONV_DIM]
        if r0 == 0:
            prev = jnp.where(i > 0, prev_ref[:, M_DIN:M_DIN + M_CONV_DIM], 0.0)
        else:
            prev = m_ref[r0 - SUBLANE:r0, M_DIN:M_DIN + M_CONV_DIM]
        acc = cb_ref[...] + cw_ref[M_CONV - 1:M_CONV, :] * xbc
        for s in range(1, M_CONV):
            head = jnp.where(row8 < s, pltpu.roll(prev, s, 0), 0.0)
            head = jnp.concatenate([head, jnp.zeros((cl - SUBLANE, M_CONV_DIM), F32)], axis=0)
            shifted = jnp.where(row < s, head, pltpu.roll(xbc, s, 0))
            acc = acc + cw_ref[M_CONV - 1 - s:M_CONV - s, :] * shifted
        xc = _silu(acc)
        x = xc[:, 0:M_DIN]
        bm = xc[:, M_DIN:M_DIN + M_GROUPS * M_N]
        cm = xc[:, M_DIN + M_GROUPS * M_N:]

        dt_c, a_c = _ssm_params(misc_ref[r0:r0 + cl, :], dtb_l_ref[...], nega_l_ref[...])
        cs_c = _cumsum(a_c, 0)
        dt_r, a_r = _ssm_params(dtt_ref[:, r0:r0 + cl], dtb_c_ref[...], nega_c_ref[...])
        cs_r = _cumsum(a_r, 1)
        xdt = x * _dot_sel(dt_c, e4)
        ecs = _dot_sel(jnp.exp(cs_c), e4)

        g = [_dot_nt(cm[:, gi * M_N:(gi + 1) * M_N], bm[:, gi * M_N:(gi + 1) * M_N]) for gi in range(M_GROUPS)]
        y = jnp.zeros((cl, M_DIN), F32)
        for h in range(M_HEADS):
            decay = jnp.exp(jnp.where(causal, cs_c[:, h:h + 1] - cs_r[h:h + 1, :], NEG_BIG))
            yh = _dot(g[h // rep] * decay, xdt)
            y = jnp.where(lane_head == h, yh, y)

        cs_last = cs_c[cl - 1:cl, :]
        xw = xdt * _dot_sel(jnp.exp(cs_last - cs_c), e4)
        zz = _dot_tn(xw, bm)
        upd = jnp.concatenate([zz[:half, :M_N], zz[half:, M_N:]], axis=0)
        dec = _dot_sel_nt(e4t_ref[...], jnp.broadcast_to(jnp.exp(cs_last), (M_N, LANE)))

        hs = hs_ref[...]
        yi = [_dot_nt(cm[:, gi * M_N:(gi + 1) * M_N], hs) for gi in range(M_GROUPS)]
        hs_ref[...] = hs * dec + upd
        y = y + jnp.where(lane_head < rep, yi[0], yi[1]) * ecs

        y = (y + md_ref[...] * x) * _silu(z)
        y = jnp.concatenate([_rms_rows(y[:, gi * half:(gi + 1) * half]) for gi in range(M_GROUPS)], axis=1)
        o_ref[r0:r0 + cl, :] = (y * gn_ref[...]).astype(BF16)

    hs_out_ref[...] = hs_ref[...]


def _ssd_prompt(m, misc, dtt, cw, cb, dtb_l, nega_l, dtb_c, nega_c, e4, e4t, md, gn, *, bsz, seq, tt):
    nt = seq // tt
    per8 = tt // SUBLANE
    const = lambda shape: pl.BlockSpec(shape, lambda b, i: (0,) * len(shape))
    return pl.pallas_call(
        _ssd_prompt_kernel,
        grid=(bsz, nt),
        in_specs=[pl.BlockSpec((tt, 1024), lambda b, i: (b * nt + i, 0)),
                  pl.BlockSpec((SUBLANE, 1024), lambda b, i: (jnp.maximum((b * nt + i) * per8 - 1, 0), 0)),
                  pl.BlockSpec((tt, LANE), lambda b, i: (b * nt + i, 0)),
                  pl.BlockSpec((None, SUBLANE, tt), lambda b, i: (b, 0, i)),
                  const((M_CONV, M_CONV_DIM)), const((1, M_CONV_DIM)), const((1, LANE)), const((1, LANE)),
                  const((SUBLANE, 1)), const((SUBLANE, 1)), const((LANE, M_DIN)), const((M_DIN, LANE)),
                  const((1, M_DIN)), const((1, M_DIN))],
        out_specs=[pl.BlockSpec((tt, M_DIN), lambda b, i: (b * nt + i, 0)),
                   pl.BlockSpec((None, M_DIN, M_N), lambda b, i: (b, 0, 0))],
        out_shape=[jax.ShapeDtypeStruct((bsz * seq, M_DIN), BF16),
                   jax.ShapeDtypeStruct((bsz, M_DIN, M_N), F32)],
        scratch_shapes=[pltpu.VMEM((M_DIN, M_N), F32)],
        compiler_params=_cp("arbitrary", "arbitrary"),
        name="ssd_prompt",
    )(m, m, misc, dtt, cw, cb, dtb_l, nega_l, dtb_c, nega_c, e4, e4t, md, gn)


def _step_kernel(gla_ref, la_ref, v4_ref, gg4_ref, s0_ref, m_ref, buf_ref, misc_ref, h0_ref,
                 cw_ref, cb_ref, dtb_ref, nega_ref, e4_ref, md_ref, gng_ref, gnm_ref,
                 g_ref, s_ref, y_ref, h_ref):
    q_col = _row_to_col(gla_ref[:, 0:GLA_DQK] * (GLA_DK ** -0.5))
    k_col = _row_to_col(gla_ref[:, GLA_DQK:2 * GLA_DQK])
    dec_col = _row_to_col(jnp.exp(la_ref[...]))
    v4 = v4_ref[...]
    v_exp = jnp.concatenate([jnp.broadcast_to(v4[h:h + 1, :], (GLA_DK, GLA_DV)) for h in range(GLA_HEADS)],
                            axis=0)
    s = s0_ref[...] * dec_col + k_col * v_exp
    s_ref[...] = s
    o4 = jnp.sum((q_col * s).reshape(GLA_HEADS, GLA_DK, GLA_DV), axis=1)
    g_ref[...] = _rms_rows(o4) * gng_ref[...] * _silu(gg4_ref[...])

    z = m_ref[:, 0:M_DIN]
    acc = cb_ref[...] + cw_ref[M_CONV - 1:M_CONV, :] * m_ref[:, M_DIN:M_DIN + M_CONV_DIM]
    for s_ in range(M_CONV - 1):
        acc = acc + cw_ref[s_:s_ + 1, :] * buf_ref[s_:s_ + 1, :]
    xc = _silu(acc)
    x = xc[:, 0:M_DIN]
    bm = xc[:, M_DIN:M_DIN + M_GROUPS * M_N]
    cm = xc[:, M_DIN + M_GROUPS * M_N:]
    dt, a = _ssm_params(misc_ref[...], dtb_ref[...], nega_ref[...])
    e4 = e4_ref[...]
    xdt_col = _row_to_col(x * _dot_sel(dt, e4))
    deca_col = _row_to_col(_dot_sel(jnp.exp(a), e4))
    half = M_DIN // M_GROUPS
    spread = lambda t: jnp.concatenate(
        [jnp.broadcast_to(t[:, gi * M_N:(gi + 1) * M_N], (half, M_N)) for gi in range(M_GROUPS)], axis=0)
    hs = h0_ref[...] * deca_col + xdt_col * spread(bm)
    h_ref[...] = hs
    y = _col_to_row(jnp.sum(hs * spread(cm), axis=1, keepdims=True))
    y = (y + md_ref[...] * x) * _silu(z)
    y = jnp.concatenate([_rms_rows(y[:, gi * half:(gi + 1) * half]) for gi in range(M_GROUPS)], axis=1)
    y_ref[...] = y * gnm_ref[...]


def _step_mixers(gla, la, v4, gg4, s0, m, buf, misc, h0, cw, cb, dtb, nega, e4, md, gng, gnm):
    nb = gla.shape[0]
    per = lambda *shape: pl.BlockSpec((None,) + shape, lambda b: (b,) + (0,) * len(shape))
    const = lambda shape: pl.BlockSpec(shape, lambda b: (0,) * len(shape))
    return pl.pallas_call(
        _step_kernel,
        grid=(nb,),
        in_specs=[per(1, 768), per(1, LANE), per(GLA_HEADS, GLA_DV), per(GLA_HEADS, GLA_DV),
                  per(GLA_DQK, GLA_DV), per(1, 1024), per(M_CONV - 1, M_CONV_DIM), per(1, LANE),
                  per(M_DIN, M_N),
                  const((M_CONV, M_CONV_DIM)), const((1, M_CONV_DIM)), const((1, LANE)), const((1, LANE)),
                  const((LANE, M_DIN)), const((1, M_DIN)), const((1, GLA_DV)), const((1, M_DIN))],
        out_specs=[per(GLA_HEADS, GLA_DV), per(GLA_DQK, GLA_DV), per(1, M_DIN), per(M_DIN, M_N)],
        out_shape=[jax.ShapeDtypeStruct((nb, GLA_HEADS, GLA_DV), F32),
                   jax.ShapeDtypeStruct((nb, GLA_DQK, GLA_DV), F32),
                   jax.ShapeDtypeStruct((nb, 1, M_DIN), F32),
                   jax.ShapeDtypeStruct((nb, M_DIN, M_N), F32)],
        compiler_params=_cp("arbitrary"),
        name="step_mixers",
    )(gla.reshape(nb, 1, 768), la.reshape(nb, 1, LANE), v4, gg4, s0, m.reshape(nb, 1, 1024), buf,
      misc.reshape(nb, 1, LANE), h0, cw, cb, dtb, nega, e4, md, gng, gnm)


SB_PP = 16


def _sb_decode_kernel(pt_ref, q_ref, *refs):
    ks = refs[0:SB_PP]
    vs = refs[SB_PP:2 * SB_PP]
    bias_ref, gn_ref, uinc_ref, o_ref, acc_ref, car_ref, qb_ref = refs[2 * SB_PP:]
    j = pl.program_id(1)
    page = ks[0].shape[2]
    uinc = uinc_ref[...]

    @pl.when(j == 0)
    def _():
        acc_ref[...] = jnp.zeros_like(acc_ref)
        car_ref[...] = jnp.zeros_like(car_ref)
        qb_ref[...] = jnp.broadcast_to(q_ref[...], qb_ref.shape)

    bias = bias_ref[...]
    for r in range(SB_PP - 1, -1, -1):
        prod = (ks[r][...] * qb_ref[...]).reshape(SB_HEADS, SB_HD // SUBLANE, SUBLANE, page)
        z = jnp.sum(jnp.sum(prod, axis=1), axis=1) + bias
        sp = jnp.maximum(z, 0.0) + jnp.log(1.0 + jnp.exp2(_neg_abs(z))) * LOG2E
        ls = z - sp
        hi = sp.astype(BF16)
        lo = (sp - hi.astype(F32)).astype(BF16)
        incl = (jnp.dot(hi, uinc, preferred_element_type=F32) + jnp.dot(lo, uinc, preferred_element_type=F32))
        car = car_ref[...]
        w = jnp.exp2(ls - (incl - sp + car))
        car_ref[...] = car + incl[:, 0:1]
        for h in range(SB_HEADS):
            acc_ref[h] += vs[r][h] * w[h:h + 1, :]

    @pl.when(j == pl.num_programs(1) - 1)
    def _():
        o_ref[...] = _rms_rows(jnp.sum(acc_ref[...], axis=-1)) * gn_ref[...]


def _sb_decode(page_table, q, cache_kt, cache_vt, layer, bias_col, gn, uinc):
    nb, n_pages = page_table.shape
    page = cache_kt.shape[4]
    ng = n_pages // SB_PP
    pt = page_table.reshape(-1)

    def kv_spec(r):
        return pl.BlockSpec((None, None, SB_HEADS, SB_HD, page),
                            lambda b, j, pt_ref: (layer, pt_ref[b * n_pages + (ng - 1 - j) * SB_PP + r], 0, 0, 0))

    const = lambda shape: pl.BlockSpec(shape, lambda b, j, pt_ref: (0,) * len(shape))
    grid_spec = pltpu.PrefetchScalarGridSpec(
        num_scalar_prefetch=1,
        grid=(nb, ng),
        in_specs=[pl.BlockSpec((None, SB_HEADS, SB_HD, 1), lambda b, j, pt_ref: (b, 0, 0, 0))]
        + [kv_spec(r) for r in range(SB_PP)] + [kv_spec(r) for r in range(SB_PP)]
        + [const((SB_HEADS, 1)), const((1, SB_HD)), const((page, page))],
        out_specs=pl.BlockSpec((None, SB_HEADS, SB_HD), lambda b, j, pt_ref: (b, 0, 0)),
        scratch_shapes=[pltpu.VMEM((SB_HEADS, SB_HD, page), F32), pltpu.VMEM((SB_HEADS, 1), F32),
                        pltpu.VMEM((SB_HEADS, SB_HD, page), F32)],
    )
    out = pl.pallas_call(
        _sb_decode_kernel,
        grid_spec=grid_spec,
        out_shape=jax.ShapeDtypeStruct((nb, SB_HEADS, SB_HD), F32),
        compiler_params=_cp("arbitrary", "arbitrary"),
        name="sb_decode",
    )(pt, q, *([cache_kt] * SB_PP), *([cache_vt] * SB_PP), bias_col, gn, uinc)
    return out.reshape(nb, SB_W)


def _outproj_kernel(x_ref, gate_ref, g_ref, s_ref, y_ref, w_ref, o_ref):
    mix = (_dot(g_ref[...], w_ref[0:GLA_DVW, :]) + _dot(s_ref[...], w_ref[GLA_DVW:GLA_DVW + SB_W, :])
           + _dot(y_ref[...], w_ref[GLA_DVW + SB_W:, :]))
    o_ref[...] = x_ref[...] + gate_ref[...] * mix


def _out_proj(x, gate, g, s, y, w, *, tm, rows_per_mod):
    r, d = x.shape
    mrows = gate.shape[1]
    row = lambda n: pl.BlockSpec((tm, n), lambda i: (i, 0))
    return pl.pallas_call(
        _outproj_kernel,
        grid=(r // tm,),
        in_specs=[row(d), pl.BlockSpec((None, mrows, d), lambda i: (i // rows_per_mod, 0, 0)),
                  row(GLA_DVW), row(SB_W), row(M_DIN), pl.BlockSpec(w.shape, lambda i: (0, 0))],
        out_specs=row(d),
        out_shape=jax.ShapeDtypeStruct((r, d), F32),
        compiler_params=_cp("arbitrary"),
        name="out_proj",
    )(x, gate, g, s, y, w)


def _ffn_up_kernel(x_ref, sc_ref, sh_ref, g_ref, w_ref, u_ref):
    h = _norm_mod(x_ref[...], g_ref[...], sc_ref[...], sh_ref[...])
    u_ref[...] = jnp.dot(h.astype(BF16), w_ref[...], preferred_element_type=F32)


def _ffn_up(x, sc, sh, g, w, *, tm, rows_per_mod):
    r, d = x.shape
    n = w.shape[1]
    mrows = sc.shape[1]
    mod_spec = pl.BlockSpec((None, mrows, d), lambda i: (i // rows_per_mod, 0, 0))
    return pl.pallas_call(
        _ffn_up_kernel,
        grid=(r // tm,),
        in_specs=[pl.BlockSpec((tm, d), lambda i: (i, 0)), mod_spec, mod_spec,
                  pl.BlockSpec((1, d), lambda i: (0, 0)), pl.BlockSpec((d, n), lambda i: (0, 0))],
        out_specs=pl.BlockSpec((tm, n), lambda i: (i, 0)),
        out_shape=jax.ShapeDtypeStruct((r, n), F32),
        compiler_params=_cp("arbitrary"),
        name="ffn_up",
    )(x, sc, sh, g, w)


FFN_CK = 256
FFN_RB = 128


def _ffn_act_down(taps, cw_ref, cb_ref, w_ref):
    dff = w_ref.shape[0]

    def conv(c0):
        u, u1, u2 = taps(c0)
        cs = slice(c0, c0 + FFN_CK)
        return cb_ref[:, cs] + cw_ref[0:1, cs] * u2 + cw_ref[1:2, cs] * u1 + cw_ref[2:3, cs] * u

    out = None
    for c0 in range(0, dff, FFN_CK):
        part = _dot(_silu(conv(c0)) * conv(dff + c0), w_ref[c0:c0 + FFN_CK, :])
        out = part if out is None else out + part
    return out


def _ffn_up_act_kernel(x_ref, sc_ref, sh_ref, g_ref, w_ref, cw_ref, cb_ref, a_ref, tail_ref, carry_ref,
                       u_ref, *, tiles_per_seq):
    i = pl.program_id(0)
    tm = x_ref.shape[0]
    dff = a_ref.shape[1]
    first = i % tiles_per_seq == 0
    h = _norm_mod(x_ref[...], g_ref[...], sc_ref[...], sh_ref[...]).astype(BF16)
    row8 = lax.broadcasted_iota(jnp.int32, (SUBLANE, FFN_CK), 0)

    def up(base, slot, half):
        cs = slice(base, base + FFN_CK)
        u = jnp.dot(h, w_ref[:, cs], preferred_element_type=F32)
        u_ref[slot, half, 0:SUBLANE, :] = jnp.where(first, 0.0, carry_ref[:, cs])
        u_ref[slot, half, SUBLANE:, :] = u
        last = u[tm - SUBLANE:, :]
        carry_ref[:, cs] = last
        tail_ref[:, cs] = last

    def conv(slot, half, base, rb):
        cs = slice(base, base + FFN_CK)
        acc = cb_ref[:, cs]
        for s in range(FFN_CONV):
            r0 = SUBLANE + rb - s
            acc = acc + cw_ref[FFN_CONV - 1 - s:FFN_CONV - s, cs] * u_ref[slot, half, r0:r0 + FFN_RB, :]
        return acc

    for n, c0 in enumerate(range(0, dff, FFN_CK)):
        slot = n % 2
        up(c0, slot, 0)
        up(dff + c0, slot, 1)
        for rb in range(0, tm, FFN_RB):
            a_ref[rb:rb + FFN_RB, c0:c0 + FFN_CK] = (
                _silu(conv(slot, 0, c0, rb)) * conv(slot, 1, dff + c0, rb)).astype(BF16)


def _ffn_up_act(x, sc, sh, g, w, cw, cb, *, tm, seq):
    r, d = x.shape
    n = w.shape[1]
    dff = n // 2
    tiles = seq // tm
    mod_spec = pl.BlockSpec((None, 1, d), lambda i: (i // tiles, 0, 0))
    const = lambda shape: pl.BlockSpec(shape, lambda i: (0,) * len(shape))
    return pl.pallas_call(
        functools.partial(_ffn_up_act_kernel, tiles_per_seq=tiles),
        grid=(r // tm,),
        in_specs=[pl.BlockSpec((tm, d), lambda i: (i, 0)), mod_spec, mod_spec, const((1, d)), const((d, n)),
                  const((FFN_CONV, n)), const((1, n))],
        out_specs=[pl.BlockSpec((tm, dff), lambda i: (i, 0)),
                   pl.BlockSpec((None, SUBLANE, n), lambda i: (i // tiles, 0, 0))],
        out_shape=[jax.ShapeDtypeStruct((r, dff), BF16), jax.ShapeDtypeStruct((r // seq, SUBLANE, n), F32)],
        scratch_shapes=[pltpu.VMEM((SUBLANE, n), F32), pltpu.VMEM((2, 2, SUBLANE + tm, FFN_CK), F32)],
        compiler_params=_cp("arbitrary"),
        name="ffn_up_act",
    )(x, sc, sh, g, w, cw, cb)


def _ffn_down_kernel(x_ref, gate_ref, a_ref, w_ref, o_ref):
    o_ref[...] = x_ref[...] + gate_ref[...] * jnp.dot(a_ref[...], w_ref[...], preferred_element_type=F32)


def _ffn_down(x, gate, a, w, *, tm, seq):
    r, d = x.shape
    tiles = seq // tm
    return pl.pallas_call(
        _ffn_down_kernel,
        grid=(r // tm,),
        in_specs=[pl.BlockSpec((tm, d), lambda i: (i, 0)),
                  pl.BlockSpec((None, 1, d), lambda i: (i // tiles, 0, 0)),
                  pl.BlockSpec((tm, a.shape[1]), lambda i: (i, 0)),
                  pl.BlockSpec(w.shape, lambda i: (0, 0))],
        out_specs=pl.BlockSpec((tm, d), lambda i: (i, 0)),
        out_shape=jax.ShapeDtypeStruct((r, d), F32),
        compiler_params=_cp("arbitrary"),
        name="ffn_down",
    )(x, gate, a, w)


def _ffn_down_step_kernel(x_ref, gate_ref, u_ref, buf_ref, cw_ref, cb_ref, w_ref, o_ref):
    taps = lambda c0: (u_ref[:, c0:c0 + FFN_CK], buf_ref[1, :, c0:c0 + FFN_CK], buf_ref[0, :, c0:c0 + FFN_CK])
    o_ref[...] = x_ref[...] + gate_ref[...] * _ffn_act_down(taps, cw_ref, cb_ref, w_ref)


def _ffn_down_step(x, gate, u, buf_t, cw, cb, w):
    full = lambda a: pl.BlockSpec(a.shape, lambda i: (0,) * a.ndim)
    args = (x, gate, u, buf_t, cw, cb, w)
    return pl.pallas_call(
        _ffn_down_step_kernel,
        grid=(1,),
        in_specs=[full(a) for a in args],
        out_specs=full(x),
        out_shape=jax.ShapeDtypeStruct(x.shape, F32),
        compiler_params=_cp("arbitrary"),
        name="ffn_down_step",
    )(*args)


def _same_segment(n, width):
    i = np.arange(n)
    return jnp.asarray((i[:, None] // width) == (i[None, :] // width), BF16)


def _constants():
    k = np.arange(GLA_DQK)
    v = np.arange(GLA_DVW)
    eexp = (k[:, None] // GLA_DK) == (v[None, :] // GLA_DV)
    j = np.arange(SB_TK)
    later = j[:, None] >= j[None, :]
    uo = np.block([[later, np.zeros_like(later)], [np.zeros_like(later), later]])
    lane = np.arange(LANE)
    hm_pair = np.zeros((SUBLANE, LANE), np.float32)
    hm_pair[0] = lane < SB_HD
    hm_pair[1] = lane >= SB_HD
    p = np.arange(M_DIN)
    e4 = lane[:, None] == (p[None, :] // M_HD)
    return dict(
        eexp=jnp.asarray(eexp, BF16), bd=jnp.asarray(eexp.T, F32),
        uo=jnp.asarray(uo, BF16), hm_pair=jnp.asarray(hm_pair, BF16), eseg64=_same_segment(LANE, 64),
        e4=jnp.asarray(e4, BF16), e4t=jnp.asarray(e4.T, BF16))


def _pad_lanes(v, n=LANE):
    return jnp.zeros((1, n), F32).at[0, :v.shape[0]].set(v)


def _pad_col(v, n=SUBLANE):
    return jnp.zeros((n, 1), F32).at[:v.shape[0], 0].set(v)


def kernel(x_prompt, x_sample, cache_sb_k, cache_sb_v, state_gla, state_mamba_conv, state_mamba_ssm, state_ffn_conv, page_table, c_prompt, c_sample, norm1_g, w_ada, b_ada, w_in, gla_w_gate2, gla_b_gate, gla_norm_g, sb_q_norm_g, sb_k_norm_g, sb_o_norm_g, sb_bias, m_conv_w, m_conv_b, m_dt_bias, m_a_log, m_d, m_norm_g, w_out, norm2_g, ffn_w_up, ffn_conv_w, ffn_conv_b, ffn_w_down):
    bsz, seq, d = x_prompt.shape
    nb = x_sample.shape[0]
    depth = w_in.shape[0]
    dff = ffn_w_down.shape[1]
    n_pool, page = cache_sb_k.shape[1], cache_sb_k.shape[2]
    assert x_sample.shape[1] == 1 and d % LANE == 0 and seq % 256 == 0
    cst = _constants()

    tm = 256
    tm_mm = 512
    tq = 512
    tt_gla = 256
    tt_ssd = 4 * SSD_C
    assert SB_UNROLL == 2 * (tq // SB_TK) or (tq // SB_TK) % SB_UNROLL == 0
    assert page_table.shape[1] % SB_PP == 0
    assert seq % tq == 0 and seq % tm == 0 and seq % tm_mm == 0 and seq % tt_gla == 0 and seq % tt_ssd == 0

    mod = _modulation(jnp.concatenate([c_prompt, c_sample], axis=0), w_ada, b_ada)
    mod = mod.reshape(depth, bsz + nb, N_MOD, d)
    mod_p = mod[:, :bsz].transpose(0, 2, 1, 3).reshape(depth, N_MOD, bsz, 1, d)
    mod_s = mod[:, bsz:].transpose(0, 2, 1, 3).reshape(depth, N_MOD, 1, nb, d)

    cache_kt = cache_sb_k.transpose(0, 1, 3, 4, 2)
    cache_vt = cache_sb_v.transpose(0, 1, 3, 4, 2)

    xp = x_prompt.reshape(bsz * seq, d)
    xs = x_sample.reshape(nb, d)
    w_in_t = w_in.transpose(2, 0, 1)
    outs = {name: [] for name in ("pk", "pv", "pg", "pc", "ph", "pf", "sk", "sv", "sg", "sc", "sh", "sf")}
    for l in range(depth):
        wi = w_in_t[:, l, :]
        o = np.cumsum([0, GLA_DQK, GLA_DQK, GLA_DVW, GLA_DVW, GLA_LR, SB_W, SB_W, SB_W, M_DIN, M_CONV_DIM, M_HEADS])
        sl = lambda a, b: wi[o[a]:o[b]]
        w_re = jnp.concatenate(
            [sl(0, 4), sl(5, 8), sl(8, 10), sl(10, 11), sl(4, 5),
             jnp.zeros((LANE - M_HEADS - GLA_LR, d), F32)], axis=0).astype(BF16)
        wg = jnp.zeros((LANE, LANE), F32).at[M_HEADS:M_HEADS + GLA_LR].set(gla_w_gate2[l]).astype(BF16)
        bg = gla_b_gate[l].reshape(1, GLA_DQK)
        gq = jnp.tile(sb_q_norm_g[l], SB_HEADS).reshape(1, SB_W)
        gk = jnp.tile(sb_k_norm_g[l], SB_HEADS).reshape(1, SB_W)
        go = jnp.tile(sb_o_norm_g[l], SB_HEADS).reshape(1, SB_W)
        gng = jnp.tile(gla_norm_g[l], GLA_HEADS).reshape(1, GLA_DVW)
        g1 = norm1_g[l].reshape(1, d)
        g2 = norm2_g[l].reshape(1, d)
        bias2 = sb_bias[l] * LOG2E
        b_hi = bias2.astype(BF16)
        b_mid = (bias2 - b_hi.astype(F32)).astype(BF16)
        b_lo = (bias2 - b_hi.astype(F32) - b_mid.astype(F32)).astype(BF16)
        pieces = jnp.stack([b_hi, b_mid, b_lo], axis=1)
        bias_cols = jnp.pad(jnp.repeat(pieces, SB_TK, axis=0).reshape(SB_HEADS // 2, 2 * SB_TK, SB_NB),
                            ((0, 0), (0, 0), (0, LANE - SB_NB)))
        bias_col = bias2.reshape(SB_HEADS, 1)
        cw = m_conv_w[l]
        cb = m_conv_b[l].reshape(1, M_CONV_DIM)
        nega = -jnp.exp(m_a_log[l])
        dtb_l, nega_l = _pad_lanes(m_dt_bias[l]), _pad_lanes(nega)
        dtb_c, nega_c = _pad_col(m_dt_bias[l]), _pad_col(nega)
        md = jnp.repeat(m_d[l], M_HD).reshape(1, M_DIN)
        gnm = m_norm_g[l].reshape(1, M_DIN)
        wo = w_out[l].astype(BF16)
        wu = ffn_w_up[l].astype(BF16)
        wd = ffn_w_down[l].astype(BF16)
        fcw = ffn_conv_w[l]
        fcb = ffn_conv_b[l].reshape(1, 2 * dff)
        sh1, sc1, gt1, sh2, sc2, gt2 = range(N_MOD)

        mp = mod_p[l]
        gla, la, q, k, kb, v, vb, m, misc = _in_proj(
            xp, mp[sc1], mp[sh1], g1, w_re, wg, bg, gq, gk, cst["eseg64"], tm=tm_mm, rows_per_mod=seq // tm_mm,
            kv_transposed=True)
        g_mix, g_state = _gla_prompt(gla, la, cst["eexp"], cst["bd"], cst["eseg64"], gng,
                                     bsz=bsz, seq=seq, tt=tt_gla)
        s_mix = _sb_prompt(q, kb, vb, bias_cols, cst["uo"], cst["hm_pair"], cst["eseg64"], go[:, :LANE],
                           bsz=bsz, seq=seq, tq=tq)
        dtt = jnp.pad(misc[:, :M_HEADS].reshape(bsz, seq, M_HEADS).transpose(0, 2, 1),
                      ((0, 0), (0, SUBLANE - M_HEADS), (0, 0)))
        y_mix, h_state = _ssd_prompt(m, misc, dtt, cw, cb, dtb_l, nega_l, dtb_c, nega_c, cst["e4"], cst["e4t"],
                                     md, gnm, bsz=bsz, seq=seq, tt=tt_ssd)
        xp = _out_proj(xp, mp[gt1], g_mix, s_mix, y_mix, wo, tm=tm_mm, rows_per_mod=seq // tm_mm)
        act, u_tail = _ffn_up_act(xp, mp[sc2], mp[sh2], g2, wu, fcw, fcb, tm=tm, seq=seq)
        xp = _ffn_down(xp, mp[gt2], act, wd, tm=tm_mm, seq=seq)

        outs["pk"].append(k.reshape(bsz, SB_HEADS, SB_HD, seq).transpose(0, 3, 1, 2))
        outs["pv"].append(v.reshape(bsz, SB_HEADS, SB_HD, seq).transpose(0, 3, 1, 2))
        gs = g_state.reshape(bsz, GLA_HEADS, GLA_DV, GLA_HEADS, GLA_DK)
        outs["pg"].append(jnp.stack([gs[:, h, :, h, :] for h in range(GLA_HEADS)], axis=1).transpose(0, 1, 3, 2))
        outs["pc"].append(m.reshape(bsz, seq, 1024)[:, seq - (M_CONV - 1):, M_DIN:])
        outs["ph"].append(h_state.reshape(bsz, M_HEADS, M_HD, M_N))
        outs["pf"].append(u_tail[:, SUBLANE - (FFN_CONV - 1):])

        ms = mod_s[l]
        gla, la, q, k, kb, v, vb, m, misc = _in_proj(
            xs, ms[sc1], ms[sh1], g1, w_re, wg, bg, gq, gk, cst["eseg64"], tm=nb, rows_per_mod=1,
            kv_transposed=False)
        g4, g_state, y_mix, h_state = _step_mixers(
            gla, la, gla[:, 2 * GLA_DQK:2 * GLA_DQK + GLA_DVW].reshape(nb, GLA_HEADS, GLA_DV),
            gla[:, 2 * GLA_DQK + GLA_DVW:].reshape(nb, GLA_HEADS, GLA_DV),
            state_gla[l].reshape(nb, GLA_DQK, GLA_DV), m, state_mamba_conv[l], misc,
            state_mamba_ssm[l].reshape(nb, M_DIN, M_N), cw, cb, dtb_l, nega_l, cst["e4"], md,
            gla_norm_g[l].reshape(1, GLA_DV), gnm)
        s_mix = _sb_decode(page_table, q.astype(F32).reshape(nb, SB_HEADS, SB_HD, 1), cache_kt, cache_vt, l,
                           bias_col, sb_o_norm_g[l].reshape(1, SB_HD),
                           jnp.asarray(np.tril(np.ones((page, page), np.float32)), BF16))
        xs = _out_proj(xs, ms[gt1], g4.reshape(nb, GLA_DVW), s_mix, y_mix.reshape(nb, M_DIN), wo,
                       tm=nb, rows_per_mod=1)
        u = _ffn_up(xs, ms[sc2], ms[sh2], g2, wu, tm=nb, rows_per_mod=1)
        xs = _ffn_down_step(xs, ms[gt2][0], u, state_ffn_conv[l].transpose(1, 0, 2), fcw, fcb, wd)

        outs["sk"].append(k.reshape(nb, 1, SB_HEADS, SB_HD))
        outs["sv"].append(v.reshape(nb, 1, SB_HEADS, SB_HD))
        outs["sg"].append(g_state.reshape(nb, GLA_HEADS, GLA_DK, GLA_DV))
        outs["sc"].append(jnp.concatenate([state_mamba_conv[l][:, 1:], m[:, None, M_DIN:]], axis=1))
        outs["sh"].append(h_state.reshape(nb, M_HEADS, M_HD, M_N))
        outs["sf"].append(jnp.concatenate([state_ffn_conv[l][:, 1:], u[:, None, :]], axis=1))

    st = {name: jnp.stack(v) for name, v in outs.items()}
    return (xp.reshape(bsz, seq, d), xs.reshape(nb, 1, d), st["pk"], st["pv"], st["pg"], st["pc"], st["ph"],
            st["pf"], st["sk"], st["sv"], st["sg"], st["sc"], st["sh"], st["sf"])
```

```python
import functools

import jax
import jax.numpy as jnp
import numpy as np
from jax import lax
from jax.experimental import pallas as pl
from jax.experimental.pallas import tpu as pltpu

F32 = jnp.float32
BF16 = jnp.bfloat16

GLA_HEADS, GLA_DK, GLA_DV, GLA_LR, GLA_TAU = 4, 32, 64, 16, 16.0
GLA_DQK = GLA_HEADS * GLA_DK
GLA_DVW = GLA_HEADS * GLA_DV
SB_HEADS, SB_HD = 8, 64
SB_W = SB_HEADS * SB_HD
M_HEADS, M_HD, M_GROUPS, M_N, M_CONV = 4, 64, 2, 128, 4
M_DIN = M_HEADS * M_HD
M_CONV_DIM = M_DIN + 2 * M_GROUPS * M_N
FFN_CONV = 3
N_MOD = 6
EPS = 1e-6
NEG_BIG = -1e30
LOG2E = 1.4426950408889634

LANE = 128
SUBLANE = 8

C_GLA = 0
C_SBQ = C_GLA + 2 * GLA_DQK + 2 * GLA_DVW
C_SBK = C_SBQ + SB_W
C_SBV = C_SBK + SB_W
C_MZ = C_SBV + SB_W
C_MISC = C_MZ + M_DIN + M_CONV_DIM
D_INP = C_MISC + LANE
W_GLA = C_SBQ - C_GLA
W_M = C_MISC - C_MZ

VMEM_LIMIT = 56 * 1024 * 1024


def _cp(*sem):
    return pltpu.CompilerParams(dimension_semantics=sem, vmem_limit_bytes=VMEM_LIMIT)


def _dot(a, b):
    return jnp.dot(a.astype(BF16), b.astype(BF16), preferred_element_type=F32)


def _dot_nt(a, b):
    return lax.dot_general(a.astype(BF16), b.astype(BF16), (((1,), (1,)), ((), ())),
                           preferred_element_type=F32)


def _dot_tn(a, b):
    return lax.dot_general(a.astype(BF16), b.astype(BF16), (((0,), (0,)), ((), ())),
                           preferred_element_type=F32)


def _split3(x):
    hi = x.astype(BF16)
    r = x - hi.astype(F32)
    mid = r.astype(BF16)
    lo = (r - mid.astype(F32)).astype(BF16)
    return hi, mid, lo


def _dot_sel(x, e):
    hi, mid, lo = _split3(x)
    d = lambda p: jnp.dot(p, e, preferred_element_type=F32)
    return d(hi) + d(mid) + d(lo)


def _dot_sel_nt(e, x):
    hi, mid, lo = _split3(x)
    d = lambda p: lax.dot_general(e, p, (((1,), (1,)), ((), ())), preferred_element_type=F32)
    return d(hi) + d(mid) + d(lo)


def _sigmoid(x):
    return 1.0 / (1.0 + jnp.exp(-x))


def _silu(x):
    return x * _sigmoid(x)


def _softplus(x):
    return jnp.maximum(x, 0.0) + jnp.log1p(jnp.exp(-jnp.abs(x)))


def _log_sigmoid(x):
    return jnp.minimum(x, 0.0) - jnp.log1p(jnp.exp(-jnp.abs(x)))


def _rms_rows(x):
    return x * lax.rsqrt(jnp.mean(x * x, axis=-1, keepdims=True) + EPS)


def _seg_rms(x, eseg, width):
    xx = x * x
    hi = xx.astype(BF16)
    lo = (xx - hi.astype(F32)).astype(BF16)
    blocks = []
    for c in range(0, x.shape[1], LANE):
        d = lambda p: jnp.dot(p[:, c:c + LANE], eseg, preferred_element_type=F32)
        blocks.append(d(hi) + d(lo))
    ms = jnp.concatenate(blocks, axis=1) * (1.0 / width)
    return x * lax.rsqrt(ms + EPS)


def _row_to_col(row):
    n = row.shape[1]
    eye = lax.broadcasted_iota(jnp.int32, (n, n), 0) == lax.broadcasted_iota(jnp.int32, (n, n), 1)
    return jnp.sum(jnp.where(eye, row, 0.0), axis=1, keepdims=True)


def _col_to_row(col):
    n = col.shape[0]
    eye = lax.broadcasted_iota(jnp.int32, (n, n), 0) == lax.broadcasted_iota(jnp.int32, (n, n), 1)
    return jnp.sum(jnp.where(eye, col, 0.0), axis=0, keepdims=True)


def _cumsum(x, axis, seg=None):
    n = x.shape[axis] if seg is None else seg
    idx = lax.broadcasted_iota(jnp.int32, x.shape, axis)
    if seg is not None:
        idx = idx % seg
    s = 1
    while s < n:
        x = x + jnp.where(idx >= s, pltpu.roll(x, s, axis), 0.0)
        s *= 2
    return x


def _mod_kernel(c_ref, w_ref, b_ref, o_ref):
    o_ref[...] = _dot(_silu(c_ref[...]), w_ref[...]) + b_ref[...]


def _modulation(c_all, w_ada, b_ada):
    depth, d, nd = w_ada.shape
    n = c_all.shape[0]
    tn = 1024
    return pl.pallas_call(
        _mod_kernel,
        grid=(depth, nd // tn),
        in_specs=[pl.BlockSpec((n, d), lambda l, j: (0, 0)),
                  pl.BlockSpec((None, d, tn), lambda l, j: (l, 0, j)),
                  pl.BlockSpec((None, 1, tn), lambda l, j: (l, 0, j))],
        out_specs=pl.BlockSpec((None, n, tn), lambda l, j: (l, 0, j)),
        out_shape=jax.ShapeDtypeStruct((depth, n, nd), F32),
        compiler_params=_cp("arbitrary", "arbitrary"),
        name="adaln_mod",
    )(c_all, w_ada, b_ada.reshape(depth, 1, nd))


def _norm_mod(x, g, sc, sh):
    return _rms_rows(x) * g * (1.0 + sc) + sh


def _inproj_kernel(x_ref, sc_ref, sh_ref, g_ref, w_ref, wg_ref, bg_ref, gq_ref, gk_ref, eseg_ref,
                   gla_ref, la_ref, q_ref, k_ref, kb_ref, v_ref, vb_ref, m_ref, misc_ref, *, kv_transposed):
    h = _norm_mod(x_ref[...], g_ref[...], sc_ref[...], sh_ref[...])
    p = lax.dot_general(h.astype(BF16), w_ref[...], (((1,), (1,)), ((), ())), preferred_element_type=F32)
    gla_ref[...] = p[:, C_GLA:C_SBQ]
    misc = p[:, C_MISC:D_INP]
    misc_ref[...] = misc
    la_ref[...] = _log_sigmoid(_dot(misc, wg_ref[...]) + bg_ref[...]) * (1.0 / GLA_TAU)
    eseg = eseg_ref[...]
    q = _seg_rms(p[:, C_SBQ:C_SBK], eseg, SB_HD) * gq_ref[...]
    q_ref[...] = (q * (SB_HD ** -0.5 * LOG2E)).astype(BF16)
    k = _seg_rms(p[:, C_SBK:C_SBV], eseg, SB_HD) * gk_ref[...]
    k_ref[...] = k.T if kv_transposed else k
    kb_ref[...] = k.astype(BF16)
    v = p[:, C_SBV:C_MZ]
    v_ref[...] = v.T if kv_transposed else v
    vb_ref[...] = v.astype(BF16)
    m_ref[...] = p[:, C_MZ:C_MISC]


def _in_proj(x, sc, sh, g, w, wg, bg, gq, gk, eseg, *, tm, rows_per_mod, kv_transposed):
    r, d = x.shape
    mrows = sc.shape[1]
    mod_spec = pl.BlockSpec((None, mrows, d), lambda i: (i // rows_per_mod, 0, 0))
    const = lambda shape: pl.BlockSpec(shape, lambda i: (0,) * len(shape))
    row = lambda n: pl.BlockSpec((tm, n), lambda i: (i, 0))
    outs = [(W_GLA, F32), (LANE, F32), (SB_W, BF16), (SB_W, F32), (SB_W, BF16), (SB_W, F32), (SB_W, BF16),
            (W_M, F32), (LANE, F32)]
    out_specs = [row(n) for n, _ in outs]
    out_shape = [jax.ShapeDtypeStruct((r, n), dt) for n, dt in outs]
    if kv_transposed:
        seq = rows_per_mod * tm
        for idx in (3, 5):
            out_specs[idx] = pl.BlockSpec((None, SB_W, tm), lambda i: (i // rows_per_mod, 0, i % rows_per_mod))
            out_shape[idx] = jax.ShapeDtypeStruct((r // seq, SB_W, seq), F32)
    return pl.pallas_call(
        functools.partial(_inproj_kernel, kv_transposed=kv_transposed),
        grid=(r // tm,),
        in_specs=[row(d), mod_spec, mod_spec, const((1, d)), const((D_INP, d)), const((LANE, LANE)),
                  const((1, LANE)), const((1, SB_W)), const((1, SB_W)), const((LANE, LANE))],
        out_specs=out_specs,
        out_shape=out_shape,
        compiler_params=_cp("arbitrary"),
        name="in_proj",
    )(x, sc, sh, g, w, wg, bg, gq, gk, eseg)


GLA_C = SUBLANE
GLA_UNROLL = 32


def _gla_prompt_kernel(gla_ref, la_ref, eexp_ref, bd_ref, eseg_ref, gn_ref, o_ref, st_out_ref,
                       st_ref, b_ref, acc_ref):
    i = pl.program_id(1)
    tt = la_ref.shape[0]

    @pl.when(i == 0)
    def _():
        st_ref[...] = jnp.zeros_like(st_ref)

    b_ref[...] = _cumsum(la_ref[...], 0, seg=GLA_C)
    eexp = eexp_ref[...]
    bd = bd_ref[...]
    rowid = lax.broadcasted_iota(jnp.int32, (GLA_C, GLA_DQK), 0)
    rowid_v = lax.broadcasted_iota(jnp.int32, (GLA_C, GLA_DVW), 0)

    def group(gi, carry):
        steps = []
        for u in range(GLA_UNROLL):
            r0 = pl.multiple_of((gi * GLA_UNROLL + u) * GLA_C, GLA_C)
            bc = b_ref[pl.ds(r0, GLA_C), :]
            qc = gla_ref[pl.ds(r0, GLA_C), 0:GLA_DQK] * (GLA_DK ** -0.5)
            kc = gla_ref[pl.ds(r0, GLA_C), GLA_DQK:2 * GLA_DQK]
            vc = gla_ref[pl.ds(r0, GLA_C), 2 * GLA_DQK:2 * GLA_DQK + GLA_DVW]
            steps.append((r0, bc, qc, kc, vc, bc[GLA_C - 1:GLA_C, :]))
        ps = []
        for r0, bc, qc, kc, vc, bl in steps:
            for t in range(GLA_C):
                d = jnp.where(rowid <= t, bc[t:t + 1, :] - bc, NEG_BIG)
                ps.append(jnp.exp(d) * (qc[t:t + 1, :] * kc))
        sc = _dot(jnp.concatenate(ps, axis=0), eexp)
        upds = [_dot_tn(vc, kc * jnp.exp(bl - bc)) * bd for r0, bc, qc, kc, vc, bl in steps]
        st = st_ref[...]
        for u, (r0, bc, qc, kc, vc, bl) in enumerate(steps):
            o = _dot_nt(qc * jnp.exp(bc), st)
            st = st * jnp.exp(bl) + upds[u]
            for t in range(GLA_C):
                row0 = (u * GLA_C + t) * GLA_C
                ot = jnp.sum(sc[row0:row0 + GLA_C, :] * vc, axis=0, keepdims=True)
                o = o + jnp.where(rowid_v == t, ot, 0.0)
            acc_ref[pl.ds(r0, GLA_C), :] = o
        st_ref[...] = st
        return carry

    lax.fori_loop(0, tt // (GLA_C * GLA_UNROLL), group, 0)
    gg = gla_ref[:, 2 * GLA_DQK + GLA_DVW:2 * GLA_DQK + 2 * GLA_DVW]
    o = _seg_rms(acc_ref[...], eseg_ref[...], GLA_DV) * gn_ref[...] * _silu(gg)
    o_ref[...] = o.astype(BF16)
    st_out_ref[...] = st_ref[...]


def _gla_prompt(gla, la, eexp, bd, eseg, gn, *, bsz, seq, tt):
    nt = seq // tt
    const = lambda shape: pl.BlockSpec(shape, lambda b, i: (0,) * len(shape))
    return pl.pallas_call(
        _gla_prompt_kernel,
        grid=(bsz, nt),
        in_specs=[pl.BlockSpec((tt, W_GLA), lambda b, i: (b * nt + i, 0)),
                  pl.BlockSpec((tt, LANE), lambda b, i: (b * nt + i, 0)),
                  const((GLA_DQK, GLA_DVW)), const((GLA_DVW, GLA_DQK)), const((LANE, LANE)),
                  const((1, GLA_DVW))],
        out_specs=[pl.BlockSpec((tt, GLA_DVW), lambda b, i: (b * nt + i, 0)),
                   pl.BlockSpec((None, GLA_DVW, GLA_DQK), lambda b, i: (b, 0, 0))],
        out_shape=[jax.ShapeDtypeStruct((bsz * seq, GLA_DVW), BF16),
                   jax.ShapeDtypeStruct((bsz, GLA_DVW, GLA_DQK), F32)],
        scratch_shapes=[pltpu.VMEM((GLA_DVW, GLA_DQK), F32), pltpu.VMEM((tt, GLA_DQK), F32),
                        pltpu.VMEM((tt, GLA_DVW), F32)],
        compiler_params=_cp("arbitrary", "arbitrary"),
        name="gla_prompt",
    )(gla, la, eexp, bd, eseg, gn)


SB_TK = LANE
SB_NB = 3
SB_UNROLL = 8


def _neg_abs(x):
    return lax.bitcast_convert_type(lax.bitcast_convert_type(x, jnp.int32) | jnp.int32(-2 ** 31), F32)


def _sb_prompt_kernel(q_ref, k_ref, v_ref, bcol_ref, uo_ref, hm_ref, eseg_ref, gn_ref, o_ref,
                      acc_ref, car_ref, kk_ref, vv_ref, qx_ref):
    i = pl.program_id(2)
    tq = q_ref.shape[0]
    ndiag = tq // SB_TK
    nblk = kk_ref.shape[0]

    @pl.when(i == 0)
    def _():
        m0 = hm_ref[0:1, :]
        m1 = hm_ref[1:2, :]
        bcol = bcol_ref[...]

        def fill(j, carry):
            k0 = pl.multiple_of(j * SB_TK, SB_TK)
            kb = k_ref[pl.ds(k0, SB_TK), :]
            vb = v_ref[pl.ds(k0, SB_TK), :]
            kk_ref[j] = jnp.concatenate([jnp.concatenate([kb * m0, kb * m1], axis=0), bcol], axis=1)
            vv_ref[j] = jnp.concatenate([vb * m0, vb * m1], axis=0)
            return carry

        lax.fori_loop(0, nblk, fill, 0)

    ones = jnp.where(lax.broadcasted_iota(jnp.int32, (tq, LANE), 1) < SB_NB, 1.0, 0.0).astype(BF16)
    qx_ref[...] = jnp.concatenate([q_ref[...], ones], axis=1)
    uo = uo_ref[...]
    acc_ref[...] = jnp.zeros_like(acc_ref)
    car_ref[...] = jnp.zeros_like(car_ref)

    def block(j, r0, diag):
        rows = tq - r0
        z = lax.dot_general(qx_ref[r0:, :], kk_ref[j], (((1,), (1,)), ((), ())),
                            preferred_element_type=F32)
        sp = jnp.maximum(z, 0.0) + jnp.log(1.0 + jnp.exp2(_neg_abs(z))) * LOG2E
        if diag:
            valid = (lax.broadcasted_iota(jnp.int32, (SB_TK, 2 * SB_TK), 1) % SB_TK
                     < lax.broadcasted_iota(jnp.int32, (SB_TK, 2 * SB_TK), 0))
            def mask_head(t):
                head = jnp.where(valid, t[:SB_TK], 0.0)
                return head if rows == SB_TK else jnp.concatenate([head, t[SB_TK:]], axis=0)

            sp = mask_head(sp)
        incl = jnp.dot(sp.astype(BF16), uo, preferred_element_type=F32) + car_ref[r0:, :]
        w = jnp.exp2(z - incl)
        if diag:
            w = mask_head(w)
        car_ref[r0:, :SB_TK] += jnp.sum(sp[:, :SB_TK], axis=1, keepdims=True)
        car_ref[r0:, SB_TK:] += jnp.sum(sp[:, SB_TK:], axis=1, keepdims=True)
        acc_ref[r0:, :] += jnp.dot(w.astype(BF16), vv_ref[j], preferred_element_type=F32)

    for c in range(ndiag - 1, -1, -1):
        block(i * ndiag + c, c * SB_TK, True)

    n_off = i * ndiag
    n_main = n_off // SB_UNROLL

    def body(jj, carry):
        for u in range(SB_UNROLL):
            block(n_off - 1 - jj * SB_UNROLL - u, 0, False)
        return carry

    lax.fori_loop(0, n_main, body, 0)
    if SB_UNROLL > ndiag:
        @pl.when(n_off - n_main * SB_UNROLL > 0)
        def _():
            for u in range(ndiag):
                block(ndiag - 1 - u, 0, False)

    o_ref[...] = (_seg_rms(acc_ref[...], eseg_ref[...], SB_HD) * gn_ref[...]).astype(BF16)


def _sb_prompt(q, k, v, bias_cols, uo, hm, eseg, gn, *, bsz, seq, tq):
    nq = seq // tq
    npair = SB_HEADS // 2
    const = lambda shape: pl.BlockSpec(shape, lambda b, p, i: (0,) * len(shape))
    return pl.pallas_call(
        _sb_prompt_kernel,
        grid=(bsz, npair, nq),
        in_specs=[pl.BlockSpec((tq, LANE), lambda b, p, i: (b * nq + i, p)),
                  pl.BlockSpec((seq, LANE), lambda b, p, i: (b, p)),
                  pl.BlockSpec((seq, LANE), lambda b, p, i: (b, p)),
                  pl.BlockSpec((None, 2 * SB_TK, LANE), lambda b, p, i: (p, 0, 0)),
                  const((2 * SB_TK, 2 * SB_TK)), const((SUBLANE, LANE)), const((LANE, LANE)), const((1, LANE))],
        out_specs=pl.BlockSpec((tq, LANE), lambda b, p, i: (b * nq + i, p)),
        out_shape=jax.ShapeDtypeStruct((bsz * seq, SB_W), BF16),
        scratch_shapes=[pltpu.VMEM((tq, LANE), F32), pltpu.VMEM((tq, 2 * SB_TK), F32),
                        pltpu.VMEM((seq // SB_TK, 2 * SB_TK, 2 * LANE), BF16),
                        pltpu.VMEM((seq // SB_TK, 2 * SB_TK, LANE), BF16),
                        pltpu.VMEM((tq, 2 * LANE), BF16)],
        compiler_params=_cp("arbitrary", "arbitrary", "arbitrary"),
        name="sb_prompt",
    )(q, k, v, bias_cols, uo, hm, eseg, gn)


def _ssm_params(dt_raw, dtb, nega):
    dt = _softplus(dt_raw + dtb)
    return dt, dt * nega


SSD_C = 128


def _ssd_prompt_kernel(m_ref, prev_ref, misc_ref, dtt_ref, cw_ref, cb_ref, dtb_l_ref, nega_l_ref,
                       dtb_c_ref, nega_c_ref, e4_ref, e4t_ref, md_ref, gn_ref, o_ref, hs_out_ref,
                       hs_ref):
    i = pl.program_id(1)
    tt = m_ref.shape[0]
    cl = SSD_C

    @pl.when(i == 0)
    def _():
        hs_ref[...] = jnp.zeros_like(hs_ref)

    e4 = e4_ref[...]
    row = lax.broadcasted_iota(jnp.int32, (cl, M_CONV_DIM), 0)
    row8 = lax.broadcasted_iota(jnp.int32, (SUBLANE, M_CONV_DIM), 0)
    causal = lax.broadcasted_iota(jnp.int32, (cl, cl), 0) >= lax.broadcasted_iota(jnp.int32, (cl, cl), 1)
    lane_head = lax.broadcasted_iota(jnp.int32, (cl, M_DIN), 1) // M_HD
    rep = M_HEADS // M_GROUPS
    half = M_DIN // M_GROUPS

    for r0 in range(0, tt, cl):
        z = m_ref[r0:r0 + cl, 0:M_DIN]
        xbc = m_ref[r0:r0 + cl, M_DIN:M_DIN + M_CONV_DIM]
        if r0 == 0:
            prev = jnp.where(i > 0, prev_ref[:, M_DIN:M_DIN + M_CONV_DIM], 0.0)
        else:
            prev = m_ref[r0 - SUBLANE:r0, M_DIN:M_DIN + M_CONV_DIM]
        acc = cb_ref[...] + cw_ref[M_CONV - 1:M_CONV, :] * xbc
        for s in range(1, M_CONV):
            head = jnp.where(row8 < s, pltpu.roll(prev, s, 0), 0.0)
            head = jnp.concatenate([head, jnp.zeros((cl - SUBLANE, M_CONV_DIM), F32)], axis=0)
            shifted = jnp.where(row < s, head, pltpu.roll(xbc, s, 0))
            acc = acc + cw_ref[M_CONV - 1 - s:M_CONV - s, :] * shifted
        xc = _silu(acc)
        x = xc[:, 0:M_DIN]
        bm = xc[:, M_DIN:M_DIN + M_GROUPS * M_N]
        cm = xc[:, M_DIN + M_GROUPS * M_N:]

        dt_c, a_c = _ssm_params(misc_ref[r0:r0 + cl, :], dtb_l_ref[...], nega_l_ref[...])
        cs_c = _cumsum(a_c, 0)
        dt_r, a_r = _ssm_params(dtt_ref[:, r0:r0 + cl], dtb_c_ref[...], nega_c_ref[...])
        cs_r = _cumsum(a_r, 1)
        xdt = x * _dot_sel(dt_c, e4)
        ecs = _dot_sel(jnp.exp(cs_c), e4)

        g = [_dot_nt(cm[:, gi * M_N:(gi + 1) * M_N], bm[:, gi * M_N:(gi + 1) * M_N]) for gi in range(M_GROUPS)]
        y = jnp.zeros((cl, M_DIN), F32)
        for h in range(M_HEADS):
            decay = jnp.exp(jnp.where(causal, cs_c[:, h:h + 1] - cs_r[h:h + 1, :], NEG_BIG))
            yh = _dot(g[h // rep] * decay, xdt)
            y = jnp.where(lane_head == h, yh, y)

        cs_last = cs_c[cl - 1:cl, :]
        xw = xdt * _dot_sel(jnp.exp(cs_last - cs_c), e4)
        zz = _dot_tn(xw, bm)
        upd = jnp.concatenate([zz[:half, :M_N], zz[half:, M_N:]], axis=0)
        dec = _dot_sel_nt(e4t_ref[...], jnp.broadcast_to(jnp.exp(cs_last), (M_N, LANE)))

        hs = hs_ref[...]
        yi = [_dot_nt(cm[:, gi * M_N:(gi + 1) * M_N], hs) for gi in range(M_GROUPS)]
        hs_ref[...] = hs * dec + upd
        y = y + jnp.where(lane_head < rep, yi[0], yi[1]) * ecs

        y = (y + md_ref[...] * x) * _silu(z)
        y = jnp.concatenate([_rms_rows(y[:, gi * half:(gi + 1) * half]) for gi in range(M_GROUPS)], axis=1)
        o_ref[r0:r0 + cl, :] = (y * gn_ref[...]).astype(BF16)

    hs_out_ref[...] = hs_ref[...]


def _ssd_prompt(m, misc, dtt, cw, cb, dtb_l, nega_l, dtb_c, nega_c, e4, e4t, md, gn, *, bsz, seq, tt):
    nt = seq // tt
    per8 = tt // SUBLANE
    const = lambda shape: pl.BlockSpec(shape, lambda b, i: (0,) * len(shape))
    return pl.pallas_call(
        _ssd_prompt_kernel,
        grid=(bsz, nt),
        in_specs=[pl.BlockSpec((tt, W_M), lambda b, i: (b * nt + i, 0)),
                  pl.BlockSpec((SUBLANE, W_M), lambda b, i: (jnp.maximum((b * nt + i) * per8 - 1, 0), 0)),
                  pl.BlockSpec((tt, LANE), lambda b, i: (b * nt + i, 0)),
                  pl.BlockSpec((None, SUBLANE, tt), lambda b, i: (b, 0, i)),
                  const((M_CONV, M_CONV_DIM)), const((1, M_CONV_DIM)), const((1, LANE)), const((1, LANE)),
                  const((SUBLANE, 1)), const((SUBLANE, 1)), const((LANE, M_DIN)), const((M_DIN, LANE)),
                  const((1, M_DIN)), const((1, M_DIN))],
        out_specs=[pl.BlockSpec((tt, M_DIN), lambda b, i: (b * nt + i, 0)),
                   pl.BlockSpec((None, M_DIN, M_N), lambda b, i: (b, 0, 0))],
        out_shape=[jax.ShapeDtypeStruct((bsz * seq, M_DIN), BF16),
                   jax.ShapeDtypeStruct((bsz, M_DIN, M_N), F32)],
        scratch_shapes=[pltpu.VMEM((M_DIN, M_N), F32)],
        compiler_params=_cp("arbitrary", "arbitrary"),
        name="ssd_prompt",
    )(m, m, misc, dtt, cw, cb, dtb_l, nega_l, dtb_c, nega_c, e4, e4t, md, gn)


def _step_kernel(gla_ref, la_ref, v4_ref, gg4_ref, s0_ref, m_ref, buf_ref, misc_ref, h0_ref,
                 cw_ref, cb_ref, dtb_ref, nega_ref, e4_ref, md_ref, gng_ref, gnm_ref,
                 g_ref, s_ref, y_ref, h_ref):
    q_col = _row_to_col(gla_ref[:, 0:GLA_DQK] * (GLA_DK ** -0.5))
    k_col = _row_to_col(gla_ref[:, GLA_DQK:2 * GLA_DQK])
    dec_col = _row_to_col(jnp.exp(la_ref[...]))
    v4 = v4_ref[...]
    v_exp = jnp.concatenate([jnp.broadcast_to(v4[h:h + 1, :], (GLA_DK, GLA_DV)) for h in range(GLA_HEADS)],
                            axis=0)
    s = s0_ref[...] * dec_col + k_col * v_exp
    s_ref[...] = s
    o4 = jnp.sum((q_col * s).reshape(GLA_HEADS, GLA_DK, GLA_DV), axis=1)
    g_ref[...] = _rms_rows(o4) * gng_ref[...] * _silu(gg4_ref[...])

    z = m_ref[:, 0:M_DIN]
    acc = cb_ref[...] + cw_ref[M_CONV - 1:M_CONV, :] * m_ref[:, M_DIN:M_DIN + M_CONV_DIM]
    for s_ in range(M_CONV - 1):
        acc = acc + cw_ref[s_:s_ + 1, :] * buf_ref[s_:s_ + 1, :]
    xc = _silu(acc)
    x = xc[:, 0:M_DIN]
    bm = xc[:, M_DIN:M_DIN + M_GROUPS * M_N]
    cm = xc[:, M_DIN + M_GROUPS * M_N:]
    dt, a = _ssm_params(misc_ref[...], dtb_ref[...], nega_ref[...])
    e4 = e4_ref[...]
    xdt_col = _row_to_col(x * _dot_sel(dt, e4))
    deca_col = _row_to_col(_dot_sel(jnp.exp(a), e4))
    half = M_DIN // M_GROUPS
    spread = lambda t: jnp.concatenate(
        [jnp.broadcast_to(t[:, gi * M_N:(gi + 1) * M_N], (half, M_N)) for gi in range(M_GROUPS)], axis=0)
    hs = h0_ref[...] * deca_col + xdt_col * spread(bm)
    h_ref[...] = hs
    y = _col_to_row(jnp.sum(hs * spread(cm), axis=1, keepdims=True))
    y = (y + md_ref[...] * x) * _silu(z)
    y = jnp.concatenate([_rms_rows(y[:, gi * half:(gi + 1) * half]) for gi in range(M_GROUPS)], axis=1)
    y_ref[...] = y * gnm_ref[...]


def _step_mixers(gla, la, v4, gg4, s0, m, buf, misc, h0, cw, cb, dtb, nega, e4, md, gng, gnm):
    nb = gla.shape[0]
    per = lambda *shape: pl.BlockSpec((None,) + shape, lambda b: (b,) + (0,) * len(shape))
    const = lambda shape: pl.BlockSpec(shape, lambda b: (0,) * len(shape))
    return pl.pallas_call(
        _step_kernel,
        grid=(nb,),
        in_specs=[per(1, W_GLA), per(1, LANE), per(GLA_HEADS, GLA_DV), per(GLA_HEADS, GLA_DV),
                  per(GLA_DQK, GLA_DV), per(1, W_M), per(M_CONV - 1, M_CONV_DIM), per(1, LANE),
                  per(M_DIN, M_N),
                  const((M_CONV, M_CONV_DIM)), const((1, M_CONV_DIM)), const((1, LANE)), const((1, LANE)),
                  const((LANE, M_DIN)), const((1, M_DIN)), const((1, GLA_DV)), const((1, M_DIN))],
        out_specs=[per(GLA_HEADS, GLA_DV), per(GLA_DQK, GLA_DV), per(1, M_DIN), per(M_DIN, M_N)],
        out_shape=[jax.ShapeDtypeStruct((nb, GLA_HEADS, GLA_DV), F32),
                   jax.ShapeDtypeStruct((nb, GLA_DQK, GLA_DV), F32),
                   jax.ShapeDtypeStruct((nb, 1, M_DIN), F32),
                   jax.ShapeDtypeStruct((nb, M_DIN, M_N), F32)],
        compiler_params=_cp("arbitrary"),
        name="step_mixers",
    )(gla.reshape(nb, 1, W_GLA), la.reshape(nb, 1, LANE), v4, gg4, s0, m.reshape(nb, 1, W_M), buf,
      misc.reshape(nb, 1, LANE), h0, cw, cb, dtb, nega, e4, md, gng, gnm)


SB_PP = 16


def _sb_decode_kernel(pt_ref, q_ref, *refs):
    ks = refs[0:SB_PP]
    vs = refs[SB_PP:2 * SB_PP]
    bias_ref, gn_ref, uinc_ref, o_ref, acc_ref, car_ref, qb_ref = refs[2 * SB_PP:]
    j = pl.program_id(1)
    page = ks[0].shape[2]
    uinc = uinc_ref[...]

    @pl.when(j == 0)
    def _():
        acc_ref[...] = jnp.zeros_like(acc_ref)
        car_ref[...] = jnp.zeros_like(car_ref)
        qb_ref[...] = jnp.broadcast_to(q_ref[...], qb_ref.shape)

    bias = bias_ref[...]
    for r in range(SB_PP - 1, -1, -1):
        prod = (ks[r][...] * qb_ref[...]).reshape(SB_HEADS, SB_HD // SUBLANE, SUBLANE, page)
        z = jnp.sum(jnp.sum(prod, axis=1), axis=1) + bias
        sp = jnp.maximum(z, 0.0) + jnp.log(1.0 + jnp.exp2(_neg_abs(z))) * LOG2E
        ls = z - sp
        hi = sp.astype(BF16)
        lo = (sp - hi.astype(F32)).astype(BF16)
        incl = (jnp.dot(hi, uinc, preferred_element_type=F32) + jnp.dot(lo, uinc, preferred_element_type=F32))
        car = car_ref[...]
        w = jnp.exp2(ls - (incl - sp + car))
        car_ref[...] = car + incl[:, 0:1]
        for h in range(SB_HEADS):
            acc_ref[h] += vs[r][h] * w[h:h + 1, :]

    @pl.when(j == pl.num_programs(1) - 1)
    def _():
        o_ref[...] = _rms_rows(jnp.sum(acc_ref[...], axis=-1)) * gn_ref[...]


def _sb_decode(page_table, q, cache_kt, cache_vt, layer, bias_col, gn, uinc):
    nb, n_pages = page_table.shape
    page = cache_kt.shape[4]
    ng = n_pages // SB_PP
    pt = page_table.reshape(-1)

    def kv_spec(r):
        return pl.BlockSpec((None, None, SB_HEADS, SB_HD, page),
                            lambda b, j, pt_ref: (layer, pt_ref[b * n_pages + (ng - 1 - j) * SB_PP + r], 0, 0, 0))

    const = lambda shape: pl.BlockSpec(shape, lambda b, j, pt_ref: (0,) * len(shape))
    grid_spec = pltpu.PrefetchScalarGridSpec(
        num_scalar_prefetch=1,
        grid=(nb, ng),
        in_specs=[pl.BlockSpec((None, SB_HEADS, SB_HD, 1), lambda b, j, pt_ref: (b, 0, 0, 0))]
        + [kv_spec(r) for r in range(SB_PP)] + [kv_spec(r) for r in range(SB_PP)]
        + [const((SB_HEADS, 1)), const((1, SB_HD)), const((page, page))],
        out_specs=pl.BlockSpec((None, SB_HEADS, SB_HD), lambda b, j, pt_ref: (b, 0, 0)),
        scratch_shapes=[pltpu.VMEM((SB_HEADS, SB_HD, page), F32), pltpu.VMEM((SB_HEADS, 1), F32),
                        pltpu.VMEM((SB_HEADS, SB_HD, page), F32)],
    )
    out = pl.pallas_call(
        _sb_decode_kernel,
        grid_spec=grid_spec,
        out_shape=jax.ShapeDtypeStruct((nb, SB_HEADS, SB_HD), F32),
        compiler_params=_cp("arbitrary", "arbitrary"),
        name="sb_decode",
    )(pt, q, *([cache_kt] * SB_PP), *([cache_vt] * SB_PP), bias_col, gn, uinc)
    return out.reshape(nb, SB_W)


def _outproj_kernel(x_ref, gate_ref, g_ref, s_ref, y_ref, w_ref, o_ref):
    mix = (_dot(g_ref[...], w_ref[0:GLA_DVW, :]) + _dot(s_ref[...], w_ref[GLA_DVW:GLA_DVW + SB_W, :])
           + _dot(y_ref[...], w_ref[GLA_DVW + SB_W:, :]))
    o_ref[...] = x_ref[...] + gate_ref[...] * mix


def _out_proj(x, gate, g, s, y, w, *, tm, rows_per_mod):
    r, d = x.shape
    mrows = gate.shape[1]
    row = lambda n: pl.BlockSpec((tm, n), lambda i: (i, 0))
    return pl.pallas_call(
        _outproj_kernel,
        grid=(r // tm,),
        in_specs=[row(d), pl.BlockSpec((None, mrows, d), lambda i: (i // rows_per_mod, 0, 0)),
                  row(GLA_DVW), row(SB_W), row(M_DIN), pl.BlockSpec(w.shape, lambda i: (0, 0))],
        out_specs=row(d),
        out_shape=jax.ShapeDtypeStruct((r, d), F32),
        compiler_params=_cp("arbitrary"),
        name="out_proj",
    )(x, gate, g, s, y, w)


def _ffn_up_kernel(x_ref, sc_ref, sh_ref, g_ref, w_ref, u_ref):
    h = _norm_mod(x_ref[...], g_ref[...], sc_ref[...], sh_ref[...])
    u_ref[...] = jnp.dot(h.astype(BF16), w_ref[...], preferred_element_type=F32)


def _ffn_up(x, sc, sh, g, w, *, tm, rows_per_mod):
    r, d = x.shape
    n = w.shape[1]
    mrows = sc.shape[1]
    mod_spec = pl.BlockSpec((None, mrows, d), lambda i: (i // rows_per_mod, 0, 0))
    return pl.pallas_call(
        _ffn_up_kernel,
        grid=(r // tm,),
        in_specs=[pl.BlockSpec((tm, d), lambda i: (i, 0)), mod_spec, mod_spec,
                  pl.BlockSpec((1, d), lambda i: (0, 0)), pl.BlockSpec((d, n), lambda i: (0, 0))],
        out_specs=pl.BlockSpec((tm, n), lambda i: (i, 0)),
        out_shape=jax.ShapeDtypeStruct((r, n), F32),
        compiler_params=_cp("arbitrary"),
        name="ffn_up",
    )(x, sc, sh, g, w)


FFN_CK = 256
FFN_RB = 128


def _ffn_act_down(taps, cw_ref, cb_ref, w_ref):
    dff = w_ref.shape[0]

    def conv(c0):
        u, u1, u2 = taps(c0)
        cs = slice(c0, c0 + FFN_CK)
        return cb_ref[:, cs] + cw_ref[0:1, cs] * u2 + cw_ref[1:2, cs] * u1 + cw_ref[2:3, cs] * u

    out = None
    for c0 in range(0, dff, FFN_CK):
        part = _dot(_silu(conv(c0)) * conv(dff + c0), w_ref[c0:c0 + FFN_CK, :])
        out = part if out is None else out + part
    return out


def _ffn_up_act_kernel(x_ref, sc_ref, sh_ref, g_ref, w_ref, cw_ref, cb_ref, a_ref, tail_ref, carry_ref,
                       u_ref, *, tiles_per_seq):
    i = pl.program_id(0)
    tm = x_ref.shape[0]
    dff = a_ref.shape[1]
    first = i % tiles_per_seq == 0
    h = _norm_mod(x_ref[...], g_ref[...], sc_ref[...], sh_ref[...]).astype(BF16)
    row8 = lax.broadcasted_iota(jnp.int32, (SUBLANE, FFN_CK), 0)

    def up(base, slot, half):
        cs = slice(base, base + FFN_CK)
        u = jnp.dot(h, w_ref[:, cs], preferred_element_type=F32)
        u_ref[slot, half, 0:SUBLANE, :] = jnp.where(first, 0.0, carry_ref[:, cs])
        u_ref[slot, half, SUBLANE:, :] = u
        last = u[tm - SUBLANE:, :]
        carry_ref[:, cs] = last
        tail_ref[:, cs] = last

    def conv(slot, half, base, rb):
        cs = slice(base, base + FFN_CK)
        acc = cb_ref[:, cs]
        for s in range(FFN_CONV):
            r0 = SUBLANE + rb - s
            acc = acc + cw_ref[FFN_CONV - 1 - s:FFN_CONV - s, cs] * u_ref[slot, half, r0:r0 + FFN_RB, :]
        return acc

    for n, c0 in enumerate(range(0, dff, FFN_CK)):
        slot = n % 2
        up(c0, slot, 0)
        up(dff + c0, slot, 1)
        for rb in range(0, tm, FFN_RB):
            a_ref[rb:rb + FFN_RB, c0:c0 + FFN_CK] = (
                _silu(conv(slot, 0, c0, rb)) * conv(slot, 1, dff + c0, rb)).astype(BF16)


def _ffn_up_act(x, sc, sh, g, w, cw, cb, *, tm, seq):
    r, d = x.shape
    n = w.shape[1]
    dff = n // 2
    tiles = seq // tm
    mod_spec = pl.BlockSpec((None, 1, d), lambda i: (i // tiles, 0, 0))
    const = lambda shape: pl.BlockSpec(shape, lambda i: (0,) * len(shape))
    return pl.pallas_call(
        functools.partial(_ffn_up_act_kernel, tiles_per_seq=tiles),
        grid=(r // tm,),
        in_specs=[pl.BlockSpec((tm, d), lambda i: (i, 0)), mod_spec, mod_spec, const((1, d)), const((d, n)),
                  const((FFN_CONV, n)), const((1, n))],
        out_specs=[pl.BlockSpec((tm, dff), lambda i: (i, 0)),
                   pl.BlockSpec((None, SUBLANE, n), lambda i: (i // tiles, 0, 0))],
        out_shape=[jax.ShapeDtypeStruct((r, dff), BF16), jax.ShapeDtypeStruct((r // seq, SUBLANE, n), F32)],
        scratch_shapes=[pltpu.VMEM((SUBLANE, n), F32), pltpu.VMEM((2, 2, SUBLANE + tm, FFN_CK), F32)],
        compiler_params=_cp("arbitrary"),
        name="ffn_up_act",
    )(x, sc, sh, g, w, cw, cb)


def _ffn_down_kernel(x_ref, gate_ref, a_ref, w_ref, o_ref):
    o_ref[...] = x_ref[...] + gate_ref[...] * jnp.dot(a_ref[...], w_ref[...], preferred_element_type=F32)


def _ffn_down(x, gate, a, w, *, tm, seq):
    r, d = x.shape
    tiles = seq // tm
    return pl.pallas_call(
        _ffn_down_kernel,
        grid=(r // tm,),
        in_specs=[pl.BlockSpec((tm, d), lambda i: (i, 0)),
                  pl.BlockSpec((None, 1, d), lambda i: (i // tiles, 0, 0)),
                  pl.BlockSpec((tm, a.shape[1]), lambda i: (i, 0)),
                  pl.BlockSpec(w.shape, lambda i: (0, 0))],
        out_specs=pl.BlockSpec((tm, d), lambda i: (i, 0)),
        out_shape=jax.ShapeDtypeStruct((r, d), F32),
        compiler_params=_cp("arbitrary"),
        name="ffn_down",
    )(x, gate, a, w)


def _ffn_down_step_kernel(x_ref, gate_ref, u_ref, buf_ref, cw_ref, cb_ref, w_ref, o_ref):
    taps = lambda c0: (u_ref[:, c0:c0 + FFN_CK], buf_ref[1, :, c0:c0 + FFN_CK], buf_ref[0, :, c0:c0 + FFN_CK])
    o_ref[...] = x_ref[...] + gate_ref[...] * _ffn_act_down(taps, cw_ref, cb_ref, w_ref)


def _ffn_down_step(x, gate, u, buf_t, cw, cb, w):
    full = lambda a: pl.BlockSpec(a.shape, lambda i: (0,) * a.ndim)
    args = (x, gate, u, buf_t, cw, cb, w)
    return pl.pallas_call(
        _ffn_down_step_kernel,
        grid=(1,),
        in_specs=[full(a) for a in args],
        out_specs=full(x),
        out_shape=jax.ShapeDtypeStruct(x.shape, F32),
        compiler_params=_cp("arbitrary"),
        name="ffn_down_step",
    )(*args)


def _same_segment(n, width):
    i = np.arange(n)
    return jnp.asarray((i[:, None] // width) == (i[None, :] // width), BF16)


def _constants():
    k = np.arange(GLA_DQK)
    v = np.arange(GLA_DVW)
    eexp = (k[:, None] // GLA_DK) == (v[None, :] // GLA_DV)
    j = np.arange(SB_TK)
    later = j[:, None] >= j[None, :]
    uo = np.block([[later, np.zeros_like(later)], [np.zeros_like(later), later]])
    lane = np.arange(LANE)
    hm_pair = np.zeros((SUBLANE, LANE), np.float32)
    hm_pair[0] = lane < SB_HD
    hm_pair[1] = lane >= SB_HD
    p = np.arange(M_DIN)
    e4 = lane[:, None] == (p[None, :] // M_HD)
    return dict(
        eexp=jnp.asarray(eexp, BF16), bd=jnp.asarray(eexp.T, F32),
        uo=jnp.asarray(uo, BF16), hm_pair=jnp.asarray(hm_pair, BF16), eseg64=_same_segment(LANE, 64),
        e4=jnp.asarray(e4, BF16), e4t=jnp.asarray(e4.T, BF16))


def _pad_lanes(v, n=LANE):
    return jnp.zeros((1, n), F32).at[0, :v.shape[0]].set(v)


def _pad_col(v, n=SUBLANE):
    return jnp.zeros((n, 1), F32).at[:v.shape[0], 0].set(v)


def kernel(x_prompt, x_sample, cache_sb_k, cache_sb_v, state_gla, state_mamba_conv, state_mamba_ssm, state_ffn_conv, page_table, c_prompt, c_sample, norm1_g, w_ada, b_ada, w_in, gla_w_gate2, gla_b_gate, gla_norm_g, sb_q_norm_g, sb_k_norm_g, sb_o_norm_g, sb_bias, m_conv_w, m_conv_b, m_dt_bias, m_a_log, m_d, m_norm_g, w_out, norm2_g, ffn_w_up, ffn_conv_w, ffn_conv_b, ffn_w_down):
    bsz, seq, d = x_prompt.shape
    nb = x_sample.shape[0]
    depth = w_in.shape[0]
    dff = ffn_w_down.shape[1]
    n_pool, page = cache_sb_k.shape[1], cache_sb_k.shape[2]
    assert x_sample.shape[1] == 1 and d % LANE == 0 and seq % 256 == 0
    cst = _constants()

    tm = 256
    tm_mm = 512
    tq = 512
    tt_gla = 256
    tt_ssd = 8 * SSD_C
    assert SB_UNROLL == 2 * (tq // SB_TK) or (tq // SB_TK) % SB_UNROLL == 0
    assert page_table.shape[1] % SB_PP == 0
    assert seq % tq == 0 and seq % tm == 0 and seq % tm_mm == 0 and seq % tt_gla == 0 and seq % tt_ssd == 0

    mod = _modulation(jnp.concatenate([c_prompt, c_sample], axis=0), w_ada, b_ada)
    mod = mod.reshape(depth, bsz + nb, N_MOD, d)
    mod_p = mod[:, :bsz].transpose(0, 2, 1, 3).reshape(depth, N_MOD, bsz, 1, d)
    mod_s = mod[:, bsz:].transpose(0, 2, 1, 3).reshape(depth, N_MOD, 1, nb, d)

    cache_kt = cache_sb_k.transpose(0, 1, 3, 4, 2)
    cache_vt = cache_sb_v.transpose(0, 1, 3, 4, 2)

    xp = x_prompt.reshape(bsz * seq, d)
    xs = x_sample.reshape(nb, d)
    w_in_t = w_in.transpose(2, 0, 1)
    outs = {name: [] for name in ("pk", "pv", "pg", "pc", "ph", "pf", "sk", "sv", "sg", "sc", "sh", "sf")}
    for l in range(depth):
        wi = w_in_t[:, l, :]
        o = np.cumsum([0, GLA_DQK, GLA_DQK, GLA_DVW, GLA_DVW, GLA_LR, SB_W, SB_W, SB_W, M_DIN, M_CONV_DIM, M_HEADS])
        sl = lambda a, b: wi[o[a]:o[b]]
        w_re = jnp.concatenate(
            [sl(0, 4), sl(5, 8), sl(8, 10), sl(10, 11), sl(4, 5),
             jnp.zeros((LANE - M_HEADS - GLA_LR, d), F32)], axis=0).astype(BF16)
        wg = jnp.zeros((LANE, LANE), F32).at[M_HEADS:M_HEADS + GLA_LR].set(gla_w_gate2[l]).astype(BF16)
        bg = gla_b_gate[l].reshape(1, GLA_DQK)
        gq = jnp.tile(sb_q_norm_g[l], SB_HEADS).reshape(1, SB_W)
        gk = jnp.tile(sb_k_norm_g[l], SB_HEADS).reshape(1, SB_W)
        go = jnp.tile(sb_o_norm_g[l], SB_HEADS).reshape(1, SB_W)
        gng = jnp.tile(gla_norm_g[l], GLA_HEADS).reshape(1, GLA_DVW)
        g1 = norm1_g[l].reshape(1, d)
        g2 = norm2_g[l].reshape(1, d)
        bias2 = sb_bias[l] * LOG2E
        b_hi = bias2.astype(BF16)
        b_mid = (bias2 - b_hi.astype(F32)).astype(BF16)
        b_lo = (bias2 - b_hi.astype(F32) - b_mid.astype(F32)).astype(BF16)
        pieces = jnp.stack([b_hi, b_mid, b_lo], axis=1)
        bias_cols = jnp.pad(jnp.repeat(pieces, SB_TK, axis=0).reshape(SB_HEADS // 2, 2 * SB_TK, SB_NB),
                            ((0, 0), (0, 0), (0, LANE - SB_NB)))
        bias_col = bias2.reshape(SB_HEADS, 1)
        cw = m_conv_w[l]
        cb = m_conv_b[l].reshape(1, M_CONV_DIM)
        nega = -jnp.exp(m_a_log[l])
        dtb_l, nega_l = _pad_lanes(m_dt_bias[l]), _pad_lanes(nega)
        dtb_c, nega_c = _pad_col(m_dt_bias[l]), _pad_col(nega)
        md = jnp.repeat(m_d[l], M_HD).reshape(1, M_DIN)
        gnm = m_norm_g[l].reshape(1, M_DIN)
        wo = w_out[l].astype(BF16)
        wu = ffn_w_up[l].astype(BF16)
        wd = ffn_w_down[l].astype(BF16)
        fcw = ffn_conv_w[l]
        fcb = ffn_conv_b[l].reshape(1, 2 * dff)
        sh1, sc1, gt1, sh2, sc2, gt2 = range(N_MOD)

        mp = mod_p[l]
        gla, la, q, k, kb, v, vb, m, misc = _in_proj(
            xp, mp[sc1], mp[sh1], g1, w_re, wg, bg, gq, gk, cst["eseg64"], tm=tm_mm, rows_per_mod=seq // tm_mm,
            kv_transposed=True)
        g_mix, g_state = _gla_prompt(gla, la, cst["eexp"], cst["bd"], cst["eseg64"], gng,
                                     bsz=bsz, seq=seq, tt=tt_gla)
        s_mix = _sb_prompt(q, kb, vb, bias_cols, cst["uo"], cst["hm_pair"], cst["eseg64"], go[:, :LANE],
                           bsz=bsz, seq=seq, tq=tq)
        dtt = jnp.pad(misc[:, :M_HEADS].reshape(bsz, seq, M_HEADS).transpose(0, 2, 1),
                      ((0, 0), (0, SUBLANE - M_HEADS), (0, 0)))
        y_mix, h_state = _ssd_prompt(m, misc, dtt, cw, cb, dtb_l, nega_l, dtb_c, nega_c, cst["e4"], cst["e4t"],
                                     md, gnm, bsz=bsz, seq=seq, tt=tt_ssd)
        xp = _out_proj(xp, mp[gt1], g_mix, s_mix, y_mix, wo, tm=tm_mm, rows_per_mod=seq // tm_mm)
        act, u_tail = _ffn_up_act(xp, mp[sc2], mp[sh2], g2, wu, fcw, fcb, tm=tm, seq=seq)
        xp = _ffn_down(xp, mp[gt2], act, wd, tm=tm_mm, seq=seq)

        outs["pk"].append(k.reshape(bsz, SB_HEADS, SB_HD, seq).transpose(0, 3, 1, 2))
        outs["pv"].append(v.reshape(bsz, SB_HEADS, SB_HD, seq).transpose(0, 3, 1, 2))
        gs = g_state.reshape(bsz, GLA_HEADS, GLA_DV, GLA_HEADS, GLA_DK)
        outs["pg"].append(jnp.stack([gs[:, h, :, h, :] for h in range(GLA_HEADS)], axis=1).transpose(0, 1, 3, 2))
        outs["pc"].append(m.reshape(bsz, seq, W_M)[:, seq - (M_CONV - 1):, M_DIN:])
        outs["ph"].append(h_state.reshape(bsz, M_HEADS, M_HD, M_N))
        outs["pf"].append(u_tail[:, SUBLANE - (FFN_CONV - 1):])

        ms = mod_s[l]
        gla, la, q, k, kb, v, vb, m, misc = _in_proj(
            xs, ms[sc1], ms[sh1], g1, w_re, wg, bg, gq, gk, cst["eseg64"], tm=nb, rows_per_mod=1,
            kv_transposed=False)
        g4, g_state, y_mix, h_state = _step_mixers(
            gla, la, gla[:, 2 * GLA_DQK:2 * GLA_DQK + GLA_DVW].reshape(nb, GLA_HEADS, GLA_DV),
            gla[:, 2 * GLA_DQK + GLA_DVW:].reshape(nb, GLA_HEADS, GLA_DV),
            state_gla[l].reshape(nb, GLA_DQK, GLA_DV), m, state_mamba_conv[l], misc,
            state_mamba_ssm[l].reshape(nb, M_DIN, M_N), cw, cb, dtb_l, nega_l, cst["e4"], md,
            gla_norm_g[l].reshape(1, GLA_DV), gnm)
        s_mix = _sb_decode(page_table, q.astype(F32).reshape(nb, SB_HEADS, SB_HD, 1), cache_kt, cache_vt, l,
                           bias_col, sb_o_norm_g[l].reshape(1, SB_HD),
                           jnp.asarray(np.tril(np.ones((page, page), np.float32)), BF16))
        xs = _out_proj(xs, ms[gt1], g4.reshape(nb, GLA_DVW), s_mix, y_mix.reshape(nb, M_DIN), wo,
                       tm=nb, rows_per_mod=1)
        u = _ffn_up(xs, ms[sc2], ms[sh2], g2, wu, tm=nb, rows_per_mod=1)
        xs = _ffn_down_step(xs, ms[gt2][0], u, state_ffn_conv[l].transpose(1, 0, 2), fcw, fcb, wd)

        outs["sk"].append(k.reshape(nb, 1, SB_HEADS, SB_HD))
        outs["sv"].append(v.reshape(nb, 1, SB_HEADS, SB_HD))
        outs["sg"].append(g_state.reshape(nb, GLA_HEADS, GLA_DK, GLA_DV))
        outs["sc"].append(jnp.concatenate([state_mamba_conv[l][:, 1:], m[:, None, M_DIN:]], axis=1))
        outs["sh"].append(h_state.reshape(nb, M_HEADS, M_HD, M_N))
        outs["sf"].append(jnp.concatenate([state_ffn_conv[l][:, 1:], u[:, None, :]], axis=1))

    st = {name: jnp.stack(v) for name, v in outs.items()}
    return (xp.reshape(bsz, seq, d), xs.reshape(nb, 1, d), st["pk"], st["pv"], st["pg"], st["pc"], st["ph"],
            st["pf"], st["sk"], st["sv"], st["sg"], st["sc"], st["sh"], st["sf"])
```

```python
import functools

import jax
import jax.numpy as jnp
import numpy as np
from jax import lax
from jax.experimental import pallas as pl
from jax.experimental.pallas import tpu as pltpu

F32 = jnp.float32
BF16 = jnp.bfloat16

GLA_HEADS, GLA_DK, GLA_DV, GLA_LR, GLA_TAU = 4, 32, 64, 16, 16.0
GLA_DQK = GLA_HEADS * GLA_DK
GLA_DVW = GLA_HEADS * GLA_DV
SB_HEADS, SB_HD = 8, 64
SB_W = SB_HEADS * SB_HD
M_HEADS, M_HD, M_GROUPS, M_N, M_CONV = 4, 64, 2, 128, 4
M_DIN = M_HEADS * M_HD
M_CONV_DIM = M_DIN + 2 * M_GROUPS * M_N
FFN_CONV = 3
N_MOD = 6
EPS = 1e-6
NEG_BIG = -1e30
LOG2E = 1.4426950408889634

LANE = 128
SUBLANE = 8

C_GLA = 0
C_SBQ = C_GLA + 2 * GLA_DQK + 2 * GLA_DVW
C_SBK = C_SBQ + SB_W
C_SBV = C_SBK + SB_W
C_MZ = C_SBV + SB_W
C_MISC = C_MZ + M_DIN + M_CONV_DIM
D_INP = C_MISC + LANE
W_GLA = C_SBQ - C_GLA
W_M = C_MISC - C_MZ

VMEM_LIMIT = 56 * 1024 * 1024


def _cp(*sem):
    return pltpu.CompilerParams(dimension_semantics=sem, vmem_limit_bytes=VMEM_LIMIT)


def _dot(a, b):
    return jnp.dot(a.astype(BF16), b.astype(BF16), preferred_element_type=F32)


def _dot_nt(a, b):
    return lax.dot_general(a.astype(BF16), b.astype(BF16), (((1,), (1,)), ((), ())),
                           preferred_element_type=F32)


def _dot_tn(a, b):
    return lax.dot_general(a.astype(BF16), b.astype(BF16), (((0,), (0,)), ((), ())),
                           preferred_element_type=F32)


def _split3(x):
    hi = x.astype(BF16)
    r = x - hi.astype(F32)
    mid = r.astype(BF16)
    lo = (r - mid.astype(F32)).astype(BF16)
    return hi, mid, lo


def _dot_sel(x, e):
    hi, mid, lo = _split3(x)
    d = lambda p: jnp.dot(p, e, preferred_element_type=F32)
    return d(hi) + d(mid) + d(lo)


def _dot_sel_nt(e, x):
    hi, mid, lo = _split3(x)
    d = lambda p: lax.dot_general(e, p, (((1,), (1,)), ((), ())), preferred_element_type=F32)
    return d(hi) + d(mid) + d(lo)


def _sigmoid(x):
    return 1.0 / (1.0 + jnp.exp(-x))


def _silu(x):
    return x * _sigmoid(x)


def _softplus(x):
    return jnp.maximum(x, 0.0) + jnp.log1p(jnp.exp(-jnp.abs(x)))


def _log_sigmoid(x):
    return jnp.minimum(x, 0.0) - jnp.log1p(jnp.exp(-jnp.abs(x)))


def _rms_rows(x):
    return x * lax.rsqrt(jnp.mean(x * x, axis=-1, keepdims=True) + EPS)


def _seg_rms(x, eseg, width):
    xx = x * x
    hi = xx.astype(BF16)
    lo = (xx - hi.astype(F32)).astype(BF16)
    blocks = []
    for c in range(0, x.shape[1], LANE):
        d = lambda p: jnp.dot(p[:, c:c + LANE], eseg, preferred_element_type=F32)
        blocks.append(d(hi) + d(lo))
    ms = jnp.concatenate(blocks, axis=1) * (1.0 / width)
    return x * lax.rsqrt(ms + EPS)


def _row_to_col(row):
    n = row.shape[1]
    eye = lax.broadcasted_iota(jnp.int32, (n, n), 0) == lax.broadcasted_iota(jnp.int32, (n, n), 1)
    return jnp.sum(jnp.where(eye, row, 0.0), axis=1, keepdims=True)


def _col_to_row(col):
    n = col.shape[0]
    eye = lax.broadcasted_iota(jnp.int32, (n, n), 0) == lax.broadcasted_iota(jnp.int32, (n, n), 1)
    return jnp.sum(jnp.where(eye, col, 0.0), axis=0, keepdims=True)


def _cumsum(x, axis, seg=None):
    n = x.shape[axis] if seg is None else seg
    idx = lax.broadcasted_iota(jnp.int32, x.shape, axis)
    if seg is not None:
        idx = idx % seg
    s = 1
    while s < n:
        x = x + jnp.where(idx >= s, pltpu.roll(x, s, axis), 0.0)
        s *= 2
    return x


def _mod_kernel(c_ref, w_ref, b_ref, o_ref):
    o_ref[...] = _dot(_silu(c_ref[...]), w_ref[...]) + b_ref[...]


def _modulation(c_all, w_ada, b_ada):
    depth, d, nd = w_ada.shape
    n = c_all.shape[0]
    tn = 1024
    return pl.pallas_call(
        _mod_kernel,
        grid=(depth, nd // tn),
        in_specs=[pl.BlockSpec((n, d), lambda l, j: (0, 0)),
                  pl.BlockSpec((None, d, tn), lambda l, j: (l, 0, j)),
                  pl.BlockSpec((None, 1, tn), lambda l, j: (l, 0, j))],
        out_specs=pl.BlockSpec((None, n, tn), lambda l, j: (l, 0, j)),
        out_shape=jax.ShapeDtypeStruct((depth, n, nd), F32),
        compiler_params=_cp("arbitrary", "arbitrary"),
        name="adaln_mod",
    )(c_all, w_ada, b_ada.reshape(depth, 1, nd))


def _norm_mod(x, g, sc, sh):
    return _rms_rows(x) * g * (1.0 + sc) + sh


def _inproj_kernel(x_ref, sc_ref, sh_ref, g_ref, w_ref, wg_ref, bg_ref, gq_ref, gk_ref, eseg_ref,
                   gla_ref, la_ref, q_ref, k_ref, kb_ref, v_ref, vb_ref, m_ref, misc_ref, *, kv_transposed):
    h = _norm_mod(x_ref[...], g_ref[...], sc_ref[...], sh_ref[...])
    p = lax.dot_general(h.astype(BF16), w_ref[...], (((1,), (1,)), ((), ())), preferred_element_type=F32)
    gla_ref[...] = p[:, C_GLA:C_SBQ]
    misc = p[:, C_MISC:D_INP]
    misc_ref[...] = misc
    la_ref[...] = _log_sigmoid(_dot(misc, wg_ref[...]) + bg_ref[...]) * (1.0 / GLA_TAU)
    eseg = eseg_ref[...]
    q = _seg_rms(p[:, C_SBQ:C_SBK], eseg, SB_HD) * gq_ref[...]
    q_ref[...] = (q * (SB_HD ** -0.5 * LOG2E)).astype(BF16)
    k = _seg_rms(p[:, C_SBK:C_SBV], eseg, SB_HD) * gk_ref[...]
    k_ref[...] = k.T if kv_transposed else k
    kb_ref[...] = k.astype(BF16)
    v = p[:, C_SBV:C_MZ]
    v_ref[...] = v.T if kv_transposed else v
    vb_ref[...] = v.astype(BF16)
    m_ref[...] = p[:, C_MZ:C_MISC]


MOD_SH1, MOD_SC1, MOD_GT1, MOD_SH2, MOD_SC2, MOD_GT2 = range(N_MOD)


def _mod_spec(mod, layer, chunk, rows_per_mod):
    return pl.BlockSpec((None, None, None) + mod.shape[3:], lambda i: (layer, chunk, i // rows_per_mod, 0, 0))


def _layer_spec(w, layer):
    return pl.BlockSpec((None,) + w.shape[1:], lambda i: (layer, 0, 0))


def _in_proj(x, mod, layer, g, w, wg, bg, gq, gk, eseg, *, tm, rows_per_mod, kv_transposed):
    r, d = x.shape
    const = lambda shape: pl.BlockSpec(shape, lambda i: (0,) * len(shape))
    row = lambda n: pl.BlockSpec((tm, n), lambda i: (i, 0))
    outs = [(W_GLA, F32), (LANE, F32), (SB_W, BF16), (SB_W, F32), (SB_W, BF16), (SB_W, F32), (SB_W, BF16),
            (W_M, F32), (LANE, F32)]
    out_specs = [row(n) for n, _ in outs]
    out_shape = [jax.ShapeDtypeStruct((r, n), dt) for n, dt in outs]
    if kv_transposed:
        seq = rows_per_mod * tm
        for idx in (3, 5):
            out_specs[idx] = pl.BlockSpec((None, SB_W, tm), lambda i: (i // rows_per_mod, 0, i % rows_per_mod))
            out_shape[idx] = jax.ShapeDtypeStruct((r // seq, SB_W, seq), F32)
    return pl.pallas_call(
        functools.partial(_inproj_kernel, kv_transposed=kv_transposed),
        grid=(r // tm,),
        in_specs=[row(d), _mod_spec(mod, layer, MOD_SC1, rows_per_mod), _mod_spec(mod, layer, MOD_SH1, rows_per_mod),
                  const((1, d)), _layer_spec(w, layer), const((LANE, LANE)),
                  const((1, LANE)), const((1, SB_W)), const((1, SB_W)), const((LANE, LANE))],
        out_specs=out_specs,
        out_shape=out_shape,
        compiler_params=_cp("arbitrary"),
        name="in_proj",
    )(x, mod, mod, g, w, wg, bg, gq, gk, eseg)


GLA_C = SUBLANE
GLA_UNROLL = 32


def _gla_prompt_kernel(gla_ref, la_ref, eexp_ref, bd_ref, eseg_ref, gn_ref, o_ref, st_out_ref,
                       st_ref, b_ref, acc_ref):
    i = pl.program_id(1)
    tt = la_ref.shape[0]

    @pl.when(i == 0)
    def _():
        st_ref[...] = jnp.zeros_like(st_ref)

    b_ref[...] = _cumsum(la_ref[...], 0, seg=GLA_C)
    eexp = eexp_ref[...]
    bd = bd_ref[...]
    rowid = lax.broadcasted_iota(jnp.int32, (GLA_C, GLA_DQK), 0)
    rowid_v = lax.broadcasted_iota(jnp.int32, (GLA_C, GLA_DVW), 0)

    def group(gi, carry):
        steps = []
        for u in range(GLA_UNROLL):
            r0 = pl.multiple_of((gi * GLA_UNROLL + u) * GLA_C, GLA_C)
            bc = b_ref[pl.ds(r0, GLA_C), :]
            qc = gla_ref[pl.ds(r0, GLA_C), 0:GLA_DQK] * (GLA_DK ** -0.5)
            kc = gla_ref[pl.ds(r0, GLA_C), GLA_DQK:2 * GLA_DQK]
            vc = gla_ref[pl.ds(r0, GLA_C), 2 * GLA_DQK:2 * GLA_DQK + GLA_DVW]
            steps.append((r0, bc, qc, kc, vc, bc[GLA_C - 1:GLA_C, :]))
        ps = []
        for r0, bc, qc, kc, vc, bl in steps:
            for t in range(GLA_C):
                d = jnp.where(rowid <= t, bc[t:t + 1, :] - bc, NEG_BIG)
                ps.append(jnp.exp(d) * (qc[t:t + 1, :] * kc))
        sc = _dot(jnp.concatenate(ps, axis=0), eexp)
        upds = [_dot_tn(vc, kc * jnp.exp(bl - bc)) * bd for r0, bc, qc, kc, vc, bl in steps]
        st = st_ref[...]
        for u, (r0, bc, qc, kc, vc, bl) in enumerate(steps):
            o = _dot_nt(qc * jnp.exp(bc), st)
            st = st * jnp.exp(bl) + upds[u]
            for t in range(GLA_C):
                row0 = (u * GLA_C + t) * GLA_C
                ot = jnp.sum(sc[row0:row0 + GLA_C, :] * vc, axis=0, keepdims=True)
                o = o + jnp.where(rowid_v == t, ot, 0.0)
            acc_ref[pl.ds(r0, GLA_C), :] = o
        st_ref[...] = st
        return carry

    lax.fori_loop(0, tt // (GLA_C * GLA_UNROLL), group, 0)
    gg = gla_ref[:, 2 * GLA_DQK + GLA_DVW:2 * GLA_DQK + 2 * GLA_DVW]
    o = _seg_rms(acc_ref[...], eseg_ref[...], GLA_DV) * gn_ref[...] * _silu(gg)
    o_ref[...] = o.astype(BF16)
    st_out_ref[...] = st_ref[...]


def _gla_prompt(gla, la, eexp, bd, eseg, gn, *, bsz, seq, tt):
    nt = seq // tt
    const = lambda shape: pl.BlockSpec(shape, lambda b, i: (0,) * len(shape))
    return pl.pallas_call(
        _gla_prompt_kernel,
        grid=(bsz, nt),
        in_specs=[pl.BlockSpec((tt, W_GLA), lambda b, i: (b * nt + i, 0)),
                  pl.BlockSpec((tt, LANE), lambda b, i: (b * nt + i, 0)),
                  const((GLA_DQK, GLA_DVW)), const((GLA_DVW, GLA_DQK)), const((LANE, LANE)),
                  const((1, GLA_DVW))],
        out_specs=[pl.BlockSpec((tt, GLA_DVW), lambda b, i: (b * nt + i, 0)),
                   pl.BlockSpec((None, GLA_DVW, GLA_DQK), lambda b, i: (b, 0, 0))],
        out_shape=[jax.ShapeDtypeStruct((bsz * seq, GLA_DVW), BF16),
                   jax.ShapeDtypeStruct((bsz, GLA_DVW, GLA_DQK), F32)],
        scratch_shapes=[pltpu.VMEM((GLA_DVW, GLA_DQK), F32), pltpu.VMEM((tt, GLA_DQK), F32),
                        pltpu.VMEM((tt, GLA_DVW), F32)],
        compiler_params=_cp("arbitrary", "arbitrary"),
        name="gla_prompt",
    )(gla, la, eexp, bd, eseg, gn)


SB_TK = LANE
SB_NB = 3
SB_UNROLL = 8


def _neg_abs(x):
    return lax.bitcast_convert_type(lax.bitcast_convert_type(x, jnp.int32) | jnp.int32(-2 ** 31), F32)


def _sb_prompt_kernel(q_ref, k_ref, v_ref, bcol_ref, uo_ref, hm_ref, eseg_ref, gn_ref, o_ref,
                      acc_ref, car_ref, kk_ref, vv_ref, qx_ref):
    i = pl.program_id(2)
    tq = q_ref.shape[0]
    ndiag = tq // SB_TK
    nblk = kk_ref.shape[0]

    @pl.when(i == 0)
    def _():
        m0 = hm_ref[0:1, :]
        m1 = hm_ref[1:2, :]
        bcol = bcol_ref[...]

        def fill(j, carry):
            k0 = pl.multiple_of(j * SB_TK, SB_TK)
            kb = k_ref[pl.ds(k0, SB_TK), :]
            vb = v_ref[pl.ds(k0, SB_TK), :]
            kk_ref[j] = jnp.concatenate([jnp.concatenate([kb * m0, kb * m1], axis=0), bcol], axis=1)
            vv_ref[j] = jnp.concatenate([vb * m0, vb * m1], axis=0)
            return carry

        lax.fori_loop(0, nblk, fill, 0)

    ones = jnp.where(lax.broadcasted_iota(jnp.int32, (tq, LANE), 1) < SB_NB, 1.0, 0.0).astype(BF16)
    qx_ref[...] = jnp.concatenate([q_ref[...], ones], axis=1)
    uo = uo_ref[...]
    acc_ref[...] = jnp.zeros_like(acc_ref)
    car_ref[...] = jnp.zeros_like(car_ref)

    def block(j, r0, diag):
        rows = tq - r0
        z = lax.dot_general(qx_ref[r0:, :], kk_ref[j], (((1,), (1,)), ((), ())),
                            preferred_element_type=F32)
        sp = jnp.maximum(z, 0.0) + jnp.log(1.0 + jnp.exp2(_neg_abs(z))) * LOG2E
        if diag:
            valid = (lax.broadcasted_iota(jnp.int32, (SB_TK, 2 * SB_TK), 1) % SB_TK
                     < lax.broadcasted_iota(jnp.int32, (SB_TK, 2 * SB_TK), 0))
            def mask_head(t):
                head = jnp.where(valid, t[:SB_TK], 0.0)
                return head if rows == SB_TK else jnp.concatenate([head, t[SB_TK:]], axis=0)

            sp = mask_head(sp)
        incl = jnp.dot(sp.astype(BF16), uo, preferred_element_type=F32) + car_ref[r0:, :]
        w = jnp.exp2(z - incl)
        if diag:
            w = mask_head(w)
        car_ref[r0:, :SB_TK] += jnp.sum(sp[:, :SB_TK], axis=1, keepdims=True)
        car_ref[r0:, SB_TK:] += jnp.sum(sp[:, SB_TK:], axis=1, keepdims=True)
        acc_ref[r0:, :] += jnp.dot(w.astype(BF16), vv_ref[j], preferred_element_type=F32)

    for c in range(ndiag - 1, -1, -1):
        block(i * ndiag + c, c * SB_TK, True)

    n_off = i * ndiag
    n_main = n_off // SB_UNROLL

    def body(jj, carry):
        for u in range(SB_UNROLL):
            block(n_off - 1 - jj * SB_UNROLL - u, 0, False)
        return carry

    lax.fori_loop(0, n_main, body, 0)
    if SB_UNROLL > ndiag:
        @pl.when(n_off - n_main * SB_UNROLL > 0)
        def _():
            for u in range(ndiag):
                block(ndiag - 1 - u, 0, False)

    o_ref[...] = (_seg_rms(acc_ref[...], eseg_ref[...], SB_HD) * gn_ref[...]).astype(BF16)


def _sb_prompt(q, k, v, bias_cols, uo, hm, eseg, gn, *, bsz, seq, tq):
    nq = seq // tq
    npair = SB_HEADS // 2
    const = lambda shape: pl.BlockSpec(shape, lambda b, p, i: (0,) * len(shape))
    return pl.pallas_call(
        _sb_prompt_kernel,
        grid=(bsz, npair, nq),
        in_specs=[pl.BlockSpec((tq, LANE), lambda b, p, i: (b * nq + i, p)),
                  pl.BlockSpec((seq, LANE), lambda b, p, i: (b, p)),
                  pl.BlockSpec((seq, LANE), lambda b, p, i: (b, p)),
                  pl.BlockSpec((None, 2 * SB_TK, LANE), lambda b, p, i: (p, 0, 0)),
                  const((2 * SB_TK, 2 * SB_TK)), const((SUBLANE, LANE)), const((LANE, LANE)), const((1, LANE))],
        out_specs=pl.BlockSpec((tq, LANE), lambda b, p, i: (b * nq + i, p)),
        out_shape=jax.ShapeDtypeStruct((bsz * seq, SB_W), BF16),
        scratch_shapes=[pltpu.VMEM((tq, LANE), F32), pltpu.VMEM((tq, 2 * SB_TK), F32),
                        pltpu.VMEM((seq // SB_TK, 2 * SB_TK, 2 * LANE), BF16),
                        pltpu.VMEM((seq // SB_TK, 2 * SB_TK, LANE), BF16),
                        pltpu.VMEM((tq, 2 * LANE), BF16)],
        compiler_params=_cp("arbitrary", "arbitrary", "arbitrary"),
        name="sb_prompt",
    )(q, k, v, bias_cols, uo, hm, eseg, gn)


def _ssm_params(dt_raw, dtb, nega):
    dt = _softplus(dt_raw + dtb)
    return dt, dt * nega


SSD_C = 128


def _ssd_prompt_kernel(m_ref, prev_ref, misc_ref, dtt_ref, cw_ref, cb_ref, dtb_l_ref, nega_l_ref,
                       dtb_c_ref, nega_c_ref, e4_ref, e4t_ref, md_ref, gn_ref, o_ref, hs_out_ref,
                       hs_ref):
    i = pl.program_id(1)
    tt = m_ref.shape[0]
    cl = SSD_C

    @pl.when(i == 0)
    def _():
        hs_ref[...] = jnp.zeros_like(hs_ref)

    e4 = e4_ref[...]
    row = lax.broadcasted_iota(jnp.int32, (cl, M_CONV_DIM), 0)
    row8 = lax.broadcasted_iota(jnp.int32, (SUBLANE, M_CONV_DIM), 0)
    causal = lax.broadcasted_iota(jnp.int32, (cl, cl), 0) >= lax.broadcasted_iota(jnp.int32, (cl, cl), 1)
    lane_head = lax.broadcasted_iota(jnp.int32, (cl, M_DIN), 1) // M_HD
    rep = M_HEADS // M_GROUPS
    half = M_DIN // M_GROUPS

    for r0 in range(0, tt, cl):
        z = m_ref[r0:r0 + cl, 0:M_DIN]
        xbc = m_ref[r0:r0 + cl, M_DIN:M_DIN + M_CONV_DIM]
        if r0 == 0:
            prev = jnp.where(i > 0, prev_ref[:, M_DIN:M_DIN + M_CONV_DIM], 0.0)
        else:
            prev = m_ref[r0 - SUBLANE:r0, M_DIN:M_DIN + M_CONV_DIM]
        acc = cb_ref[...] + cw_ref[M_CONV - 1:M_CONV, :] * xbc
        for s in range(1, M_CONV):
            head = jnp.where(row8 < s, pltpu.roll(prev, s, 0), 0.0)
            head = jnp.concatenate([head, jnp.zeros((cl - SUBLANE, M_CONV_DIM), F32)], axis=0)
            shifted = jnp.where(row < s, head, pltpu.roll(xbc, s, 0))
            acc = acc + cw_ref[M_CONV - 1 - s:M_CONV - s, :] * shifted
        xc = _silu(acc)
        x = xc[:, 0:M_DIN]
        bm = xc[:, M_DIN:M_DIN + M_GROUPS * M_N]
        cm = xc[:, M_DIN + M_GROUPS * M_N:]

        dt_c, a_c = _ssm_params(misc_ref[r0:r0 + cl, :], dtb_l_ref[...], nega_l_ref[...])
        cs_c = _cumsum(a_c, 0)
        dt_r, a_r = _ssm_params(dtt_ref[:, r0:r0 + cl], dtb_c_ref[...], nega_c_ref[...])
        cs_r = _cumsum(a_r, 1)
        xdt = x * _dot_sel(dt_c, e4)
        ecs = _dot_sel(jnp.exp(cs_c), e4)

        g = [_dot_nt(cm[:, gi * M_N:(gi + 1) * M_N], bm[:, gi * M_N:(gi + 1) * M_N]) for gi in range(M_GROUPS)]
        y = jnp.zeros((cl, M_DIN), F32)
        for h in range(M_HEADS):
            decay = jnp.exp(jnp.where(causal, cs_c[:, h:h + 1] - cs_r[h:h + 1, :], NEG_BIG))
            yh = _dot(g[h // rep] * decay, xdt)
            y = jnp.where(lane_head == h, yh, y)

        cs_last = cs_c[cl - 1:cl, :]
        xw = xdt * _dot_sel(jnp.exp(cs_last - cs_c), e4)
        zz = _dot_tn(xw, bm)
        upd = jnp.concatenate([zz[:half, :M_N], zz[half:, M_N:]], axis=0)
        dec = _dot_sel_nt(e4t_ref[...], jnp.broadcast_to(jnp.exp(cs_last), (M_N, LANE)))

        hs = hs_ref[...]
        yi = [_dot_nt(cm[:, gi * M_N:(gi + 1) * M_N], hs) for gi in range(M_GROUPS)]
        hs_ref[...] = hs * dec + upd
        y = y + jnp.where(lane_head < rep, yi[0], yi[1]) * ecs

        y = (y + md_ref[...] * x) * _silu(z)
        y = jnp.concatenate([_rms_rows(y[:, gi * half:(gi + 1) * half]) for gi in range(M_GROUPS)], axis=1)
        o_ref[r0:r0 + cl, :] = (y * gn_ref[...]).astype(BF16)

    hs_out_ref[...] = hs_ref[...]


def _ssd_prompt(m, misc, dtt, cw, cb, dtb_l, nega_l, dtb_c, nega_c, e4, e4t, md, gn, *, bsz, seq, tt):
    nt = seq // tt
    per8 = tt // SUBLANE
    const = lambda shape: pl.BlockSpec(shape, lambda b, i: (0,) * len(shape))
    return pl.pallas_call(
        _ssd_prompt_kernel,
        grid=(bsz, nt),
        in_specs=[pl.BlockSpec((tt, W_M), lambda b, i: (b * nt + i, 0)),
                  pl.BlockSpec((SUBLANE, W_M), lambda b, i: (jnp.maximum((b * nt + i) * per8 - 1, 0), 0)),
                  pl.BlockSpec((tt, LANE), lambda b, i: (b * nt + i, 0)),
                  pl.BlockSpec((None, SUBLANE, tt), lambda b, i: (b, 0, i)),
                  const((M_CONV, M_CONV_DIM)), const((1, M_CONV_DIM)), const((1, LANE)), const((1, LANE)),
                  const((SUBLANE, 1)), const((SUBLANE, 1)), const((LANE, M_DIN)), const((M_DIN, LANE)),
                  const((1, M_DIN)), const((1, M_DIN))],
        out_specs=[pl.BlockSpec((tt, M_DIN), lambda b, i: (b * nt + i, 0)),
                   pl.BlockSpec((None, M_DIN, M_N), lambda b, i: (b, 0, 0))],
        out_shape=[jax.ShapeDtypeStruct((bsz * seq, M_DIN), BF16),
                   jax.ShapeDtypeStruct((bsz, M_DIN, M_N), F32)],
        scratch_shapes=[pltpu.VMEM((M_DIN, M_N), F32)],
        compiler_params=_cp("arbitrary", "arbitrary"),
        name="ssd_prompt",
    )(m, m, misc, dtt, cw, cb, dtb_l, nega_l, dtb_c, nega_c, e4, e4t, md, gn)


def _step_kernel(gla_ref, la_ref, v4_ref, gg4_ref, s0_ref, m_ref, buf_ref, misc_ref, h0_ref,
                 cw_ref, cb_ref, dtb_ref, nega_ref, e4_ref, md_ref, gng_ref, gnm_ref,
                 g_ref, s_ref, y_ref, h_ref):
    q_col = _row_to_col(gla_ref[:, 0:GLA_DQK] * (GLA_DK ** -0.5))
    k_col = _row_to_col(gla_ref[:, GLA_DQK:2 * GLA_DQK])
    dec_col = _row_to_col(jnp.exp(la_ref[...]))
    v4 = v4_ref[...]
    v_exp = jnp.concatenate([jnp.broadcast_to(v4[h:h + 1, :], (GLA_DK, GLA_DV)) for h in range(GLA_HEADS)],
                            axis=0)
    s = s0_ref[...] * dec_col + k_col * v_exp
    s_ref[...] = s
    o4 = jnp.sum((q_col * s).reshape(GLA_HEADS, GLA_DK, GLA_DV), axis=1)
    g_ref[...] = _rms_rows(o4) * gng_ref[...] * _silu(gg4_ref[...])

    z = m_ref[:, 0:M_DIN]
    acc = cb_ref[...] + cw_ref[M_CONV - 1:M_CONV, :] * m_ref[:, M_DIN:M_DIN + M_CONV_DIM]
    for s_ in range(M_CONV - 1):
        acc = acc + cw_ref[s_:s_ + 1, :] * buf_ref[s_:s_ + 1, :]
    xc = _silu(acc)
    x = xc[:, 0:M_DIN]
    bm = xc[:, M_DIN:M_DIN + M_GROUPS * M_N]
    cm = xc[:, M_DIN + M_GROUPS * M_N:]
    dt, a = _ssm_params(misc_ref[...], dtb_ref[...], nega_ref[...])
    e4 = e4_ref[...]
    xdt_col = _row_to_col(x * _dot_sel(dt, e4))
    deca_col = _row_to_col(_dot_sel(jnp.exp(a), e4))
    half = M_DIN // M_GROUPS
    spread = lambda t: jnp.concatenate(
        [jnp.broadcast_to(t[:, gi * M_N:(gi + 1) * M_N], (half, M_N)) for gi in range(M_GROUPS)], axis=0)
    hs = h0_ref[...] * deca_col + xdt_col * spread(bm)
    h_ref[...] = hs
    y = _col_to_row(jnp.sum(hs * spread(cm), axis=1, keepdims=True))
    y = (y + md_ref[...] * x) * _silu(z)
    y = jnp.concatenate([_rms_rows(y[:, gi * half:(gi + 1) * half]) for gi in range(M_GROUPS)], axis=1)
    y_ref[...] = y * gnm_ref[...]


def _step_mixers(gla, la, v4, gg4, s0, m, buf, misc, h0, cw, cb, dtb, nega, e4, md, gng, gnm):
    nb = gla.shape[0]
    per = lambda *shape: pl.BlockSpec((None,) + shape, lambda b: (b,) + (0,) * len(shape))
    const = lambda shape: pl.BlockSpec(shape, lambda b: (0,) * len(shape))
    return pl.pallas_call(
        _step_kernel,
        grid=(nb,),
        in_specs=[per(1, W_GLA), per(1, LANE), per(GLA_HEADS, GLA_DV), per(GLA_HEADS, GLA_DV),
                  per(GLA_DQK, GLA_DV), per(1, W_M), per(M_CONV - 1, M_CONV_DIM), per(1, LANE),
                  per(M_DIN, M_N),
                  const((M_CONV, M_CONV_DIM)), const((1, M_CONV_DIM)), const((1, LANE)), const((1, LANE)),
                  const((LANE, M_DIN)), const((1, M_DIN)), const((1, GLA_DV)), const((1, M_DIN))],
        out_specs=[per(GLA_HEADS, GLA_DV), per(GLA_DQK, GLA_DV), per(1, M_DIN), per(M_DIN, M_N)],
        out_shape=[jax.ShapeDtypeStruct((nb, GLA_HEADS, GLA_DV), F32),
                   jax.ShapeDtypeStruct((nb, GLA_DQK, GLA_DV), F32),
                   jax.ShapeDtypeStruct((nb, 1, M_DIN), F32),
                   jax.ShapeDtypeStruct((nb, M_DIN, M_N), F32)],
        compiler_params=_cp("arbitrary"),
        name="step_mixers",
    )(gla.reshape(nb, 1, W_GLA), la.reshape(nb, 1, LANE), v4, gg4, s0, m.reshape(nb, 1, W_M), buf,
      misc.reshape(nb, 1, LANE), h0, cw, cb, dtb, nega, e4, md, gng, gnm)


SB_PP = 16


def _sb_decode_kernel(pt_ref, q_ref, *refs):
    ks = refs[0:SB_PP]
    vs = refs[SB_PP:2 * SB_PP]
    bias_ref, gn_ref, uinc_ref, o_ref, acc_ref, car_ref, qb_ref = refs[2 * SB_PP:]
    j = pl.program_id(1)
    page = ks[0].shape[2]
    uinc = uinc_ref[...]

    @pl.when(j == 0)
    def _():
        acc_ref[...] = jnp.zeros_like(acc_ref)
        car_ref[...] = jnp.zeros_like(car_ref)
        qb_ref[...] = jnp.broadcast_to(q_ref[...], qb_ref.shape)

    bias = bias_ref[...]
    for r in range(SB_PP - 1, -1, -1):
        prod = (ks[r][...] * qb_ref[...]).reshape(SB_HEADS, SB_HD // SUBLANE, SUBLANE, page)
        z = jnp.sum(jnp.sum(prod, axis=1), axis=1) + bias
        sp = jnp.maximum(z, 0.0) + jnp.log(1.0 + jnp.exp2(_neg_abs(z))) * LOG2E
        ls = z - sp
        hi = sp.astype(BF16)
        lo = (sp - hi.astype(F32)).astype(BF16)
        incl = (jnp.dot(hi, uinc, preferred_element_type=F32) + jnp.dot(lo, uinc, preferred_element_type=F32))
        car = car_ref[...]
        w = jnp.exp2(ls - (incl - sp + car))
        car_ref[...] = car + incl[:, 0:1]
        for h in range(SB_HEADS):
            acc_ref[h] += vs[r][h] * w[h:h + 1, :]

    @pl.when(j == pl.num_programs(1) - 1)
    def _():
        o_ref[...] = _rms_rows(jnp.sum(acc_ref[...], axis=-1)) * gn_ref[...]


def _sb_decode(page_table, q, cache_kt, cache_vt, layer, bias_col, gn, uinc):
    nb, n_pages = page_table.shape
    page = cache_kt.shape[4]
    ng = n_pages // SB_PP
    pt = page_table.reshape(-1)

    def kv_spec(r):
        return pl.BlockSpec((None, None, SB_HEADS, SB_HD, page),
                            lambda b, j, pt_ref: (layer, pt_ref[b * n_pages + (ng - 1 - j) * SB_PP + r], 0, 0, 0))

    const = lambda shape: pl.BlockSpec(shape, lambda b, j, pt_ref: (0,) * len(shape))
    grid_spec = pltpu.PrefetchScalarGridSpec(
        num_scalar_prefetch=1,
        grid=(nb, ng),
        in_specs=[pl.BlockSpec((None, SB_HEADS, SB_HD, 1), lambda b, j, pt_ref: (b, 0, 0, 0))]
        + [kv_spec(r) for r in range(SB_PP)] + [kv_spec(r) for r in range(SB_PP)]
        + [const((SB_HEADS, 1)), const((1, SB_HD)), const((page, page))],
        out_specs=pl.BlockSpec((None, SB_HEADS, SB_HD), lambda b, j, pt_ref: (b, 0, 0)),
        scratch_shapes=[pltpu.VMEM((SB_HEADS, SB_HD, page), F32), pltpu.VMEM((SB_HEADS, 1), F32),
                        pltpu.VMEM((SB_HEADS, SB_HD, page), F32)],
    )
    out = pl.pallas_call(
        _sb_decode_kernel,
        grid_spec=grid_spec,
        out_shape=jax.ShapeDtypeStruct((nb, SB_HEADS, SB_HD), F32),
        compiler_params=_cp("arbitrary", "arbitrary"),
        name="sb_decode",
    )(pt, q, *([cache_kt] * SB_PP), *([cache_vt] * SB_PP), bias_col, gn, uinc)
    return out.reshape(nb, SB_W)


def _outproj_kernel(x_ref, gate_ref, g_ref, s_ref, y_ref, w_ref, o_ref):
    mix = (_dot(g_ref[...], w_ref[0:GLA_DVW, :]) + _dot(s_ref[...], w_ref[GLA_DVW:GLA_DVW + SB_W, :])
           + _dot(y_ref[...], w_ref[GLA_DVW + SB_W:, :]))
    o_ref[...] = x_ref[...] + gate_ref[...] * mix


def _out_proj(x, mod, layer, g, s, y, w, *, tm, rows_per_mod):
    r, d = x.shape
    row = lambda n: pl.BlockSpec((tm, n), lambda i: (i, 0))
    return pl.pallas_call(
        _outproj_kernel,
        grid=(r // tm,),
        in_specs=[row(d), _mod_spec(mod, layer, MOD_GT1, rows_per_mod),
                  row(GLA_DVW), row(SB_W), row(M_DIN), _layer_spec(w, layer)],
        out_specs=row(d),
        out_shape=jax.ShapeDtypeStruct((r, d), F32),
        compiler_params=_cp("arbitrary"),
        name="out_proj",
    )(x, mod, g, s, y, w)


def _ffn_up_kernel(x_ref, sc_ref, sh_ref, g_ref, w_ref, u_ref):
    h = _norm_mod(x_ref[...], g_ref[...], sc_ref[...], sh_ref[...])
    u_ref[...] = jnp.dot(h.astype(BF16), w_ref[...], preferred_element_type=F32)


def _ffn_up(x, mod, layer, g, w, *, tm, rows_per_mod):
    r, d = x.shape
    n = w.shape[2]
    return pl.pallas_call(
        _ffn_up_kernel,
        grid=(r // tm,),
        in_specs=[pl.BlockSpec((tm, d), lambda i: (i, 0)), _mod_spec(mod, layer, MOD_SC2, rows_per_mod),
                  _mod_spec(mod, layer, MOD_SH2, rows_per_mod),
                  pl.BlockSpec((1, d), lambda i: (0, 0)), _layer_spec(w, layer)],
        out_specs=pl.BlockSpec((tm, n), lambda i: (i, 0)),
        out_shape=jax.ShapeDtypeStruct((r, n), F32),
        compiler_params=_cp("arbitrary"),
        name="ffn_up",
    )(x, mod, mod, g, w)


FFN_CK = 256
FFN_RB = 128


def _ffn_act_down(taps, cw_ref, cb_ref, w_ref):
    dff = w_ref.shape[0]

    def conv(c0):
        u, u1, u2 = taps(c0)
        cs = slice(c0, c0 + FFN_CK)
        return cb_ref[:, cs] + cw_ref[0:1, cs] * u2 + cw_ref[1:2, cs] * u1 + cw_ref[2:3, cs] * u

    out = None
    for c0 in range(0, dff, FFN_CK):
        part = _dot(_silu(conv(c0)) * conv(dff + c0), w_ref[c0:c0 + FFN_CK, :])
        out = part if out is None else out + part
    return out


def _ffn_up_act_kernel(x_ref, sc_ref, sh_ref, g_ref, w_ref, cw_ref, cb_ref, a_ref, tail_ref, carry_ref,
                       u_ref, *, tiles_per_seq):
    i = pl.program_id(0)
    tm = x_ref.shape[0]
    dff = a_ref.shape[1]
    first = i % tiles_per_seq == 0
    h = _norm_mod(x_ref[...], g_ref[...], sc_ref[...], sh_ref[...]).astype(BF16)
    row8 = lax.broadcasted_iota(jnp.int32, (SUBLANE, FFN_CK), 0)

    def up(base, slot, half):
        cs = slice(base, base + FFN_CK)
        u = jnp.dot(h, w_ref[:, cs], preferred_element_type=F32)
        u_ref[slot, half, 0:SUBLANE, :] = jnp.where(first, 0.0, carry_ref[:, cs])
        u_ref[slot, half, SUBLANE:, :] = u
        last = u[tm - SUBLANE:, :]
        carry_ref[:, cs] = last
        tail_ref[:, cs] = last

    def conv(slot, half, base, rb):
        cs = slice(base, base + FFN_CK)
        acc = cb_ref[:, cs]
        for s in range(FFN_CONV):
            r0 = SUBLANE + rb - s
            acc = acc + cw_ref[FFN_CONV - 1 - s:FFN_CONV - s, cs] * u_ref[slot, half, r0:r0 + FFN_RB, :]
        return acc

    for n, c0 in enumerate(range(0, dff, FFN_CK)):
        slot = n % 2
        up(c0, slot, 0)
        up(dff + c0, slot, 1)
        for rb in range(0, tm, FFN_RB):
            a_ref[rb:rb + FFN_RB, c0:c0 + FFN_CK] = (
                _silu(conv(slot, 0, c0, rb)) * conv(slot, 1, dff + c0, rb)).astype(BF16)


def _ffn_up_act(x, mod, layer, g, w, cw, cb, *, tm, seq):
    r, d = x.shape
    n = w.shape[2]
    dff = n // 2
    tiles = seq // tm
    const = lambda shape: pl.BlockSpec(shape, lambda i: (0,) * len(shape))
    return pl.pallas_call(
        functools.partial(_ffn_up_act_kernel, tiles_per_seq=tiles),
        grid=(r // tm,),
        in_specs=[pl.BlockSpec((tm, d), lambda i: (i, 0)), _mod_spec(mod, layer, MOD_SC2, tiles),
                  _mod_spec(mod, layer, MOD_SH2, tiles), const((1, d)), _layer_spec(w, layer),
                  const((FFN_CONV, n)), const((1, n))],
        out_specs=[pl.BlockSpec((tm, dff), lambda i: (i, 0)),
                   pl.BlockSpec((None, SUBLANE, n), lambda i: (i // tiles, 0, 0))],
        out_shape=[jax.ShapeDtypeStruct((r, dff), BF16), jax.ShapeDtypeStruct((r // seq, SUBLANE, n), F32)],
        scratch_shapes=[pltpu.VMEM((SUBLANE, n), F32), pltpu.VMEM((2, 2, SUBLANE + tm, FFN_CK), F32)],
        compiler_params=_cp("arbitrary"),
        name="ffn_up_act",
    )(x, mod, mod, g, w, cw, cb)


def _ffn_down_kernel(x_ref, gate_ref, a_ref, w_ref, o_ref):
    o_ref[...] = x_ref[...] + gate_ref[...] * jnp.dot(a_ref[...], w_ref[...], preferred_element_type=F32)


def _ffn_down(x, mod, layer, a, w, *, tm, seq):
    r, d = x.shape
    return pl.pallas_call(
        _ffn_down_kernel,
        grid=(r // tm,),
        in_specs=[pl.BlockSpec((tm, d), lambda i: (i, 0)), _mod_spec(mod, layer, MOD_GT2, seq // tm),
                  pl.BlockSpec((tm, a.shape[1]), lambda i: (i, 0)), _layer_spec(w, layer)],
        out_specs=pl.BlockSpec((tm, d), lambda i: (i, 0)),
        out_shape=jax.ShapeDtypeStruct((r, d), F32),
        compiler_params=_cp("arbitrary"),
        name="ffn_down",
    )(x, mod, a, w)


def _ffn_down_step_kernel(x_ref, gate_ref, u_ref, buf_ref, cw_ref, cb_ref, w_ref, o_ref):
    taps = lambda c0: (u_ref[:, c0:c0 + FFN_CK], buf_ref[1, :, c0:c0 + FFN_CK], buf_ref[0, :, c0:c0 + FFN_CK])
    o_ref[...] = x_ref[...] + gate_ref[...] * _ffn_act_down(taps, cw_ref, cb_ref, w_ref)


def _ffn_down_step(x, mod, layer, u, buf_t, cw, cb, w):
    full = lambda a: pl.BlockSpec(a.shape, lambda i: (0,) * a.ndim)
    args = (x, mod, u, buf_t, cw, cb, w)
    return pl.pallas_call(
        _ffn_down_step_kernel,
        grid=(1,),
        in_specs=[full(x), _mod_spec(mod, layer, MOD_GT2, 1), full(u), full(buf_t), full(cw), full(cb),
                  _layer_spec(w, layer)],
        out_specs=full(x),
        out_shape=jax.ShapeDtypeStruct(x.shape, F32),
        compiler_params=_cp("arbitrary"),
        name="ffn_down_step",
    )(*args)


def _same_segment(n, width):
    i = np.arange(n)
    return jnp.asarray((i[:, None] // width) == (i[None, :] // width), BF16)


def _constants():
    k = np.arange(GLA_DQK)
    v = np.arange(GLA_DVW)
    eexp = (k[:, None] // GLA_DK) == (v[None, :] // GLA_DV)
    j = np.arange(SB_TK)
    later = j[:, None] >= j[None, :]
    uo = np.block([[later, np.zeros_like(later)], [np.zeros_like(later), later]])
    lane = np.arange(LANE)
    hm_pair = np.zeros((SUBLANE, LANE), np.float32)
    hm_pair[0] = lane < SB_HD
    hm_pair[1] = lane >= SB_HD
    p = np.arange(M_DIN)
    e4 = lane[:, None] == (p[None, :] // M_HD)
    return dict(
        eexp=jnp.asarray(eexp, BF16), bd=jnp.asarray(eexp.T, F32),
        uo=jnp.asarray(uo, BF16), hm_pair=jnp.asarray(hm_pair, BF16), eseg64=_same_segment(LANE, 64),
        e4=jnp.asarray(e4, BF16), e4t=jnp.asarray(e4.T, BF16))


def _pad_lanes(v, n=LANE):
    return jnp.zeros((1, n), F32).at[0, :v.shape[0]].set(v)


def _pad_col(v, n=SUBLANE):
    return jnp.zeros((n, 1), F32).at[:v.shape[0], 0].set(v)


def kernel(x_prompt, x_sample, cache_sb_k, cache_sb_v, state_gla, state_mamba_conv, state_mamba_ssm, state_ffn_conv, page_table, c_prompt, c_sample, norm1_g, w_ada, b_ada, w_in, gla_w_gate2, gla_b_gate, gla_norm_g, sb_q_norm_g, sb_k_norm_g, sb_o_norm_g, sb_bias, m_conv_w, m_conv_b, m_dt_bias, m_a_log, m_d, m_norm_g, w_out, norm2_g, ffn_w_up, ffn_conv_w, ffn_conv_b, ffn_w_down):
    bsz, seq, d = x_prompt.shape
    nb = x_sample.shape[0]
    depth = w_in.shape[0]
    dff = ffn_w_down.shape[1]
    n_pool, page = cache_sb_k.shape[1], cache_sb_k.shape[2]
    assert x_sample.shape[1] == 1 and d % LANE == 0 and seq % 256 == 0
    cst = _constants()

    tm = 256
    tm_mm = 512
    tq = 512
    tt_gla = 256
    tt_ssd = 8 * SSD_C
    assert SB_UNROLL == 2 * (tq // SB_TK) or (tq // SB_TK) % SB_UNROLL == 0
    assert page_table.shape[1] % SB_PP == 0
    assert seq % tq == 0 and seq % tm == 0 and seq % tm_mm == 0 and seq % tt_gla == 0 and seq % tt_ssd == 0

    mod = _modulation(jnp.concatenate([c_prompt, c_sample], axis=0), w_ada, b_ada)
    mod = mod.reshape(depth, bsz + nb, N_MOD, d)
    mod_p = mod[:, :bsz].transpose(0, 2, 1, 3).reshape(depth, N_MOD, bsz, 1, d)
    mod_s = mod[:, bsz:].transpose(0, 2, 1, 3).reshape(depth, N_MOD, 1, nb, d)

    cache_kt = cache_sb_k.transpose(0, 1, 3, 4, 2)
    cache_vt = cache_sb_v.transpose(0, 1, 3, 4, 2)

    xp = x_prompt.reshape(bsz * seq, d)
    xs = x_sample.reshape(nb, d)
    w_in_t = w_in.transpose(2, 0, 1)
    o = np.cumsum([0, GLA_DQK, GLA_DQK, GLA_DVW, GLA_DVW, GLA_LR, SB_W, SB_W, SB_W, M_DIN, M_CONV_DIM, M_HEADS])
    sl = lambda a, b: w_in_t[o[a]:o[b]]
    w_re = jnp.concatenate(
        [sl(0, 4), sl(5, 8), sl(8, 10), sl(10, 11), sl(4, 5),
         jnp.zeros((LANE - M_HEADS - GLA_LR, depth, d), F32)], axis=0).astype(BF16).transpose(1, 0, 2)
    wo = w_out.astype(BF16)
    wu = ffn_w_up.astype(BF16)
    wd = ffn_w_down.astype(BF16)
    outs = {name: [] for name in ("pk", "pv", "pg", "pc", "ph", "pf", "sk", "sv", "sg", "sc", "sh", "sf")}
    for l in range(depth):
        wg = jnp.zeros((LANE, LANE), F32).at[M_HEADS:M_HEADS + GLA_LR].set(gla_w_gate2[l]).astype(BF16)
        bg = gla_b_gate[l].reshape(1, GLA_DQK)
        gq = jnp.tile(sb_q_norm_g[l], SB_HEADS).reshape(1, SB_W)
        gk = jnp.tile(sb_k_norm_g[l], SB_HEADS).reshape(1, SB_W)
        go = jnp.tile(sb_o_norm_g[l], SB_HEADS).reshape(1, SB_W)
        gng = jnp.tile(gla_norm_g[l], GLA_HEADS).reshape(1, GLA_DVW)
        g1 = norm1_g[l].reshape(1, d)
        g2 = norm2_g[l].reshape(1, d)
        bias2 = sb_bias[l] * LOG2E
        b_hi = bias2.astype(BF16)
        b_mid = (bias2 - b_hi.astype(F32)).astype(BF16)
        b_lo = (bias2 - b_hi.astype(F32) - b_mid.astype(F32)).astype(BF16)
        pieces = jnp.stack([b_hi, b_mid, b_lo], axis=1)
        bias_cols = jnp.pad(jnp.repeat(pieces, SB_TK, axis=0).reshape(SB_HEADS // 2, 2 * SB_TK, SB_NB),
                            ((0, 0), (0, 0), (0, LANE - SB_NB)))
        bias_col = bias2.reshape(SB_HEADS, 1)
        cw = m_conv_w[l]
        cb = m_conv_b[l].reshape(1, M_CONV_DIM)
        nega = -jnp.exp(m_a_log[l])
        dtb_l, nega_l = _pad_lanes(m_dt_bias[l]), _pad_lanes(nega)
        dtb_c, nega_c = _pad_col(m_dt_bias[l]), _pad_col(nega)
        md = jnp.repeat(m_d[l], M_HD).reshape(1, M_DIN)
        gnm = m_norm_g[l].reshape(1, M_DIN)
        fcw = ffn_conv_w[l]
        fcb = ffn_conv_b[l].reshape(1, 2 * dff)

        gla, la, q, k, kb, v, vb, m, misc = _in_proj(
            xp, mod_p, l, g1, w_re, wg, bg, gq, gk, cst["eseg64"], tm=tm_mm, rows_per_mod=seq // tm_mm,
            kv_transposed=True)
        g_mix, g_state = _gla_prompt(gla, la, cst["eexp"], cst["bd"], cst["eseg64"], gng,
                                     bsz=bsz, seq=seq, tt=tt_gla)
        s_mix = _sb_prompt(q, kb, vb, bias_cols, cst["uo"], cst["hm_pair"], cst["eseg64"], go[:, :LANE],
                           bsz=bsz, seq=seq, tq=tq)
        dtt = jnp.pad(misc[:, :M_HEADS].reshape(bsz, seq, M_HEADS).transpose(0, 2, 1),
                      ((0, 0), (0, SUBLANE - M_HEADS), (0, 0)))
        y_mix, h_state = _ssd_prompt(m, misc, dtt, cw, cb, dtb_l, nega_l, dtb_c, nega_c, cst["e4"], cst["e4t"],
                                     md, gnm, bsz=bsz, seq=seq, tt=tt_ssd)
        xp = _out_proj(xp, mod_p, l, g_mix, s_mix, y_mix, wo, tm=tm_mm, rows_per_mod=seq // tm_mm)
        act, u_tail = _ffn_up_act(xp, mod_p, l, g2, wu, fcw, fcb, tm=tm, seq=seq)
        xp = _ffn_down(xp, mod_p, l, act, wd, tm=tm_mm, seq=seq)

        outs["pk"].append(k.reshape(bsz, SB_HEADS, SB_HD, seq).transpose(0, 3, 1, 2))
        outs["pv"].append(v.reshape(bsz, SB_HEADS, SB_HD, seq).transpose(0, 3, 1, 2))
        gs = g_state.reshape(bsz, GLA_HEADS, GLA_DV, GLA_HEADS, GLA_DK)
        outs["pg"].append(jnp.stack([gs[:, h, :, h, :] for h in range(GLA_HEADS)], axis=1).transpose(0, 1, 3, 2))
        outs["pc"].append(m.reshape(bsz, seq, W_M)[:, seq - (M_CONV - 1):, M_DIN:])
        outs["ph"].append(h_state.reshape(bsz, M_HEADS, M_HD, M_N))
        outs["pf"].append(u_tail[:, SUBLANE - (FFN_CONV - 1):])

        gla, la, q, k, kb, v, vb, m, misc = _in_proj(
            xs, mod_s, l, g1, w_re, wg, bg, gq, gk, cst["eseg64"], tm=nb, rows_per_mod=1,
            kv_transposed=False)
        g4, g_state, y_mix, h_state = _step_mixers(
            gla, la, gla[:, 2 * GLA_DQK:2 * GLA_DQK + GLA_DVW].reshape(nb, GLA_HEADS, GLA_DV),
            gla[:, 2 * GLA_DQK + GLA_DVW:].reshape(nb, GLA_HEADS, GLA_DV),
            state_gla[l].reshape(nb, GLA_DQK, GLA_DV), m, state_mamba_conv[l], misc,
            state_mamba_ssm[l].reshape(nb, M_DIN, M_N), cw, cb, dtb_l, nega_l, cst["e4"], md,
            gla_norm_g[l].reshape(1, GLA_DV), gnm)
        s_mix = _sb_decode(page_table, q.astype(F32).reshape(nb, SB_HEADS, SB_HD, 1), cache_kt, cache_vt, l,
                           bias_col, sb_o_norm_g[l].reshape(1, SB_HD),
                           jnp.asarray(np.tril(np.ones((page, page), np.float32)), BF16))
        xs = _out_proj(xs, mod_s, l, g4.reshape(nb, GLA_DVW), s_mix, y_mix.reshape(nb, M_DIN), wo,
                       tm=nb, rows_per_mod=1)
        u = _ffn_up(xs, mod_s, l, g2, wu, tm=nb, rows_per_mod=1)
        xs = _ffn_down_step(xs, mod_s, l, u, state_ffn_conv[l].transpose(1, 0, 2), fcw, fcb, wd)

        outs["sk"].append(k.reshape(nb, 1, SB_HEADS, SB_HD))
        outs["sv"].append(v.reshape(nb, 1, SB_HEADS, SB_HD))
        outs["sg"].append(g_state.reshape(nb, GLA_HEADS, GLA_DK, GLA_DV))
        outs["sc"].append(jnp.concatenate([state_mamba_conv[l][:, 1:], m[:, None, M_DIN:]], axis=1))
        outs["sh"].append(h_state.reshape(nb, M_HEADS, M_HD, M_N))
        outs["sf"].append(jnp.concatenate([state_ffn_conv[l][:, 1:], u[:, None, :]], axis=1))

    st = {name: jnp.stack(v) for name, v in outs.items()}
    return (xp.reshape(bsz, seq, d), xs.reshape(nb, 1, d), st["pk"], st["pv"], st["pg"], st["pc"], st["ph"],
            st["pf"], st["sk"], st["sv"], st["sg"], st["sc"], st["sh"], st["sf"])
```

```python
import functools

import jax
import jax.numpy as jnp
import numpy as np
from jax import lax
from jax.experimental import pallas as pl
from jax.experimental.pallas import tpu as pltpu

F32 = jnp.float32
BF16 = jnp.bfloat16

GLA_HEADS, GLA_DK, GLA_DV, GLA_LR, GLA_TAU = 4, 32, 64, 16, 16.0
GLA_DQK = GLA_HEADS * GLA_DK
GLA_DVW = GLA_HEADS * GLA_DV
SB_HEADS, SB_HD = 8, 64
SB_W = SB_HEADS * SB_HD
M_HEADS, M_HD, M_GROUPS, M_N, M_CONV = 4, 64, 2, 128, 4
M_DIN = M_HEADS * M_HD
M_CONV_DIM = M_DIN + 2 * M_GROUPS * M_N
FFN_CONV = 3
N_MOD = 6
EPS = 1e-6
NEG_BIG = -1e30
LOG2E = 1.4426950408889634

LANE = 128
SUBLANE = 8

C_GLA = 0
C_SBQ = C_GLA + 2 * GLA_DQK + 2 * GLA_DVW
C_SBK = C_SBQ + SB_W
C_SBV = C_SBK + SB_W
C_MZ = C_SBV + SB_W
C_MISC = C_MZ + M_DIN + M_CONV_DIM
D_INP = C_MISC + LANE
W_GLA = C_SBQ - C_GLA
W_M = C_MISC - C_MZ

VMEM_LIMIT = 56 * 1024 * 1024


def _cp(*sem):
    return pltpu.CompilerParams(dimension_semantics=sem, vmem_limit_bytes=VMEM_LIMIT)


def _dot(a, b):
    return jnp.dot(a.astype(BF16), b.astype(BF16), preferred_element_type=F32)


def _dot_nt(a, b):
    return lax.dot_general(a.astype(BF16), b.astype(BF16), (((1,), (1,)), ((), ())),
                           preferred_element_type=F32)


def _dot_tn(a, b):
    return lax.dot_general(a.astype(BF16), b.astype(BF16), (((0,), (0,)), ((), ())),
                           preferred_element_type=F32)


def _split3(x):
    hi = x.astype(BF16)
    r = x - hi.astype(F32)
    mid = r.astype(BF16)
    lo = (r - mid.astype(F32)).astype(BF16)
    return hi, mid, lo


def _dot_sel(x, e):
    hi, mid, lo = _split3(x)
    d = lambda p: jnp.dot(p, e, preferred_element_type=F32)
    return d(hi) + d(mid) + d(lo)


def _dot_sel_nt(e, x):
    hi, mid, lo = _split3(x)
    d = lambda p: lax.dot_general(e, p, (((1,), (1,)), ((), ())), preferred_element_type=F32)
    return d(hi) + d(mid) + d(lo)


def _sigmoid(x):
    return 1.0 / (1.0 + jnp.exp(-x))


def _silu(x):
    return x * _sigmoid(x)


def _softplus(x):
    return jnp.maximum(x, 0.0) + jnp.log1p(jnp.exp(-jnp.abs(x)))


def _log_sigmoid(x):
    return jnp.minimum(x, 0.0) - jnp.log1p(jnp.exp(-jnp.abs(x)))


def _rms_rows(x):
    return x * lax.rsqrt(jnp.mean(x * x, axis=-1, keepdims=True) + EPS)


def _seg_rms(x, eseg, width):
    xx = x * x
    hi = xx.astype(BF16)
    lo = (xx - hi.astype(F32)).astype(BF16)
    blocks = []
    for c in range(0, x.shape[1], LANE):
        d = lambda p: jnp.dot(p[:, c:c + LANE], eseg, preferred_element_type=F32)
        blocks.append(d(hi) + d(lo))
    ms = jnp.concatenate(blocks, axis=1) * (1.0 / width)
    return x * lax.rsqrt(ms + EPS)


def _row_to_col(row):
    n = row.shape[1]
    eye = lax.broadcasted_iota(jnp.int32, (n, n), 0) == lax.broadcasted_iota(jnp.int32, (n, n), 1)
    return jnp.sum(jnp.where(eye, row, 0.0), axis=1, keepdims=True)


def _col_to_row(col):
    n = col.shape[0]
    eye = lax.broadcasted_iota(jnp.int32, (n, n), 0) == lax.broadcasted_iota(jnp.int32, (n, n), 1)
    return jnp.sum(jnp.where(eye, col, 0.0), axis=0, keepdims=True)


def _cumsum(x, axis, seg=None):
    n = x.shape[axis] if seg is None else seg
    idx = lax.broadcasted_iota(jnp.int32, x.shape, axis)
    if seg is not None:
        idx = idx % seg
    s = 1
    while s < n:
        x = x + jnp.where(idx >= s, pltpu.roll(x, s, axis), 0.0)
        s *= 2
    return x


def _mod_kernel(c_ref, w_ref, b_ref, o_ref):
    o_ref[...] = _dot(_silu(c_ref[...]), w_ref[...]) + b_ref[...]


def _modulation(c_all, w_ada, b_ada):
    depth, d, nd = w_ada.shape
    n = c_all.shape[0]
    tn = 1024
    return pl.pallas_call(
        _mod_kernel,
        grid=(depth, nd // tn),
        in_specs=[pl.BlockSpec((n, d), lambda l, j: (0, 0)),
                  pl.BlockSpec((None, d, tn), lambda l, j: (l, 0, j)),
                  pl.BlockSpec((None, 1, tn), lambda l, j: (l, 0, j))],
        out_specs=pl.BlockSpec((None, n, tn), lambda l, j: (l, 0, j)),
        out_shape=jax.ShapeDtypeStruct((depth, n, nd), F32),
        compiler_params=_cp("arbitrary", "arbitrary"),
        name="adaln_mod",
    )(c_all, w_ada, b_ada.reshape(depth, 1, nd))


def _norm_mod(x, g, sc, sh):
    return _rms_rows(x) * g * (1.0 + sc) + sh


def _inproj_kernel(x_ref, sc_ref, sh_ref, g_ref, w_ref, wg_ref, bg_ref, gq_ref, gk_ref, eseg_ref,
                   gla_ref, la_ref, q_ref, k_ref, kb_ref, v_ref, vb_ref, m_ref, misc_ref, *, kv_transposed):
    h = _norm_mod(x_ref[...], g_ref[...], sc_ref[...], sh_ref[...])
    p = lax.dot_general(h.astype(BF16), w_ref[...], (((1,), (1,)), ((), ())), preferred_element_type=F32)
    gla_ref[...] = p[:, C_GLA:C_SBQ]
    misc = p[:, C_MISC:D_INP]
    misc_ref[...] = misc
    la_ref[...] = _log_sigmoid(_dot(misc, wg_ref[...]) + bg_ref[...]) * (1.0 / GLA_TAU)
    eseg = eseg_ref[...]
    q = _seg_rms(p[:, C_SBQ:C_SBK], eseg, SB_HD) * gq_ref[...]
    q_ref[...] = (q * (SB_HD ** -0.5 * LOG2E)).astype(BF16)
    k = _seg_rms(p[:, C_SBK:C_SBV], eseg, SB_HD) * gk_ref[...]
    k_ref[...] = k.T if kv_transposed else k
    kb_ref[...] = k.astype(BF16)
    v = p[:, C_SBV:C_MZ]
    v_ref[...] = v.T if kv_transposed else v
    vb_ref[...] = v.astype(BF16)
    m_ref[...] = p[:, C_MZ:C_MISC]


MOD_SH1, MOD_SC1, MOD_GT1, MOD_SH2, MOD_SC2, MOD_GT2 = range(N_MOD)


def _mod_spec(mod, layer, chunk, rows_per_mod):
    return pl.BlockSpec((None, None, None) + mod.shape[3:], lambda i: (layer, chunk, i // rows_per_mod, 0, 0))


def _layer_spec(w, layer):
    return pl.BlockSpec((None,) + w.shape[1:], lambda i: (layer, 0, 0))


def _in_proj(x, mod, layer, g, w, wg, bg, gq, gk, eseg, *, tm, rows_per_mod, kv_transposed):
    r, d = x.shape
    const = lambda shape: pl.BlockSpec(shape, lambda i: (0,) * len(shape))
    row = lambda n: pl.BlockSpec((tm, n), lambda i: (i, 0))
    outs = [(W_GLA, F32), (LANE, F32), (SB_W, BF16), (SB_W, F32), (SB_W, BF16), (SB_W, F32), (SB_W, BF16),
            (W_M, F32), (LANE, F32)]
    out_specs = [row(n) for n, _ in outs]
    out_shape = [jax.ShapeDtypeStruct((r, n), dt) for n, dt in outs]
    if kv_transposed:
        seq = rows_per_mod * tm
        for idx in (3, 5):
            out_specs[idx] = pl.BlockSpec((None, SB_W, tm), lambda i: (i // rows_per_mod, 0, i % rows_per_mod))
            out_shape[idx] = jax.ShapeDtypeStruct((r // seq, SB_W, seq), F32)
    return pl.pallas_call(
        functools.partial(_inproj_kernel, kv_transposed=kv_transposed),
        grid=(r // tm,),
        in_specs=[row(d), _mod_spec(mod, layer, MOD_SC1, rows_per_mod), _mod_spec(mod, layer, MOD_SH1, rows_per_mod),
                  const((1, d)), _layer_spec(w, layer), const((LANE, LANE)),
                  const((1, LANE)), const((1, SB_W)), const((1, SB_W)), const((LANE, LANE))],
        out_specs=out_specs,
        out_shape=out_shape,
        compiler_params=_cp("arbitrary"),
        name="in_proj",
    )(x, mod, mod, g, w, wg, bg, gq, gk, eseg)


GLA_C = SUBLANE
GLA_UNROLL = 32


def _gla_prompt_kernel(gla_ref, la_ref, eexp_ref, bd_ref, eseg_ref, gn_ref, o_ref, st_out_ref,
                       st_ref, b_ref, acc_ref):
    i = pl.program_id(1)
    tt = la_ref.shape[0]

    @pl.when(i == 0)
    def _():
        st_ref[...] = jnp.zeros_like(st_ref)

    b_ref[...] = _cumsum(la_ref[...], 0, seg=GLA_C)
    eexp = eexp_ref[...]
    bd = bd_ref[...]
    rowid = lax.broadcasted_iota(jnp.int32, (GLA_C, GLA_DQK), 0)
    rowid_v = lax.broadcasted_iota(jnp.int32, (GLA_C, GLA_DVW), 0)

    def group(gi, carry):
        steps = []
        for u in range(GLA_UNROLL):
            r0 = pl.multiple_of((gi * GLA_UNROLL + u) * GLA_C, GLA_C)
            bc = b_ref[pl.ds(r0, GLA_C), :]
            qc = gla_ref[pl.ds(r0, GLA_C), 0:GLA_DQK] * (GLA_DK ** -0.5)
            kc = gla_ref[pl.ds(r0, GLA_C), GLA_DQK:2 * GLA_DQK]
            vc = gla_ref[pl.ds(r0, GLA_C), 2 * GLA_DQK:2 * GLA_DQK + GLA_DVW]
            steps.append((r0, bc, qc, kc, vc, bc[GLA_C - 1:GLA_C, :]))
        ps = []
        for r0, bc, qc, kc, vc, bl in steps:
            for t in range(GLA_C):
                d = jnp.where(rowid <= t, bc[t:t + 1, :] - bc, NEG_BIG)
                ps.append(jnp.exp(d) * (qc[t:t + 1, :] * kc))
        sc = _dot(jnp.concatenate(ps, axis=0), eexp)
        upds = [_dot_tn(vc, kc * jnp.exp(bl - bc)) * bd for r0, bc, qc, kc, vc, bl in steps]
        st = st_ref[...]
        for u, (r0, bc, qc, kc, vc, bl) in enumerate(steps):
            o = _dot_nt(qc * jnp.exp(bc), st)
            st = st * jnp.exp(bl) + upds[u]
            for t in range(GLA_C):
                row0 = (u * GLA_C + t) * GLA_C
                ot = jnp.sum(sc[row0:row0 + GLA_C, :] * vc, axis=0, keepdims=True)
                o = o + jnp.where(rowid_v == t, ot, 0.0)
            acc_ref[pl.ds(r0, GLA_C), :] = o
        st_ref[...] = st
        return carry

    lax.fori_loop(0, tt // (GLA_C * GLA_UNROLL), group, 0)
    gg = gla_ref[:, 2 * GLA_DQK + GLA_DVW:2 * GLA_DQK + 2 * GLA_DVW]
    o = _seg_rms(acc_ref[...], eseg_ref[...], GLA_DV) * gn_ref[...] * _silu(gg)
    o_ref[...] = o.astype(BF16)
    st_out_ref[...] = st_ref[...]


def _gla_prompt(gla, la, eexp, bd, eseg, gn, *, bsz, seq, tt):
    nt = seq // tt
    const = lambda shape: pl.BlockSpec(shape, lambda b, i: (0,) * len(shape))
    return pl.pallas_call(
        _gla_prompt_kernel,
        grid=(bsz, nt),
        in_specs=[pl.BlockSpec((tt, W_GLA), lambda b, i: (b * nt + i, 0)),
                  pl.BlockSpec((tt, LANE), lambda b, i: (b * nt + i, 0)),
                  const((GLA_DQK, GLA_DVW)), const((GLA_DVW, GLA_DQK)), const((LANE, LANE)),
                  const((1, GLA_DVW))],
        out_specs=[pl.BlockSpec((tt, GLA_DVW), lambda b, i: (b * nt + i, 0)),
                   pl.BlockSpec((None, GLA_DVW, GLA_DQK), lambda b, i: (b, 0, 0))],
        out_shape=[jax.ShapeDtypeStruct((bsz * seq, GLA_DVW), BF16),
                   jax.ShapeDtypeStruct((bsz, GLA_DVW, GLA_DQK), F32)],
        scratch_shapes=[pltpu.VMEM((GLA_DVW, GLA_DQK), F32), pltpu.VMEM((tt, GLA_DQK), F32),
                        pltpu.VMEM((tt, GLA_DVW), F32)],
        compiler_params=_cp("arbitrary", "arbitrary"),
        name="gla_prompt",
    )(gla, la, eexp, bd, eseg, gn)


SB_TK = LANE
SB_NB = 3
SB_UNROLL = 8


def _neg_abs(x):
    return lax.bitcast_convert_type(lax.bitcast_convert_type(x, jnp.int32) | jnp.int32(-2 ** 31), F32)


def _sb_prompt_kernel(q_ref, k_ref, v_ref, bcol_ref, uo_ref, hm_ref, eseg_ref, gn_ref, o_ref,
                      acc_ref, car_ref, kk_ref, vv_ref, qx_ref):
    i = pl.program_id(2)
    tq = q_ref.shape[0]
    ndiag = tq // SB_TK
    nblk = kk_ref.shape[0]

    @pl.when(i == 0)
    def _():
        m0 = hm_ref[0:1, :]
        m1 = hm_ref[1:2, :]
        bcol = bcol_ref[...]

        def fill(j, carry):
            k0 = pl.multiple_of(j * SB_TK, SB_TK)
            kb = k_ref[pl.ds(k0, SB_TK), :]
            vb = v_ref[pl.ds(k0, SB_TK), :]
            kk_ref[j] = jnp.concatenate([jnp.concatenate([kb * m0, kb * m1], axis=0), bcol], axis=1)
            vv_ref[j] = jnp.concatenate([vb * m0, vb * m1], axis=0)
            return carry

        lax.fori_loop(0, nblk, fill, 0)

    ones = jnp.where(lax.broadcasted_iota(jnp.int32, (tq, LANE), 1) < SB_NB, 1.0, 0.0).astype(BF16)
    qx_ref[...] = jnp.concatenate([q_ref[...], ones], axis=1)
    uo = uo_ref[...]
    acc_ref[...] = jnp.zeros_like(acc_ref)
    car_ref[...] = jnp.zeros_like(car_ref)

    def block(j, r0, diag):
        rows = tq - r0
        z = lax.dot_general(qx_ref[r0:, :], kk_ref[j], (((1,), (1,)), ((), ())),
                            preferred_element_type=F32)
        sp = jnp.maximum(z, 0.0) + jnp.log(1.0 + jnp.exp2(_neg_abs(z))) * LOG2E
        if diag:
            valid = (lax.broadcasted_iota(jnp.int32, (SB_TK, 2 * SB_TK), 1) % SB_TK
                     < lax.broadcasted_iota(jnp.int32, (SB_TK, 2 * SB_TK), 0))
            def mask_head(t):
                head = jnp.where(valid, t[:SB_TK], 0.0)
                return head if rows == SB_TK else jnp.concatenate([head, t[SB_TK:]], axis=0)

            sp = mask_head(sp)
        incl = jnp.dot(sp.astype(BF16), uo, preferred_element_type=F32) + car_ref[r0:, :]
        w = jnp.exp2(z - incl)
        if diag:
            w = mask_head(w)
        car_ref[r0:, :SB_TK] += jnp.sum(sp[:, :SB_TK], axis=1, keepdims=True)
        car_ref[r0:, SB_TK:] += jnp.sum(sp[:, SB_TK:], axis=1, keepdims=True)
        acc_ref[r0:, :] += jnp.dot(w.astype(BF16), vv_ref[j], preferred_element_type=F32)

    for c in range(ndiag - 1, -1, -1):
        block(i * ndiag + c, c * SB_TK, True)

    n_off = i * ndiag
    n_main = n_off // SB_UNROLL

    def body(jj, carry):
        for u in range(SB_UNROLL):
            block(n_off - 1 - jj * SB_UNROLL - u, 0, False)
        return carry

    lax.fori_loop(0, n_main, body, 0)
    if SB_UNROLL > ndiag:
        @pl.when(n_off - n_main * SB_UNROLL > 0)
        def _():
            for u in range(ndiag):
                block(ndiag - 1 - u, 0, False)

    o_ref[...] = (_seg_rms(acc_ref[...], eseg_ref[...], SB_HD) * gn_ref[...]).astype(BF16)


def _sb_prompt(q, k, v, bias_cols, uo, hm, eseg, gn, *, bsz, seq, tq):
    nq = seq // tq
    npair = SB_HEADS // 2
    const = lambda shape: pl.BlockSpec(shape, lambda b, p, i: (0,) * len(shape))
    return pl.pallas_call(
        _sb_prompt_kernel,
        grid=(bsz, npair, nq),
        in_specs=[pl.BlockSpec((tq, LANE), lambda b, p, i: (b * nq + i, p)),
                  pl.BlockSpec((seq, LANE), lambda b, p, i: (b, p)),
                  pl.BlockSpec((seq, LANE), lambda b, p, i: (b, p)),
                  pl.BlockSpec((None, 2 * SB_TK, LANE), lambda b, p, i: (p, 0, 0)),
                  const((2 * SB_TK, 2 * SB_TK)), const((SUBLANE, LANE)), const((LANE, LANE)), const((1, LANE))],
        out_specs=pl.BlockSpec((tq, LANE), lambda b, p, i: (b * nq + i, p)),
        out_shape=jax.ShapeDtypeStruct((bsz * seq, SB_W), BF16),
        scratch_shapes=[pltpu.VMEM((tq, LANE), F32), pltpu.VMEM((tq, 2 * SB_TK), F32),
                        pltpu.VMEM((seq // SB_TK, 2 * SB_TK, 2 * LANE), BF16),
                        pltpu.VMEM((seq // SB_TK, 2 * SB_TK, LANE), BF16),
                        pltpu.VMEM((tq, 2 * LANE), BF16)],
        compiler_params=_cp("arbitrary", "arbitrary", "arbitrary"),
        name="sb_prompt",
    )(q, k, v, bias_cols, uo, hm, eseg, gn)


def _ssm_params(dt_raw, dtb, nega):
    dt = _softplus(dt_raw + dtb)
    return dt, dt * nega


SSD_C = 128


def _ssd_prompt_kernel(m_ref, prev_ref, misc_ref, dtt_ref, cw_ref, cb_ref, dtb_l_ref, nega_l_ref,
                       dtb_c_ref, nega_c_ref, e4_ref, e4t_ref, md_ref, gn_ref, o_ref, hs_out_ref,
                       hs_ref):
    i = pl.program_id(1)
    tt = m_ref.shape[0]
    cl = SSD_C

    @pl.when(i == 0)
    def _():
        hs_ref[...] = jnp.zeros_like(hs_ref)

    e4 = e4_ref[...]
    row = lax.broadcasted_iota(jnp.int32, (cl, M_CONV_DIM), 0)
    row8 = lax.broadcasted_iota(jnp.int32, (SUBLANE, M_CONV_DIM), 0)
    causal = lax.broadcasted_iota(jnp.int32, (cl, cl), 0) >= lax.broadcasted_iota(jnp.int32, (cl, cl), 1)
    lane_head = lax.broadcasted_iota(jnp.int32, (cl, M_DIN), 1) // M_HD
    rep = M_HEADS // M_GROUPS
    half = M_DIN // M_GROUPS

    for r0 in range(0, tt, cl):
        z = m_ref[r0:r0 + cl, 0:M_DIN]
        xbc = m_ref[r0:r0 + cl, M_DIN:M_DIN + M_CONV_DIM]
        if r0 == 0:
            prev = jnp.where(i > 0, prev_ref[:, M_DIN:M_DIN + M_CONV_DIM], 0.0)
        else:
            prev = m_ref[r0 - SUBLANE:r0, M_DIN:M_DIN + M_CONV_DIM]
        acc = cb_ref[...] + cw_ref[M_CONV - 1:M_CONV, :] * xbc
        for s in range(1, M_CONV):
            head = jnp.where(row8 < s, pltpu.roll(prev, s, 0), 0.0)
            head = jnp.concatenate([head, jnp.zeros((cl - SUBLANE, M_CONV_DIM), F32)], axis=0)
            shifted = jnp.where(row < s, head, pltpu.roll(xbc, s, 0))
            acc = acc + cw_ref[M_CONV - 1 - s:M_CONV - s, :] * shifted
        xc = _silu(acc)
        x = xc[:, 0:M_DIN]
        bm = xc[:, M_DIN:M_DIN + M_GROUPS * M_N]
        cm = xc[:, M_DIN + M_GROUPS * M_N:]

        dt_c, a_c = _ssm_params(misc_ref[r0:r0 + cl, :], dtb_l_ref[...], nega_l_ref[...])
        cs_c = _cumsum(a_c, 0)
        dt_r, a_r = _ssm_params(dtt_ref[:, r0:r0 + cl], dtb_c_ref[...], nega_c_ref[...])
        cs_r = _cumsum(a_r, 1)
        xdt = x * _dot_sel(dt_c, e4)
        ecs = _dot_sel(jnp.exp(cs_c), e4)

        g = [_dot_nt(cm[:, gi * M_N:(gi + 1) * M_N], bm[:, gi * M_N:(gi + 1) * M_N]) for gi in range(M_GROUPS)]
        y = jnp.zeros((cl, M_DIN), F32)
        for h in range(M_HEADS):
            decay = jnp.exp(jnp.where(causal, cs_c[:, h:h + 1] - cs_r[h:h + 1, :], NEG_BIG))
            yh = _dot(g[h // rep] * decay, xdt)
            y = jnp.where(lane_head == h, yh, y)

        cs_last = cs_c[cl - 1:cl, :]
        xw = xdt * _dot_sel(jnp.exp(cs_last - cs_c), e4)
        zz = _dot_tn(xw, bm)
        upd = jnp.concatenate([zz[:half, :M_N], zz[half:, M_N:]], axis=0)
        dec = _dot_sel_nt(e4t_ref[...], jnp.broadcast_to(jnp.exp(cs_last), (M_N, LANE)))

        hs = hs_ref[...]
        yi = [_dot_nt(cm[:, gi * M_N:(gi + 1) * M_N], hs) for gi in range(M_GROUPS)]
        hs_ref[...] = hs * dec + upd
        y = y + jnp.where(lane_head < rep, yi[0], yi[1]) * ecs

        y = (y + md_ref[...] * x) * _silu(z)
        y = jnp.concatenate([_rms_rows(y[:, gi * half:(gi + 1) * half]) for gi in range(M_GROUPS)], axis=1)
        o_ref[r0:r0 + cl, :] = (y * gn_ref[...]).astype(BF16)

    hs_out_ref[...] = hs_ref[...]


def _ssd_prompt(m, misc, dtt, cw, cb, dtb_l, nega_l, dtb_c, nega_c, e4, e4t, md, gn, *, bsz, seq, tt):
    nt = seq // tt
    per8 = tt // SUBLANE
    const = lambda shape: pl.BlockSpec(shape, lambda b, i: (0,) * len(shape))
    return pl.pallas_call(
        _ssd_prompt_kernel,
        grid=(bsz, nt),
        in_specs=[pl.BlockSpec((tt, W_M), lambda b, i: (b * nt + i, 0)),
                  pl.BlockSpec((SUBLANE, W_M), lambda b, i: (jnp.maximum((b * nt + i) * per8 - 1, 0), 0)),
                  pl.BlockSpec((tt, LANE), lambda b, i: (b * nt + i, 0)),
                  pl.BlockSpec((None, SUBLANE, tt), lambda b, i: (b, 0, i)),
                  const((M_CONV, M_CONV_DIM)), const((1, M_CONV_DIM)), const((1, LANE)), const((1, LANE)),
                  const((SUBLANE, 1)), const((SUBLANE, 1)), const((LANE, M_DIN)), const((M_DIN, LANE)),
                  const((1, M_DIN)), const((1, M_DIN))],
        out_specs=[pl.BlockSpec((tt, M_DIN), lambda b, i: (b * nt + i, 0)),
                   pl.BlockSpec((None, M_DIN, M_N), lambda b, i: (b, 0, 0))],
        out_shape=[jax.ShapeDtypeStruct((bsz * seq, M_DIN), BF16),
                   jax.ShapeDtypeStruct((bsz, M_DIN, M_N), F32)],
        scratch_shapes=[pltpu.VMEM((M_DIN, M_N), F32)],
        compiler_params=_cp("arbitrary", "arbitrary"),
        name="ssd_prompt",
    )(m, m, misc, dtt, cw, cb, dtb_l, nega_l, dtb_c, nega_c, e4, e4t, md, gn)


def _step_kernel(gla_ref, la_ref, v4_ref, gg4_ref, s0_ref, m_ref, buf_ref, misc_ref, h0_ref,
                 cw_ref, cb_ref, dtb_ref, nega_ref, e4_ref, md_ref, gng_ref, gnm_ref,
                 g_ref, s_ref, y_ref, h_ref):
    q_col = _row_to_col(gla_ref[:, 0:GLA_DQK] * (GLA_DK ** -0.5))
    k_col = _row_to_col(gla_ref[:, GLA_DQK:2 * GLA_DQK])
    dec_col = _row_to_col(jnp.exp(la_ref[...]))
    v4 = v4_ref[...]
    v_exp = jnp.concatenate([jnp.broadcast_to(v4[h:h + 1, :], (GLA_DK, GLA_DV)) for h in range(GLA_HEADS)],
                            axis=0)
    s = s0_ref[...] * dec_col + k_col * v_exp
    s_ref[...] = s
    o4 = jnp.sum((q_col * s).reshape(GLA_HEADS, GLA_DK, GLA_DV), axis=1)
    g_ref[...] = _rms_rows(o4) * gng_ref[...] * _silu(gg4_ref[...])

    z = m_ref[:, 0:M_DIN]
    acc = cb_ref[...] + cw_ref[M_CONV - 1:M_CONV, :] * m_ref[:, M_DIN:M_DIN + M_CONV_DIM]
    for s_ in range(M_CONV - 1):
        acc = acc + cw_ref[s_:s_ + 1, :] * buf_ref[s_:s_ + 1, :]
    xc = _silu(acc)
    x = xc[:, 0:M_DIN]
    bm = xc[:, M_DIN:M_DIN + M_GROUPS * M_N]
    cm = xc[:, M_DIN + M_GROUPS * M_N:]
    dt, a = _ssm_params(misc_ref[...], dtb_ref[...], nega_ref[...])
    e4 = e4_ref[...]
    xdt_col = _row_to_col(x * _dot_sel(dt, e4))
    deca_col = _row_to_col(_dot_sel(jnp.exp(a), e4))
    half = M_DIN // M_GROUPS
    spread = lambda t: jnp.concatenate(
        [jnp.broadcast_to(t[:, gi * M_N:(gi + 1) * M_N], (half, M_N)) for gi in range(M_GROUPS)], axis=0)
    hs = h0_ref[...] * deca_col + xdt_col * spread(bm)
    h_ref[...] = hs
    y = _col_to_row(jnp.sum(hs * spread(cm), axis=1, keepdims=True))
    y = (y + md_ref[...] * x) * _silu(z)
    y = jnp.concatenate([_rms_rows(y[:, gi * half:(gi + 1) * half]) for gi in range(M_GROUPS)], axis=1)
    y_ref[...] = y * gnm_ref[...]


def _step_mixers(gla, la, v4, gg4, s0, m, buf, misc, h0, cw, cb, dtb, nega, e4, md, gng, gnm):
    nb = gla.shape[0]
    per = lambda *shape: pl.BlockSpec((None,) + shape, lambda b: (b,) + (0,) * len(shape))
    const = lambda shape: pl.BlockSpec(shape, lambda b: (0,) * len(shape))
    return pl.pallas_call(
        _step_kernel,
        grid=(nb,),
        in_specs=[per(1, W_GLA), per(1, LANE), per(GLA_HEADS, GLA_DV), per(GLA_HEADS, GLA_DV),
                  per(GLA_DQK, GLA_DV), per(1, W_M), per(M_CONV - 1, M_CONV_DIM), per(1, LANE),
                  per(M_DIN, M_N),
                  const((M_CONV, M_CONV_DIM)), const((1, M_CONV_DIM)), const((1, LANE)), const((1, LANE)),
                  const((LANE, M_DIN)), const((1, M_DIN)), const((1, GLA_DV)), const((1, M_DIN))],
        out_specs=[per(GLA_HEADS, GLA_DV), per(GLA_DQK, GLA_DV), per(1, M_DIN), per(M_DIN, M_N)],
        out_shape=[jax.ShapeDtypeStruct((nb, GLA_HEADS, GLA_DV), F32),
                   jax.ShapeDtypeStruct((nb, GLA_DQK, GLA_DV), F32),
                   jax.ShapeDtypeStruct((nb, 1, M_DIN), F32),
                   jax.ShapeDtypeStruct((nb, M_DIN, M_N), F32)],
        compiler_params=_cp("arbitrary"),
        name="step_mixers",
    )(gla.reshape(nb, 1, W_GLA), la.reshape(nb, 1, LANE), v4, gg4, s0, m.reshape(nb, 1, W_M), buf,
      misc.reshape(nb, 1, LANE), h0, cw, cb, dtb, nega, e4, md, gng, gnm)


SB_PP = 32


def _sb_decode_kernel(pt_ref, q_ref, *refs):
    ks = refs[0:SB_PP]
    vs = refs[SB_PP:2 * SB_PP]
    bias_ref, gn_ref, uinc_ref, o_ref, acc_ref, car_ref, qb_ref = refs[2 * SB_PP:]
    j = pl.program_id(1)
    page = ks[0].shape[2]
    uinc = uinc_ref[...]

    @pl.when(j == 0)
    def _():
        acc_ref[...] = jnp.zeros_like(acc_ref)
        car_ref[...] = jnp.zeros_like(car_ref)
        qb_ref[...] = jnp.broadcast_to(q_ref[...], qb_ref.shape)

    bias = bias_ref[...]
    for r in range(SB_PP - 1, -1, -1):
        prod = (ks[r][...] * qb_ref[...]).reshape(SB_HEADS, SB_HD // SUBLANE, SUBLANE, page)
        z = jnp.sum(jnp.sum(prod, axis=1), axis=1) + bias
        sp = jnp.maximum(z, 0.0) + jnp.log(1.0 + jnp.exp2(_neg_abs(z))) * LOG2E
        ls = z - sp
        hi = sp.astype(BF16)
        lo = (sp - hi.astype(F32)).astype(BF16)
        incl = (jnp.dot(hi, uinc, preferred_element_type=F32) + jnp.dot(lo, uinc, preferred_element_type=F32))
        car = car_ref[...]
        w = jnp.exp2(ls - (incl - sp + car))
        car_ref[...] = car + incl[:, 0:1]
        for h in range(SB_HEADS):
            acc_ref[h] += vs[r][h] * w[h:h + 1, :]

    @pl.when(j == pl.num_programs(1) - 1)
    def _():
        o_ref[...] = _rms_rows(jnp.sum(acc_ref[...], axis=-1)) * gn_ref[...]


def _sb_decode(page_table, q, cache_kt, cache_vt, layer, bias_col, gn, uinc):
    nb, n_pages = page_table.shape
    page = cache_kt.shape[4]
    ng = n_pages // SB_PP
    pt = page_table.reshape(-1)

    def kv_spec(r):
        return pl.BlockSpec((None, None, SB_HEADS, SB_HD, page),
                            lambda b, j, pt_ref: (layer, pt_ref[b * n_pages + (ng - 1 - j) * SB_PP + r], 0, 0, 0))

    const = lambda shape: pl.BlockSpec(shape, lambda b, j, pt_ref: (0,) * len(shape))
    grid_spec = pltpu.PrefetchScalarGridSpec(
        num_scalar_prefetch=1,
        grid=(nb, ng),
        in_specs=[pl.BlockSpec((None, SB_HEADS, SB_HD, 1), lambda b, j, pt_ref: (b, 0, 0, 0))]
        + [kv_spec(r) for r in range(SB_PP)] + [kv_spec(r) for r in range(SB_PP)]
        + [const((SB_HEADS, 1)), const((1, SB_HD)), const((page, page))],
        out_specs=pl.BlockSpec((None, SB_HEADS, SB_HD), lambda b, j, pt_ref: (b, 0, 0)),
        scratch_shapes=[pltpu.VMEM((SB_HEADS, SB_HD, page), F32), pltpu.VMEM((SB_HEADS, 1), F32),
                        pltpu.VMEM((SB_HEADS, SB_HD, page), F32)],
    )
    out = pl.pallas_call(
        _sb_decode_kernel,
        grid_spec=grid_spec,
        out_shape=jax.ShapeDtypeStruct((nb, SB_HEADS, SB_HD), F32),
        compiler_params=_cp("arbitrary", "arbitrary"),
        name="sb_decode",
    )(pt, q, *([cache_kt] * SB_PP), *([cache_vt] * SB_PP), bias_col, gn, uinc)
    return out.reshape(nb, SB_W)


def _outproj_kernel(x_ref, gate_ref, g_ref, s_ref, y_ref, w_ref, o_ref):
    mix = (_dot(g_ref[...], w_ref[0:GLA_DVW, :]) + _dot(s_ref[...], w_ref[GLA_DVW:GLA_DVW + SB_W, :])
           + _dot(y_ref[...], w_ref[GLA_DVW + SB_W:, :]))
    o_ref[...] = x_ref[...] + gate_ref[...] * mix


def _out_proj(x, mod, layer, g, s, y, w, *, tm, rows_per_mod):
    r, d = x.shape
    row = lambda n: pl.BlockSpec((tm, n), lambda i: (i, 0))
    return pl.pallas_call(
        _outproj_kernel,
        grid=(r // tm,),
        in_specs=[row(d), _mod_spec(mod, layer, MOD_GT1, rows_per_mod),
                  row(GLA_DVW), row(SB_W), row(M_DIN), _layer_spec(w, layer)],
        out_specs=row(d),
        out_shape=jax.ShapeDtypeStruct((r, d), F32),
        compiler_params=_cp("arbitrary"),
        name="out_proj",
    )(x, mod, g, s, y, w)


def _ffn_up_kernel(x_ref, sc_ref, sh_ref, g_ref, w_ref, u_ref):
    h = _norm_mod(x_ref[...], g_ref[...], sc_ref[...], sh_ref[...])
    u_ref[...] = jnp.dot(h.astype(BF16), w_ref[...], preferred_element_type=F32)


def _ffn_up(x, mod, layer, g, w, *, tm, rows_per_mod):
    r, d = x.shape
    n = w.shape[2]
    return pl.pallas_call(
        _ffn_up_kernel,
        grid=(r // tm,),
        in_specs=[pl.BlockSpec((tm, d), lambda i: (i, 0)), _mod_spec(mod, layer, MOD_SC2, rows_per_mod),
                  _mod_spec(mod, layer, MOD_SH2, rows_per_mod),
                  pl.BlockSpec((1, d), lambda i: (0, 0)), _layer_spec(w, layer)],
        out_specs=pl.BlockSpec((tm, n), lambda i: (i, 0)),
        out_shape=jax.ShapeDtypeStruct((r, n), F32),
        compiler_params=_cp("arbitrary"),
        name="ffn_up",
    )(x, mod, mod, g, w)


FFN_CK = 256
FFN_RB = 128


def _ffn_act_down(taps, cw_ref, cb_ref, w_ref):
    dff = w_ref.shape[0]

    def conv(c0):
        u, u1, u2 = taps(c0)
        cs = slice(c0, c0 + FFN_CK)
        return cb_ref[:, cs] + cw_ref[0:1, cs] * u2 + cw_ref[1:2, cs] * u1 + cw_ref[2:3, cs] * u

    out = None
    for c0 in range(0, dff, FFN_CK):
        part = _dot(_silu(conv(c0)) * conv(dff + c0), w_ref[c0:c0 + FFN_CK, :])
        out = part if out is None else out + part
    return out


def _ffn_up_act_kernel(x_ref, sc_ref, sh_ref, g_ref, w_ref, cw_ref, cb_ref, a_ref, tail_ref, carry_ref,
                       u_ref, *, tiles_per_seq):
    i = pl.program_id(0)
    tm = x_ref.shape[0]
    dff = a_ref.shape[1]
    first = i % tiles_per_seq == 0
    h = _norm_mod(x_ref[...], g_ref[...], sc_ref[...], sh_ref[...]).astype(BF16)
    row8 = lax.broadcasted_iota(jnp.int32, (SUBLANE, FFN_CK), 0)

    def up(base, slot, half):
        cs = slice(base, base + FFN_CK)
        u = jnp.dot(h, w_ref[:, cs], preferred_element_type=F32)
        u_ref[slot, half, 0:SUBLANE, :] = jnp.where(first, 0.0, carry_ref[:, cs])
        u_ref[slot, half, SUBLANE:, :] = u
        last = u[tm - SUBLANE:, :]
        carry_ref[:, cs] = last
        tail_ref[:, cs] = last

    def conv(slot, half, base, rb):
        cs = slice(base, base + FFN_CK)
        acc = cb_ref[:, cs]
        for s in range(FFN_CONV):
            r0 = SUBLANE + rb - s
            acc = acc + cw_ref[FFN_CONV - 1 - s:FFN_CONV - s, cs] * u_ref[slot, half, r0:r0 + FFN_RB, :]
        return acc

    for n, c0 in enumerate(range(0, dff, FFN_CK)):
        slot = n % 2
        up(c0, slot, 0)
        up(dff + c0, slot, 1)
        for rb in range(0, tm, FFN_RB):
            a_ref[rb:rb + FFN_RB, c0:c0 + FFN_CK] = (
                _silu(conv(slot, 0, c0, rb)) * conv(slot, 1, dff + c0, rb)).astype(BF16)


def _ffn_up_act(x, mod, layer, g, w, cw, cb, *, tm, seq):
    r, d = x.shape
    n = w.shape[2]
    dff = n // 2
    tiles = seq // tm
    const = lambda shape: pl.BlockSpec(shape, lambda i: (0,) * len(shape))
    return pl.pallas_call(
        functools.partial(_ffn_up_act_kernel, tiles_per_seq=tiles),
        grid=(r // tm,),
        in_specs=[pl.BlockSpec((tm, d), lambda i: (i, 0)), _mod_spec(mod, layer, MOD_SC2, tiles),
                  _mod_spec(mod, layer, MOD_SH2, tiles), const((1, d)), _layer_spec(w, layer),
                  const((FFN_CONV, n)), const((1, n))],
        out_specs=[pl.BlockSpec((tm, dff), lambda i: (i, 0)),
                   pl.BlockSpec((None, SUBLANE, n), lambda i: (i // tiles, 0, 0))],
        out_shape=[jax.ShapeDtypeStruct((r, dff), BF16), jax.ShapeDtypeStruct((r // seq, SUBLANE, n), F32)],
        scratch_shapes=[pltpu.VMEM((SUBLANE, n), F32), pltpu.VMEM((2, 2, SUBLANE + tm, FFN_CK), F32)],
        compiler_params=_cp("arbitrary"),
        name="ffn_up_act",
    )(x, mod, mod, g, w, cw, cb)


def _ffn_down_kernel(x_ref, gate_ref, a_ref, w_ref, o_ref):
    o_ref[...] = x_ref[...] + gate_ref[...] * jnp.dot(a_ref[...], w_ref[...], preferred_element_type=F32)


def _ffn_down(x, mod, layer, a, w, *, tm, seq):
    r, d = x.shape
    return pl.pallas_call(
        _ffn_down_kernel,
        grid=(r // tm,),
        in_specs=[pl.BlockSpec((tm, d), lambda i: (i, 0)), _mod_spec(mod, layer, MOD_GT2, seq // tm),
                  pl.BlockSpec((tm, a.shape[1]), lambda i: (i, 0)), _layer_spec(w, layer)],
        out_specs=pl.BlockSpec((tm, d), lambda i: (i, 0)),
        out_shape=jax.ShapeDtypeStruct((r, d), F32),
        compiler_params=_cp("arbitrary"),
        name="ffn_down",
    )(x, mod, a, w)


def _ffn_down_step_kernel(x_ref, gate_ref, u_ref, buf_ref, cw_ref, cb_ref, w_ref, o_ref):
    taps = lambda c0: (u_ref[:, c0:c0 + FFN_CK], buf_ref[1, :, c0:c0 + FFN_CK], buf_ref[0, :, c0:c0 + FFN_CK])
    o_ref[...] = x_ref[...] + gate_ref[...] * _ffn_act_down(taps, cw_ref, cb_ref, w_ref)


def _ffn_down_step(x, mod, layer, u, buf_t, cw, cb, w):
    full = lambda a: pl.BlockSpec(a.shape, lambda i: (0,) * a.ndim)
    args = (x, mod, u, buf_t, cw, cb, w)
    return pl.pallas_call(
        _ffn_down_step_kernel,
        grid=(1,),
        in_specs=[full(x), _mod_spec(mod, layer, MOD_GT2, 1), full(u), full(buf_t), full(cw), full(cb),
                  _layer_spec(w, layer)],
        out_specs=full(x),
        out_shape=jax.ShapeDtypeStruct(x.shape, F32),
        compiler_params=_cp("arbitrary"),
        name="ffn_down_step",
    )(*args)


def _same_segment(n, width):
    i = np.arange(n)
    return jnp.asarray((i[:, None] // width) == (i[None, :] // width), BF16)


def _constants():
    k = np.arange(GLA_DQK)
    v = np.arange(GLA_DVW)
    eexp = (k[:, None] // GLA_DK) == (v[None, :] // GLA_DV)
    j = np.arange(SB_TK)
    later = j[:, None] >= j[None, :]
    uo = np.block([[later, np.zeros_like(later)], [np.zeros_like(later), later]])
    lane = np.arange(LANE)
    hm_pair = np.zeros((SUBLANE, LANE), np.float32)
    hm_pair[0] = lane < SB_HD
    hm_pair[1] = lane >= SB_HD
    p = np.arange(M_DIN)
    e4 = lane[:, None] == (p[None, :] // M_HD)
    return dict(
        eexp=jnp.asarray(eexp, BF16), bd=jnp.asarray(eexp.T, F32),
        uo=jnp.asarray(uo, BF16), hm_pair=jnp.asarray(hm_pair, BF16), eseg64=_same_segment(LANE, 64),
        e4=jnp.asarray(e4, BF16), e4t=jnp.asarray(e4.T, BF16))


def _pad_lanes(v, n=LANE):
    return jnp.zeros((1, n), F32).at[0, :v.shape[0]].set(v)


def _pad_col(v, n=SUBLANE):
    return jnp.zeros((n, 1), F32).at[:v.shape[0], 0].set(v)


def kernel(x_prompt, x_sample, cache_sb_k, cache_sb_v, state_gla, state_mamba_conv, state_mamba_ssm, state_ffn_conv, page_table, c_prompt, c_sample, norm1_g, w_ada, b_ada, w_in, gla_w_gate2, gla_b_gate, gla_norm_g, sb_q_norm_g, sb_k_norm_g, sb_o_norm_g, sb_bias, m_conv_w, m_conv_b, m_dt_bias, m_a_log, m_d, m_norm_g, w_out, norm2_g, ffn_w_up, ffn_conv_w, ffn_conv_b, ffn_w_down):
    bsz, seq, d = x_prompt.shape
    nb = x_sample.shape[0]
    depth = w_in.shape[0]
    dff = ffn_w_down.shape[1]
    n_pool, page = cache_sb_k.shape[1], cache_sb_k.shape[2]
    assert x_sample.shape[1] == 1 and d % LANE == 0 and seq % 256 == 0
    cst = _constants()

    tm = 256
    tm_mm = 512
    tq = 512
    tt_gla = 256
    tt_ssd = 8 * SSD_C
    assert SB_UNROLL == 2 * (tq // SB_TK) or (tq // SB_TK) % SB_UNROLL == 0
    assert page_table.shape[1] % SB_PP == 0
    assert seq % tq == 0 and seq % tm == 0 and seq % tm_mm == 0 and seq % tt_gla == 0 and seq % tt_ssd == 0

    mod = _modulation(jnp.concatenate([c_prompt, c_sample], axis=0), w_ada, b_ada)
    mod = mod.reshape(depth, bsz + nb, N_MOD, d)
    mod_p = mod[:, :bsz].transpose(0, 2, 1, 3).reshape(depth, N_MOD, bsz, 1, d)
    mod_s = mod[:, bsz:].transpose(0, 2, 1, 3).reshape(depth, N_MOD, 1, nb, d)

    cache_kt = cache_sb_k.transpose(0, 1, 3, 4, 2)
    cache_vt = cache_sb_v.transpose(0, 1, 3, 4, 2)

    xp = x_prompt.reshape(bsz * seq, d)
    xs = x_sample.reshape(nb, d)
    w_in_t = w_in.transpose(2, 0, 1)
    o = np.cumsum([0, GLA_DQK, GLA_DQK, GLA_DVW, GLA_DVW, GLA_LR, SB_W, SB_W, SB_W, M_DIN, M_CONV_DIM, M_HEADS])
    sl = lambda a, b: w_in_t[o[a]:o[b]]
    w_re = jnp.concatenate(
        [sl(0, 4), sl(5, 8), sl(8, 10), sl(10, 11), sl(4, 5),
         jnp.zeros((LANE - M_HEADS - GLA_LR, depth, d), F32)], axis=0).astype(BF16).transpose(1, 0, 2)
    wo = w_out.astype(BF16)
    wu = ffn_w_up.astype(BF16)
    wd = ffn_w_down.astype(BF16)
    outs = {name: [] for name in ("pk", "pv", "pg", "pc", "ph", "pf", "sk", "sv", "sg", "sc", "sh", "sf")}
    for l in range(depth):
        wg = jnp.zeros((LANE, LANE), F32).at[M_HEADS:M_HEADS + GLA_LR].set(gla_w_gate2[l]).astype(BF16)
        bg = gla_b_gate[l].reshape(1, GLA_DQK)
        gq = jnp.tile(sb_q_norm_g[l], SB_HEADS).reshape(1, SB_W)
        gk = jnp.tile(sb_k_norm_g[l], SB_HEADS).reshape(1, SB_W)
        go = jnp.tile(sb_o_norm_g[l], SB_HEADS).reshape(1, SB_W)
        gng = jnp.tile(gla_norm_g[l], GLA_HEADS).reshape(1, GLA_DVW)
        g1 = norm1_g[l].reshape(1, d)
        g2 = norm2_g[l].reshape(1, d)
        bias2 = sb_bias[l] * LOG2E
        b_hi = bias2.astype(BF16)
        b_mid = (bias2 - b_hi.astype(F32)).astype(BF16)
        b_lo = (bias2 - b_hi.astype(F32) - b_mid.astype(F32)).astype(BF16)
        pieces = jnp.stack([b_hi, b_mid, b_lo], axis=1)
        bias_cols = jnp.pad(jnp.repeat(pieces, SB_TK, axis=0).reshape(SB_HEADS // 2, 2 * SB_TK, SB_NB),
                            ((0, 0), (0, 0), (0, LANE - SB_NB)))
        bias_col = bias2.reshape(SB_HEADS, 1)
        cw = m_conv_w[l]
        cb = m_conv_b[l].reshape(1, M_CONV_DIM)
        nega = -jnp.exp(m_a_log[l])
        dtb_l, nega_l = _pad_lanes(m_dt_bias[l]), _pad_lanes(nega)
        dtb_c, nega_c = _pad_col(m_dt_bias[l]), _pad_col(nega)
        md = jnp.repeat(m_d[l], M_HD).reshape(1, M_DIN)
        gnm = m_norm_g[l].reshape(1, M_DIN)
        fcw = ffn_conv_w[l]
        fcb = ffn_conv_b[l].reshape(1, 2 * dff)

        gla, la, q, k, kb, v, vb, m, misc = _in_proj(
            xp, mod_p, l, g1, w_re, wg, bg, gq, gk, cst["eseg64"], tm=tm_mm, rows_per_mod=seq // tm_mm,
            kv_transposed=True)
        g_mix, g_state = _gla_prompt(gla, la, cst["eexp"], cst["bd"], cst["eseg64"], gng,
                                     bsz=bsz, seq=seq, tt=tt_gla)
        s_mix = _sb_prompt(q, kb, vb, bias_cols, cst["uo"], cst["hm_pair"], cst["eseg64"], go[:, :LANE],
                           bsz=bsz, seq=seq, tq=tq)
        dtt = jnp.pad(misc[:, :M_HEADS].reshape(bsz, seq, M_HEADS).transpose(0, 2, 1),
                      ((0, 0), (0, SUBLANE - M_HEADS), (0, 0)))
        y_mix, h_state = _ssd_prompt(m, misc, dtt, cw, cb, dtb_l, nega_l, dtb_c, nega_c, cst["e4"], cst["e4t"],
                                     md, gnm, bsz=bsz, seq=seq, tt=tt_ssd)
        xp = _out_proj(xp, mod_p, l, g_mix, s_mix, y_mix, wo, tm=tm_mm, rows_per_mod=seq // tm_mm)
        act, u_tail = _ffn_up_act(xp, mod_p, l, g2, wu, fcw, fcb, tm=tm, seq=seq)
        xp = _ffn_down(xp, mod_p, l, act, wd, tm=tm_mm, seq=seq)

        outs["pk"].append(k.reshape(bsz, SB_HEADS, SB_HD, seq).transpose(0, 3, 1, 2))
        outs["pv"].append(v.reshape(bsz, SB_HEADS, SB_HD, seq).transpose(0, 3, 1, 2))
        gs = g_state.reshape(bsz, GLA_HEADS, GLA_DV, GLA_HEADS, GLA_DK)
        outs["pg"].append(jnp.stack([gs[:, h, :, h, :] for h in range(GLA_HEADS)], axis=1).transpose(0, 1, 3, 2))
        outs["pc"].append(m.reshape(bsz, seq, W_M)[:, seq - (M_CONV - 1):, M_DIN:])
        outs["ph"].append(h_state.reshape(bsz, M_HEADS, M_HD, M_N))
        outs["pf"].append(u_tail[:, SUBLANE - (FFN_CONV - 1):])

        gla, la, q, k, kb, v, vb, m, misc = _in_proj(
            xs, mod_s, l, g1, w_re, wg, bg, gq, gk, cst["eseg64"], tm=nb, rows_per_mod=1,
            kv_transposed=False)
        g4, g_state, y_mix, h_state = _step_mixers(
            gla, la, gla[:, 2 * GLA_DQK:2 * GLA_DQK + GLA_DVW].reshape(nb, GLA_HEADS, GLA_DV),
            gla[:, 2 * GLA_DQK + GLA_DVW:].reshape(nb, GLA_HEADS, GLA_DV),
            state_gla[l].reshape(nb, GLA_DQK, GLA_DV), m, state_mamba_conv[l], misc,
            state_mamba_ssm[l].reshape(nb, M_DIN, M_N), cw, cb, dtb_l, nega_l, cst["e4"], md,
            gla_norm_g[l].reshape(1, GLA_DV), gnm)
        s_mix = _sb_decode(page_table, q.astype(F32).reshape(nb, SB_HEADS, SB_HD, 1), cache_kt, cache_vt, l,
                           bias_col, sb_o_norm_g[l].reshape(1, SB_HD),
                           jnp.asarray(np.tril(np.ones((page, page), np.float32)), BF16))
        xs = _out_proj(xs, mod_s, l, g4.reshape(nb, GLA_DVW), s_mix, y_mix.reshape(nb, M_DIN), wo,
                       tm=nb, rows_per_mod=1)
        u = _ffn_up(xs, mod_s, l, g2, wu, tm=nb, rows_per_mod=1)
        xs = _ffn_down_step(xs, mod_s, l, u, state_ffn_conv[l].transpose(1, 0, 2), fcw, fcb, wd)

        outs["sk"].append(k.reshape(nb, 1, SB_HEADS, SB_HD))
        outs["sv"].append(v.reshape(nb, 1, SB_HEADS, SB_HD))
        outs["sg"].append(g_state.reshape(nb, GLA_HEADS, GLA_DK, GLA_DV))
        outs["sc"].append(jnp.concatenate([state_mamba_conv[l][:, 1:], m[:, None, M_DIN:]], axis=1))
        outs["sh"].append(h_state.reshape(nb, M_HEADS, M_HD, M_N))
        outs["sf"].append(jnp.concatenate([state_ffn_conv[l][:, 1:], u[:, None, :]], axis=1))

    st = {name: jnp.stack(v) for name, v in outs.items()}
    return (xp.reshape(bsz, seq, d), xs.reshape(nb, 1, d), st["pk"], st["pv"], st["pg"], st["pc"], st["ph"],
            st["pf"], st["sk"], st["sv"], st["sg"], st["sc"], st["sh"], st["sf"])
```

```python
import functools

import jax
import jax.numpy as jnp
import numpy as np
from jax import lax
from jax.experimental import pallas as pl
from jax.experimental.pallas import tpu as pltpu

F32 = jnp.float32
BF16 = jnp.bfloat16

GLA_HEADS, GLA_DK, GLA_DV, GLA_LR, GLA_TAU = 4, 32, 64, 16, 16.0
GLA_DQK = GLA_HEADS * GLA_DK
GLA_DVW = GLA_HEADS * GLA_DV
SB_HEADS, SB_HD = 8, 64
SB_W = SB_HEADS * SB_HD
M_HEADS, M_HD, M_GROUPS, M_N, M_CONV = 4, 64, 2, 128, 4
M_DIN = M_HEADS * M_HD
M_CONV_DIM = M_DIN + 2 * M_GROUPS * M_N
FFN_CONV = 3
N_MOD = 6
EPS = 1e-6
NEG_BIG = -1e30
LOG2E = 1.4426950408889634

LANE = 128
SUBLANE = 8

C_GLA = 0
C_SBQ = C_GLA + 2 * GLA_DQK + 2 * GLA_DVW
C_SBK = C_SBQ + SB_W
C_SBV = C_SBK + SB_W
C_MZ = C_SBV + SB_W
C_MISC = C_MZ + M_DIN + M_CONV_DIM
D_INP = C_MISC + LANE
W_GLA = C_SBQ - C_GLA
W_M = C_MISC - C_MZ

VMEM_LIMIT = 56 * 1024 * 1024


def _cp(*sem):
    return pltpu.CompilerParams(dimension_semantics=sem, vmem_limit_bytes=VMEM_LIMIT)


def _dot(a, b):
    return jnp.dot(a.astype(BF16), b.astype(BF16), preferred_element_type=F32)


def _dot_nt(a, b):
    return lax.dot_general(a.astype(BF16), b.astype(BF16), (((1,), (1,)), ((), ())),
                           preferred_element_type=F32)


def _dot_tn(a, b):
    return lax.dot_general(a.astype(BF16), b.astype(BF16), (((0,), (0,)), ((), ())),
                           preferred_element_type=F32)


def _split3(x):
    hi = x.astype(BF16)
    r = x - hi.astype(F32)
    mid = r.astype(BF16)
    lo = (r - mid.astype(F32)).astype(BF16)
    return hi, mid, lo


def _dot_sel(x, e):
    hi, mid, lo = _split3(x)
    d = lambda p: jnp.dot(p, e, preferred_element_type=F32)
    return d(hi) + d(mid) + d(lo)


def _dot_sel_nt(e, x):
    hi, mid, lo = _split3(x)
    d = lambda p: lax.dot_general(e, p, (((1,), (1,)), ((), ())), preferred_element_type=F32)
    return d(hi) + d(mid) + d(lo)


def _sigmoid(x):
    return 1.0 / (1.0 + jnp.exp(-x))


def _silu(x):
    return x * _sigmoid(x)


def _softplus(x):
    return jnp.maximum(x, 0.0) + jnp.log1p(jnp.exp(-jnp.abs(x)))


def _log_sigmoid(x):
    return jnp.minimum(x, 0.0) - jnp.log1p(jnp.exp(-jnp.abs(x)))


def _rms_rows(x):
    return x * lax.rsqrt(jnp.mean(x * x, axis=-1, keepdims=True) + EPS)


def _seg_rms(x, eseg, width):
    xx = x * x
    hi = xx.astype(BF16)
    lo = (xx - hi.astype(F32)).astype(BF16)
    blocks = []
    for c in range(0, x.shape[1], LANE):
        d = lambda p: jnp.dot(p[:, c:c + LANE], eseg, preferred_element_type=F32)
        blocks.append(d(hi) + d(lo))
    ms = jnp.concatenate(blocks, axis=1) * (1.0 / width)
    return x * lax.rsqrt(ms + EPS)


def _row_to_col(row):
    n = row.shape[1]
    eye = lax.broadcasted_iota(jnp.int32, (n, n), 0) == lax.broadcasted_iota(jnp.int32, (n, n), 1)
    return jnp.sum(jnp.where(eye, row, 0.0), axis=1, keepdims=True)


def _col_to_row(col):
    n = col.shape[0]
    eye = lax.broadcasted_iota(jnp.int32, (n, n), 0) == lax.broadcasted_iota(jnp.int32, (n, n), 1)
    return jnp.sum(jnp.where(eye, col, 0.0), axis=0, keepdims=True)


def _cumsum(x, axis, seg=None):
    n = x.shape[axis] if seg is None else seg
    idx = lax.broadcasted_iota(jnp.int32, x.shape, axis)
    if seg is not None:
        idx = idx % seg
    s = 1
    while s < n:
        x = x + jnp.where(idx >= s, pltpu.roll(x, s, axis), 0.0)
        s *= 2
    return x


def _mod_kernel(c_ref, w_ref, b_ref, o_ref):
    o_ref[...] = _dot(_silu(c_ref[...]), w_ref[...]) + b_ref[...]


def _modulation(c_all, w_ada, b_ada):
    depth, d, nd = w_ada.shape
    n = c_all.shape[0]
    tn = 1024
    return pl.pallas_call(
        _mod_kernel,
        grid=(depth, nd // tn),
        in_specs=[pl.BlockSpec((n, d), lambda l, j: (0, 0)),
                  pl.BlockSpec((None, d, tn), lambda l, j: (l, 0, j)),
                  pl.BlockSpec((None, 1, tn), lambda l, j: (l, 0, j))],
        out_specs=pl.BlockSpec((None, n, tn), lambda l, j: (l, 0, j)),
        out_shape=jax.ShapeDtypeStruct((depth, n, nd), F32),
        compiler_params=_cp("arbitrary", "arbitrary"),
        name="adaln_mod",
    )(c_all, w_ada, b_ada.reshape(depth, 1, nd))


def _norm_mod(x, g, sc, sh):
    return _rms_rows(x) * g * (1.0 + sc) + sh


def _inproj_kernel(x_ref, sc_ref, sh_ref, g_ref, w_ref, wg_ref, bg_ref, gq_ref, gk_ref, eseg_ref,
                   gla_ref, la_ref, q_ref, k_ref, kb_ref, v_ref, vb_ref, m_ref, misc_ref, *, kv_transposed):
    h = _norm_mod(x_ref[...], g_ref[...], sc_ref[...], sh_ref[...])
    p = lax.dot_general(h.astype(BF16), w_ref[...], (((1,), (1,)), ((), ())), preferred_element_type=F32)
    gla_ref[...] = p[:, C_GLA:C_SBQ]
    misc = p[:, C_MISC:D_INP]
    misc_ref[...] = misc
    la_ref[...] = _log_sigmoid(_dot(misc, wg_ref[...]) + bg_ref[...]) * (1.0 / GLA_TAU)
    eseg = eseg_ref[...]
    q = _seg_rms(p[:, C_SBQ:C_SBK], eseg, SB_HD) * gq_ref[...]
    q_ref[...] = (q * (SB_HD ** -0.5 * LOG2E)).astype(BF16)
    k = _seg_rms(p[:, C_SBK:C_SBV], eseg, SB_HD) * gk_ref[...]
    k_ref[...] = k.T if kv_transposed else k
    kb_ref[...] = k.astype(BF16)
    v = p[:, C_SBV:C_MZ]
    v_ref[...] = v.T if kv_transposed else v
    vb_ref[...] = v.astype(BF16)
    m_ref[...] = p[:, C_MZ:C_MISC]


MOD_SH1, MOD_SC1, MOD_GT1, MOD_SH2, MOD_SC2, MOD_GT2 = range(N_MOD)


def _mod_spec(mod, layer, chunk, rows_per_mod):
    return pl.BlockSpec((None, None, None) + mod.shape[3:], lambda i: (layer, chunk, i // rows_per_mod, 0, 0))


def _layer_spec(w, layer):
    return pl.BlockSpec((None,) + w.shape[1:], lambda i: (layer, 0, 0))


def _in_proj(x, mod, layer, g, w, wg, bg, gq, gk, eseg, *, tm, rows_per_mod, kv_transposed):
    r, d = x.shape
    const = lambda shape: pl.BlockSpec(shape, lambda i: (0,) * len(shape))
    row = lambda n: pl.BlockSpec((tm, n), lambda i: (i, 0))
    outs = [(W_GLA, F32), (LANE, F32), (SB_W, BF16), (SB_W, F32), (SB_W, BF16), (SB_W, F32), (SB_W, BF16),
            (W_M, F32), (LANE, F32)]
    out_specs = [row(n) for n, _ in outs]
    out_shape = [jax.ShapeDtypeStruct((r, n), dt) for n, dt in outs]
    if kv_transposed:
        seq = rows_per_mod * tm
        for idx in (3, 5):
            out_specs[idx] = pl.BlockSpec((None, SB_W, tm), lambda i: (i // rows_per_mod, 0, i % rows_per_mod))
            out_shape[idx] = jax.ShapeDtypeStruct((r // seq, SB_W, seq), F32)
    return pl.pallas_call(
        functools.partial(_inproj_kernel, kv_transposed=kv_transposed),
        grid=(r // tm,),
        in_specs=[row(d), _mod_spec(mod, layer, MOD_SC1, rows_per_mod), _mod_spec(mod, layer, MOD_SH1, rows_per_mod),
                  const((1, d)), _layer_spec(w, layer), const((LANE, LANE)),
                  const((1, LANE)), const((1, SB_W)), const((1, SB_W)), const((LANE, LANE))],
        out_specs=out_specs,
        out_shape=out_shape,
        compiler_params=_cp("arbitrary"),
        name="in_proj",
    )(x, mod, mod, g, w, wg, bg, gq, gk, eseg)


GLA_C = SUBLANE
GLA_UNROLL = 32


def _gla_prompt_kernel(gla_ref, la_ref, eexp_ref, bd_ref, eseg_ref, gn_ref, o_ref, st_out_ref,
                       st_ref, b_ref, acc_ref):
    i = pl.program_id(1)
    tt = la_ref.shape[0]

    @pl.when(i == 0)
    def _():
        st_ref[...] = jnp.zeros_like(st_ref)

    b_ref[...] = _cumsum(la_ref[...], 0, seg=GLA_C)
    eexp = eexp_ref[...]
    bd = bd_ref[...]
    rowid = lax.broadcasted_iota(jnp.int32, (GLA_C, GLA_DQK), 0)
    rowid_v = lax.broadcasted_iota(jnp.int32, (GLA_C, GLA_DVW), 0)

    def group(gi, carry):
        steps = []
        for u in range(GLA_UNROLL):
            r0 = pl.multiple_of((gi * GLA_UNROLL + u) * GLA_C, GLA_C)
            bc = b_ref[pl.ds(r0, GLA_C), :]
            qc = gla_ref[pl.ds(r0, GLA_C), 0:GLA_DQK] * (GLA_DK ** -0.5)
            kc = gla_ref[pl.ds(r0, GLA_C), GLA_DQK:2 * GLA_DQK]
            vc = gla_ref[pl.ds(r0, GLA_C), 2 * GLA_DQK:2 * GLA_DQK + GLA_DVW]
            steps.append((r0, bc, qc, kc, vc, bc[GLA_C - 1:GLA_C, :]))
        ps = []
        for r0, bc, qc, kc, vc, bl in steps:
            for t in range(GLA_C):
                d = jnp.where(rowid <= t, bc[t:t + 1, :] - bc, NEG_BIG)
                ps.append(jnp.exp(d) * (qc[t:t + 1, :] * kc))
        sc = _dot(jnp.concatenate(ps, axis=0), eexp)
        upds = [_dot_tn(vc, kc * jnp.exp(bl - bc)) * bd for r0, bc, qc, kc, vc, bl in steps]
        st = st_ref[...]
        for u, (r0, bc, qc, kc, vc, bl) in enumerate(steps):
            o = _dot_nt(qc * jnp.exp(bc), st)
            st = st * jnp.exp(bl) + upds[u]
            for t in range(GLA_C):
                row0 = (u * GLA_C + t) * GLA_C
                ot = jnp.sum(sc[row0:row0 + GLA_C, :] * vc, axis=0, keepdims=True)
                o = o + jnp.where(rowid_v == t, ot, 0.0)
            acc_ref[pl.ds(r0, GLA_C), :] = o
        st_ref[...] = st
        return carry

    lax.fori_loop(0, tt // (GLA_C * GLA_UNROLL), group, 0)
    gg = gla_ref[:, 2 * GLA_DQK + GLA_DVW:2 * GLA_DQK + 2 * GLA_DVW]
    o = _seg_rms(acc_ref[...], eseg_ref[...], GLA_DV) * gn_ref[...] * _silu(gg)
    o_ref[...] = o.astype(BF16)
    st_out_ref[...] = st_ref[...]


def _gla_prompt(gla, la, eexp, bd, eseg, gn, *, bsz, seq, tt):
    nt = seq // tt
    const = lambda shape: pl.BlockSpec(shape, lambda b, i: (0,) * len(shape))
    return pl.pallas_call(
        _gla_prompt_kernel,
        grid=(bsz, nt),
        in_specs=[pl.BlockSpec((tt, W_GLA), lambda b, i: (b * nt + i, 0)),
                  pl.BlockSpec((tt, LANE), lambda b, i: (b * nt + i, 0)),
                  const((GLA_DQK, GLA_DVW)), const((GLA_DVW, GLA_DQK)), const((LANE, LANE)),
                  const((1, GLA_DVW))],
        out_specs=[pl.BlockSpec((tt, GLA_DVW), lambda b, i: (b * nt + i, 0)),
                   pl.BlockSpec((None, GLA_DVW, GLA_DQK), lambda b, i: (b, 0, 0))],
        out_shape=[jax.ShapeDtypeStruct((bsz * seq, GLA_DVW), BF16),
                   jax.ShapeDtypeStruct((bsz, GLA_DVW, GLA_DQK), F32)],
        scratch_shapes=[pltpu.VMEM((GLA_DVW, GLA_DQK), F32), pltpu.VMEM((tt, GLA_DQK), F32),
                        pltpu.VMEM((tt, GLA_DVW), F32)],
        compiler_params=_cp("arbitrary", "arbitrary"),
        name="gla_prompt",
    )(gla, la, eexp, bd, eseg, gn)


SB_TK = LANE
SB_NB = 3
SB_UNROLL = 8


def _neg_abs(x):
    return lax.bitcast_convert_type(lax.bitcast_convert_type(x, jnp.int32) | jnp.int32(-2 ** 31), F32)


def _sb_prompt_kernel(q_ref, k_ref, v_ref, bcol_ref, uo_ref, hm_ref, eseg_ref, gn_ref, o_ref,
                      acc_ref, car_ref, kk_ref, vv_ref, qx_ref):
    i = pl.program_id(2)
    tq = q_ref.shape[0]
    ndiag = tq // SB_TK
    nblk = kk_ref.shape[0]

    @pl.when(i == 0)
    def _():
        m0 = hm_ref[0:1, :]
        m1 = hm_ref[1:2, :]
        bcol = bcol_ref[...]

        def fill(j, carry):
            k0 = pl.multiple_of(j * SB_TK, SB_TK)
            kb = k_ref[pl.ds(k0, SB_TK), :]
            vb = v_ref[pl.ds(k0, SB_TK), :]
            kk_ref[j] = jnp.concatenate([jnp.concatenate([kb * m0, kb * m1], axis=0), bcol], axis=1)
            vv_ref[j] = jnp.concatenate([vb * m0, vb * m1], axis=0)
            return carry

        lax.fori_loop(0, nblk, fill, 0)

    ones = jnp.where(lax.broadcasted_iota(jnp.int32, (tq, LANE), 1) < SB_NB, 1.0, 0.0).astype(BF16)
    qx_ref[...] = jnp.concatenate([q_ref[...], ones], axis=1)
    uo = uo_ref[...]
    acc_ref[...] = jnp.zeros_like(acc_ref)
    car_ref[...] = jnp.zeros_like(car_ref)

    def block(j, r0, diag):
        rows = tq - r0
        z = lax.dot_general(qx_ref[r0:, :], kk_ref[j], (((1,), (1,)), ((), ())),
                            preferred_element_type=F32)
        sp = jnp.maximum(z, 0.0) + jnp.log(1.0 + jnp.exp2(_neg_abs(z))) * LOG2E
        if diag:
            valid = (lax.broadcasted_iota(jnp.int32, (SB_TK, 2 * SB_TK), 1) % SB_TK
                     < lax.broadcasted_iota(jnp.int32, (SB_TK, 2 * SB_TK), 0))
            def mask_head(t):
                head = jnp.where(valid, t[:SB_TK], 0.0)
                return head if rows == SB_TK else jnp.concatenate([head, t[SB_TK:]], axis=0)

            sp = mask_head(sp)
        incl = jnp.dot(sp.astype(BF16), uo, preferred_element_type=F32) + car_ref[r0:, :]
        w = jnp.exp2(z - incl)
        if diag:
            w = mask_head(w)
        car_ref[r0:, :SB_TK] += jnp.sum(sp[:, :SB_TK], axis=1, keepdims=True)
        car_ref[r0:, SB_TK:] += jnp.sum(sp[:, SB_TK:], axis=1, keepdims=True)
        acc_ref[r0:, :] += jnp.dot(w.astype(BF16), vv_ref[j], preferred_element_type=F32)

    for c in range(ndiag - 1, -1, -1):
        block(i * ndiag + c, c * SB_TK, True)

    n_off = i * ndiag
    n_main = n_off // SB_UNROLL

    def body(jj, carry):
        for u in range(SB_UNROLL):
            block(n_off - 1 - jj * SB_UNROLL - u, 0, False)
        return carry

    lax.fori_loop(0, n_main, body, 0)
    if SB_UNROLL > ndiag:
        @pl.when(n_off - n_main * SB_UNROLL > 0)
        def _():
            for u in range(ndiag):
                block(ndiag - 1 - u, 0, False)

    o_ref[...] = (_seg_rms(acc_ref[...], eseg_ref[...], SB_HD) * gn_ref[...]).astype(BF16)


def _sb_prompt(q, k, v, bias_cols, uo, hm, eseg, gn, *, bsz, seq, tq):
    nq = seq // tq
    npair = SB_HEADS // 2
    const = lambda shape: pl.BlockSpec(shape, lambda b, p, i: (0,) * len(shape))
    return pl.pallas_call(
        _sb_prompt_kernel,
        grid=(bsz, npair, nq),
        in_specs=[pl.BlockSpec((tq, LANE), lambda b, p, i: (b * nq + i, p)),
                  pl.BlockSpec((seq, LANE), lambda b, p, i: (b, p)),
                  pl.BlockSpec((seq, LANE), lambda b, p, i: (b, p)),
                  pl.BlockSpec((None, 2 * SB_TK, LANE), lambda b, p, i: (p, 0, 0)),
                  const((2 * SB_TK, 2 * SB_TK)), const((SUBLANE, LANE)), const((LANE, LANE)), const((1, LANE))],
        out_specs=pl.BlockSpec((tq, LANE), lambda b, p, i: (b * nq + i, p)),
        out_shape=jax.ShapeDtypeStruct((bsz * seq, SB_W), BF16),
        scratch_shapes=[pltpu.VMEM((tq, LANE), F32), pltpu.VMEM((tq, 2 * SB_TK), F32),
                        pltpu.VMEM((seq // SB_TK, 2 * SB_TK, 2 * LANE), BF16),
                        pltpu.VMEM((seq // SB_TK, 2 * SB_TK, LANE), BF16),
                        pltpu.VMEM((tq, 2 * LANE), BF16)],
        compiler_params=_cp("arbitrary", "arbitrary", "arbitrary"),
        name="sb_prompt",
    )(q, k, v, bias_cols, uo, hm, eseg, gn)


def _ssm_params(dt_raw, dtb, nega):
    dt = _softplus(dt_raw + dtb)
    return dt, dt * nega


SSD_C = 128


def _ssd_prompt_kernel(m_ref, prev_ref, misc_ref, dtt_ref, cw_ref, cb_ref, dtb_l_ref, nega_l_ref,
                       dtb_c_ref, nega_c_ref, e4_ref, e4t_ref, md_ref, gn_ref, o_ref, hs_out_ref,
                       hs_ref):
    i = pl.program_id(1)
    tt = m_ref.shape[0]
    cl = SSD_C

    @pl.when(i == 0)
    def _():
        hs_ref[...] = jnp.zeros_like(hs_ref)

    e4 = e4_ref[...]
    row = lax.broadcasted_iota(jnp.int32, (cl, M_CONV_DIM), 0)
    row8 = lax.broadcasted_iota(jnp.int32, (SUBLANE, M_CONV_DIM), 0)
    causal = lax.broadcasted_iota(jnp.int32, (cl, cl), 0) >= lax.broadcasted_iota(jnp.int32, (cl, cl), 1)
    lane_head = lax.broadcasted_iota(jnp.int32, (cl, M_DIN), 1) // M_HD
    rep = M_HEADS // M_GROUPS
    half = M_DIN // M_GROUPS

    for r0 in range(0, tt, cl):
        z = m_ref[r0:r0 + cl, 0:M_DIN]
        xbc = m_ref[r0:r0 + cl, M_DIN:M_DIN + M_CONV_DIM]
        if r0 == 0:
            prev = jnp.where(i > 0, prev_ref[:, M_DIN:M_DIN + M_CONV_DIM], 0.0)
        else:
            prev = m_ref[r0 - SUBLANE:r0, M_DIN:M_DIN + M_CONV_DIM]
        acc = cb_ref[...] + cw_ref[M_CONV - 1:M_CONV, :] * xbc
        for s in range(1, M_CONV):
            head = jnp.where(row8 < s, pltpu.roll(prev, s, 0), 0.0)
            head = jnp.concatenate([head, jnp.zeros((cl - SUBLANE, M_CONV_DIM), F32)], axis=0)
            shifted = jnp.where(row < s, head, pltpu.roll(xbc, s, 0))
            acc = acc + cw_ref[M_CONV - 1 - s:M_CONV - s, :] * shifted
        xc = _silu(acc)
        x = xc[:, 0:M_DIN]
        bm = xc[:, M_DIN:M_DIN + M_GROUPS * M_N]
        cm = xc[:, M_DIN + M_GROUPS * M_N:]

        dt_c, a_c = _ssm_params(misc_ref[r0:r0 + cl, :], dtb_l_ref[...], nega_l_ref[...])
        cs_c = _cumsum(a_c, 0)
        dt_r, a_r = _ssm_params(dtt_ref[:, r0:r0 + cl], dtb_c_ref[...], nega_c_ref[...])
        cs_r = _cumsum(a_r, 1)
        xdt = x * _dot_sel(dt_c, e4)
        ecs = _dot_sel(jnp.exp(cs_c), e4)

        g = [_dot_nt(cm[:, gi * M_N:(gi + 1) * M_N], bm[:, gi * M_N:(gi + 1) * M_N]) for gi in range(M_GROUPS)]
        y = jnp.zeros((cl, M_DIN), F32)
        for h in range(M_HEADS):
            decay = jnp.exp(jnp.where(causal, cs_c[:, h:h + 1] - cs_r[h:h + 1, :], NEG_BIG))
            yh = _dot(g[h // rep] * decay, xdt)
            y = jnp.where(lane_head == h, yh, y)

        cs_last = cs_c[cl - 1:cl, :]
        xw = xdt * _dot_sel(jnp.exp(cs_last - cs_c), e4)
        zz = _dot_tn(xw, bm)
        upd = jnp.concatenate([zz[:half, :M_N], zz[half:, M_N:]], axis=0)
        dec = _dot_sel_nt(e4t_ref[...], jnp.broadcast_to(jnp.exp(cs_last), (M_N, LANE)))

        hs = hs_ref[...]
        yi = [_dot_nt(cm[:, gi * M_N:(gi + 1) * M_N], hs) for gi in range(M_GROUPS)]
        hs_ref[...] = hs * dec + upd
        y = y + jnp.where(lane_head < rep, yi[0], yi[1]) * ecs

        y = (y + md_ref[...] * x) * _silu(z)
        y = jnp.concatenate([_rms_rows(y[:, gi * half:(gi + 1) * half]) for gi in range(M_GROUPS)], axis=1)
        o_ref[r0:r0 + cl, :] = (y * gn_ref[...]).astype(BF16)

    hs_out_ref[...] = hs_ref[...]


def _ssd_prompt(m, misc, dtt, cw, cb, dtb_l, nega_l, dtb_c, nega_c, e4, e4t, md, gn, *, bsz, seq, tt):
    nt = seq // tt
    per8 = tt // SUBLANE
    const = lambda shape: pl.BlockSpec(shape, lambda b, i: (0,) * len(shape))
    return pl.pallas_call(
        _ssd_prompt_kernel,
        grid=(bsz, nt),
        in_specs=[pl.BlockSpec((tt, W_M), lambda b, i: (b * nt + i, 0)),
                  pl.BlockSpec((SUBLANE, W_M), lambda b, i: (jnp.maximum((b * nt + i) * per8 - 1, 0), 0)),
                  pl.BlockSpec((tt, LANE), lambda b, i: (b * nt + i, 0)),
                  pl.BlockSpec((None, SUBLANE, tt), lambda b, i: (b, 0, i)),
                  const((M_CONV, M_CONV_DIM)), const((1, M_CONV_DIM)), const((1, LANE)), const((1, LANE)),
                  const((SUBLANE, 1)), const((SUBLANE, 1)), const((LANE, M_DIN)), const((M_DIN, LANE)),
                  const((1, M_DIN)), const((1, M_DIN))],
        out_specs=[pl.BlockSpec((tt, M_DIN), lambda b, i: (b * nt + i, 0)),
                   pl.BlockSpec((None, M_DIN, M_N), lambda b, i: (b, 0, 0))],
        out_shape=[jax.ShapeDtypeStruct((bsz * seq, M_DIN), BF16),
                   jax.ShapeDtypeStruct((bsz, M_DIN, M_N), F32)],
        scratch_shapes=[pltpu.VMEM((M_DIN, M_N), F32)],
        compiler_params=_cp("arbitrary", "arbitrary"),
        name="ssd_prompt",
    )(m, m, misc, dtt, cw, cb, dtb_l, nega_l, dtb_c, nega_c, e4, e4t, md, gn)


def _step_kernel(gla_ref, la_ref, v4_ref, gg4_ref, s0_ref, m_ref, buf_ref, misc_ref, h0_ref,
                 cw_ref, cb_ref, dtb_ref, nega_ref, e4_ref, md_ref, gng_ref, gnm_ref,
                 g_ref, s_ref, y_ref, h_ref):
    q_col = _row_to_col(gla_ref[:, 0:GLA_DQK] * (GLA_DK ** -0.5))
    k_col = _row_to_col(gla_ref[:, GLA_DQK:2 * GLA_DQK])
    dec_col = _row_to_col(jnp.exp(la_ref[...]))
    v4 = v4_ref[...]
    v_exp = jnp.concatenate([jnp.broadcast_to(v4[h:h + 1, :], (GLA_DK, GLA_DV)) for h in range(GLA_HEADS)],
                            axis=0)
    s = s0_ref[...] * dec_col + k_col * v_exp
    s_ref[...] = s
    o4 = jnp.sum((q_col * s).reshape(GLA_HEADS, GLA_DK, GLA_DV), axis=1)
    g_ref[...] = _rms_rows(o4) * gng_ref[...] * _silu(gg4_ref[...])

    z = m_ref[:, 0:M_DIN]
    acc = cb_ref[...] + cw_ref[M_CONV - 1:M_CONV, :] * m_ref[:, M_DIN:M_DIN + M_CONV_DIM]
    for s_ in range(M_CONV - 1):
        acc = acc + cw_ref[s_:s_ + 1, :] * buf_ref[s_:s_ + 1, :]
    xc = _silu(acc)
    x = xc[:, 0:M_DIN]
    bm = xc[:, M_DIN:M_DIN + M_GROUPS * M_N]
    cm = xc[:, M_DIN + M_GROUPS * M_N:]
    dt, a = _ssm_params(misc_ref[...], dtb_ref[...], nega_ref[...])
    e4 = e4_ref[...]
    xdt_col = _row_to_col(x * _dot_sel(dt, e4))
    deca_col = _row_to_col(_dot_sel(jnp.exp(a), e4))
    half = M_DIN // M_GROUPS
    spread = lambda t: jnp.concatenate(
        [jnp.broadcast_to(t[:, gi * M_N:(gi + 1) * M_N], (half, M_N)) for gi in range(M_GROUPS)], axis=0)
    hs = h0_ref[...] * deca_col + xdt_col * spread(bm)
    h_ref[...] = hs
    y = _col_to_row(jnp.sum(hs * spread(cm), axis=1, keepdims=True))
    y = (y + md_ref[...] * x) * _silu(z)
    y = jnp.concatenate([_rms_rows(y[:, gi * half:(gi + 1) * half]) for gi in range(M_GROUPS)], axis=1)
    y_ref[...] = y * gnm_ref[...]


def _step_mixers(gla, la, v4, gg4, s0, m, buf, misc, h0, cw, cb, dtb, nega, e4, md, gng, gnm):
    nb = gla.shape[0]
    per = lambda *shape: pl.BlockSpec((None,) + shape, lambda b: (b,) + (0,) * len(shape))
    const = lambda shape: pl.BlockSpec(shape, lambda b: (0,) * len(shape))
    return pl.pallas_call(
        _step_kernel,
        grid=(nb,),
        in_specs=[per(1, W_GLA), per(1, LANE), per(GLA_HEADS, GLA_DV), per(GLA_HEADS, GLA_DV),
                  per(GLA_DQK, GLA_DV), per(1, W_M), per(M_CONV - 1, M_CONV_DIM), per(1, LANE),
                  per(M_DIN, M_N),
                  const((M_CONV, M_CONV_DIM)), const((1, M_CONV_DIM)), const((1, LANE)), const((1, LANE)),
                  const((LANE, M_DIN)), const((1, M_DIN)), const((1, GLA_DV)), const((1, M_DIN))],
        out_specs=[per(GLA_HEADS, GLA_DV), per(GLA_DQK, GLA_DV), per(1, M_DIN), per(M_DIN, M_N)],
        out_shape=[jax.ShapeDtypeStruct((nb, GLA_HEADS, GLA_DV), F32),
                   jax.ShapeDtypeStruct((nb, GLA_DQK, GLA_DV), F32),
                   jax.ShapeDtypeStruct((nb, 1, M_DIN), F32),
                   jax.ShapeDtypeStruct((nb, M_DIN, M_N), F32)],
        compiler_params=_cp("arbitrary"),
        name="step_mixers",
    )(gla.reshape(nb, 1, W_GLA), la.reshape(nb, 1, LANE), v4, gg4, s0, m.reshape(nb, 1, W_M), buf,
      misc.reshape(nb, 1, LANE), h0, cw, cb, dtb, nega, e4, md, gng, gnm)


SB_PP = 32


def _sb_decode_kernel(pt_ref, q_ref, *refs):
    ks = refs[0:SB_PP]
    vs = refs[SB_PP:2 * SB_PP]
    bias_ref, gn_ref, uinc_ref, o_ref, acc_ref, car_ref, qb_ref = refs[2 * SB_PP:]
    j = pl.program_id(1)
    page = ks[0].shape[2]
    uinc = uinc_ref[...]

    @pl.when(j == 0)
    def _():
        acc_ref[...] = jnp.zeros_like(acc_ref)
        car_ref[...] = jnp.zeros_like(car_ref)
        qb_ref[...] = jnp.broadcast_to(q_ref[...], qb_ref.shape)

    bias = bias_ref[...]
    for r in range(SB_PP - 1, -1, -1):
        prod = (ks[r][...] * qb_ref[...]).reshape(SB_HEADS, SB_HD // SUBLANE, SUBLANE, page)
        z = jnp.sum(jnp.sum(prod, axis=1), axis=1) + bias
        sp = jnp.maximum(z, 0.0) + jnp.log(1.0 + jnp.exp2(_neg_abs(z))) * LOG2E
        ls = z - sp
        hi = sp.astype(BF16)
        lo = (sp - hi.astype(F32)).astype(BF16)
        incl = (jnp.dot(hi, uinc, preferred_element_type=F32) + jnp.dot(lo, uinc, preferred_element_type=F32))
        car = car_ref[...]
        w = jnp.exp2(ls - (incl - sp + car))
        car_ref[...] = car + incl[:, 0:1]
        for h in range(SB_HEADS):
            acc_ref[h] += vs[r][h] * w[h:h + 1, :]

    @pl.when(j == pl.num_programs(1) - 1)
    def _():
        o_ref[...] = _rms_rows(jnp.sum(acc_ref[...], axis=-1)) * gn_ref[...]


def _sb_decode(page_table, q, cache_kt, cache_vt, layer, bias_col, gn, uinc):
    nb, n_pages = page_table.shape
    page = cache_kt.shape[4]
    ng = n_pages // SB_PP
    pt = page_table.reshape(-1)

    def kv_spec(r):
        return pl.BlockSpec((None, None, SB_HEADS, SB_HD, page),
                            lambda b, j, pt_ref: (layer, pt_ref[b * n_pages + (ng - 1 - j) * SB_PP + r], 0, 0, 0))

    const = lambda shape: pl.BlockSpec(shape, lambda b, j, pt_ref: (0,) * len(shape))
    grid_spec = pltpu.PrefetchScalarGridSpec(
        num_scalar_prefetch=1,
        grid=(nb, ng),
        in_specs=[pl.BlockSpec((None, SB_HEADS, SB_HD, 1), lambda b, j, pt_ref: (b, 0, 0, 0))]
        + [kv_spec(r) for r in range(SB_PP)] + [kv_spec(r) for r in range(SB_PP)]
        + [const((SB_HEADS, 1)), const((1, SB_HD)), const((page, page))],
        out_specs=pl.BlockSpec((None, SB_HEADS, SB_HD), lambda b, j, pt_ref: (b, 0, 0)),
        scratch_shapes=[pltpu.VMEM((SB_HEADS, SB_HD, page), F32), pltpu.VMEM((SB_HEADS, 1), F32),
                        pltpu.VMEM((SB_HEADS, SB_HD, page), F32)],
    )
    out = pl.pallas_call(
        _sb_decode_kernel,
        grid_spec=grid_spec,
        out_shape=jax.ShapeDtypeStruct((nb, SB_HEADS, SB_HD), F32),
        compiler_params=_cp("arbitrary", "arbitrary"),
        name="sb_decode",
    )(pt, q, *([cache_kt] * SB_PP), *([cache_vt] * SB_PP), bias_col, gn, uinc)
    return out.reshape(nb, SB_W)


def _outproj_kernel(x_ref, gate_ref, g_ref, s_ref, y_ref, w_ref, o_ref):
    mix = (_dot(g_ref[...], w_ref[0:GLA_DVW, :]) + _dot(s_ref[...], w_ref[GLA_DVW:GLA_DVW + SB_W, :])
           + _dot(y_ref[...], w_ref[GLA_DVW + SB_W:, :]))
    o_ref[...] = x_ref[...] + gate_ref[...] * mix


def _out_proj(x, mod, layer, g, s, y, w, *, tm, rows_per_mod):
    r, d = x.shape
    row = lambda n: pl.BlockSpec((tm, n), lambda i: (i, 0))
    return pl.pallas_call(
        _outproj_kernel,
        grid=(r // tm,),
        in_specs=[row(d), _mod_spec(mod, layer, MOD_GT1, rows_per_mod),
                  row(GLA_DVW), row(SB_W), row(M_DIN), _layer_spec(w, layer)],
        out_specs=row(d),
        out_shape=jax.ShapeDtypeStruct((r, d), F32),
        compiler_params=_cp("arbitrary"),
        name="out_proj",
    )(x, mod, g, s, y, w)


def _ffn_up_kernel(x_ref, sc_ref, sh_ref, g_ref, w_ref, u_ref):
    h = _norm_mod(x_ref[...], g_ref[...], sc_ref[...], sh_ref[...])
    u_ref[...] = jnp.dot(h.astype(BF16), w_ref[...], preferred_element_type=F32)


def _ffn_up(x, mod, layer, g, w, *, tm, rows_per_mod):
    r, d = x.shape
    n = w.shape[2]
    return pl.pallas_call(
        _ffn_up_kernel,
        grid=(r // tm,),
        in_specs=[pl.BlockSpec((tm, d), lambda i: (i, 0)), _mod_spec(mod, layer, MOD_SC2, rows_per_mod),
                  _mod_spec(mod, layer, MOD_SH2, rows_per_mod),
                  pl.BlockSpec((1, d), lambda i: (0, 0)), _layer_spec(w, layer)],
        out_specs=pl.BlockSpec((tm, n), lambda i: (i, 0)),
        out_shape=jax.ShapeDtypeStruct((r, n), F32),
        compiler_params=_cp("arbitrary"),
        name="ffn_up",
    )(x, mod, mod, g, w)


FFN_CK = 256
FFN_RB = 128


def _ffn_act_down(taps, cw_ref, cb_ref, w_ref):
    dff = w_ref.shape[0]

    def conv(c0):
        u, u1, u2 = taps(c0)
        cs = slice(c0, c0 + FFN_CK)
        return cb_ref[:, cs] + cw_ref[0:1, cs] * u2 + cw_ref[1:2, cs] * u1 + cw_ref[2:3, cs] * u

    out = None
    for c0 in range(0, dff, FFN_CK):
        part = _dot(_silu(conv(c0)) * conv(dff + c0), w_ref[c0:c0 + FFN_CK, :])
        out = part if out is None else out + part
    return out


def _ffn_up_act_kernel(x_ref, sc_ref, sh_ref, g_ref, w_ref, cw_ref, cb_ref, a_ref, tail_ref, carry_ref,
                       u_ref, *, tiles_per_seq):
    i = pl.program_id(0)
    tm = x_ref.shape[0]
    dff = a_ref.shape[1]
    first = i % tiles_per_seq == 0
    h = _norm_mod(x_ref[...], g_ref[...], sc_ref[...], sh_ref[...]).astype(BF16)
    row8 = lax.broadcasted_iota(jnp.int32, (SUBLANE, FFN_CK), 0)

    def up(base, slot, half):
        cs = slice(base, base + FFN_CK)
        u = jnp.dot(h, w_ref[:, cs], preferred_element_type=F32)
        u_ref[slot, half, 0:SUBLANE, :] = jnp.where(first, 0.0, carry_ref[:, cs])
        u_ref[slot, half, SUBLANE:, :] = u
        last = u[tm - SUBLANE:, :]
        carry_ref[:, cs] = last
        tail_ref[:, cs] = last

    def conv(slot, half, base, rb):
        cs = slice(base, base + FFN_CK)
        acc = cb_ref[:, cs]
        for s in range(FFN_CONV):
            r0 = SUBLANE + rb - s
            acc = acc + cw_ref[FFN_CONV - 1 - s:FFN_CONV - s, cs] * u_ref[slot, half, r0:r0 + FFN_RB, :]
        return acc

    for n, c0 in enumerate(range(0, dff, FFN_CK)):
        slot = n % 2
        up(c0, slot, 0)
        up(dff + c0, slot, 1)
        for rb in range(0, tm, FFN_RB):
            a_ref[rb:rb + FFN_RB, c0:c0 + FFN_CK] = (
                _silu(conv(slot, 0, c0, rb)) * conv(slot, 1, dff + c0, rb)).astype(BF16)


def _ffn_up_act(x, mod, layer, g, w, cw, cb, *, tm, seq):
    r, d = x.shape
    n = w.shape[2]
    dff = n // 2
    tiles = seq // tm
    const = lambda shape: pl.BlockSpec(shape, lambda i: (0,) * len(shape))
    return pl.pallas_call(
        functools.partial(_ffn_up_act_kernel, tiles_per_seq=tiles),
        grid=(r // tm,),
        in_specs=[pl.BlockSpec((tm, d), lambda i: (i, 0)), _mod_spec(mod, layer, MOD_SC2, tiles),
                  _mod_spec(mod, layer, MOD_SH2, tiles), const((1, d)), _layer_spec(w, layer),
                  const((FFN_CONV, n)), const((1, n))],
        out_specs=[pl.BlockSpec((tm, dff), lambda i: (i, 0)),
                   pl.BlockSpec((None, SUBLANE, n), lambda i: (i // tiles, 0, 0))],
        out_shape=[jax.ShapeDtypeStruct((r, dff), BF16), jax.ShapeDtypeStruct((r // seq, SUBLANE, n), F32)],
        scratch_shapes=[pltpu.VMEM((SUBLANE, n), F32), pltpu.VMEM((2, 2, SUBLANE + tm, FFN_CK), F32)],
        compiler_params=_cp("arbitrary"),
        name="ffn_up_act",
    )(x, mod, mod, g, w, cw, cb)


def _ffn_down_kernel(x_ref, gate_ref, a_ref, w_ref, o_ref):
    o_ref[...] = x_ref[...] + gate_ref[...] * jnp.dot(a_ref[...], w_ref[...], preferred_element_type=F32)


def _ffn_down(x, mod, layer, a, w, *, tm, seq):
    r, d = x.shape
    return pl.pallas_call(
        _ffn_down_kernel,
        grid=(r // tm,),
        in_specs=[pl.BlockSpec((tm, d), lambda i: (i, 0)), _mod_spec(mod, layer, MOD_GT2, seq // tm),
                  pl.BlockSpec((tm, a.shape[1]), lambda i: (i, 0)), _layer_spec(w, layer)],
        out_specs=pl.BlockSpec((tm, d), lambda i: (i, 0)),
        out_shape=jax.ShapeDtypeStruct((r, d), F32),
        compiler_params=_cp("arbitrary"),
        name="ffn_down",
    )(x, mod, a, w)


def _ffn_down_step_kernel(x_ref, gate_ref, u_ref, buf_ref, cw_ref, cb_ref, w_ref, o_ref):
    taps = lambda c0: (u_ref[:, c0:c0 + FFN_CK], buf_ref[1, :, c0:c0 + FFN_CK], buf_ref[0, :, c0:c0 + FFN_CK])
    o_ref[...] = x_ref[...] + gate_ref[...] * _ffn_act_down(taps, cw_ref, cb_ref, w_ref)


def _ffn_down_step(x, mod, layer, u, buf_t, cw, cb, w):
    full = lambda a: pl.BlockSpec(a.shape, lambda i: (0,) * a.ndim)
    args = (x, mod, u, buf_t, cw, cb, w)
    return pl.pallas_call(
        _ffn_down_step_kernel,
        grid=(1,),
        in_specs=[full(x), _mod_spec(mod, layer, MOD_GT2, 1), full(u), full(buf_t), full(cw), full(cb),
                  _layer_spec(w, layer)],
        out_specs=full(x),
        out_shape=jax.ShapeDtypeStruct(x.shape, F32),
        compiler_params=_cp("arbitrary"),
        name="ffn_down_step",
    )(*args)


def _same_segment(n, width):
    i = np.arange(n)
    return jnp.asarray((i[:, None] // width) == (i[None, :] // width), BF16)


def _constants():
    k = np.arange(GLA_DQK)
    v = np.arange(GLA_DVW)
    eexp = (k[:, None] // GLA_DK) == (v[None, :] // GLA_DV)
    j = np.arange(SB_TK)
    later = j[:, None] >= j[None, :]
    uo = np.block([[later, np.zeros_like(later)], [np.zeros_like(later), later]])
    lane = np.arange(LANE)
    hm_pair = np.zeros((SUBLANE, LANE), np.float32)
    hm_pair[0] = lane < SB_HD
    hm_pair[1] = lane >= SB_HD
    p = np.arange(M_DIN)
    e4 = lane[:, None] == (p[None, :] // M_HD)
    return dict(
        eexp=jnp.asarray(eexp, BF16), bd=jnp.asarray(eexp.T, F32),
        uo=jnp.asarray(uo, BF16), hm_pair=jnp.asarray(hm_pair, BF16), eseg64=_same_segment(LANE, 64),
        e4=jnp.asarray(e4, BF16), e4t=jnp.asarray(e4.T, BF16))


def _pad_lanes(v, n=LANE):
    return jnp.zeros((1, n), F32).at[0, :v.shape[0]].set(v)


def _pad_col(v, n=SUBLANE):
    return jnp.zeros((n, 1), F32).at[:v.shape[0], 0].set(v)


def kernel(x_prompt, x_sample, cache_sb_k, cache_sb_v, state_gla, state_mamba_conv, state_mamba_ssm, state_ffn_conv, page_table, c_prompt, c_sample, norm1_g, w_ada, b_ada, w_in, gla_w_gate2, gla_b_gate, gla_norm_g, sb_q_norm_g, sb_k_norm_g, sb_o_norm_g, sb_bias, m_conv_w, m_conv_b, m_dt_bias, m_a_log, m_d, m_norm_g, w_out, norm2_g, ffn_w_up, ffn_conv_w, ffn_conv_b, ffn_w_down):
    bsz, seq, d = x_prompt.shape
    nb = x_sample.shape[0]
    depth = w_in.shape[0]
    dff = ffn_w_down.shape[1]
    n_pool, page = cache_sb_k.shape[1], cache_sb_k.shape[2]
    assert x_sample.shape[1] == 1 and d % LANE == 0 and seq % 256 == 0
    cst = _constants()

    tm = 256
    tm_mm = 512
    tm_res = 1024
    tq = 512
    tt_gla = 256
    tt_ssd = 8 * SSD_C
    assert SB_UNROLL == 2 * (tq // SB_TK) or (tq // SB_TK) % SB_UNROLL == 0
    assert page_table.shape[1] % SB_PP == 0
    assert seq % tq == 0 and seq % tm == 0 and seq % tm_mm == 0 and seq % tm_res == 0
    assert seq % tt_gla == 0 and seq % tt_ssd == 0

    mod = _modulation(jnp.concatenate([c_prompt, c_sample], axis=0), w_ada, b_ada)
    mod = mod.reshape(depth, bsz + nb, N_MOD, d)
    mod_p = mod[:, :bsz].transpose(0, 2, 1, 3).reshape(depth, N_MOD, bsz, 1, d)
    mod_s = mod[:, bsz:].transpose(0, 2, 1, 3).reshape(depth, N_MOD, 1, nb, d)

    cache_kt = cache_sb_k.transpose(0, 1, 3, 4, 2)
    cache_vt = cache_sb_v.transpose(0, 1, 3, 4, 2)

    xp = x_prompt.reshape(bsz * seq, d)
    xs = x_sample.reshape(nb, d)
    w_in_t = w_in.transpose(2, 0, 1)
    o = np.cumsum([0, GLA_DQK, GLA_DQK, GLA_DVW, GLA_DVW, GLA_LR, SB_W, SB_W, SB_W, M_DIN, M_CONV_DIM, M_HEADS])
    sl = lambda a, b: w_in_t[o[a]:o[b]]
    w_re = jnp.concatenate(
        [sl(0, 4), sl(5, 8), sl(8, 10), sl(10, 11), sl(4, 5),
         jnp.zeros((LANE - M_HEADS - GLA_LR, depth, d), F32)], axis=0).astype(BF16).transpose(1, 0, 2)
    wo = w_out.astype(BF16)
    wu = ffn_w_up.astype(BF16)
    wd = ffn_w_down.astype(BF16)
    outs = {name: [] for name in ("pk", "pv", "pg", "pc", "ph", "pf", "sk", "sv", "sg", "sc", "sh", "sf")}
    for l in range(depth):
        wg = jnp.zeros((LANE, LANE), F32).at[M_HEADS:M_HEADS + GLA_LR].set(gla_w_gate2[l]).astype(BF16)
        bg = gla_b_gate[l].reshape(1, GLA_DQK)
        gq = jnp.tile(sb_q_norm_g[l], SB_HEADS).reshape(1, SB_W)
        gk = jnp.tile(sb_k_norm_g[l], SB_HEADS).reshape(1, SB_W)
        go = jnp.tile(sb_o_norm_g[l], SB_HEADS).reshape(1, SB_W)
        gng = jnp.tile(gla_norm_g[l], GLA_HEADS).reshape(1, GLA_DVW)
        g1 = norm1_g[l].reshape(1, d)
        g2 = norm2_g[l].reshape(1, d)
        bias2 = sb_bias[l] * LOG2E
        b_hi = bias2.astype(BF16)
        b_mid = (bias2 - b_hi.astype(F32)).astype(BF16)
        b_lo = (bias2 - b_hi.astype(F32) - b_mid.astype(F32)).astype(BF16)
        pieces = jnp.stack([b_hi, b_mid, b_lo], axis=1)
        bias_cols = jnp.pad(jnp.repeat(pieces, SB_TK, axis=0).reshape(SB_HEADS // 2, 2 * SB_TK, SB_NB),
                            ((0, 0), (0, 0), (0, LANE - SB_NB)))
        bias_col = bias2.reshape(SB_HEADS, 1)
        cw = m_conv_w[l]
        cb = m_conv_b[l].reshape(1, M_CONV_DIM)
        nega = -jnp.exp(m_a_log[l])
        dtb_l, nega_l = _pad_lanes(m_dt_bias[l]), _pad_lanes(nega)
        dtb_c, nega_c = _pad_col(m_dt_bias[l]), _pad_col(nega)
        md = jnp.repeat(m_d[l], M_HD).reshape(1, M_DIN)
        gnm = m_norm_g[l].reshape(1, M_DIN)
        fcw = ffn_conv_w[l]
        fcb = ffn_conv_b[l].reshape(1, 2 * dff)

        gla, la, q, k, kb, v, vb, m, misc = _in_proj(
            xp, mod_p, l, g1, w_re, wg, bg, gq, gk, cst["eseg64"], tm=tm_mm, rows_per_mod=seq // tm_mm,
            kv_transposed=True)
        g_mix, g_state = _gla_prompt(gla, la, cst["eexp"], cst["bd"], cst["eseg64"], gng,
                                     bsz=bsz, seq=seq, tt=tt_gla)
        s_mix = _sb_prompt(q, kb, vb, bias_cols, cst["uo"], cst["hm_pair"], cst["eseg64"], go[:, :LANE],
                           bsz=bsz, seq=seq, tq=tq)
        dtt = jnp.pad(misc[:, :M_HEADS].reshape(bsz, seq, M_HEADS).transpose(0, 2, 1),
                      ((0, 0), (0, SUBLANE - M_HEADS), (0, 0)))
        y_mix, h_state = _ssd_prompt(m, misc, dtt, cw, cb, dtb_l, nega_l, dtb_c, nega_c, cst["e4"], cst["e4t"],
                                     md, gnm, bsz=bsz, seq=seq, tt=tt_ssd)
        xp = _out_proj(xp, mod_p, l, g_mix, s_mix, y_mix, wo, tm=tm_res, rows_per_mod=seq // tm_res)
        act, u_tail = _ffn_up_act(xp, mod_p, l, g2, wu, fcw, fcb, tm=tm, seq=seq)
        xp = _ffn_down(xp, mod_p, l, act, wd, tm=tm_res, seq=seq)

        outs["pk"].append(k.reshape(bsz, SB_HEADS, SB_HD, seq).transpose(0, 3, 1, 2))
        outs["pv"].append(v.reshape(bsz, SB_HEADS, SB_HD, seq).transpose(0, 3, 1, 2))
        gs = g_state.reshape(bsz, GLA_HEADS, GLA_DV, GLA_HEADS, GLA_DK)
        outs["pg"].append(jnp.stack([gs[:, h, :, h, :] for h in range(GLA_HEADS)], axis=1).transpose(0, 1, 3, 2))
        outs["pc"].append(m.reshape(bsz, seq, W_M)[:, seq - (M_CONV - 1):, M_DIN:])
        outs["ph"].append(h_state.reshape(bsz, M_HEADS, M_HD, M_N))
        outs["pf"].append(u_tail[:, SUBLANE - (FFN_CONV - 1):])

        gla, la, q, k, kb, v, vb, m, misc = _in_proj(
            xs, mod_s, l, g1, w_re, wg, bg, gq, gk, cst["eseg64"], tm=nb, rows_per_mod=1,
            kv_transposed=False)
        g4, g_state, y_mix, h_state = _step_mixers(
            gla, la, gla[:, 2 * GLA_DQK:2 * GLA_DQK + GLA_DVW].reshape(nb, GLA_HEADS, GLA_DV),
            gla[:, 2 * GLA_DQK + GLA_DVW:].reshape(nb, GLA_HEADS, GLA_DV),
            state_gla[l].reshape(nb, GLA_DQK, GLA_DV), m, state_mamba_conv[l], misc,
            state_mamba_ssm[l].reshape(nb, M_DIN, M_N), cw, cb, dtb_l, nega_l, cst["e4"], md,
            gla_norm_g[l].reshape(1, GLA_DV), gnm)
        s_mix = _sb_decode(page_table, q.astype(F32).reshape(nb, SB_HEADS, SB_HD, 1), cache_kt, cache_vt, l,
                           bias_col, sb_o_norm_g[l].reshape(1, SB_HD),
                           jnp.asarray(np.tril(np.ones((page, page), np.float32)), BF16))
        xs = _out_proj(xs, mod_s, l, g4.reshape(nb, GLA_DVW), s_mix, y_mix.reshape(nb, M_DIN), wo,
                       tm=nb, rows_per_mod=1)
        u = _ffn_up(xs, mod_s, l, g2, wu, tm=nb, rows_per_mod=1)
        xs = _ffn_down_step(xs, mod_s, l, u, state_ffn_conv[l].transpose(1, 0, 2), fcw, fcb, wd)

        outs["sk"].append(k.reshape(nb, 1, SB_HEADS, SB_HD))
        outs["sv"].append(v.reshape(nb, 1, SB_HEADS, SB_HD))
        outs["sg"].append(g_state.reshape(nb, GLA_HEADS, GLA_DK, GLA_DV))
        outs["sc"].append(jnp.concatenate([state_mamba_conv[l][:, 1:], m[:, None, M_DIN:]], axis=1))
        outs["sh"].append(h_state.reshape(nb, M_HEADS, M_HD, M_N))
        outs["sf"].append(jnp.concatenate([state_ffn_conv[l][:, 1:], u[:, None, :]], axis=1))

    st = {name: jnp.stack(v) for name, v in outs.items()}
    return (xp.reshape(bsz, seq, d), xs.reshape(nb, 1, d), st["pk"], st["pv"], st["pg"], st["pc"], st["ph"],
            st["pf"], st["sk"], st["sv"], st["sg"], st["sc"], st["sh"], st["sf"])
```
